```python
import jax, jax.numpy as jnp
from jax import lax
import numpy as np

D_MODEL = 4096
BATCH = 2
SEQ = 4096
DEPTH = 1
DEC_BATCH = 32
DEC_SEQ = 8
PAST_LEN = 8192
PAGE_SIZE = 128

HEAD_DIM = 128
N_HEADS_A = D_MODEL // 2 // HEAD_DIM
N_KV_A = 2
N_HEADS_B = D_MODEL // 2 // HEAD_DIM
N_IDX_HEADS = N_HEADS_B // 2
IDX_DIM = 128
D_FF = ((8 * D_MODEL // 3 + 255) // 256) * 256
CMP_BLOCK = 32
CMP_STRIDE = 16
CMP_RATIO = CMP_BLOCK // CMP_STRIDE
CMP_HIDDEN = 2 * HEAD_DIM
SEL_BLOCK = 64
N_SEL = 16
N_LOCAL_SEL = 2
WINDOW = 512
DSA_TOPK = 256
ROPE_THETA = 10000.0
RMS_EPS = 1e-6
Q_BLOCK = 128
NEG = -1e30
BIG = 1e30
TINY = 1e-30
IN_SPLITS = (N_HEADS_A * HEAD_DIM,
             6 * N_KV_A * HEAD_DIM,
             3 * N_HEADS_A,
             N_HEADS_B * HEAD_DIM,
             2 * HEAD_DIM,
             N_IDX_HEADS * IDX_DIM,
             N_IDX_HEADS,
             IDX_DIM,
             2 * D_MODEL)
N_IN = sum(IN_SPLITS)

kernel_name = "nsa_dsa_gated_macaron_decoder_step"


def rms_norm(x, g):
    xf = x.astype(jnp.float32)
    y = xf * lax.rsqrt(jnp.mean(xf * xf, axis=-1, keepdims=True) + RMS_EPS)
    return (y * g.astype(jnp.float32)).astype(x.dtype)


def swiglu(x, wg, wu, wd):
    return (jax.nn.silu(x @ wg) * (x @ wu)) @ wd


def rope(x, pos):
    half = x.shape[-1] // 2
    inv = ROPE_THETA ** (-jnp.arange(half, dtype=jnp.float32) / half)
    ang = pos.astype(jnp.float32)[:, None] * inv[None, :]
    shape = (ang.shape[0],) + (1,) * (x.ndim - 3) + (half,)
    cos = jnp.cos(ang).reshape(shape)
    sin = jnp.sin(ang).reshape(shape)
    xf = x.astype(jnp.float32)
    x1, x2 = xf[..., :half], xf[..., half:]
    return jnp.concatenate([x1 * cos - x2 * sin, x2 * cos + x1 * sin], axis=-1).astype(x.dtype)


def masked_softmax(s, mask):
    s = jnp.where(mask, s.astype(jnp.float32), NEG)
    m = jnp.max(s, axis=-1, keepdims=True)
    p = jnp.where(mask, jnp.exp(s - m), 0.0)
    return p / jnp.maximum(jnp.sum(p, axis=-1, keepdims=True), TINY)


def blockify(x, nb):
    return jnp.swapaxes(x.reshape(x.shape[0], nb, x.shape[1] // nb, *x.shape[2:]), 0, 1)


def unblockify(x):
    x = jnp.swapaxes(x, 0, 1)
    return x.reshape(x.shape[0], -1, *x.shape[3:])


def gather_pages(pool, page_table):
    g = pool[page_table]
    return g.reshape(g.shape[0], -1, *pool.shape[2:])


def in_project(u, w_in, pos):
    B, T, _ = u.shape
    offs = tuple(int(c) for c in np.cumsum(IN_SPLITS)[:-1])
    q_a, kv_a, g_a, q_b, kv_b, iq, iw, ik, mg = jnp.split(u @ w_in, offs, axis=-1)
    q_a = rope(q_a.reshape(B, T, N_HEADS_A, HEAD_DIM), pos)
    kv_a = kv_a.reshape(B, T, 6, N_KV_A, HEAD_DIM)
    nsa_rows = jnp.stack([kv_a[:, :, 0], kv_a[:, :, 1], rope(kv_a[:, :, 2], pos), kv_a[:, :, 3]], axis=2)
    win_rows = jnp.stack([rope(kv_a[:, :, 4], pos), kv_a[:, :, 5]], axis=2)
    g_a = jax.nn.sigmoid(g_a.astype(jnp.float32)).reshape(B, T, 3, N_HEADS_A)
    q_b = rope(q_b.reshape(B, T, N_HEADS_B, HEAD_DIM), pos)
    kv_b = kv_b.reshape(B, T, 2, HEAD_DIM)
    dsa_rows = jnp.stack([rope(kv_b[:, :, 0], pos), kv_b[:, :, 1]], axis=2)
    iq = rope(iq.reshape(B, T, N_IDX_HEADS, IDX_DIM), pos)
    ik = rope(ik, pos)
    return q_a, nsa_rows, win_rows, g_a, q_b, dsa_rows, iq, iw, ik, mg


def compress(kv, w1, w2, pos_emb):
    B, L, G, Dh = kv.shape
    n_c = (L - CMP_BLOCK) // CMP_STRIDE + 1
    n_ch = n_c + CMP_RATIO - 1
    ch = kv[:, :n_ch * CMP_STRIDE].reshape(B, n_ch, CMP_STRIDE, G, Dh)
    w1r = w1.reshape(CMP_RATIO, CMP_STRIDE, Dh, -1)
    per = pos_emb.reshape(CMP_RATIO, CMP_STRIDE, 1, Dh)
    h = sum(jnp.einsum('bcsgd,sdf->bcgf', ch[:, r:r + n_c] + per[r], w1r[r]) for r in range(CMP_RATIO))
    return jnp.einsum('bcgf,fd->bcgd', jax.nn.silu(h), w2)


def nsa_compressed(q, q_pos, k_raw, v_raw, w_cmp1, w_cmp2, cmp_pos):
    k_c = compress(k_raw, w_cmp1[0], w_cmp2[0], cmp_pos[0])
    v_c = compress(v_raw, w_cmp1[1], w_cmp2[1], cmp_pos[1])
    n_c = k_c.shape[1]
    end = jnp.arange(n_c, dtype=jnp.int32) * CMP_STRIDE + CMP_BLOCK - 1
    k_c = rope(k_c, end)
    mask = end[None, :] <= q_pos[:, None]
    B, T, H, Dh = q.shape
    G = k_c.shape[2]
    qg = q.reshape(B, T, G, H // G, Dh)
    s = jnp.einsum('btgrd,bcgd->btgrc', qg, k_c) * HEAD_DIM ** -0.5
    p = masked_softmax(s, mask[None, :, None, None, :])
    o = jnp.einsum('btgrc,bcgd->btgrd', p.astype(v_c.dtype), v_c).reshape(B, T, H, Dh)
    return o, p


def nsa_block_choice(p_cmp, q_pos, n_s):
    n_c = p_cmp.shape[-1]
    ci = jnp.arange(n_c)[:, None] * CMP_STRIDE
    sj = jnp.arange(n_s)[None, :] * SEL_BLOCK
    cover = ((ci < sj + SEL_BLOCK) & (ci + CMP_BLOCK > sj)).astype(jnp.float32)
    imp = jnp.einsum('btgrc,cj->btgj', p_cmp, cover)
    j = jnp.arange(n_s)[None, :]
    jt = (q_pos // SEL_BLOCK)[:, None]
    adm = j <= jt
    forced = adm & ((j == 0) | (j > jt - N_LOCAL_SEL))
    score = jnp.where(forced[None, :, None, :], BIG, jnp.where(adm[None, :, None, :], imp, NEG))
    _, idx = lax.top_k(score, min(N_SEL, n_s))
    return idx


def to_blocks(k, n_s):
    B, L, G, Dh = k.shape
    k = jnp.pad(k, ((0, 0), (0, n_s * SEL_BLOCK - L), (0, 0), (0, 0)))
    return k.reshape(B, n_s, SEL_BLOCK, G, Dh).transpose(0, 3, 1, 2, 4)


def nsa_select_attend(q, q_pos, blk_idx, k_blk, v_blk):
    B, Tq, H, Dh = q.shape
    G = k_blk.shape[1]
    n = blk_idx.shape[-1]
    bi = jnp.arange(B)[:, None, None, None]
    gi = jnp.arange(G)[None, None, :, None]
    kg = k_blk[bi, gi, blk_idx].reshape(B, Tq, G, n * SEL_BLOCK, Dh)
    vg = v_blk[bi, gi, blk_idx].reshape(B, Tq, G, n * SEL_BLOCK, Dh)
    key_pos = (blk_idx[..., None] * SEL_BLOCK + jnp.arange(SEL_BLOCK)).reshape(B, Tq, G, n * SEL_BLOCK)
    mask = key_pos <= q_pos[None, :, None, None]
    qg = q.reshape(B, Tq, G, H // G, Dh)
    s = jnp.einsum('bqgrd,bqgmd->bqgrm', qg, kg) * HEAD_DIM ** -0.5
    p = masked_softmax(s, mask[:, :, :, None, :])
    return jnp.einsum('bqgrm,bqgmd->bqgrd', p.astype(vg.dtype), vg).reshape(B, Tq, H, Dh)


def gqa_attend(q, k, v, mask):
    B, N, Tq, H, Dh = q.shape
    G = k.shape[3]
    qg = q.reshape(B, N, Tq, G, H // G, Dh)
    s = jnp.einsum('bnqgrd,bnkgd->bnqgrk', qg, k) * HEAD_DIM ** -0.5
    p = masked_softmax(s, mask[None, :, :, None, None, :])
    return jnp.einsum('bnqgrk,bnkgd->bnqgrd', p.astype(v.dtype), v).reshape(B, N, Tq, H, Dh)


def window_attend(q, q_pos, k, v, k_pos):
    d = q_pos[:, :, None] - k_pos[:, None, :]
    mask = (d >= 0) & (d < WINDOW) & (k_pos[:, None, :] >= 0)
    return gqa_attend(q, k, v, mask)


def nsa_combine(gate, o_cmp, o_slc, o_win):
    g = gate[..., None]
    o = g[:, :, 0] * o_cmp.astype(jnp.float32) + g[:, :, 1] * o_slc.astype(jnp.float32) + g[:, :, 2] * o_win.astype(jnp.float32)
    return o.astype(o_cmp.dtype)


def nsa_prompt(q, gate, nsa_rows, win_rows, pos, w_cmp1, w_cmp2, cmp_pos):
    B, T, H, Dh = q.shape
    nqb = T // Q_BLOCK
    o_cmp, p_cmp = nsa_compressed(q, pos, nsa_rows[:, :, 0], nsa_rows[:, :, 1], w_cmp1, w_cmp2, cmp_pos)
    n_s = -(-T // SEL_BLOCK)
    idx = nsa_block_choice(p_cmp, pos, n_s)
    k_blk = to_blocks(nsa_rows[:, :, 2], n_s)
    v_blk = to_blocks(nsa_rows[:, :, 3], n_s)
    o_slc = unblockify(lax.map(lambda a: nsa_select_attend(a[0], a[1], a[2], k_blk, v_blk),
                               (blockify(q, nqb), pos.reshape(nqb, Q_BLOCK), blockify(idx, nqb))))
    kw = jnp.pad(win_rows, ((0, 0), (WINDOW, 0), (0, 0), (0, 0), (0, 0)))
    kidx = (jnp.arange(nqb) * Q_BLOCK)[:, None] + jnp.arange(WINDOW + Q_BLOCK)[None, :]
    kwb = kw[:, kidx]
    o_win = window_attend(q.reshape(B, nqb, Q_BLOCK, H, Dh), pos.reshape(nqb, Q_BLOCK),
                          kwb[:, :, :, 0], kwb[:, :, :, 1], kidx - WINDOW).reshape(B, T, H, Dh)
    return nsa_combine(gate, o_cmp, o_slc, o_win)


def nsa_sample(q, gate, nsa_rows, win_rows, pos, past_nsa, win_buf, past_len, w_cmp1, w_cmp2, cmp_pos):
    B, Tq, H, Dh = q.shape
    kv_all = jnp.concatenate([past_nsa, nsa_rows], axis=1)
    L = kv_all.shape[1]
    o_cmp, p_cmp = nsa_compressed(q, pos, kv_all[:, :, 0], kv_all[:, :, 1], w_cmp1, w_cmp2, cmp_pos)
    n_s = -(-L // SEL_BLOCK)
    idx = nsa_block_choice(p_cmp, pos, n_s)
    o_slc = nsa_select_attend(q, pos, idx, to_blocks(kv_all[:, :, 2], n_s), to_blocks(kv_all[:, :, 3], n_s))
    W = win_buf.shape[1]
    wk = jnp.concatenate([win_buf, win_rows], axis=1)
    k_pos = past_len - W + jnp.arange(W + Tq, dtype=jnp.int32)
    o_win = window_attend(q[:, None], pos[None], wk[:, None, :, 0], wk[:, None, :, 1], k_pos[None])[:, 0]
    return nsa_combine(gate, o_cmp, o_slc, o_win), wk[:, Tq:]


def dsa_attend(q, iq, iw, q_pos, k, v, ik, n_top):
    B = q.shape[0]
    L = k.shape[1]
    logits = jnp.einsum('bqhd,bsd->bqhs', iq, ik).astype(jnp.float32) * IDX_DIM ** -0.5
    score = jnp.einsum('bqhs,bqh->bqs', jax.nn.relu(logits), iw.astype(jnp.float32)) * N_IDX_HEADS ** -0.5
    causal = jnp.arange(L)[None, :] <= q_pos[:, None]
    score = jnp.where(causal[None], score, NEG)
    _, idx = lax.top_k(score, n_top)
    valid = idx <= q_pos[None, :, None]
    bi = jnp.arange(B)[:, None, None]
    kg = k[bi, idx]
    vg = v[bi, idx]
    s = jnp.einsum('bqhd,bqkd->bqhk', q, kg) * HEAD_DIM ** -0.5
    p = masked_softmax(s, valid[:, :, None, :])
    return jnp.einsum('bqhk,bqkd->bqhd', p.astype(vg.dtype), vg)


def dsa_prompt(q, iq, iw, dsa_rows, ik, pos):
    T = q.shape[1]
    nqb = T // Q_BLOCK
    n_top = min(DSA_TOPK, T // 4)
    k, v = dsa_rows[:, :, 0], dsa_rows[:, :, 1]
    o = lax.map(lambda a: dsa_attend(a[0], a[1], a[2], a[3], k, v, ik, n_top),
                (blockify(q, nqb), blockify(iq, nqb), blockify(iw, nqb), pos.reshape(nqb, Q_BLOCK)))
    return unblockify(o)


def merge_out(o_a, o_b, mg, w_br_a, w_br_b, w_out):
    B, T = o_a.shape[:2]
    gates = jax.nn.sigmoid(mg.astype(jnp.float32)).astype(o_a.dtype)
    g_a, g_b = gates[..., :D_MODEL], gates[..., D_MODEL:]
    m = g_a * (o_a.reshape(B, T, -1) @ w_br_a) + g_b * (o_b.reshape(B, T, -1) @ w_br_b)
    return m @ w_out


def setup_inputs(seed: int = 0) -> dict:
    key = jax.random.key(seed)
    ks = jax.random.split(key, 20)
    n_pages = PAST_LEN // PAGE_SIZE
    n_pool = (DEC_BATCH * n_pages * 5) // 4
    w_buf = min(WINDOW, PAST_LEN)

    def nrm(k, shape, scale=1.0):
        return jax.random.normal(k, shape, jnp.float32) * scale

    return {
        "x_prompt": nrm(ks[0], (BATCH, SEQ, D_MODEL)),
        "x_sample": nrm(ks[1], (DEC_BATCH, DEC_SEQ, D_MODEL)),
        "cache_nsa_kv": nrm(ks[2], (DEPTH, n_pool, PAGE_SIZE, 4, N_KV_A, HEAD_DIM)),
        "cache_nsa_win": nrm(ks[3], (DEPTH, DEC_BATCH, w_buf, 2, N_KV_A, HEAD_DIM)),
        "cache_dsa_kv": nrm(ks[4], (DEPTH, n_pool, PAGE_SIZE, 2, HEAD_DIM)),
        "cache_dsa_idx": nrm(ks[5], (DEPTH, n_pool, PAGE_SIZE, IDX_DIM)),
        "page_table": jax.random.permutation(ks[6], n_pool)[:DEC_BATCH * n_pages].reshape(DEC_BATCH, n_pages).astype(jnp.int32),
        "g_norm": 1.0 + nrm(ks[7], (DEPTH, 3, D_MODEL), 0.01),
        "w_ffn_gate": nrm(ks[8], (DEPTH, 2, D_MODEL, D_FF), D_MODEL ** -0.5),
        "w_ffn_up": nrm(ks[9], (DEPTH, 2, D_MODEL, D_FF), D_MODEL ** -0.5),
        "w_ffn_down": nrm(ks[10], (DEPTH, 2, D_FF, D_MODEL), D_FF ** -0.5),
        "w_in": nrm(ks[11], (DEPTH, D_MODEL, N_IN), D_MODEL ** -0.5),
        "w_cmp1": nrm(ks[12], (DEPTH, 2, CMP_BLOCK * HEAD_DIM, CMP_HIDDEN), (CMP_BLOCK * HEAD_DIM) ** -0.5),
        "w_cmp2": nrm(ks[13], (DEPTH, 2, CMP_HIDDEN, HEAD_DIM), CMP_HIDDEN ** -0.5),
        "cmp_pos": nrm(ks[14], (DEPTH, 2, CMP_BLOCK, HEAD_DIM), 0.1),
        "w_br_a": nrm(ks[15], (DEPTH, N_HEADS_A * HEAD_DIM, D_MODEL), (N_HEADS_A * HEAD_DIM) ** -0.5),
        "w_br_b": nrm(ks[16], (DEPTH, N_HEADS_B * HEAD_DIM, D_MODEL), (N_HEADS_B * HEAD_DIM) ** -0.5),
        "w_out": nrm(ks[17], (DEPTH, D_MODEL, D_MODEL), D_MODEL ** -0.5),
        "g_final": 1.0 + nrm(ks[18], (D_MODEL,), 0.01),
    }


def reference(x_prompt, x_sample, cache_nsa_kv, cache_nsa_win, cache_dsa_kv, cache_dsa_idx, page_table,
              g_norm, w_ffn_gate, w_ffn_up, w_ffn_down, w_in, w_cmp1, w_cmp2, cmp_pos,
              w_br_a, w_br_b, w_out, g_final):
    T = x_prompt.shape[1]
    Ts = x_sample.shape[1]
    past_len = page_table.shape[1] * cache_nsa_kv.shape[2]
    pos_p = jnp.arange(T, dtype=jnp.int32)
    pos_s = past_len + jnp.arange(Ts, dtype=jnp.int32)
    hp, hs = x_prompt, x_sample
    nsa_kv_p, nsa_win_p, dsa_kv_p, dsa_idx_p = [], [], [], []
    nsa_kv_s, nsa_win_s, dsa_kv_s, dsa_idx_s = [], [], [], []
    for l in range(DEPTH):
        hp = hp + 0.5 * swiglu(rms_norm(hp, g_norm[l, 0]), w_ffn_gate[l, 0], w_ffn_up[l, 0], w_ffn_down[l, 0])
        hs = hs + 0.5 * swiglu(rms_norm(hs, g_norm[l, 0]), w_ffn_gate[l, 0], w_ffn_up[l, 0], w_ffn_down[l, 0])

        q_a, nsa_rows, win_rows, g_a, q_b, dsa_rows, iq, iw, ik, mg = in_project(rms_norm(hp, g_norm[l, 1]), w_in[l], pos_p)
        o_a = nsa_prompt(q_a, g_a, nsa_rows, win_rows, pos_p, w_cmp1[l], w_cmp2[l], cmp_pos[l])
        o_b = dsa_prompt(q_b, iq, iw, dsa_rows, ik, pos_p)
        hp = hp + merge_out(o_a, o_b, mg, w_br_a[l], w_br_b[l], w_out[l])
        nsa_kv_p.append(nsa_rows)
        nsa_win_p.append(win_rows[:, T - min(WINDOW, T):])
        dsa_kv_p.append(dsa_rows)
        dsa_idx_p.append(ik)

        q_a, nsa_rows, win_rows, g_a, q_b, dsa_rows, iq, iw, ik, mg = in_project(rms_norm(hs, g_norm[l, 1]), w_in[l], pos_s)
        past_nsa = gather_pages(cache_nsa_kv[l], page_table)
        o_a, new_buf = nsa_sample(q_a, g_a, nsa_rows, win_rows, pos_s, past_nsa, cache_nsa_win[l], past_len,
                                  w_cmp1[l], w_cmp2[l], cmp_pos[l])
        kv_all = jnp.concatenate([gather_pages(cache_dsa_kv[l], page_table), dsa_rows], axis=1)
        ik_all = jnp.concatenate([gather_pages(cache_dsa_idx[l], page_table), ik], axis=1)
        n_top = min(DSA_TOPK, kv_all.shape[1] // 4)
        o_b = dsa_attend(q_b, iq, iw, pos_s, kv_all[:, :, 0], kv_all[:, :, 1], ik_all, n_top)
        hs = hs + merge_out(o_a, o_b, mg, w_br_a[l], w_br_b[l], w_out[l])
        nsa_kv_s.append(nsa_rows)
        nsa_win_s.append(new_buf)
        dsa_kv_s.append(dsa_rows)
        dsa_idx_s.append(ik)

        hp = hp + 0.5 * swiglu(rms_norm(hp, g_norm[l, 2]), w_ffn_gate[l, 1], w_ffn_up[l, 1], w_ffn_down[l, 1])
        hs = hs + 0.5 * swiglu(rms_norm(hs, g_norm[l, 2]), w_ffn_gate[l, 1], w_ffn_up[l, 1], w_ffn_down[l, 1])

    y_prompt = rms_norm(hp, g_final)
    y_sample = rms_norm(hs, g_final)
    return (y_prompt, y_sample,
            jnp.stack(nsa_kv_p), jnp.stack(nsa_win_p), jnp.stack(dsa_kv_p), jnp.stack(dsa_idx_p),
            jnp.stack(nsa_kv_s), jnp.stack(nsa_win_s), jnp.stack(dsa_kv_s), jnp.stack(dsa_idx_s))
```

```python
import functools

import jax
import jax.numpy as jnp
import numpy as np
from jax import lax
from jax.experimental import pallas as pl
from jax.experimental.pallas import tpu as pltpu

HEAD_DIM = 128
N_KV_A = 2
IDX_DIM = 128
CMP_BLOCK = 32
CMP_STRIDE = 16
CMP_RATIO = CMP_BLOCK // CMP_STRIDE
SEL_BLOCK = 64
N_SEL = 16
N_LOCAL_SEL = 2
WINDOW = 512
DSA_TOPK = 256
ROPE_THETA = 10000.0
RMS_EPS = 1e-6
Q_BLOCK = 128
NEG = -1e30
BIG = 1e30
TINY = 1e-30

LANES = 128
VMEM_LIMIT = 56 * 1024 * 1024


def _params(*sem):
    return pltpu.CompilerParams(dimension_semantics=sem, vmem_limit_bytes=VMEM_LIMIT)


def _rmsnorm_kernel(x_ref, g_ref, o_ref):
    x = x_ref[...]
    y = x * lax.rsqrt(jnp.mean(x * x, axis=-1, keepdims=True) + RMS_EPS)
    o_ref[...] = (y * g_ref[...]).astype(o_ref.dtype)


def rmsnorm(x, g, out_dtype, tm=256):
    M, D = x.shape
    return pl.pallas_call(
        _rmsnorm_kernel,
        grid=(M // tm,),
        in_specs=[pl.BlockSpec((tm, D), lambda i: (i, 0)),
                  pl.BlockSpec((1, D), lambda i: (0, 0))],
        out_specs=pl.BlockSpec((tm, D), lambda i: (i, 0)),
        out_shape=jax.ShapeDtypeStruct((M, D), out_dtype),
        compiler_params=_params("parallel"),
    )(x, g.reshape(1, D))


def _gateup_kernel(x_ref, wg_ref, wu_ref, o_ref):
    x = x_ref[...]
    g = jnp.dot(x, wg_ref[...], preferred_element_type=jnp.float32)
    u = jnp.dot(x, wu_ref[...], preferred_element_type=jnp.float32)
    o_ref[...] = (g * jax.nn.sigmoid(g) * u).astype(o_ref.dtype)


def ffn_gate_up(xn, wg, wu, tm, tn):
    M, D = xn.shape
    F = wg.shape[1]
    return pl.pallas_call(
        _gateup_kernel,
        grid=(M // tm, F // tn),
        in_specs=[pl.BlockSpec((tm, D), lambda i, j: (i, 0)),
                  pl.BlockSpec((D, tn), lambda i, j: (0, j)),
                  pl.BlockSpec((D, tn), lambda i, j: (0, j))],
        out_specs=pl.BlockSpec((tm, tn), lambda i, j: (i, j)),
        out_shape=jax.ShapeDtypeStruct((M, F), jnp.bfloat16),
        compiler_params=_params("parallel", "parallel"),
    )(xn, wg, wu)


def _resid_matmul_kernel(a_ref, w_ref, r_ref, o_ref, *, scale):
    acc = jnp.dot(a_ref[...], w_ref[...], preferred_element_type=jnp.float32)
    o_ref[...] = r_ref[...] + scale * acc


def resid_matmul(a, w, resid, scale, tm, tn):
    M, K = a.shape
    N = w.shape[1]
    return pl.pallas_call(
        functools.partial(_resid_matmul_kernel, scale=scale),
        grid=(M // tm, N // tn),
        in_specs=[pl.BlockSpec((tm, K), lambda i, j: (i, 0)),
                  pl.BlockSpec((K, tn), lambda i, j: (0, j)),
                  pl.BlockSpec((tm, tn), lambda i, j: (i, j))],
        out_specs=pl.BlockSpec((tm, tn), lambda i, j: (i, j)),
        out_shape=jax.ShapeDtypeStruct((M, N), jnp.float32),
        compiler_params=_params("parallel", "parallel"),
    )(a, w, resid)


def _inproj_kernel(flags_ref, x_ref, w_ref, cos_ref, sin_ref, o_ref, *, n_chunks):
    j = pl.program_id(1)
    acc = jnp.dot(x_ref[...], w_ref[...], preferred_element_type=jnp.float32)
    for c in range(n_chunks):
        sl = slice(c * LANES, (c + 1) * LANES)
        y = acc[:, sl]
        flag = flags_ref[j * n_chunks + c]

        @pl.when(flag == 1)
        def _():
            o_ref[:, sl] = y * cos_ref[...] + pltpu.roll(y, HEAD_DIM // 2, axis=1) * sin_ref[...]

        @pl.when(flag == 0)
        def _():
            o_ref[:, sl] = y


def in_project(u, w, rope_flags, cos, sin, tm, tn):
    M, D = u.shape
    N = w.shape[1]
    n_chunks = tn // LANES
    grid_spec = pltpu.PrefetchScalarGridSpec(
        num_scalar_prefetch=1,
        grid=(M // tm, N // tn),
        in_specs=[pl.BlockSpec((tm, D), lambda i, j, f: (i, 0)),
                  pl.BlockSpec((D, tn), lambda i, j, f: (0, j)),
                  pl.BlockSpec((tm, LANES), lambda i, j, f: (i, 0)),
                  pl.BlockSpec((tm, LANES), lambda i, j, f: (i, 0))],
        out_specs=pl.BlockSpec((tm, tn), lambda i, j, f: (i, j)),
    )
    return pl.pallas_call(
        functools.partial(_inproj_kernel, n_chunks=n_chunks),
        grid_spec=grid_spec,
        out_shape=jax.ShapeDtypeStruct((M, N), jnp.float32),
        compiler_params=_params("parallel", "parallel"),
    )(rope_flags, u, w, cos, sin)


def _merge_kernel(oa_ref, ob_ref, wa_ref, wb_ref, ga_ref, gb_ref, o_ref):
    ya = jnp.dot(oa_ref[...], wa_ref[...], preferred_element_type=jnp.float32)
    yb = jnp.dot(ob_ref[...], wb_ref[...], preferred_element_type=jnp.float32)
    m = jax.nn.sigmoid(ga_ref[...]) * ya + jax.nn.sigmoid(gb_ref[...]) * yb
    o_ref[...] = m.astype(o_ref.dtype)


def merge_branches(o_a, o_b, w_a, w_b, proj, mg_col, tm, tn):
    M, K = o_a.shape
    N = w_a.shape[1]
    ja = mg_col // tn
    jb = (mg_col + N) // tn
    return pl.pallas_call(
        _merge_kernel,
        grid=(M // tm, N // tn),
        in_specs=[pl.BlockSpec((tm, K), lambda i, j: (i, 0)),
                  pl.BlockSpec((tm, K), lambda i, j: (i, 0)),
                  pl.BlockSpec((K, tn), lambda i, j: (0, j)),
                  pl.BlockSpec((K, tn), lambda i, j: (0, j)),
                  pl.BlockSpec((tm, tn), lambda i, j: (i, ja + j)),
                  pl.BlockSpec((tm, tn), lambda i, j: (i, jb + j))],
        out_specs=pl.BlockSpec((tm, tn), lambda i, j: (i, j)),
        out_shape=jax.ShapeDtypeStruct((M, N), jnp.bfloat16),
        compiler_params=_params("parallel", "parallel"),
    )(o_a, o_b, w_a, w_b, proj, proj)


class Layout:
    def __init__(self, d_model, n_heads_a, n_heads_b, n_idx_heads):
        self.n_heads_a, self.n_heads_b, self.n_idx_heads = n_heads_a, n_heads_b, n_idx_heads
        src = np.cumsum([0, n_heads_a * HEAD_DIM, 6 * N_KV_A * HEAD_DIM, 3 * n_heads_a,
                         n_heads_b * HEAD_DIM, 2 * HEAD_DIM, n_idx_heads * IDX_DIM, n_idx_heads,
                         IDX_DIM, 2 * d_model])
        s_qa, s_kva, s_ga, s_qb, s_kvb, s_iq, s_iw, s_ik, s_mg, s_end = (int(v) for v in src)
        small = 3 * n_heads_a + n_idx_heads
        assert small <= LANES
        self.small_pad = LANES - small
        self.pieces = [
            (s_qa, s_kva, True),
            (s_kva, s_kva + 2 * 256, False),
            (s_kva + 2 * 256, s_kva + 3 * 256, True),
            (s_kva + 3 * 256, s_kva + 4 * 256, False),
            (s_kva + 4 * 256, s_kva + 5 * 256, True),
            (s_kva + 5 * 256, s_ga, False),
            (s_qb, s_kvb, True),
            (s_iq, s_iw, True),
            (s_kvb, s_kvb + HEAD_DIM, True),
            (s_ik, s_mg, True),
            (s_kvb + HEAD_DIM, s_iq, False),
            (s_ga, s_qb, False),
            (s_iw, s_ik, False),
            None,
            (s_mg, s_end, False),
        ]
        off = 0
        starts = []
        for p in self.pieces:
            starts.append(off)
            off += self.small_pad if p is None else p[1] - p[0]
        (self.qa, self.cmp, self.selk, self.selv, self.wink, self.winv, self.qb, self.iq, self.dk,
         self.ik, self.dv, self.ga, self.iw, _, self.mg) = starts
        self.kva = self.cmp
        self.width = off
        assert self.width % LANES == 0
        flags = np.zeros(self.width // LANES, np.int32)
        for st, p in zip(starts, self.pieces):
            if p is not None and p[2]:
                assert st % LANES == 0 and (p[1] - p[0]) % LANES == 0
                flags[st // LANES:(st + p[1] - p[0]) // LANES] = 1
        self.rope_flags = flags

    def pack(self, w_in):
        cols = []
        for p in self.pieces:
            if p is None:
                cols.append(jnp.zeros((w_in.shape[0], self.small_pad), w_in.dtype))
            else:
                cols.append(w_in[:, p[0]:p[1]])
        return jnp.concatenate(cols, axis=1).astype(jnp.bfloat16)


def rope_tables(pos):
    half = HEAD_DIM // 2
    inv = ROPE_THETA ** (-jnp.arange(half, dtype=jnp.float32) / half)
    ang = pos.astype(jnp.float32)[:, None] * inv[None, :]
    cos, sin = jnp.cos(ang), jnp.sin(ang)
    return jnp.concatenate([cos, cos], axis=1), jnp.concatenate([-sin, sin], axis=1)


def _rope(x, pos):
    half = x.shape[-1] // 2
    inv = ROPE_THETA ** (-jnp.arange(half, dtype=jnp.float32) / half)
    ang = pos.astype(jnp.float32)[:, None] * inv[None, :]
    shape = (ang.shape[0],) + (1,) * (x.ndim - 3) + (half,)
    cos = jnp.cos(ang).reshape(shape)
    sin = jnp.sin(ang).reshape(shape)
    xf = x.astype(jnp.float32)
    x1, x2 = xf[..., :half], xf[..., half:]
    return jnp.concatenate([x1 * cos - x2 * sin, x2 * cos + x1 * sin], axis=-1).astype(x.dtype)


def _masked_softmax(s, mask):
    s = jnp.where(mask, s.astype(jnp.float32), NEG)
    m = jnp.max(s, axis=-1, keepdims=True)
    p = jnp.where(mask, jnp.exp(s - m), 0.0)
    return p / jnp.maximum(jnp.sum(p, axis=-1, keepdims=True), TINY)


def _blockify(x, nb):
    return jnp.swapaxes(x.reshape(x.shape[0], nb, x.shape[1] // nb, *x.shape[2:]), 0, 1)


def _unblockify(x):
    x = jnp.swapaxes(x, 0, 1)
    return x.reshape(x.shape[0], -1, *x.shape[3:])


def _gather_pages(pool, page_table):
    g = pool[page_table]
    return g.reshape(g.shape[0], -1, *pool.shape[2:])


def _compress(kv, w1, w2, pos_emb):
    B, L, G, Dh = kv.shape
    n_c = (L - CMP_BLOCK) // CMP_STRIDE + 1
    n_ch = n_c + CMP_RATIO - 1
    ch = kv[:, :n_ch * CMP_STRIDE].reshape(B, n_ch, CMP_STRIDE, G, Dh)
    w1r = w1.reshape(CMP_RATIO, CMP_STRIDE, Dh, -1)
    per = pos_emb.reshape(CMP_RATIO, CMP_STRIDE, 1, Dh)
    h = sum(jnp.einsum('bcsgd,sdf->bcgf', ch[:, r:r + n_c] + per[r], w1r[r]) for r in range(CMP_RATIO))
    return jnp.einsum('bcgf,fd->bcgd', jax.nn.silu(h), w2)


def _nsa_compressed(q, q_pos, k_raw, v_raw, w_cmp1, w_cmp2, cmp_pos):
    k_c = _compress(k_raw, w_cmp1[0], w_cmp2[0], cmp_pos[0])
    v_c = _compress(v_raw, w_cmp1[1], w_cmp2[1], cmp_pos[1])
    n_c = k_c.shape[1]
    end = jnp.arange(n_c, dtype=jnp.int32) * CMP_STRIDE + CMP_BLOCK - 1
    k_c = _rope(k_c, end)
    mask = end[None, :] <= q_pos[:, None]
    B, T, H, Dh = q.shape
    G = k_c.shape[2]
    qg = q.reshape(B, T, G, H // G, Dh)
    s = jnp.einsum('btgrd,bcgd->btgrc', qg, k_c) * HEAD_DIM ** -0.5
    p = _masked_softmax(s, mask[None, :, None, None, :])
    o = jnp.einsum('btgrc,bcgd->btgrd', p.astype(v_c.dtype), v_c).reshape(B, T, H, Dh)
    return o, p


def _nsa_block_choice(p_cmp, q_pos, n_s):
    n_c = p_cmp.shape[-1]
    ci = jnp.arange(n_c)[:, None] * CMP_STRIDE
    sj = jnp.arange(n_s)[None, :] * SEL_BLOCK
    cover = ((ci < sj + SEL_BLOCK) & (ci + CMP_BLOCK > sj)).astype(jnp.float32)
    imp = jnp.einsum('btgrc,cj->btgj', p_cmp, cover)
    j = jnp.arange(n_s)[None, :]
    jt = (q_pos // SEL_BLOCK)[:, None]
    adm = j <= jt
    forced = adm & ((j == 0) | (j > jt - N_LOCAL_SEL))
    score = jnp.where(forced[None, :, None, :], BIG, jnp.where(adm[None, :, None, :], imp, NEG))
    _, idx = lax.top_k(score, min(N_SEL, n_s))
    return idx


def _to_blocks(k, n_s):
    B, L, G, Dh = k.shape
    k = jnp.pad(k, ((0, 0), (0, n_s * SEL_BLOCK - L), (0, 0), (0, 0)))
    return k.reshape(B, n_s, SEL_BLOCK, G, Dh).transpose(0, 3, 1, 2, 4)


def _nsa_select_attend(q, q_pos, blk_idx, k_blk, v_blk):
    B, Tq, H, Dh = q.shape
    G = k_blk.shape[1]
    n = blk_idx.shape[-1]
    bi = jnp.arange(B)[:, None, None, None]
    gi = jnp.arange(G)[None, None, :, None]
    kg = k_blk[bi, gi, blk_idx].reshape(B, Tq, G, n * SEL_BLOCK, Dh)
    vg = v_blk[bi, gi, blk_idx].reshape(B, Tq, G, n * SEL_BLOCK, Dh)
    key_pos = (blk_idx[..., None] * SEL_BLOCK + jnp.arange(SEL_BLOCK)).reshape(B, Tq, G, n * SEL_BLOCK)
    mask = key_pos <= q_pos[None, :, None, None]
    qg = q.reshape(B, Tq, G, H // G, Dh)
    s = jnp.einsum('bqgrd,bqgmd->bqgrm', qg, kg) * HEAD_DIM ** -0.5
    p = _masked_softmax(s, mask[:, :, :, None, :])
    return jnp.einsum('bqgrm,bqgmd->bqgrd', p.astype(vg.dtype), vg).reshape(B, Tq, H, Dh)


def _gqa_attend(q, k, v, mask):
    B, N, Tq, H, Dh = q.shape
    G = k.shape[3]
    qg = q.reshape(B, N, Tq, G, H // G, Dh)
    s = jnp.einsum('bnqgrd,bnkgd->bnqgrk', qg, k) * HEAD_DIM ** -0.5
    p = _masked_softmax(s, mask[None, :, :, None, None, :])
    return jnp.einsum('bnqgrk,bnkgd->bnqgrd', p.astype(v.dtype), v).reshape(B, N, Tq, H, Dh)


def _window_attend(q, q_pos, k, v, k_pos):
    d = q_pos[:, :, None] - k_pos[:, None, :]
    mask = (d >= 0) & (d < WINDOW) & (k_pos[:, None, :] >= 0)
    return _gqa_attend(q, k, v, mask)


def _nsa_combine(gate, o_cmp, o_slc, o_win):
    g = gate[..., None]
    return g[:, :, 0] * o_cmp + g[:, :, 1] * o_slc + g[:, :, 2] * o_win


def _nsa_prompt(q, gate, nsa_rows, win_rows, pos, w_cmp1, w_cmp2, cmp_pos):
    B, T, H, Dh = q.shape
    nqb = T // Q_BLOCK
    o_cmp, p_cmp = _nsa_compressed(q, pos, nsa_rows[:, :, 0], nsa_rows[:, :, 1], w_cmp1, w_cmp2, cmp_pos)
    n_s = -(-T // SEL_BLOCK)
    idx = _nsa_block_choice(p_cmp, pos, n_s)
    k_blk = _to_blocks(nsa_rows[:, :, 2], n_s)
    v_blk = _to_blocks(nsa_rows[:, :, 3], n_s)
    o_slc = _unblockify(lax.map(lambda a: _nsa_select_attend(a[0], a[1], a[2], k_blk, v_blk),
                                (_blockify(q, nqb), pos.reshape(nqb, Q_BLOCK), _blockify(idx, nqb))))
    kw = jnp.pad(win_rows, ((0, 0), (WINDOW, 0), (0, 0), (0, 0), (0, 0)))
    kidx = (jnp.arange(nqb) * Q_BLOCK)[:, None] + jnp.arange(WINDOW + Q_BLOCK)[None, :]
    kwb = kw[:, kidx]
    o_win = _window_attend(q.reshape(B, nqb, Q_BLOCK, H, Dh), pos.reshape(nqb, Q_BLOCK),
                           kwb[:, :, :, 0], kwb[:, :, :, 1], kidx - WINDOW).reshape(B, T, H, Dh)
    return _nsa_combine(gate, o_cmp, o_slc, o_win)


def _nsa_sample(q, gate, nsa_rows, win_rows, pos, past_nsa, win_buf, past_len, w_cmp1, w_cmp2, cmp_pos):
    B, Tq, H, Dh = q.shape
    kv_all = jnp.concatenate([past_nsa, nsa_rows], axis=1)
    L = kv_all.shape[1]
    o_cmp, p_cmp = _nsa_compressed(q, pos, kv_all[:, :, 0], kv_all[:, :, 1], w_cmp1, w_cmp2, cmp_pos)
    n_s = -(-L // SEL_BLOCK)
    idx = _nsa_block_choice(p_cmp, pos, n_s)
    o_slc = _nsa_select_attend(q, pos, idx, _to_blocks(kv_all[:, :, 2], n_s), _to_blocks(kv_all[:, :, 3], n_s))
    W = win_buf.shape[1]
    wk = jnp.concatenate([win_buf, win_rows], axis=1)
    k_pos = past_len - W + jnp.arange(W + Tq, dtype=jnp.int32)
    o_win = _window_attend(q[:, None], pos[None], wk[:, None, :, 0], wk[:, None, :, 1], k_pos[None])[:, 0]
    return _nsa_combine(gate, o_cmp, o_slc, o_win)


def _dsa_attend(q, iq, iw, q_pos, k, v, ik, n_top):
    B = q.shape[0]
    L = k.shape[1]
    n_idx = iq.shape[2]
    logits = jnp.einsum('bqhd,bsd->bqhs', iq, ik).astype(jnp.float32) * IDX_DIM ** -0.5
    score = jnp.einsum('bqhs,bqh->bqs', jax.nn.relu(logits), iw.astype(jnp.float32)) * n_idx ** -0.5
    causal = jnp.arange(L)[None, :] <= q_pos[:, None]
    score = jnp.where(causal[None], score, NEG)
    _, idx = lax.top_k(score, n_top)
    valid = idx <= q_pos[None, :, None]
    bi = jnp.arange(B)[:, None, None]
    kg = k[bi, idx]
    vg = v[bi, idx]
    s = jnp.einsum('bqhd,bqkd->bqhk', q, kg) * HEAD_DIM ** -0.5
    p = _masked_softmax(s, valid[:, :, None, :])
    return jnp.einsum('bqhk,bqkd->bqhd', p.astype(vg.dtype), vg)


def _dsa_prompt(q, iq, iw, k, v, ik, pos):
    T = q.shape[1]
    nqb = T // Q_BLOCK
    n_top = min(DSA_TOPK, T // 4)
    o = lax.map(lambda a: _dsa_attend(a[0], a[1], a[2], a[3], k, v, ik, n_top),
                (_blockify(q, nqb), _blockify(iq, nqb), _blockify(iw, nqb), pos.reshape(nqb, Q_BLOCK)))
    return _unblockify(o)


def _swiglu_half_step(h, g, wg, wu, wd):
    xn = rmsnorm(h, g, jnp.bfloat16)
    a = ffn_gate_up(xn, wg, wu, tm=1408, tn=256)
    return resid_matmul(a, wd, h, 0.5, tm=768, tn=256)


def kernel(x_prompt, x_sample, cache_nsa_kv, cache_nsa_win, cache_dsa_kv, cache_dsa_idx, page_table,
           g_norm, w_ffn_gate, w_ffn_up, w_ffn_down, w_in, w_cmp1, w_cmp2, cmp_pos,
           w_br_a, w_br_b, w_out, g_final):
    B, T, D = x_prompt.shape
    DB, Ts, _ = x_sample.shape
    depth = g_norm.shape[0]
    page = cache_nsa_kv.shape[2]
    past_len = page_table.shape[1] * page
    n_heads_a = w_br_a.shape[1] // HEAD_DIM
    n_heads_b = w_br_b.shape[1] // HEAD_DIM
    n_idx = n_heads_b // 2
    G = N_KV_A
    lay = Layout(D, n_heads_a, n_heads_b, n_idx)
    Mp, Ms = B * T, DB * Ts

    pos_p = jnp.arange(T, dtype=jnp.int32)
    pos_s = past_len + jnp.arange(Ts, dtype=jnp.int32)
    pos_rows = jnp.concatenate([jnp.tile(pos_p, B), jnp.tile(pos_s, DB)])
    cos, sin = rope_tables(pos_rows)
    rope_flags = jnp.asarray(lay.rope_flags)

    h = jnp.concatenate([x_prompt.reshape(Mp, D), x_sample.reshape(Ms, D)], axis=0)
    outs = [[] for _ in range(8)]
    bf = jnp.bfloat16
    for l in range(depth):
        h = _swiglu_half_step(h, g_norm[l, 0], w_ffn_gate[l, 0].astype(bf), w_ffn_up[l, 0].astype(bf),
                              w_ffn_down[l, 0].astype(bf))

        u = rmsnorm(h, g_norm[l, 1], bf)
        proj = in_project(u, lay.pack(w_in[l]), rope_flags, cos, sin, tm=1408, tn=512)

        def split(rows, lead):
            sl = lambda a, n: rows[:, a:a + n]
            q_a = sl(lay.qa, n_heads_a * HEAD_DIM).reshape(*lead, n_heads_a, HEAD_DIM)
            nsa_rows = sl(lay.cmp, 4 * G * HEAD_DIM).reshape(*lead, 4, G, HEAD_DIM)
            win_rows = sl(lay.wink, 2 * G * HEAD_DIM).reshape(*lead, 2, G, HEAD_DIM)
            g_a = jax.nn.sigmoid(sl(lay.ga, 3 * n_heads_a)).reshape(*lead, 3, n_heads_a)
            q_b = sl(lay.qb, n_heads_b * HEAD_DIM).reshape(*lead, n_heads_b, HEAD_DIM)
            dsa_rows = jnp.stack([sl(lay.dk, HEAD_DIM), sl(lay.dv, HEAD_DIM)], axis=1).reshape(*lead, 2, HEAD_DIM)
            iq = sl(lay.iq, n_idx * IDX_DIM).reshape(*lead, n_idx, IDX_DIM)
            iw = sl(lay.iw, n_idx).reshape(*lead, n_idx)
            ik = sl(lay.ik, IDX_DIM).reshape(*lead, IDX_DIM)
            return q_a, nsa_rows, win_rows, g_a, q_b, dsa_rows, iq, iw, ik

        q_a, nsa_rows, win_rows, g_a, q_b, dsa_rows, iq, iw, ik = split(proj[:Mp], (B, T))
        o_a_p = _nsa_prompt(q_a, g_a, nsa_rows, win_rows, pos_p, w_cmp1[l], w_cmp2[l], cmp_pos[l])
        o_b_p = _dsa_prompt(q_b, iq, iw, dsa_rows[:, :, 0], dsa_rows[:, :, 1], ik, pos_p)
        outs[0].append(nsa_rows)
        outs[1].append(win_rows[:, T - min(WINDOW, T):])
        outs[2].append(dsa_rows)
        outs[3].append(ik)

        q_a, nsa_rows, win_rows, g_a, q_b, dsa_rows, iq, iw, ik = split(proj[Mp:], (DB, Ts))
        past_nsa = _gather_pages(cache_nsa_kv[l], page_table)
        o_a_s = _nsa_sample(q_a, g_a, nsa_rows, win_rows, pos_s, past_nsa, cache_nsa_win[l], past_len,
                            w_cmp1[l], w_cmp2[l], cmp_pos[l])
        kv_all = jnp.concatenate([_gather_pages(cache_dsa_kv[l], page_table), dsa_rows], axis=1)
        ik_all = jnp.concatenate([_gather_pages(cache_dsa_idx[l], page_table), ik], axis=1)
        n_top = min(DSA_TOPK, kv_all.shape[1] // 4)
        o_b_s = _dsa_attend(q_b, iq, iw, pos_s, kv_all[:, :, 0], kv_all[:, :, 1], ik_all, n_top)
        outs[4].append(nsa_rows)
        outs[5].append(jnp.concatenate([cache_nsa_win[l], win_rows], axis=1)[:, Ts:])
        outs[6].append(dsa_rows)
        outs[7].append(ik)

        o_a = jnp.concatenate([o_a_p.reshape(Mp, -1), o_a_s.reshape(Ms, -1)], axis=0).astype(bf)
        o_b = jnp.concatenate([o_b_p.reshape(Mp, -1), o_b_s.reshape(Ms, -1)], axis=0).astype(bf)
        m = merge_branches(o_a, o_b, w_br_a[l].astype(bf), w_br_b[l].astype(bf), proj, lay.mg, tm=768, tn=512)
        h = resid_matmul(m, w_out[l].astype(bf), h, 1.0, tm=768, tn=512)

        h = _swiglu_half_step(h, g_norm[l, 2], w_ffn_gate[l, 1].astype(bf), w_ffn_up[l, 1].astype(bf),
                              w_ffn_down[l, 1].astype(bf))

    y = rmsnorm(h, g_final, jnp.float32)
    return (y[:Mp].reshape(B, T, D), y[Mp:].reshape(DB, Ts, D), *(jnp.stack(o) for o in outs))
```

```python
import functools

import jax
import jax.numpy as jnp
import numpy as np
from jax import lax
from jax.experimental import pallas as pl
from jax.experimental.pallas import tpu as pltpu

HEAD_DIM = 128
N_KV_A = 2
IDX_DIM = 128
CMP_BLOCK = 32
CMP_STRIDE = 16
CMP_RATIO = CMP_BLOCK // CMP_STRIDE
SEL_BLOCK = 64
N_SEL = 16
N_LOCAL_SEL = 2
WINDOW = 512
DSA_TOPK = 256
ROPE_THETA = 10000.0
RMS_EPS = 1e-6
Q_BLOCK = 128
NEG = -1e30
BIG = 1e30
TINY = 1e-30

LANES = 128
VMEM_LIMIT = 56 * 1024 * 1024


def _params(*sem):
    return pltpu.CompilerParams(dimension_semantics=sem, vmem_limit_bytes=VMEM_LIMIT)


def _rmsnorm_kernel(x_ref, g_ref, o_ref):
    x = x_ref[...]
    y = x * lax.rsqrt(jnp.mean(x * x, axis=-1, keepdims=True) + RMS_EPS)
    o_ref[...] = (y * g_ref[...]).astype(o_ref.dtype)


def rmsnorm(x, g, out_dtype, tm=256):
    M, D = x.shape
    return pl.pallas_call(
        _rmsnorm_kernel,
        grid=(M // tm,),
        in_specs=[pl.BlockSpec((tm, D), lambda i: (i, 0)),
                  pl.BlockSpec((1, D), lambda i: (0, 0))],
        out_specs=pl.BlockSpec((tm, D), lambda i: (i, 0)),
        out_shape=jax.ShapeDtypeStruct((M, D), out_dtype),
        compiler_params=_params("parallel"),
    )(x, g.reshape(1, D))


def _gateup_kernel(x_ref, wg_ref, wu_ref, o_ref):
    x = x_ref[...]
    g = jnp.dot(x, wg_ref[...], preferred_element_type=jnp.float32)
    u = jnp.dot(x, wu_ref[...], preferred_element_type=jnp.float32)
    o_ref[...] = (g * jax.nn.sigmoid(g) * u).astype(o_ref.dtype)


def ffn_gate_up(xn, wg, wu, tm, tn):
    M, D = xn.shape
    F = wg.shape[1]
    return pl.pallas_call(
        _gateup_kernel,
        grid=(M // tm, F // tn),
        in_specs=[pl.BlockSpec((tm, D), lambda i, j: (i, 0)),
                  pl.BlockSpec((D, tn), lambda i, j: (0, j)),
                  pl.BlockSpec((D, tn), lambda i, j: (0, j))],
        out_specs=pl.BlockSpec((tm, tn), lambda i, j: (i, j)),
        out_shape=jax.ShapeDtypeStruct((M, F), jnp.bfloat16),
        compiler_params=_params("parallel", "parallel"),
    )(xn, wg, wu)


def _resid_matmul_kernel(a_ref, w_ref, r_ref, o_ref, *, scale):
    acc = jnp.dot(a_ref[...], w_ref[...], preferred_element_type=jnp.float32)
    o_ref[...] = r_ref[...] + scale * acc


def resid_matmul(a, w, resid, scale, tm, tn):
    M, K = a.shape
    N = w.shape[1]
    return pl.pallas_call(
        functools.partial(_resid_matmul_kernel, scale=scale),
        grid=(M // tm, N // tn),
        in_specs=[pl.BlockSpec((tm, K), lambda i, j: (i, 0)),
                  pl.BlockSpec((K, tn), lambda i, j: (0, j)),
                  pl.BlockSpec((tm, tn), lambda i, j: (i, j))],
        out_specs=pl.BlockSpec((tm, tn), lambda i, j: (i, j)),
        out_shape=jax.ShapeDtypeStruct((M, N), jnp.float32),
        compiler_params=_params("parallel", "parallel"),
    )(a, w, resid)


def _inproj_kernel(flags_ref, x_ref, w_ref, cos_ref, sin_ref, o_ref, *, n_chunks):
    j = pl.program_id(1)
    acc = jnp.dot(x_ref[...], w_ref[...], preferred_element_type=jnp.float32)
    for c in range(n_chunks):
        sl = slice(c * LANES, (c + 1) * LANES)
        y = acc[:, sl]
        flag = flags_ref[j * n_chunks + c]

        @pl.when(flag == 1)
        def _():
            o_ref[:, sl] = y * cos_ref[...] + pltpu.roll(y, HEAD_DIM // 2, axis=1) * sin_ref[...]

        @pl.when(flag == 0)
        def _():
            o_ref[:, sl] = y


def in_project(u, w, rope_flags, cos, sin, tm, tn):
    M, D = u.shape
    N = w.shape[1]
    n_chunks = tn // LANES
    grid_spec = pltpu.PrefetchScalarGridSpec(
        num_scalar_prefetch=1,
        grid=(M // tm, N // tn),
        in_specs=[pl.BlockSpec((tm, D), lambda i, j, f: (i, 0)),
                  pl.BlockSpec((D, tn), lambda i, j, f: (0, j)),
                  pl.BlockSpec((tm, LANES), lambda i, j, f: (i, 0)),
                  pl.BlockSpec((tm, LANES), lambda i, j, f: (i, 0))],
        out_specs=pl.BlockSpec((tm, tn), lambda i, j, f: (i, j)),
    )
    return pl.pallas_call(
        functools.partial(_inproj_kernel, n_chunks=n_chunks),
        grid_spec=grid_spec,
        out_shape=jax.ShapeDtypeStruct((M, N), jnp.float32),
        compiler_params=_params("parallel", "parallel"),
    )(rope_flags, u, w, cos, sin)


def _merge_kernel(oa_ref, ob_ref, wa_ref, wb_ref, ga_ref, gb_ref, o_ref):
    ya = jnp.dot(oa_ref[...], wa_ref[...], preferred_element_type=jnp.float32)
    yb = jnp.dot(ob_ref[...], wb_ref[...], preferred_element_type=jnp.float32)
    m = jax.nn.sigmoid(ga_ref[...]) * ya + jax.nn.sigmoid(gb_ref[...]) * yb
    o_ref[...] = m.astype(o_ref.dtype)


def merge_branches(o_a, o_b, w_a, w_b, proj, mg_col, tm, tn):
    M, K = o_a.shape
    N = w_a.shape[1]
    ja = mg_col // tn
    jb = (mg_col + N) // tn
    return pl.pallas_call(
        _merge_kernel,
        grid=(M // tm, N // tn),
        in_specs=[pl.BlockSpec((tm, K), lambda i, j: (i, 0)),
                  pl.BlockSpec((tm, K), lambda i, j: (i, 0)),
                  pl.BlockSpec((K, tn), lambda i, j: (0, j)),
                  pl.BlockSpec((K, tn), lambda i, j: (0, j)),
                  pl.BlockSpec((tm, tn), lambda i, j: (i, ja + j)),
                  pl.BlockSpec((tm, tn), lambda i, j: (i, jb + j))],
        out_specs=pl.BlockSpec((tm, tn), lambda i, j: (i, j)),
        out_shape=jax.ShapeDtypeStruct((M, N), jnp.bfloat16),
        compiler_params=_params("parallel", "parallel"),
    )(o_a, o_b, w_a, w_b, proj, proj)


_NT = (((1,), (1,)), ((), ()))


def _stack_heads(x, n_heads):
    return jnp.concatenate([x[:, h * HEAD_DIM:(h + 1) * HEAD_DIM] for h in range(n_heads)], axis=0)


def _flash(qs, k_ref, v_ref, kt_lo, kt_hi, tk, n_rep, mask_fn):
    rows = qs.shape[0]
    tq = rows // n_rep
    scale = HEAD_DIM ** -0.5

    def body(kt, carry):
        m, l, acc = carry
        start = pl.multiple_of(kt * tk, tk)
        k = k_ref[pl.ds(start, tk), :].astype(jnp.bfloat16)
        v = v_ref[pl.ds(start, tk), :].astype(jnp.bfloat16)
        s = lax.dot_general(qs, k, _NT, preferred_element_type=jnp.float32).reshape(n_rep, tq, tk)
        mask = mask_fn(kt)[None]
        s = jnp.where(mask, s, NEG)
        m_new = jnp.maximum(m, jnp.max(s, axis=-1, keepdims=True))
        p = jnp.where(mask, jnp.exp((s - m_new) * scale), 0.0)
        alpha = jnp.exp((m - m_new) * scale)
        l = alpha * l + jnp.sum(p, axis=-1, keepdims=True)
        pv = jnp.dot(p.reshape(rows, tk).astype(jnp.bfloat16), v, preferred_element_type=jnp.float32)
        acc = alpha * acc + pv.reshape(n_rep, tq, HEAD_DIM)
        return m_new, l, acc

    init = (jnp.full((n_rep, tq, 1), NEG, jnp.float32), jnp.zeros((n_rep, tq, 1), jnp.float32),
            jnp.zeros((n_rep, tq, HEAD_DIM), jnp.float32))
    _, l, acc = lax.fori_loop(kt_lo, kt_hi, body, init)
    return acc * (1.0 / jnp.maximum(l, TINY))


def _compress_kernel(x_ref, w1_ref, w2_ref, pe_ref, cos_ref, sin_ref, o_ref):
    n_ch = o_ref.shape[-2]
    hidden = w1_ref.shape[-1]
    h0 = jnp.zeros((n_ch, hidden), jnp.float32)
    h1 = jnp.zeros((n_ch, hidden), jnp.float32)
    for s in range(CMP_STRIDE):
        a = x_ref[pl.ds(s, n_ch, stride=CMP_STRIDE), :]
        a0 = (a + pe_ref[0, s:s + 1, :]).astype(jnp.bfloat16)
        a1 = (a + pe_ref[0, CMP_STRIDE + s:CMP_STRIDE + s + 1, :]).astype(jnp.bfloat16)
        h0 = h0 + jnp.dot(a0, w1_ref[0, s], preferred_element_type=jnp.float32)
        h1 = h1 + jnp.dot(a1, w1_ref[0, CMP_STRIDE + s], preferred_element_type=jnp.float32)
    h = h0 + pltpu.roll(h1, n_ch - 1, axis=0)
    y = jnp.dot((h * jax.nn.sigmoid(h)).astype(jnp.bfloat16), w2_ref[0], preferred_element_type=jnp.float32)

    @pl.when(pl.program_id(1) == 0)
    def _():
        yr = y * cos_ref[...] + pltpu.roll(y, HEAD_DIM // 2, axis=1) * sin_ref[...]
        o_ref[0, 0, 0] = yr.astype(o_ref.dtype)

    @pl.when(pl.program_id(1) != 0)
    def _():
        o_ref[0, 0, 0] = y.astype(o_ref.dtype)


def compress_prompt(proj, n_batch, seq, cmp_col, w1, w2, pe, cos_end, sin_end):
    n_ch = seq // CMP_STRIDE
    col0 = cmp_col // HEAD_DIM
    return pl.pallas_call(
        _compress_kernel,
        grid=(n_batch, 2, N_KV_A),
        in_specs=[pl.BlockSpec((seq, HEAD_DIM), lambda b, kv, g: (b, col0 + kv * N_KV_A + g)),
                  pl.BlockSpec((1, CMP_BLOCK, HEAD_DIM, w1.shape[-1]), lambda b, kv, g: (kv, 0, 0, 0)),
                  pl.BlockSpec((1, w2.shape[1], HEAD_DIM), lambda b, kv, g: (kv, 0, 0)),
                  pl.BlockSpec((1, CMP_BLOCK, HEAD_DIM), lambda b, kv, g: (kv, 0, 0)),
                  pl.BlockSpec((n_ch, HEAD_DIM), lambda b, kv, g: (0, 0)),
                  pl.BlockSpec((n_ch, HEAD_DIM), lambda b, kv, g: (0, 0))],
        out_specs=pl.BlockSpec((1, 1, 1, n_ch, HEAD_DIM), lambda b, kv, g: (b, kv, g, 0, 0)),
        out_shape=jax.ShapeDtypeStruct((n_batch, 2, N_KV_A, n_ch, HEAD_DIM), jnp.bfloat16),
        compiler_params=_params("parallel", "parallel", "parallel"),
    )(proj, w1, w2, pe, cos_end, sin_end)


def _cmp_attend_and_choose(qs, kc, vc, cover, t_col, n_heads, n_s):
    rows = qs.shape[0]
    tq = rows // n_heads
    n_ch = kc.shape[0]
    scale = HEAD_DIM ** -0.5
    s = lax.dot_general(qs, kc, _NT, preferred_element_type=jnp.float32).reshape(n_heads, tq, n_ch)
    end = lax.broadcasted_iota(jnp.int32, (tq, n_ch), 1) * CMP_STRIDE + (CMP_BLOCK - 1)
    cmask = (end <= t_col)[None]
    s = jnp.where(cmask, s, NEG)
    m = jnp.max(s, axis=-1, keepdims=True)
    p = jnp.where(cmask, jnp.exp((s - m) * scale), 0.0)
    p = p * (1.0 / jnp.maximum(jnp.sum(p, axis=-1, keepdims=True), TINY))
    o_cmp = jnp.dot(p.reshape(rows, n_ch).astype(jnp.bfloat16), vc, preferred_element_type=jnp.float32)
    imp = jnp.dot(jnp.sum(p, axis=0).astype(jnp.bfloat16), cover, preferred_element_type=jnp.float32)

    lane = lax.broadcasted_iota(jnp.int32, (tq, LANES), 1)
    lane_f = lane.astype(jnp.float32)
    jt = lax.shift_right_arithmetic(t_col, jnp.int32(SEL_BLOCK.bit_length() - 1))
    adm = lane <= jt
    forced = adm & ((lane == 0) | (lane > jt - N_LOCAL_SEL))
    work = jnp.where(forced, BIG, jnp.where(adm, imp, NEG))
    work = jnp.where(lane < n_s, work, -jnp.inf)
    sel = jnp.zeros((tq, LANES), jnp.bool_)
    for _ in range(min(N_SEL, n_s)):
        mx = jnp.max(work, axis=-1, keepdims=True)
        first = jnp.min(jnp.where(work == mx, lane_f, float(LANES)), axis=-1, keepdims=True)
        pick = lane_f == first
        sel = sel | pick
        work = jnp.where(pick, -jnp.inf, work)
    return o_cmp.reshape(n_heads, tq, HEAD_DIM), sel


def _nsa_prompt_kernel(q_ref, kc_ref, vc_ref, selk_ref, selv_ref, wink_ref, winv_ref, gate_ref,
                       cover_ref, expand_ref, o_ref, selexp_ref, *, n_heads, n_s):
    tq = q_ref.shape[0]
    grp = pl.program_id(1)
    qi = pl.program_id(2)
    t_col = qi * tq + lax.broadcasted_iota(jnp.int32, (tq, 1), 0)
    key_iota = lax.broadcasted_iota(jnp.int32, (tq, tq), 1)
    qs = _stack_heads(q_ref[...], n_heads).astype(jnp.bfloat16)

    o_cmp, sel = _cmp_attend_and_choose(qs, kc_ref[0, 0, 0], vc_ref[0, 0, 0], cover_ref[...], t_col,
                                        n_heads, n_s)
    selexp_ref[...] = jnp.dot(jnp.where(sel, 1.0, 0.0).astype(jnp.bfloat16), expand_ref[...],
                              preferred_element_type=jnp.float32)

    def sel_mask(kt):
        chosen = selexp_ref[:, pl.ds(pl.multiple_of(kt * tq, tq), tq)] > 0.5
        return chosen & (kt * tq + key_iota <= t_col)

    o_slc = _flash(qs, selk_ref, selv_ref, 0, qi + 1, tq, n_heads, sel_mask)

    def win_mask(kt):
        d = t_col - (kt * tq + key_iota)
        return (d >= 0) & (d < WINDOW)

    o_win = _flash(qs, wink_ref, winv_ref, jnp.maximum(qi - WINDOW // tq, 0), qi + 1, tq, n_heads, win_mask)

    gates = jax.nn.sigmoid(gate_ref[...])
    gates = jnp.where(grp == 0, gates, pltpu.roll(gates, LANES - n_heads, axis=1))
    n_all = N_KV_A * n_heads
    for r in range(n_heads):
        o = (gates[:, r:r + 1] * o_cmp[r] + gates[:, n_all + r:n_all + r + 1] * o_slc[r]
             + gates[:, 2 * n_all + r:2 * n_all + r + 1] * o_win[r])
        o_ref[:, r * HEAD_DIM:(r + 1) * HEAD_DIM] = o.astype(o_ref.dtype)


def nsa_prompt(proj, kvc, lay, n_batch, seq, tq):
    n_heads = lay.n_heads_a // N_KV_A
    n_ch = kvc.shape[-2]
    n_s = -(-seq // SEL_BLOCK)
    assert n_s <= LANES and seq % tq == 0 and tq % SEL_BLOCK == 0 and WINDOW % tq == 0
    nq = seq // tq
    qw = n_heads * HEAD_DIM
    ci = np.arange(n_ch)[:, None] * CMP_STRIDE
    sj = np.arange(LANES)[None, :] * SEL_BLOCK
    cover = jnp.asarray((ci < sj + SEL_BLOCK) & (ci + CMP_BLOCK > sj) & (sj < n_s * SEL_BLOCK), jnp.bfloat16)
    expand = jnp.asarray(np.arange(LANES)[:, None] == np.arange(seq)[None, :] // SEL_BLOCK, jnp.bfloat16)

    def slab(col):
        c0 = col // HEAD_DIM
        return pl.BlockSpec((seq, HEAD_DIM), lambda b, g, i: (b, c0 + g))

    return pl.pallas_call(
        functools.partial(_nsa_prompt_kernel, n_heads=n_heads, n_s=n_s),
        grid=(n_batch, N_KV_A, nq),
        in_specs=[pl.BlockSpec((tq, qw), lambda b, g, i: (b * nq + i, lay.qa // qw + g)),
                  pl.BlockSpec((1, 1, 1, n_ch, HEAD_DIM), lambda b, g, i: (b, 0, g, 0, 0)),
                  pl.BlockSpec((1, 1, 1, n_ch, HEAD_DIM), lambda b, g, i: (b, 1, g, 0, 0)),
                  slab(lay.selk), slab(lay.selv), slab(lay.wink), slab(lay.winv),
                  pl.BlockSpec((tq, LANES), lambda b, g, i: (b * nq + i, lay.ga // LANES)),
                  pl.BlockSpec((n_ch, LANES), lambda b, g, i: (0, 0)),
                  pl.BlockSpec((LANES, seq), lambda b, g, i: (0, 0))],
        out_specs=pl.BlockSpec((tq, qw), lambda b, g, i: (b * nq + i, g)),
        out_shape=jax.ShapeDtypeStruct((n_batch * seq, N_KV_A * qw), jnp.bfloat16),
        scratch_shapes=[pltpu.VMEM((tq, seq), jnp.float32)],
        compiler_params=_params("parallel", "parallel", "arbitrary"),
    )(proj, kvc, kvc, proj, proj, proj, proj, proj, cover, expand)


def _topk_mask(score_ref, key_ref, mask_ref, tri_ref, n_tiles, n_top):
    tq = score_ref.shape[0]
    int_min = jnp.int32(-2 ** 31)

    def to_key(t, _):
        sl = pl.ds(pl.multiple_of(t * LANES, LANES), LANES)
        bits = lax.bitcast_convert_type(score_ref[:, sl], jnp.int32)
        key_ref[:, sl] = bits ^ ((bits >> 31) & jnp.int32(0x7FFFFFFF))
        return 0

    lax.fori_loop(0, n_tiles, to_key, 0)

    def count_ge(cand):
        def body(t, acc):
            sl = pl.ds(pl.multiple_of(t * LANES, LANES), LANES)
            return acc + jnp.where(key_ref[:, sl] >= cand, 1.0, 0.0)
        acc = lax.fori_loop(0, n_tiles, body, jnp.zeros((tq, LANES), jnp.float32))
        return jnp.sum(acc, axis=-1, keepdims=True)

    tau = jnp.zeros((tq, 1), jnp.int32)
    for bit in range(31, -1, -1):
        cand = tau | jnp.int32(-2 ** 31 if bit == 31 else 1 << bit)
        tau = jnp.where(count_ge(cand ^ int_min) >= float(n_top), cand, tau)
    thr = tau ^ int_min

    def count_gt(t, acc):
        sl = pl.ds(pl.multiple_of(t * LANES, LANES), LANES)
        return acc + jnp.where(key_ref[:, sl] > thr, 1.0, 0.0)

    n_gt = jnp.sum(lax.fori_loop(0, n_tiles, count_gt, jnp.zeros((tq, LANES), jnp.float32)),
                   axis=-1, keepdims=True)
    need = float(n_top) - n_gt
    ones = jnp.ones((LANES, LANES), jnp.bfloat16)

    def cut(t, eq_before):
        sl = pl.ds(pl.multiple_of(t * LANES, LANES), LANES)
        key = key_ref[:, sl]
        eq = jnp.where(key == thr, 1.0, 0.0)
        rank = (jnp.dot(eq.astype(jnp.bfloat16), tri_ref[...], preferred_element_type=jnp.float32)
                + jnp.dot(eq_before.astype(jnp.bfloat16), ones, preferred_element_type=jnp.float32))
        keep = (key > thr) | ((key == thr) & (rank <= need))
        mask_ref[:, sl] = jnp.where(keep, 1.0, 0.0)
        return eq_before + eq

    lax.fori_loop(0, n_tiles, cut, jnp.zeros((tq, LANES), jnp.float32))


def _dsa_prompt_kernel(iq_ref, iw_ref, q_ref, ik_ref, k_ref, v_ref, tri_ref, o_ref,
                       score_ref, key_ref, mask_ref, *, n_idx, n_heads, n_top, iw_lane):
    tq = iq_ref.shape[0]
    qi = pl.program_id(1)
    n_tiles = qi + 1
    t_col = qi * tq + lax.broadcasted_iota(jnp.int32, (tq, 1), 0)
    key_iota = lax.broadcasted_iota(jnp.int32, (tq, tq), 1)
    iqs = _stack_heads(iq_ref[...], n_idx).astype(jnp.bfloat16)
    iw = iw_ref[...]

    def score_tile(kt, _):
        sl = pl.ds(pl.multiple_of(kt * tq, tq), tq)
        ik = ik_ref[sl, :].astype(jnp.bfloat16)
        logits = lax.dot_general(iqs, ik, _NT, preferred_element_type=jnp.float32).reshape(n_idx, tq, tq)
        logits = jnp.maximum(logits * IDX_DIM ** -0.5, 0.0)
        sc = jnp.zeros((tq, tq), jnp.float32)
        for h in range(n_idx):
            sc = sc + logits[h] * iw[:, iw_lane + h:iw_lane + h + 1]
        sc = sc * n_idx ** -0.5
        score_ref[:, sl] = jnp.where(kt * tq + key_iota <= t_col, sc, NEG)
        return 0

    lax.fori_loop(0, n_tiles, score_tile, 0)
    _topk_mask(score_ref, key_ref, mask_ref, tri_ref, n_tiles, n_top)

    qs = _stack_heads(q_ref[...], n_heads).astype(jnp.bfloat16)

    def dsa_mask(kt):
        chosen = mask_ref[:, pl.ds(pl.multiple_of(kt * tq, tq), tq)] > 0.5
        return chosen & (kt * tq + key_iota <= t_col)

    o = _flash(qs, k_ref, v_ref, 0, n_tiles, tq, n_heads, dsa_mask)
    for h in range(n_heads):
        o_ref[:, h * HEAD_DIM:(h + 1) * HEAD_DIM] = o[h].astype(o_ref.dtype)


def dsa_prompt(proj, lay, n_batch, seq, tq):
    assert tq == LANES and seq % tq == 0
    nq = seq // tq
    n_top = min(DSA_TOPK, seq // 4)
    iqw = lay.n_idx_heads * IDX_DIM
    qw = lay.n_heads_b * HEAD_DIM
    tri = jnp.asarray(np.arange(LANES)[:, None] <= np.arange(LANES)[None, :], jnp.bfloat16)

    def slab(col):
        c0 = col // HEAD_DIM
        return pl.BlockSpec((seq, HEAD_DIM), lambda b, i: (b, c0))

    return pl.pallas_call(
        functools.partial(_dsa_prompt_kernel, n_idx=lay.n_idx_heads, n_heads=lay.n_heads_b, n_top=n_top,
                          iw_lane=lay.iw % LANES),
        grid=(n_batch, nq),
        in_specs=[pl.BlockSpec((tq, iqw), lambda b, i: (b * nq + i, lay.iq // iqw)),
                  pl.BlockSpec((tq, LANES), lambda b, i: (b * nq + i, lay.iw // LANES)),
                  pl.BlockSpec((tq, qw), lambda b, i: (b * nq + i, lay.qb // qw)),
                  slab(lay.ik), slab(lay.dk), slab(lay.dv),
                  pl.BlockSpec((LANES, LANES), lambda b, i: (0, 0))],
        out_specs=pl.BlockSpec((tq, qw), lambda b, i: (b * nq + i, 0)),
        out_shape=jax.ShapeDtypeStruct((n_batch * seq, qw), jnp.bfloat16),
        scratch_shapes=[pltpu.VMEM((tq, seq), jnp.float32), pltpu.VMEM((tq, seq), jnp.int32),
                        pltpu.VMEM((tq, seq), jnp.float32)],
        compiler_params=_params("parallel", "arbitrary"),
    )(proj, proj, proj, proj, proj, proj, tri)


class Layout:
    def __init__(self, d_model, n_heads_a, n_heads_b, n_idx_heads):
        self.n_heads_a, self.n_heads_b, self.n_idx_heads = n_heads_a, n_heads_b, n_idx_heads
        src = np.cumsum([0, n_heads_a * HEAD_DIM, 6 * N_KV_A * HEAD_DIM, 3 * n_heads_a,
                         n_heads_b * HEAD_DIM, 2 * HEAD_DIM, n_idx_heads * IDX_DIM, n_idx_heads,
                         IDX_DIM, 2 * d_model])
        s_qa, s_kva, s_ga, s_qb, s_kvb, s_iq, s_iw, s_ik, s_mg, s_end = (int(v) for v in src)
        small = 3 * n_heads_a + n_idx_heads
        assert small <= LANES
        self.small_pad = LANES - small
        kvw = N_KV_A * HEAD_DIM
        self.pieces = [
            (s_qa, s_kva, True),
            (s_qb, s_kvb, True),
            (s_iq, s_iw, True),
            (s_kva, s_kva + 2 * kvw, False),
            (s_kva + 2 * kvw, s_kva + 3 * kvw, True),
            (s_kva + 3 * kvw, s_kva + 4 * kvw, False),
            (s_kva + 4 * kvw, s_kva + 5 * kvw, True),
            (s_kva + 5 * kvw, s_ga, False),
            (s_kvb, s_kvb + HEAD_DIM, True),
            (s_ik, s_mg, True),
            (s_kvb + HEAD_DIM, s_iq, False),
            (s_ga, s_qb, False),
            (s_iw, s_ik, False),
            None,
            (s_mg, s_end, False),
        ]
        off = 0
        starts = []
        for p in self.pieces:
            starts.append(off)
            off += self.small_pad if p is None else p[1] - p[0]
        (self.qa, self.qb, self.iq, self.cmp, self.selk, self.selv, self.wink, self.winv, self.dk,
         self.ik, self.dv, self.ga, self.iw, _, self.mg) = starts
        self.width = off
        assert self.width % LANES == 0
        flags = np.zeros(self.width // LANES, np.int32)
        for st, p in zip(starts, self.pieces):
            if p is not None and p[2]:
                assert st % LANES == 0 and (p[1] - p[0]) % LANES == 0
                flags[st // LANES:(st + p[1] - p[0]) // LANES] = 1
        self.rope_flags = flags

    def pack(self, w_in):
        cols = []
        for p in self.pieces:
            if p is None:
                cols.append(jnp.zeros((w_in.shape[0], self.small_pad), w_in.dtype))
            else:
                cols.append(w_in[:, p[0]:p[1]])
        return jnp.concatenate(cols, axis=1).astype(jnp.bfloat16)


def rope_tables(pos):
    half = HEAD_DIM // 2
    inv = ROPE_THETA ** (-jnp.arange(half, dtype=jnp.float32) / half)
    ang = pos.astype(jnp.float32)[:, None] * inv[None, :]
    cos, sin = jnp.cos(ang), jnp.sin(ang)
    return jnp.concatenate([cos, cos], axis=1), jnp.concatenate([-sin, sin], axis=1)


def _rope(x, pos):
    half = x.shape[-1] // 2
    inv = ROPE_THETA ** (-jnp.arange(half, dtype=jnp.float32) / half)
    ang = pos.astype(jnp.float32)[:, None] * inv[None, :]
    shape = (ang.shape[0],) + (1,) * (x.ndim - 3) + (half,)
    cos = jnp.cos(ang).reshape(shape)
    sin = jnp.sin(ang).reshape(shape)
    xf = x.astype(jnp.float32)
    x1, x2 = xf[..., :half], xf[..., half:]
    return jnp.concatenate([x1 * cos - x2 * sin, x2 * cos + x1 * sin], axis=-1).astype(x.dtype)


def _masked_softmax(s, mask):
    s = jnp.where(mask, s.astype(jnp.float32), NEG)
    m = jnp.max(s, axis=-1, keepdims=True)
    p = jnp.where(mask, jnp.exp(s - m), 0.0)
    return p / jnp.maximum(jnp.sum(p, axis=-1, keepdims=True), TINY)


def _blockify(x, nb):
    return jnp.swapaxes(x.reshape(x.shape[0], nb, x.shape[1] // nb, *x.shape[2:]), 0, 1)


def _unblockify(x):
    x = jnp.swapaxes(x, 0, 1)
    return x.reshape(x.shape[0], -1, *x.shape[3:])


def _gather_pages(pool, page_table):
    g = pool[page_table]
    return g.reshape(g.shape[0], -1, *pool.shape[2:])


def _compress(kv, w1, w2, pos_emb):
    B, L, G, Dh = kv.shape
    n_c = (L - CMP_BLOCK) // CMP_STRIDE + 1
    n_ch = n_c + CMP_RATIO - 1
    ch = kv[:, :n_ch * CMP_STRIDE].reshape(B, n_ch, CMP_STRIDE, G, Dh)
    w1r = w1.reshape(CMP_RATIO, CMP_STRIDE, Dh, -1)
    per = pos_emb.reshape(CMP_RATIO, CMP_STRIDE, 1, Dh)
    h = sum(jnp.einsum('bcsgd,sdf->bcgf', ch[:, r:r + n_c] + per[r], w1r[r]) for r in range(CMP_RATIO))
    return jnp.einsum('bcgf,fd->bcgd', jax.nn.silu(h), w2)


def _nsa_compressed(q, q_pos, k_raw, v_raw, w_cmp1, w_cmp2, cmp_pos):
    k_c = _compress(k_raw, w_cmp1[0], w_cmp2[0], cmp_pos[0])
    v_c = _compress(v_raw, w_cmp1[1], w_cmp2[1], cmp_pos[1])
    n_c = k_c.shape[1]
    end = jnp.arange(n_c, dtype=jnp.int32) * CMP_STRIDE + CMP_BLOCK - 1
    k_c = _rope(k_c, end)
    mask = end[None, :] <= q_pos[:, None]
    B, T, H, Dh = q.shape
    G = k_c.shape[2]
    qg = q.reshape(B, T, G, H // G, Dh)
    s = jnp.einsum('btgrd,bcgd->btgrc', qg, k_c) * HEAD_DIM ** -0.5
    p = _masked_softmax(s, mask[None, :, None, None, :])
    o = jnp.einsum('btgrc,bcgd->btgrd', p.astype(v_c.dtype), v_c).reshape(B, T, H, Dh)
    return o, p


def _nsa_block_choice(p_cmp, q_pos, n_s):
    n_c = p_cmp.shape[-1]
    ci = jnp.arange(n_c)[:, None] * CMP_STRIDE
    sj = jnp.arange(n_s)[None, :] * SEL_BLOCK
    cover = ((ci < sj + SEL_BLOCK) & (ci + CMP_BLOCK > sj)).astype(jnp.float32)
    imp = jnp.einsum('btgrc,cj->btgj', p_cmp, cover)
    j = jnp.arange(n_s)[None, :]
    jt = (q_pos // SEL_BLOCK)[:, None]
    adm = j <= jt
    forced = adm & ((j == 0) | (j > jt - N_LOCAL_SEL))
    score = jnp.where(forced[None, :, None, :], BIG, jnp.where(adm[None, :, None, :], imp, NEG))
    _, idx = lax.top_k(score, min(N_SEL, n_s))
    return idx


def _to_blocks(k, n_s):
    B, L, G, Dh = k.shape
    k = jnp.pad(k, ((0, 0), (0, n_s * SEL_BLOCK - L), (0, 0), (0, 0)))
    return k.reshape(B, n_s, SEL_BLOCK, G, Dh).transpose(0, 3, 1, 2, 4)


def _nsa_select_attend(q, q_pos, blk_idx, k_blk, v_blk):
    B, Tq, H, Dh = q.shape
    G = k_blk.shape[1]
    n = blk_idx.shape[-1]
    bi = jnp.arange(B)[:, None, None, None]
    gi = jnp.arange(G)[None, None, :, None]
    kg = k_blk[bi, gi, blk_idx].reshape(B, Tq, G, n * SEL_BLOCK, Dh)
    vg = v_blk[bi, gi, blk_idx].reshape(B, Tq, G, n * SEL_BLOCK, Dh)
    key_pos = (blk_idx[..., None] * SEL_BLOCK + jnp.arange(SEL_BLOCK)).reshape(B, Tq, G, n * SEL_BLOCK)
    mask = key_pos <= q_pos[None, :, None, None]
    qg = q.reshape(B, Tq, G, H // G, Dh)
    s = jnp.einsum('bqgrd,bqgmd->bqgrm', qg, kg) * HEAD_DIM ** -0.5
    p = _masked_softmax(s, mask[:, :, :, None, :])
    return jnp.einsum('bqgrm,bqgmd->bqgrd', p.astype(vg.dtype), vg).reshape(B, Tq, H, Dh)


def _gqa_attend(q, k, v, mask):
    B, N, Tq, H, Dh = q.shape
    G = k.shape[3]
    qg = q.reshape(B, N, Tq, G, H // G, Dh)
    s = jnp.einsum('bnqgrd,bnkgd->bnqgrk', qg, k) * HEAD_DIM ** -0.5
    p = _masked_softmax(s, mask[None, :, :, None, None, :])
    return jnp.einsum('bnqgrk,bnkgd->bnqgrd', p.astype(v.dtype), v).reshape(B, N, Tq, H, Dh)


def _window_attend(q, q_pos, k, v, k_pos):
    d = q_pos[:, :, None] - k_pos[:, None, :]
    mask = (d >= 0) & (d < WINDOW) & (k_pos[:, None, :] >= 0)
    return _gqa_attend(q, k, v, mask)


def _nsa_combine(gate, o_cmp, o_slc, o_win):
    g = gate[..., None]
    return g[:, :, 0] * o_cmp + g[:, :, 1] * o_slc + g[:, :, 2] * o_win


def _nsa_prompt(q, gate, nsa_rows, win_rows, pos, w_cmp1, w_cmp2, cmp_pos):
    B, T, H, Dh = q.shape
    nqb = T // Q_BLOCK
    o_cmp, p_cmp = _nsa_compressed(q, pos, nsa_rows[:, :, 0], nsa_rows[:, :, 1], w_cmp1, w_cmp2, cmp_pos)
    n_s = -(-T // SEL_BLOCK)
    idx = _nsa_block_choice(p_cmp, pos, n_s)
    k_blk = _to_blocks(nsa_rows[:, :, 2], n_s)
    v_blk = _to_blocks(nsa_rows[:, :, 3], n_s)
    o_slc = _unblockify(lax.map(lambda a: _nsa_select_attend(a[0], a[1], a[2], k_blk, v_blk),
                                (_blockify(q, nqb), pos.reshape(nqb, Q_BLOCK), _blockify(idx, nqb))))
    kw = jnp.pad(win_rows, ((0, 0), (WINDOW, 0), (0, 0), (0, 0), (0, 0)))
    kidx = (jnp.arange(nqb) * Q_BLOCK)[:, None] + jnp.arange(WINDOW + Q_BLOCK)[None, :]
    kwb = kw[:, kidx]
    o_win = _window_attend(q.reshape(B, nqb, Q_BLOCK, H, Dh), pos.reshape(nqb, Q_BLOCK),
                           kwb[:, :, :, 0], kwb[:, :, :, 1], kidx - WINDOW).reshape(B, T, H, Dh)
    return _nsa_combine(gate, o_cmp, o_slc, o_win)


def _nsa_sample(q, gate, nsa_rows, win_rows, pos, past_nsa, win_buf, past_len, w_cmp1, w_cmp2, cmp_pos):
    B, Tq, H, Dh = q.shape
    kv_all = jnp.concatenate([past_nsa, nsa_rows], axis=1)
    L = kv_all.shape[1]
    o_cmp, p_cmp = _nsa_compressed(q, pos, kv_all[:, :, 0], kv_all[:, :, 1], w_cmp1, w_cmp2, cmp_pos)
    n_s = -(-L // SEL_BLOCK)
    idx = _nsa_block_choice(p_cmp, pos, n_s)
    o_slc = _nsa_select_attend(q, pos, idx, _to_blocks(kv_all[:, :, 2], n_s), _to_blocks(kv_all[:, :, 3], n_s))
    W = win_buf.shape[1]
    wk = jnp.concatenate([win_buf, win_rows], axis=1)
    k_pos = past_len - W + jnp.arange(W + Tq, dtype=jnp.int32)
    o_win = _window_attend(q[:, None], pos[None], wk[:, None, :, 0], wk[:, None, :, 1], k_pos[None])[:, 0]
    return _nsa_combine(gate, o_cmp, o_slc, o_win)


def _dsa_attend(q, iq, iw, q_pos, k, v, ik, n_top):
    B = q.shape[0]
    L = k.shape[1]
    n_idx = iq.shape[2]
    logits = jnp.einsum('bqhd,bsd->bqhs', iq, ik).astype(jnp.float32) * IDX_DIM ** -0.5
    score = jnp.einsum('bqhs,bqh->bqs', jax.nn.relu(logits), iw.astype(jnp.float32)) * n_idx ** -0.5
    causal = jnp.arange(L)[None, :] <= q_pos[:, None]
    score = jnp.where(causal[None], score, NEG)
    _, idx = lax.top_k(score, n_top)
    valid = idx <= q_pos[None, :, None]
    bi = jnp.arange(B)[:, None, None]
    kg = k[bi, idx]
    vg = v[bi, idx]
    s = jnp.einsum('bqhd,bqkd->bqhk', q, kg) * HEAD_DIM ** -0.5
    p = _masked_softmax(s, valid[:, :, None, :])
    return jnp.einsum('bqhk,bqkd->bqhd', p.astype(vg.dtype), vg)


def _dsa_prompt(q, iq, iw, k, v, ik, pos):
    T = q.shape[1]
    nqb = T // Q_BLOCK
    n_top = min(DSA_TOPK, T // 4)
    o = lax.map(lambda a: _dsa_attend(a[0], a[1], a[2], a[3], k, v, ik, n_top),
                (_blockify(q, nqb), _blockify(iq, nqb), _blockify(iw, nqb), pos.reshape(nqb, Q_BLOCK)))
    return _unblockify(o)


def _swiglu_half_step(h, g, wg, wu, wd):
    xn = rmsnorm(h, g, jnp.bfloat16)
    a = ffn_gate_up(xn, wg, wu, tm=1408, tn=256)
    return resid_matmul(a, wd, h, 0.5, tm=768, tn=256)


def kernel(x_prompt, x_sample, cache_nsa_kv, cache_nsa_win, cache_dsa_kv, cache_dsa_idx, page_table,
           g_norm, w_ffn_gate, w_ffn_up, w_ffn_down, w_in, w_cmp1, w_cmp2, cmp_pos,
           w_br_a, w_br_b, w_out, g_final):
    B, T, D = x_prompt.shape
    DB, Ts, _ = x_sample.shape
    depth = g_norm.shape[0]
    page = cache_nsa_kv.shape[2]
    past_len = page_table.shape[1] * page
    n_heads_a = w_br_a.shape[1] // HEAD_DIM
    n_heads_b = w_br_b.shape[1] // HEAD_DIM
    n_idx = n_heads_b // 2
    G = N_KV_A
    lay = Layout(D, n_heads_a, n_heads_b, n_idx)
    Mp, Ms = B * T, DB * Ts

    pos_p = jnp.arange(T, dtype=jnp.int32)
    pos_s = past_len + jnp.arange(Ts, dtype=jnp.int32)
    pos_rows = jnp.concatenate([jnp.tile(pos_p, B), jnp.tile(pos_s, DB)])
    cos, sin = rope_tables(pos_rows)
    rope_flags = jnp.asarray(lay.rope_flags)

    h = jnp.concatenate([x_prompt.reshape(Mp, D), x_sample.reshape(Ms, D)], axis=0)
    outs = [[] for _ in range(8)]
    bf = jnp.bfloat16
    for l in range(depth):
        h = _swiglu_half_step(h, g_norm[l, 0], w_ffn_gate[l, 0].astype(bf), w_ffn_up[l, 0].astype(bf),
                              w_ffn_down[l, 0].astype(bf))

        u = rmsnorm(h, g_norm[l, 1], bf)
        proj = in_project(u, lay.pack(w_in[l]), rope_flags, cos, sin, tm=1408, tn=512)

        def split(rows, lead):
            sl = lambda a, n: rows[:, a:a + n]
            q_a = sl(lay.qa, n_heads_a * HEAD_DIM).reshape(*lead, n_heads_a, HEAD_DIM)
            nsa_rows = sl(lay.cmp, 4 * G * HEAD_DIM).reshape(*lead, 4, G, HEAD_DIM)
            win_rows = sl(lay.wink, 2 * G * HEAD_DIM).reshape(*lead, 2, G, HEAD_DIM)
            g_a = jax.nn.sigmoid(sl(lay.ga, 3 * n_heads_a)).reshape(*lead, 3, n_heads_a)
            q_b = sl(lay.qb, n_heads_b * HEAD_DIM).reshape(*lead, n_heads_b, HEAD_DIM)
            dsa_rows = jnp.stack([sl(lay.dk, HEAD_DIM), sl(lay.dv, HEAD_DIM)], axis=1).reshape(*lead, 2, HEAD_DIM)
            iq = sl(lay.iq, n_idx * IDX_DIM).reshape(*lead, n_idx, IDX_DIM)
            iw = sl(lay.iw, n_idx).reshape(*lead, n_idx)
            ik = sl(lay.ik, IDX_DIM).reshape(*lead, IDX_DIM)
            return q_a, nsa_rows, win_rows, g_a, q_b, dsa_rows, iq, iw, ik

        q_a, nsa_rows, win_rows, g_a, q_b, dsa_rows, iq, iw, ik = split(proj[:Mp], (B, T))
        n_ch = T // CMP_STRIDE
        cos_end, sin_end = rope_tables(jnp.arange(n_ch, dtype=jnp.int32) * CMP_STRIDE + CMP_BLOCK - 1)
        w1 = w_cmp1[l].reshape(2, CMP_BLOCK, HEAD_DIM, -1).astype(bf)
        kvc = compress_prompt(proj, B, T, lay.cmp, w1, w_cmp2[l].astype(bf), cmp_pos[l], cos_end, sin_end)
        o_a_p = nsa_prompt(proj, kvc, lay, B, T, tq=LANES)
        o_b_p = dsa_prompt(proj, lay, B, T, tq=LANES)
        outs[0].append(nsa_rows)
        outs[1].append(win_rows[:, T - min(WINDOW, T):])
        outs[2].append(dsa_rows)
        outs[3].append(ik)

        q_a, nsa_rows, win_rows, g_a, q_b, dsa_rows, iq, iw, ik = split(proj[Mp:], (DB, Ts))
        past_nsa = _gather_pages(cache_nsa_kv[l], page_table)
        o_a_s = _nsa_sample(q_a, g_a, nsa_rows, win_rows, pos_s, past_nsa, cache_nsa_win[l], past_len,
                            w_cmp1[l], w_cmp2[l], cmp_pos[l])
        kv_all = jnp.concatenate([_gather_pages(cache_dsa_kv[l], page_table), dsa_rows], axis=1)
        ik_all = jnp.concatenate([_gather_pages(cache_dsa_idx[l], page_table), ik], axis=1)
        n_top = min(DSA_TOPK, kv_all.shape[1] // 4)
        o_b_s = _dsa_attend(q_b, iq, iw, pos_s, kv_all[:, :, 0], kv_all[:, :, 1], ik_all, n_top)
        outs[4].append(nsa_rows)
        outs[5].append(jnp.concatenate([cache_nsa_win[l], win_rows], axis=1)[:, Ts:])
        outs[6].append(dsa_rows)
        outs[7].append(ik)

        o_a = jnp.concatenate([o_a_p, o_a_s.reshape(Ms, -1).astype(bf)], axis=0)
        o_b = jnp.concatenate([o_b_p, o_b_s.reshape(Ms, -1).astype(bf)], axis=0)
        m = merge_branches(o_a, o_b, w_br_a[l].astype(bf), w_br_b[l].astype(bf), proj, lay.mg, tm=768, tn=512)
        h = resid_matmul(m, w_out[l].astype(bf), h, 1.0, tm=768, tn=512)

        h = _swiglu_half_step(h, g_norm[l, 2], w_ffn_gate[l, 1].astype(bf), w_ffn_up[l, 1].astype(bf),
                              w_ffn_down[l, 1].astype(bf))

    y = rmsnorm(h, g_final, jnp.float32)
    return (y[:Mp].reshape(B, T, D), y[Mp:].reshape(DB, Ts, D), *(jnp.stack(o) for o in outs))
```

```python
import functools

import jax
import jax.numpy as jnp
import numpy as np
from jax import lax
from jax.experimental import pallas as pl
from jax.experimental.pallas import tpu as pltpu

HEAD_DIM = 128
N_KV_A = 2
IDX_DIM = 128
CMP_BLOCK = 32
CMP_STRIDE = 16
SEL_BLOCK = 64
N_SEL = 16
N_LOCAL_SEL = 2
WINDOW = 512
DSA_TOPK = 256
ROPE_THETA = 10000.0
RMS_EPS = 1e-6
NEG = -1e30
BIG = 1e30
TINY = 1e-30

LANES = 128
VMEM_LIMIT = 56 * 1024 * 1024
PAGES_PER_STEP = 8

_NT = (((1,), (1,)), ((), ()))
_BF = jnp.bfloat16


def _params(*sem):
    return pltpu.CompilerParams(dimension_semantics=sem, vmem_limit_bytes=VMEM_LIMIT)


def _rmsnorm_kernel(x_ref, g_ref, o_ref):
    x = x_ref[...]
    y = x * lax.rsqrt(jnp.mean(x * x, axis=-1, keepdims=True) + RMS_EPS)
    o_ref[...] = (y * g_ref[...]).astype(o_ref.dtype)


def rmsnorm(x, g, out_dtype, tm=256):
    M, D = x.shape
    return pl.pallas_call(
        _rmsnorm_kernel,
        grid=(M // tm,),
        in_specs=[pl.BlockSpec((tm, D), lambda i: (i, 0)),
                  pl.BlockSpec((1, D), lambda i: (0, 0))],
        out_specs=pl.BlockSpec((tm, D), lambda i: (i, 0)),
        out_shape=jax.ShapeDtypeStruct((M, D), out_dtype),
        compiler_params=_params("parallel"),
    )(x, g.reshape(1, D))


def _gateup_kernel(x_ref, wg_ref, wu_ref, o_ref):
    x = x_ref[...]
    g = jnp.dot(x, wg_ref[...], preferred_element_type=jnp.float32)
    u = jnp.dot(x, wu_ref[...], preferred_element_type=jnp.float32)
    o_ref[...] = (g * jax.nn.sigmoid(g) * u).astype(o_ref.dtype)


def ffn_gate_up(xn, wg, wu, tm, tn):
    M, D = xn.shape
    F = wg.shape[1]
    return pl.pallas_call(
        _gateup_kernel,
        grid=(M // tm, F // tn),
        in_specs=[pl.BlockSpec((tm, D), lambda i, j: (i, 0)),
                  pl.BlockSpec((D, tn), lambda i, j: (0, j)),
                  pl.BlockSpec((D, tn), lambda i, j: (0, j))],
        out_specs=pl.BlockSpec((tm, tn), lambda i, j: (i, j)),
        out_shape=jax.ShapeDtypeStruct((M, F), _BF),
        compiler_params=_params("parallel", "parallel"),
    )(xn, wg, wu)


def _resid_matmul_kernel(a_ref, w_ref, r_ref, o_ref, *, scale):
    acc = jnp.dot(a_ref[...], w_ref[...], preferred_element_type=jnp.float32)
    o_ref[...] = r_ref[...] + scale * acc


def resid_matmul(a, w, resid, scale, tm, tn):
    M, K = a.shape
    N = w.shape[1]
    return pl.pallas_call(
        functools.partial(_resid_matmul_kernel, scale=scale),
        grid=(M // tm, N // tn),
        in_specs=[pl.BlockSpec((tm, K), lambda i, j: (i, 0)),
                  pl.BlockSpec((K, tn), lambda i, j: (0, j)),
                  pl.BlockSpec((tm, tn), lambda i, j: (i, j))],
        out_specs=pl.BlockSpec((tm, tn), lambda i, j: (i, j)),
        out_shape=jax.ShapeDtypeStruct((M, N), jnp.float32),
        compiler_params=_params("parallel", "parallel"),
    )(a, w, resid)


def _rotary(y, cos, sin):
    return y * cos + pltpu.roll(y, HEAD_DIM // 2, axis=1) * sin


def _inproj_kernel(flags_ref, x_ref, w_ref, cos_ref, sin_ref, o_ref, *, n_chunks):
    j = pl.program_id(1)
    acc = jnp.dot(x_ref[...], w_ref[...], preferred_element_type=jnp.float32)
    for c in range(n_chunks):
        sl = slice(c * LANES, (c + 1) * LANES)
        y = acc[:, sl]
        flag = flags_ref[j * n_chunks + c]

        @pl.when(flag == 1)
        def _():
            o_ref[:, sl] = _rotary(y, cos_ref[...], sin_ref[...])

        @pl.when(flag == 0)
        def _():
            o_ref[:, sl] = y


def in_project(u, w, rope_flags, cos, sin, tm, tn):
    M, D = u.shape
    N = w.shape[1]
    n_chunks = tn // LANES
    grid_spec = pltpu.PrefetchScalarGridSpec(
        num_scalar_prefetch=1,
        grid=(M // tm, N // tn),
        in_specs=[pl.BlockSpec((tm, D), lambda i, j, f: (i, 0)),
                  pl.BlockSpec((D, tn), lambda i, j, f: (0, j)),
                  pl.BlockSpec((tm, LANES), lambda i, j, f: (i, 0)),
                  pl.BlockSpec((tm, LANES), lambda i, j, f: (i, 0))],
        out_specs=pl.BlockSpec((tm, tn), lambda i, j, f: (i, j)),
    )
    return pl.pallas_call(
        functools.partial(_inproj_kernel, n_chunks=n_chunks),
        grid_spec=grid_spec,
        out_shape=jax.ShapeDtypeStruct((M, N), jnp.float32),
        compiler_params=_params("parallel", "parallel"),
    )(rope_flags, u, w, cos, sin)


def _merge_kernel(oa_ref, ob_ref, wa_ref, wb_ref, ga_ref, gb_ref, o_ref):
    ya = jnp.dot(oa_ref[...], wa_ref[...], preferred_element_type=jnp.float32)
    yb = jnp.dot(ob_ref[...], wb_ref[...], preferred_element_type=jnp.float32)
    m = jax.nn.sigmoid(ga_ref[...]) * ya + jax.nn.sigmoid(gb_ref[...]) * yb
    o_ref[...] = m.astype(o_ref.dtype)


def merge_branches(o_a, o_b, w_a, w_b, proj, mg_col, tm, tn):
    M, K = o_a.shape
    N = w_a.shape[1]
    ja = mg_col // tn
    jb = (mg_col + N) // tn
    return pl.pallas_call(
        _merge_kernel,
        grid=(M // tm, N // tn),
        in_specs=[pl.BlockSpec((tm, K), lambda i, j: (i, 0)),
                  pl.BlockSpec((tm, K), lambda i, j: (i, 0)),
                  pl.BlockSpec((K, tn), lambda i, j: (0, j)),
                  pl.BlockSpec((K, tn), lambda i, j: (0, j)),
                  pl.BlockSpec((tm, tn), lambda i, j: (i, ja + j)),
                  pl.BlockSpec((tm, tn), lambda i, j: (i, jb + j))],
        out_specs=pl.BlockSpec((tm, tn), lambda i, j: (i, j)),
        out_shape=jax.ShapeDtypeStruct((M, N), _BF),
        compiler_params=_params("parallel", "parallel"),
    )(o_a, o_b, w_a, w_b, proj, proj)


def _stack_heads(x, n_heads):
    return jnp.concatenate([x[:, h * HEAD_DIM:(h + 1) * HEAD_DIM] for h in range(n_heads)], axis=0)


def _flash_update(qs, k, v, mask, carry, n_rep):
    m, l, acc = carry
    rows, tk = qs.shape[0], k.shape[0]
    tq = rows // n_rep
    scale = HEAD_DIM ** -0.5
    s = lax.dot_general(qs, k, _NT, preferred_element_type=jnp.float32).reshape(n_rep, tq, tk)
    mask = mask[None]
    s = jnp.where(mask, s, NEG)
    m_new = jnp.maximum(m, jnp.max(s, axis=-1, keepdims=True))
    p = jnp.where(mask, jnp.exp((s - m_new) * scale), 0.0)
    alpha = jnp.exp((m - m_new) * scale)
    l = alpha * l + jnp.sum(p, axis=-1, keepdims=True)
    pv = jnp.dot(p.reshape(rows, tk).astype(_BF), v, preferred_element_type=jnp.float32)
    return m_new, l, alpha * acc + pv.reshape(n_rep, tq, HEAD_DIM)


def _flash_init(n_rep, tq):
    return (jnp.full((n_rep, tq, 1), NEG, jnp.float32), jnp.zeros((n_rep, tq, 1), jnp.float32),
            jnp.zeros((n_rep, tq, HEAD_DIM), jnp.float32))


def _flash_finish(carry):
    _, l, acc = carry
    return acc * (1.0 / jnp.maximum(l, TINY))


def _flash(qs, k_ref, v_ref, kt_lo, kt_hi, tk, n_rep, mask_fn):
    def body(kt, carry):
        start = pl.multiple_of(kt * tk, tk)
        k = k_ref[pl.ds(start, tk), :].astype(_BF)
        v = v_ref[pl.ds(start, tk), :].astype(_BF)
        return _flash_update(qs, k, v, mask_fn(kt), carry, n_rep)

    return _flash_finish(lax.fori_loop(kt_lo, kt_hi, body, _flash_init(n_rep, qs.shape[0] // n_rep)))


def _compress_rows(load, pe, w1, w2, n_ch):
    h0 = h1 = None
    for s in range(CMP_STRIDE):
        a = load(s)
        a0 = (a + pe[s:s + 1, :]).astype(_BF)
        a1 = (a + pe[CMP_STRIDE + s:CMP_STRIDE + s + 1, :]).astype(_BF)
        d0 = jnp.dot(a0, w1(s), preferred_element_type=jnp.float32)
        d1 = jnp.dot(a1, w1(CMP_STRIDE + s), preferred_element_type=jnp.float32)
        h0 = d0 if h0 is None else h0 + d0
        h1 = d1 if h1 is None else h1 + d1
    h = h0 + pltpu.roll(h1, n_ch - 1, axis=0)
    return jnp.dot((h * jax.nn.sigmoid(h)).astype(_BF), w2, preferred_element_type=jnp.float32)


def _cmp_attend_and_choose(qs, kc, vc, cover, t_col, n_heads, n_s):
    rows = qs.shape[0]
    tq = rows // n_heads
    n_ch, lanes = cover.shape
    scale = HEAD_DIM ** -0.5
    s = lax.dot_general(qs, kc, _NT, preferred_element_type=jnp.float32).reshape(n_heads, tq, n_ch)
    end = lax.broadcasted_iota(jnp.int32, (tq, n_ch), 1) * CMP_STRIDE + (CMP_BLOCK - 1)
    cmask = (end <= t_col)[None]
    s = jnp.where(cmask, s, NEG)
    m = jnp.max(s, axis=-1, keepdims=True)
    p = jnp.where(cmask, jnp.exp((s - m) * scale), 0.0)
    p = p * (1.0 / jnp.maximum(jnp.sum(p, axis=-1, keepdims=True), TINY))
    o_cmp = jnp.dot(p.reshape(rows, n_ch).astype(_BF), vc, preferred_element_type=jnp.float32)
    imp = jnp.dot(jnp.sum(p, axis=0).astype(_BF), cover, preferred_element_type=jnp.float32)

    lane = lax.broadcasted_iota(jnp.int32, (tq, lanes), 1)
    lane_f = lane.astype(jnp.float32)
    jt = lax.shift_right_arithmetic(t_col, jnp.int32(SEL_BLOCK.bit_length() - 1))
    adm = lane <= jt
    forced = adm & ((lane == 0) | (lane > jt - N_LOCAL_SEL))
    work = jnp.where(forced, BIG, jnp.where(adm, imp, NEG))
    work = jnp.where(lane < n_s, work, -jnp.inf)
    sel = jnp.zeros((tq, lanes), jnp.bool_)
    for _ in range(min(N_SEL, n_s)):
        mx = jnp.max(work, axis=-1, keepdims=True)
        first = jnp.min(jnp.where(work == mx, lane_f, float(lanes)), axis=-1, keepdims=True)
        pick = lane_f == first
        sel = sel | pick
        work = jnp.where(pick, -jnp.inf, work)
    return o_cmp.reshape(n_heads, tq, HEAD_DIM), sel


def _topk_mask(score_ref, key_ref, mask_ref, tri_ref, n_tiles, n_top):
    tq = score_ref.shape[0]
    int_min = jnp.int32(-2 ** 31)

    def to_key(t, _):
        sl = pl.ds(pl.multiple_of(t * LANES, LANES), LANES)
        bits = lax.bitcast_convert_type(score_ref[:, sl], jnp.int32)
        key_ref[:, sl] = bits ^ ((bits >> 31) & jnp.int32(0x7FFFFFFF))
        return 0

    lax.fori_loop(0, n_tiles, to_key, 0)

    def count_ge(cand):
        def body(t, acc):
            sl = pl.ds(pl.multiple_of(t * LANES, LANES), LANES)
            return acc + jnp.where(key_ref[:, sl] >= cand, 1.0, 0.0)
        acc = lax.fori_loop(0, n_tiles, body, jnp.zeros((tq, LANES), jnp.float32))
        return jnp.sum(acc, axis=-1, keepdims=True)

    tau = jnp.zeros((tq, 1), jnp.int32)
    for bit in range(31, -1, -1):
        cand = tau | jnp.int32(-2 ** 31 if bit == 31 else 1 << bit)
        tau = jnp.where(count_ge(cand ^ int_min) >= float(n_top), cand, tau)
    thr = tau ^ int_min

    def count_gt(t, acc):
        sl = pl.ds(pl.multiple_of(t * LANES, LANES), LANES)
        return acc + jnp.where(key_ref[:, sl] > thr, 1.0, 0.0)

    n_gt = jnp.sum(lax.fori_loop(0, n_tiles, count_gt, jnp.zeros((tq, LANES), jnp.float32)),
                   axis=-1, keepdims=True)
    need = float(n_top) - n_gt
    ones = jnp.ones((LANES, LANES), _BF)

    def cut(t, eq_before):
        sl = pl.ds(pl.multiple_of(t * LANES, LANES), LANES)
        key = key_ref[:, sl]
        eq = jnp.where(key == thr, 1.0, 0.0)
        rank = (jnp.dot(eq.astype(_BF), tri_ref[...], preferred_element_type=jnp.float32)
                + jnp.dot(eq_before.astype(_BF), ones, preferred_element_type=jnp.float32))
        keep = (key > thr) | ((key == thr) & (rank <= need))
        mask_ref[:, sl] = jnp.where(keep, 1.0, 0.0)
        return eq_before + eq

    lax.fori_loop(0, n_tiles, cut, jnp.zeros((tq, LANES), jnp.float32))


def _indexer_scores(iqs, ik, iw, iw_lane, n_idx):
    tq = iqs.shape[0] // n_idx
    n = ik.shape[0]
    logits = lax.dot_general(iqs, ik, _NT, preferred_element_type=jnp.float32).reshape(n_idx, tq, n)
    logits = jnp.maximum(logits * IDX_DIM ** -0.5, 0.0)
    sc = jnp.zeros((tq, n), jnp.float32)
    for h in range(n_idx):
        sc = sc + logits[h] * iw[:, iw_lane + h:iw_lane + h + 1]
    return sc * n_idx ** -0.5


def _cover_matrix(n_ch, lanes, n_s):
    ci = np.arange(n_ch)[:, None] * CMP_STRIDE
    sj = np.arange(lanes)[None, :] * SEL_BLOCK
    return jnp.asarray((ci < sj + SEL_BLOCK) & (ci + CMP_BLOCK > sj) & (sj < n_s * SEL_BLOCK), _BF)


def _expand_matrix(lanes, n_keys):
    return jnp.asarray(np.arange(lanes)[:, None] == np.arange(n_keys)[None, :] // SEL_BLOCK, _BF)


def _tri_matrix():
    return jnp.asarray(np.arange(LANES)[:, None] <= np.arange(LANES)[None, :], _BF)


def _pad_rows(x, n):
    return jnp.concatenate([x, jnp.zeros((n - x.shape[0], x.shape[1]), x.dtype)], axis=0)


def _compress_kernel(x_ref, w1_ref, w2_ref, pe_ref, cos_ref, sin_ref, o_ref):
    n_ch = o_ref.shape[-2]
    y = _compress_rows(lambda s: x_ref[pl.ds(s, n_ch, stride=CMP_STRIDE), :], pe_ref[0],
                       lambda s: w1_ref[0, s], w2_ref[0], n_ch)

    @pl.when(pl.program_id(1) == 0)
    def _():
        o_ref[0, 0, 0] = _rotary(y, cos_ref[...], sin_ref[...]).astype(o_ref.dtype)

    @pl.when(pl.program_id(1) != 0)
    def _():
        o_ref[0, 0, 0] = y.astype(o_ref.dtype)


def compress_prompt(proj, n_batch, seq, cmp_col, w1, w2, pe, cos_end, sin_end):
    n_ch = seq // CMP_STRIDE
    col0 = cmp_col // HEAD_DIM
    return pl.pallas_call(
        _compress_kernel,
        grid=(n_batch, 2, N_KV_A),
        in_specs=[pl.BlockSpec((seq, HEAD_DIM), lambda b, kv, g: (b, col0 + kv * N_KV_A + g)),
                  pl.BlockSpec((1, CMP_BLOCK, HEAD_DIM, w1.shape[-1]), lambda b, kv, g: (kv, 0, 0, 0)),
                  pl.BlockSpec((1, w2.shape[1], HEAD_DIM), lambda b, kv, g: (kv, 0, 0)),
                  pl.BlockSpec((1, CMP_BLOCK, HEAD_DIM), lambda b, kv, g: (kv, 0, 0)),
                  pl.BlockSpec((n_ch, HEAD_DIM), lambda b, kv, g: (0, 0)),
                  pl.BlockSpec((n_ch, HEAD_DIM), lambda b, kv, g: (0, 0))],
        out_specs=pl.BlockSpec((1, 1, 1, n_ch, HEAD_DIM), lambda b, kv, g: (b, kv, g, 0, 0)),
        out_shape=jax.ShapeDtypeStruct((n_batch, 2, N_KV_A, n_ch, HEAD_DIM), _BF),
        compiler_params=_params("parallel", "parallel", "parallel"),
    )(proj, w1, w2, pe, cos_end, sin_end)


def _nsa_prompt_kernel(q_ref, kc_ref, vc_ref, selk_ref, selv_ref, wink_ref, winv_ref, gate_ref,
                       cover_ref, expand_ref, o_ref, selexp_ref, *, n_heads, n_s):
    tq = q_ref.shape[0]
    grp = pl.program_id(1)
    qi = pl.program_id(2)
    t_col = qi * tq + lax.broadcasted_iota(jnp.int32, (tq, 1), 0)
    key_iota = lax.broadcasted_iota(jnp.int32, (tq, tq), 1)
    qs = _stack_heads(q_ref[...], n_heads).astype(_BF)

    o_cmp, sel = _cmp_attend_and_choose(qs, kc_ref[0, 0, 0], vc_ref[0, 0, 0], cover_ref[...], t_col,
                                        n_heads, n_s)
    selexp_ref[...] = jnp.dot(jnp.where(sel, 1.0, 0.0).astype(_BF), expand_ref[...],
                              preferred_element_type=jnp.float32)

    def sel_mask(kt):
        chosen = selexp_ref[:, pl.ds(pl.multiple_of(kt * tq, tq), tq)] > 0.5
        return chosen & (kt * tq + key_iota <= t_col)

    o_slc = _flash(qs, selk_ref, selv_ref, 0, qi + 1, tq, n_heads, sel_mask)

    def win_mask(kt):
        d = t_col - (kt * tq + key_iota)
        return (d >= 0) & (d < WINDOW)

    o_win = _flash(qs, wink_ref, winv_ref, jnp.maximum(qi - WINDOW // tq, 0), qi + 1, tq, n_heads, win_mask)

    gates = jax.nn.sigmoid(gate_ref[...])
    gates = jnp.where(grp == 0, gates, pltpu.roll(gates, LANES - n_heads, axis=1))
    n_all = N_KV_A * n_heads
    for r in range(n_heads):
        o = (gates[:, r:r + 1] * o_cmp[r] + gates[:, n_all + r:n_all + r + 1] * o_slc[r]
             + gates[:, 2 * n_all + r:2 * n_all + r + 1] * o_win[r])
        o_ref[:, r * HEAD_DIM:(r + 1) * HEAD_DIM] = o.astype(o_ref.dtype)


def nsa_prompt(proj, kvc, lay, n_batch, seq, tq):
    n_heads = lay.n_heads_a // N_KV_A
    n_ch = kvc.shape[-2]
    n_s = -(-seq // SEL_BLOCK)
    assert n_s <= LANES and seq % tq == 0 and tq % SEL_BLOCK == 0 and WINDOW % tq == 0
    assert lay.ga % LANES == 0 and N_KV_A == 2
    nq = seq // tq
    qw = n_heads * HEAD_DIM

    def slab(col):
        c0 = col // HEAD_DIM
        return pl.BlockSpec((seq, HEAD_DIM), lambda b, g, i: (b, c0 + g))

    return pl.pallas_call(
        functools.partial(_nsa_prompt_kernel, n_heads=n_heads, n_s=n_s),
        grid=(n_batch, N_KV_A, nq),
        in_specs=[pl.BlockSpec((tq, qw), lambda b, g, i: (b * nq + i, lay.qa // qw + g)),
                  pl.BlockSpec((1, 1, 1, n_ch, HEAD_DIM), lambda b, g, i: (b, 0, g, 0, 0)),
                  pl.BlockSpec((1, 1, 1, n_ch, HEAD_DIM), lambda b, g, i: (b, 1, g, 0, 0)),
                  slab(lay.selk), slab(lay.selv), slab(lay.wink), slab(lay.winv),
                  pl.BlockSpec((tq, LANES), lambda b, g, i: (b * nq + i, lay.ga // LANES)),
                  pl.BlockSpec((n_ch, LANES), lambda b, g, i: (0, 0)),
                  pl.BlockSpec((LANES, seq), lambda b, g, i: (0, 0))],
        out_specs=pl.BlockSpec((tq, qw), lambda b, g, i: (b * nq + i, g)),
        out_shape=jax.ShapeDtypeStruct((n_batch * seq, N_KV_A * qw), _BF),
        scratch_shapes=[pltpu.VMEM((tq, seq), jnp.float32)],
        compiler_params=_params("parallel", "parallel", "arbitrary"),
    )(proj, kvc, kvc, proj, proj, proj, proj, proj, _cover_matrix(n_ch, LANES, n_s), _expand_matrix(LANES, seq))


def _dsa_prompt_kernel(iq_ref, iw_ref, q_ref, ik_ref, k_ref, v_ref, tri_ref, o_ref,
                       score_ref, key_ref, mask_ref, *, n_idx, n_heads, n_top, iw_lane):
    tq = iq_ref.shape[0]
    qi = pl.program_id(1)
    n_tiles = qi + 1
    t_col = qi * tq + lax.broadcasted_iota(jnp.int32, (tq, 1), 0)
    key_iota = lax.broadcasted_iota(jnp.int32, (tq, tq), 1)
    iqs = _stack_heads(iq_ref[...], n_idx).astype(_BF)
    iw = iw_ref[...]

    def score_tile(kt, _):
        sl = pl.ds(pl.multiple_of(kt * tq, tq), tq)
        sc = _indexer_scores(iqs, ik_ref[sl, :].astype(_BF), iw, iw_lane, n_idx)
        score_ref[:, sl] = jnp.where(kt * tq + key_iota <= t_col, sc, NEG)
        return 0

    lax.fori_loop(0, n_tiles, score_tile, 0)
    _topk_mask(score_ref, key_ref, mask_ref, tri_ref, n_tiles, n_top)

    qs = _stack_heads(q_ref[...], n_heads).astype(_BF)

    def dsa_mask(kt):
        chosen = mask_ref[:, pl.ds(pl.multiple_of(kt * tq, tq), tq)] > 0.5
        return chosen & (kt * tq + key_iota <= t_col)

    o = _flash(qs, k_ref, v_ref, 0, n_tiles, tq, n_heads, dsa_mask)
    for h in range(n_heads):
        o_ref[:, h * HEAD_DIM:(h + 1) * HEAD_DIM] = o[h].astype(o_ref.dtype)


def dsa_prompt(proj, lay, n_batch, seq, tq):
    assert tq == LANES and seq % tq == 0
    nq = seq // tq
    n_top = min(DSA_TOPK, seq // 4)
    iqw = lay.n_idx_heads * IDX_DIM
    qw = lay.n_heads_b * HEAD_DIM

    def slab(col):
        c0 = col // HEAD_DIM
        return pl.BlockSpec((seq, HEAD_DIM), lambda b, i: (b, c0))

    return pl.pallas_call(
        functools.partial(_dsa_prompt_kernel, n_idx=lay.n_idx_heads, n_heads=lay.n_heads_b, n_top=n_top,
                          iw_lane=lay.iw % LANES),
        grid=(n_batch, nq),
        in_specs=[pl.BlockSpec((tq, iqw), lambda b, i: (b * nq + i, lay.iq // iqw)),
                  pl.BlockSpec((tq, LANES), lambda b, i: (b * nq + i, lay.iw // LANES)),
                  pl.BlockSpec((tq, qw), lambda b, i: (b * nq + i, lay.qb // qw)),
                  slab(lay.ik), slab(lay.dk), slab(lay.dv),
                  pl.BlockSpec((LANES, LANES), lambda b, i: (0, 0))],
        out_specs=pl.BlockSpec((tq, qw), lambda b, i: (b * nq + i, 0)),
        out_shape=jax.ShapeDtypeStruct((n_batch * seq, qw), _BF),
        scratch_shapes=[pltpu.VMEM((tq, seq), jnp.float32), pltpu.VMEM((tq, seq), jnp.int32),
                        pltpu.VMEM((tq, seq), jnp.float32)],
        compiler_params=_params("parallel", "arbitrary"),
    )(proj, proj, proj, proj, proj, proj, _tri_matrix())


def _page_specs(block, lane_block):
    def spec(k):
        return pl.BlockSpec(block, lambda b, s, pt: (pt[b, s * PAGES_PER_STEP + k], 0, lane_block))
    return [spec(k) for k in range(PAGES_PER_STEP)]


def _compress_sample_kernel(pt_ref, *refs):
    pages = refs[:PAGES_PER_STEP]
    w1_ref, w2_ref, pe_ref, cos_ref, sin_ref, o_ref, rows_ref = refs[PAGES_PER_STEP:]
    step = pl.program_id(1)
    page = pages[0].shape[1]
    n_slabs = rows_ref.shape[0]
    for k in range(PAGES_PER_STEP):
        start = pl.multiple_of((step * PAGES_PER_STEP + k) * page, page)
        for c in range(n_slabs):
            rows_ref.at[c][pl.ds(start, page), :] = pages[k][0, :, c * HEAD_DIM:(c + 1) * HEAD_DIM]

    @pl.when(step == pl.num_programs(1) - 1)
    def _():
        n_ch = o_ref.shape[-2]
        for kv in range(2):
            for g in range(N_KV_A):
                c = kv * N_KV_A + g
                y = _compress_rows(lambda s: rows_ref.at[c][pl.ds(s, n_ch, stride=CMP_STRIDE), :],
                                   pe_ref[kv], lambda s: w1_ref[kv, s], w2_ref[kv], n_ch)
                if kv == 0:
                    y = _rotary(y, cos_ref[...], sin_ref[...])
                o_ref[0, kv, g] = y.astype(o_ref.dtype)


def compress_sample(cache, page_table, w1, w2, pe, cos_end, sin_end):
    n_batch, n_pages = page_table.shape
    page = cache.shape[1]
    half = 2 * N_KV_A * HEAD_DIM
    n_ch = n_pages * page // CMP_STRIDE
    assert n_pages % PAGES_PER_STEP == 0
    const = lambda nd: (lambda b, s, pt: (0,) * nd)
    grid_spec = pltpu.PrefetchScalarGridSpec(
        num_scalar_prefetch=1,
        grid=(n_batch, n_pages // PAGES_PER_STEP),
        in_specs=_page_specs((1, page, half), 0) + [
            pl.BlockSpec(w1.shape, const(4)), pl.BlockSpec(w2.shape, const(3)), pl.BlockSpec(pe.shape, const(3)),
            pl.BlockSpec((n_ch, HEAD_DIM), const(2)), pl.BlockSpec((n_ch, HEAD_DIM), const(2))],
        out_specs=pl.BlockSpec((1, 2, N_KV_A, n_ch, HEAD_DIM), lambda b, s, pt: (b, 0, 0, 0, 0)),
        scratch_shapes=[pltpu.VMEM((half // HEAD_DIM, n_pages * page, HEAD_DIM), jnp.float32)],
    )
    return pl.pallas_call(
        _compress_sample_kernel,
        grid_spec=grid_spec,
        out_shape=jax.ShapeDtypeStruct((n_batch, 2, N_KV_A, n_ch, HEAD_DIM), _BF),
        compiler_params=_params("parallel", "arbitrary"),
    )(page_table, *([cache] * PAGES_PER_STEP), w1, w2, pe, cos_end, sin_end)


def _nsa_sample_kernel(pt_ref, *refs, n_heads, n_s, past_len):
    pages = refs[:PAGES_PER_STEP]
    (q_ref, gate_ref, kvc_ref, nsk_ref, nsv_ref, nwk_ref, nwv_ref, win_ref, cover_ref, expand_ref,
     o_ref, m_ref, l_ref, acc_ref, ocmp_ref, sel_ref) = refs[PAGES_PER_STEP:]
    step = pl.program_id(1)
    ts = q_ref.shape[0]
    qw = n_heads * HEAD_DIM
    n_grp = N_KV_A
    t_col = past_len + lax.broadcasted_iota(jnp.int32, (ts, 1), 0)

    def queries(g):
        return _stack_heads(q_ref[:, g * qw:(g + 1) * qw], n_heads).astype(_BF)

    def group_lanes(x, j, g):
        return x[:, (j * n_grp + g) * HEAD_DIM:(j * n_grp + g + 1) * HEAD_DIM]

    @pl.when(step == 0)
    def _():
        for g in range(n_grp):
            o_cmp, sel = _cmp_attend_and_choose(queries(g), kvc_ref[0, 0, g], kvc_ref[0, 1, g], cover_ref[...],
                                                t_col, n_heads, n_s)
            ocmp_ref[g] = o_cmp
            sel_ref[g] = jnp.where(sel, 1.0, 0.0)
            m_ref[g], l_ref[g], acc_ref[g] = _flash_init(n_heads, ts)

    for g in range(n_grp):
        chosen = jnp.dot(sel_ref[g].astype(_BF), expand_ref[...], preferred_element_type=jnp.float32) > 0.5
        k = jnp.concatenate([group_lanes(p[0], 0, g) for p in pages], axis=0).astype(_BF)
        v = jnp.concatenate([group_lanes(p[0], 1, g) for p in pages], axis=0).astype(_BF)
        m_ref[g], l_ref[g], acc_ref[g] = _flash_update(queries(g), k, v, chosen,
                                                       (m_ref[g], l_ref[g], acc_ref[g]), n_heads)

    @pl.when(step == pl.num_programs(1) - 1)
    def _():
        gates = jax.nn.sigmoid(gate_ref[...])
        n_all = n_grp * n_heads
        w_len = win_ref.shape[1]
        row = lax.broadcasted_iota(jnp.int32, (ts, LANES), 0)
        lane = lax.broadcasted_iota(jnp.int32, (ts, LANES), 1)
        new_causal = (lane <= row) & (lane < ts)
        wlane = lax.broadcasted_iota(jnp.int32, (ts, w_len + LANES), 1)
        k_pos = past_len - w_len + wlane
        d = t_col - k_pos
        win_mask = (d >= 0) & (d < WINDOW) & (k_pos >= 0) & (wlane < w_len + ts)
        new_blk = past_len // SEL_BLOCK
        for g in range(n_grp):
            qs = queries(g)
            chosen = sel_ref[g][:, new_blk:new_blk + 1] > 0.5
            k_new = _pad_rows(group_lanes(nsk_ref[...], 0, g), LANES).astype(_BF)
            v_new = _pad_rows(group_lanes(nsv_ref[...], 0, g), LANES).astype(_BF)
            o_slc = _flash_finish(_flash_update(qs, k_new, v_new, new_causal & chosen,
                                                (m_ref[g], l_ref[g], acc_ref[g]), n_heads))
            kw = jnp.concatenate([group_lanes(win_ref[0], 0, g),
                                  _pad_rows(group_lanes(nwk_ref[...], 0, g), LANES)], axis=0).astype(_BF)
            vw = jnp.concatenate([group_lanes(win_ref[0], 1, g),
                                  _pad_rows(group_lanes(nwv_ref[...], 0, g), LANES)], axis=0).astype(_BF)
            o_win = _flash_finish(_flash_update(qs, kw, vw, win_mask, _flash_init(n_heads, ts), n_heads))
            o_cmp = ocmp_ref[g]
            for r in range(n_heads):
                h = g * n_heads + r
                o = (gates[:, h:h + 1] * o_cmp[r] + gates[:, n_all + h:n_all + h + 1] * o_slc[r]
                     + gates[:, 2 * n_all + h:2 * n_all + h + 1] * o_win[r])
                o_ref[0, :, h * HEAD_DIM:(h + 1) * HEAD_DIM] = o.astype(o_ref.dtype)


def nsa_sample(proj, row0, kvc, cache, win_buf, page_table, lay, ts):
    n_batch, n_pages = page_table.shape
    page = cache.shape[1]
    past_len = n_pages * page
    n_heads = lay.n_heads_a // N_KV_A
    n_ch = kvc.shape[-2]
    n_s = -(-(past_len + ts) // SEL_BLOCK)
    sel_lanes = -(-n_s // LANES) * LANES
    kvw = N_KV_A * HEAD_DIM
    assert n_pages % PAGES_PER_STEP == 0 and row0 % ts == 0 and ts <= SEL_BLOCK and past_len % SEL_BLOCK == 0
    assert ts % 8 == 0 and lay.ga % LANES == 0
    assert (past_len + ts - CMP_BLOCK) // CMP_STRIDE + 1 <= n_ch and n_ch * CMP_STRIDE <= past_len
    r0 = row0 // ts
    keys_per_step = PAGES_PER_STEP * page
    qw = lay.n_heads_a * HEAD_DIM
    rows = lambda width, col: pl.BlockSpec((ts, width), lambda b, s, pt: (r0 + b, col // width))
    const = lambda nd: (lambda b, s, pt: (0,) * nd)
    grid_spec = pltpu.PrefetchScalarGridSpec(
        num_scalar_prefetch=1,
        grid=(n_batch, n_pages // PAGES_PER_STEP),
        in_specs=_page_specs((1, page, 2 * kvw), 1) + [
            rows(qw, lay.qa), rows(LANES, lay.ga),
            pl.BlockSpec((1, 2, N_KV_A, n_ch, HEAD_DIM), lambda b, s, pt: (b, 0, 0, 0, 0)),
            rows(kvw, lay.selk), rows(kvw, lay.selv), rows(kvw, lay.wink), rows(kvw, lay.winv),
            pl.BlockSpec((1,) + win_buf.shape[1:], lambda b, s, pt: (b, 0, 0)),
            pl.BlockSpec((n_ch, sel_lanes), const(2)),
            pl.BlockSpec((sel_lanes, keys_per_step), lambda b, s, pt: (0, s))],
        out_specs=pl.BlockSpec((1, ts, qw), lambda b, s, pt: (b, 0, 0)),
        scratch_shapes=[pltpu.VMEM((N_KV_A, n_heads, ts, 1), jnp.float32),
                        pltpu.VMEM((N_KV_A, n_heads, ts, 1), jnp.float32),
                        pltpu.VMEM((N_KV_A, n_heads, ts, HEAD_DIM), jnp.float32),
                        pltpu.VMEM((N_KV_A, n_heads, ts, HEAD_DIM), jnp.float32),
                        pltpu.VMEM((N_KV_A, ts, sel_lanes), jnp.float32)],
    )
    return pl.pallas_call(
        functools.partial(_nsa_sample_kernel, n_heads=n_heads, n_s=n_s, past_len=past_len),
        grid_spec=grid_spec,
        out_shape=jax.ShapeDtypeStruct((n_batch, ts, qw), _BF),
        compiler_params=_params("parallel", "arbitrary"),
    )(page_table, *([cache] * PAGES_PER_STEP), proj, proj, kvc, proj, proj, proj, proj, win_buf,
      _cover_matrix(n_ch, sel_lanes, n_s), _expand_matrix(sel_lanes, past_len))


def _dsa_sample_select_kernel(pt_ref, *refs, n_idx, n_top, iw_lane, past_len):
    pages = refs[:PAGES_PER_STEP]
    iq_ref, iw_ref, nik_ref, tri_ref, mask_ref, score_ref, key_ref = refs[PAGES_PER_STEP:]
    step = pl.program_id(1)
    ts = iq_ref.shape[0]
    keys_per_step = PAGES_PER_STEP * pages[0].shape[1]
    iqs = _stack_heads(iq_ref[...], n_idx).astype(_BF)
    iw = iw_ref[...]
    ik = jnp.concatenate([p[0] for p in pages], axis=0).astype(_BF)
    start = pl.multiple_of(step * keys_per_step, keys_per_step)
    score_ref[:, pl.ds(start, keys_per_step)] = _indexer_scores(iqs, ik, iw, iw_lane, n_idx)

    @pl.when(step == pl.num_programs(1) - 1)
    def _():
        sc = _indexer_scores(iqs, _pad_rows(nik_ref[...], LANES).astype(_BF), iw, iw_lane, n_idx)
        row = lax.broadcasted_iota(jnp.int32, (ts, LANES), 0)
        lane = lax.broadcasted_iota(jnp.int32, (ts, LANES), 1)
        score_ref[:, past_len:past_len + LANES] = jnp.where(lane < ts, jnp.where(lane <= row, sc, NEG), -jnp.inf)
        _topk_mask(score_ref, key_ref, mask_ref.at[0], tri_ref, past_len // LANES + 1, n_top)


def dsa_sample_select(proj, row0, cache_idx, page_table, lay, ts):
    n_batch, n_pages = page_table.shape
    page = cache_idx.shape[1]
    past_len = n_pages * page
    assert n_pages % PAGES_PER_STEP == 0 and row0 % ts == 0 and ts <= LANES and past_len % LANES == 0
    n_top = min(DSA_TOPK, (past_len + ts) // 4)
    r0 = row0 // ts
    iqw = lay.n_idx_heads * IDX_DIM
    width = past_len + LANES
    rows = lambda w, col: pl.BlockSpec((ts, w), lambda b, s, pt: (r0 + b, col // w))
    grid_spec = pltpu.PrefetchScalarGridSpec(
        num_scalar_prefetch=1,
        grid=(n_batch, n_pages // PAGES_PER_STEP),
        in_specs=_page_specs((1, page, IDX_DIM), 0) + [
            rows(iqw, lay.iq), rows(LANES, lay.iw), rows(IDX_DIM, lay.ik),
            pl.BlockSpec((LANES, LANES), lambda b, s, pt: (0, 0))],
        out_specs=pl.BlockSpec((1, ts, width), lambda b, s, pt: (b, 0, 0)),
        scratch_shapes=[pltpu.VMEM((ts, width), jnp.float32), pltpu.VMEM((ts, width), jnp.int32)],
    )
    return pl.pallas_call(
        functools.partial(_dsa_sample_select_kernel, n_idx=lay.n_idx_heads, n_top=n_top,
                          iw_lane=lay.iw % LANES, past_len=past_len),
        grid_spec=grid_spec,
        out_shape=jax.ShapeDtypeStruct((n_batch, ts, width), jnp.float32),
        compiler_params=_params("parallel", "arbitrary"),
    )(page_table, *([cache_idx] * PAGES_PER_STEP), proj, proj, proj, _tri_matrix())


def _dsa_sample_attend_kernel(pt_ref, *refs, n_heads):
    pages = refs[:PAGES_PER_STEP]
    q_ref, mask_ref, nmask_ref, nk_ref, nv_ref, o_ref, m_ref, l_ref, acc_ref = refs[PAGES_PER_STEP:]
    step = pl.program_id(1)
    ts = q_ref.shape[0]
    qs = _stack_heads(q_ref[...], n_heads).astype(_BF)

    @pl.when(step == 0)
    def _():
        m_ref[...], l_ref[...], acc_ref[...] = _flash_init(n_heads, ts)

    k = jnp.concatenate([p[0][:, :HEAD_DIM] for p in pages], axis=0).astype(_BF)
    v = jnp.concatenate([p[0][:, HEAD_DIM:] for p in pages], axis=0).astype(_BF)
    m_ref[...], l_ref[...], acc_ref[...] = _flash_update(qs, k, v, mask_ref[0] > 0.5,
                                                         (m_ref[...], l_ref[...], acc_ref[...]), n_heads)

    @pl.when(step == pl.num_programs(1) - 1)
    def _():
        row = lax.broadcasted_iota(jnp.int32, (ts, LANES), 0)
        lane = lax.broadcasted_iota(jnp.int32, (ts, LANES), 1)
        mask = (nmask_ref[0] > 0.5) & (lane <= row) & (lane < ts)
        o = _flash_finish(_flash_update(qs, _pad_rows(nk_ref[...], LANES).astype(_BF),
                                        _pad_rows(nv_ref[...], LANES).astype(_BF), mask,
                                        (m_ref[...], l_ref[...], acc_ref[...]), n_heads))
        for h in range(n_heads):
            o_ref[0, :, h * HEAD_DIM:(h + 1) * HEAD_DIM] = o[h].astype(o_ref.dtype)


def dsa_sample_attend(proj, row0, mask, cache_kv, page_table, lay, ts):
    n_batch, n_pages = page_table.shape
    page = cache_kv.shape[1]
    past_len = n_pages * page
    keys_per_step = PAGES_PER_STEP * page
    r0 = row0 // ts
    qw = lay.n_heads_b * HEAD_DIM
    rows = lambda w, col: pl.BlockSpec((ts, w), lambda b, s, pt: (r0 + b, col // w))
    grid_spec = pltpu.PrefetchScalarGridSpec(
        num_scalar_prefetch=1,
        grid=(n_batch, n_pages // PAGES_PER_STEP),
        in_specs=_page_specs((1, page, 2 * HEAD_DIM), 0) + [
            rows(qw, lay.qb),
            pl.BlockSpec((1, ts, keys_per_step), lambda b, s, pt: (b, 0, s)),
            pl.BlockSpec((1, ts, LANES), lambda b, s, pt: (b, 0, past_len // LANES)),
            rows(HEAD_DIM, lay.dk), rows(HEAD_DIM, lay.dv)],
        out_specs=pl.BlockSpec((1, ts, qw), lambda b, s, pt: (b, 0, 0)),
        scratch_shapes=[pltpu.VMEM((lay.n_heads_b, ts, 1), jnp.float32),
                        pltpu.VMEM((lay.n_heads_b, ts, 1), jnp.float32),
                        pltpu.VMEM((lay.n_heads_b, ts, HEAD_DIM), jnp.float32)],
    )
    return pl.pallas_call(
        functools.partial(_dsa_sample_attend_kernel, n_heads=lay.n_heads_b),
        grid_spec=grid_spec,
        out_shape=jax.ShapeDtypeStruct((n_batch, ts, qw), _BF),
        compiler_params=_params("parallel", "arbitrary"),
    )(page_table, *([cache_kv] * PAGES_PER_STEP), proj, mask, mask, proj, proj)


class Layout:
    def __init__(self, d_model, n_heads_a, n_heads_b, n_idx_heads):
        self.n_heads_a, self.n_heads_b, self.n_idx_heads = n_heads_a, n_heads_b, n_idx_heads
        src = np.cumsum([0, n_heads_a * HEAD_DIM, 6 * N_KV_A * HEAD_DIM, 3 * n_heads_a,
                         n_heads_b * HEAD_DIM, 2 * HEAD_DIM, n_idx_heads * IDX_DIM, n_idx_heads,
                         IDX_DIM, 2 * d_model])
        s_qa, s_kva, s_ga, s_qb, s_kvb, s_iq, s_iw, s_ik, s_mg, s_end = (int(v) for v in src)
        small = 3 * n_heads_a + n_idx_heads
        assert small <= LANES
        self.small_pad = LANES - small
        kvw = N_KV_A * HEAD_DIM
        self.pieces = [
            (s_qa, s_kva, True),
            (s_qb, s_kvb, True),
            (s_iq, s_iw, True),
            (s_kva, s_kva + 2 * kvw, False),
            (s_kva + 2 * kvw, s_kva + 3 * kvw, True),
            (s_kva + 3 * kvw, s_kva + 4 * kvw, False),
            (s_kva + 4 * kvw, s_kva + 5 * kvw, True),
            (s_kva + 5 * kvw, s_ga, False),
            (s_kvb, s_kvb + HEAD_DIM, True),
            (s_kvb + HEAD_DIM, s_iq, False),
            (s_ik, s_mg, True),
            (s_ga, s_qb, False),
            (s_iw, s_ik, False),
            None,
            (s_mg, s_end, False),
        ]
        off = 0
        starts = []
        for p in self.pieces:
            starts.append(off)
            off += self.small_pad if p is None else p[1] - p[0]
        (self.qa, self.qb, self.iq, self.cmp, self.selk, self.selv, self.wink, self.winv, self.dk,
         self.dv, self.ik, self.ga, self.iw, _, self.mg) = starts
        self.width = off
        assert self.width % LANES == 0
        flags = np.zeros(self.width // LANES, np.int32)
        for st, p in zip(starts, self.pieces):
            if p is not None and p[2]:
                assert st % LANES == 0 and (p[1] - p[0]) % LANES == 0
                flags[st // LANES:(st + p[1] - p[0]) // LANES] = 1
        self.rope_flags = flags

    def pack(self, w_in):
        cols = []
        for p in self.pieces:
            if p is None:
                cols.append(jnp.zeros((w_in.shape[0], self.small_pad), w_in.dtype))
            else:
                cols.append(w_in[:, p[0]:p[1]])
        return jnp.concatenate(cols, axis=1).astype(_BF)


def rope_tables(pos):
    half = HEAD_DIM // 2
    inv = ROPE_THETA ** (-jnp.arange(half, dtype=jnp.float32) / half)
    ang = pos.astype(jnp.float32)[:, None] * inv[None, :]
    cos, sin = jnp.cos(ang), jnp.sin(ang)
    return jnp.concatenate([cos, cos], axis=1), jnp.concatenate([-sin, sin], axis=1)


def _swiglu_half_step(h, g, wg, wu, wd):
    xn = rmsnorm(h, g, _BF)
    a = ffn_gate_up(xn, wg, wu, tm=1408, tn=256)
    return resid_matmul(a, wd, h, 0.5, tm=768, tn=256)


def kernel(x_prompt, x_sample, cache_nsa_kv, cache_nsa_win, cache_dsa_kv, cache_dsa_idx, page_table,
           g_norm, w_ffn_gate, w_ffn_up, w_ffn_down, w_in, w_cmp1, w_cmp2, cmp_pos,
           w_br_a, w_br_b, w_out, g_final):
    B, T, D = x_prompt.shape
    DB, Ts, _ = x_sample.shape
    depth = g_norm.shape[0]
    n_pool, page = cache_nsa_kv.shape[1:3]
    past_len = page_table.shape[1] * page
    n_heads_a = w_br_a.shape[1] // HEAD_DIM
    n_heads_b = w_br_b.shape[1] // HEAD_DIM
    G = N_KV_A
    lay = Layout(D, n_heads_a, n_heads_b, n_heads_b // 2)
    Mp, Ms = B * T, DB * Ts
    w_buf = cache_nsa_win.shape[2]

    pos_p = jnp.arange(T, dtype=jnp.int32)
    pos_s = past_len + jnp.arange(Ts, dtype=jnp.int32)
    cos, sin = rope_tables(jnp.concatenate([jnp.tile(pos_p, B), jnp.tile(pos_s, DB)]))
    rope_flags = jnp.asarray(lay.rope_flags)

    def block_end_tables(n_ch):
        return rope_tables(jnp.arange(n_ch, dtype=jnp.int32) * CMP_STRIDE + CMP_BLOCK - 1)

    h = jnp.concatenate([x_prompt.reshape(Mp, D), x_sample.reshape(Ms, D)], axis=0)
    outs = [[] for _ in range(8)]
    for l in range(depth):
        h = _swiglu_half_step(h, g_norm[l, 0], w_ffn_gate[l, 0].astype(_BF), w_ffn_up[l, 0].astype(_BF),
                              w_ffn_down[l, 0].astype(_BF))

        u = rmsnorm(h, g_norm[l, 1], _BF)
        proj = in_project(u, lay.pack(w_in[l]), rope_flags, cos, sin, tm=1408, tn=512)
        w1 = w_cmp1[l].reshape(2, CMP_BLOCK, HEAD_DIM, -1).astype(_BF)
        w2 = w_cmp2[l].astype(_BF)

        kvc_p = compress_prompt(proj, B, T, lay.cmp, w1, w2, cmp_pos[l], *block_end_tables(T // CMP_STRIDE))
        o_a_p = nsa_prompt(proj, kvc_p, lay, B, T, tq=LANES)
        o_b_p = dsa_prompt(proj, lay, B, T, tq=LANES)

        nsa_pool = cache_nsa_kv[l].reshape(n_pool, page, -1)
        kvc_s = compress_sample(nsa_pool, page_table, w1, w2, cmp_pos[l], *block_end_tables(past_len // CMP_STRIDE))
        o_a_s = nsa_sample(proj, Mp, kvc_s, nsa_pool, cache_nsa_win[l].reshape(DB, w_buf, -1), page_table, lay, Ts)
        top_mask = dsa_sample_select(proj, Mp, cache_dsa_idx[l], page_table, lay, Ts)
        o_b_s = dsa_sample_attend(proj, Mp, top_mask, cache_dsa_kv[l].reshape(n_pool, page, -1), page_table, lay, Ts)

        o_a = jnp.concatenate([o_a_p, o_a_s.reshape(Ms, -1)], axis=0)
        o_b = jnp.concatenate([o_b_p, o_b_s.reshape(Ms, -1)], axis=0)
        m = merge_branches(o_a, o_b, w_br_a[l].astype(_BF), w_br_b[l].astype(_BF), proj, lay.mg, tm=768, tn=512)
        h = resid_matmul(m, w_out[l].astype(_BF), h, 1.0, tm=768, tn=512)

        nsa_rows = proj[:, lay.cmp:lay.cmp + 4 * G * HEAD_DIM]
        win_rows = proj[:, lay.wink:lay.wink + 2 * G * HEAD_DIM]
        dsa_rows = proj[:, lay.dk:lay.dk + 2 * HEAD_DIM]
        ik_rows = proj[:, lay.ik:lay.ik + IDX_DIM]
        win_p = win_rows[:Mp].reshape(B, T, 2, G, HEAD_DIM)
        win_s = win_rows[Mp:].reshape(DB, Ts, 2, G, HEAD_DIM)
        outs[0].append(nsa_rows[:Mp].reshape(B, T, 4, G, HEAD_DIM))
        outs[1].append(win_p[:, T - min(WINDOW, T):])
        outs[2].append(dsa_rows[:Mp].reshape(B, T, 2, HEAD_DIM))
        outs[3].append(ik_rows[:Mp].reshape(B, T, IDX_DIM))
        outs[4].append(nsa_rows[Mp:].reshape(DB, Ts, 4, G, HEAD_DIM))
        outs[5].append(jnp.concatenate([cache_nsa_win[l], win_s], axis=1)[:, Ts:])
        outs[6].append(dsa_rows[Mp:].reshape(DB, Ts, 2, HEAD_DIM))
        outs[7].append(ik_rows[Mp:].reshape(DB, Ts, IDX_DIM))

        h = _swiglu_half_step(h, g_norm[l, 2], w_ffn_gate[l, 1].astype(_BF), w_ffn_up[l, 1].astype(_BF),
                              w_ffn_down[l, 1].astype(_BF))

    y = rmsnorm(h, g_final, jnp.float32)
    return (y[:Mp].reshape(B, T, D), y[Mp:].reshape(DB, Ts, D), *(jnp.stack(o) for o in outs))
```

```python
import functools

import jax
import jax.numpy as jnp
import numpy as np
from jax import lax
from jax.experimental import pallas as pl
from jax.experimental.pallas import tpu as pltpu

HEAD_DIM = 128
N_KV_A = 2
IDX_DIM = 128
CMP_BLOCK = 32
CMP_STRIDE = 16
SEL_BLOCK = 64
N_SEL = 16
N_LOCAL_SEL = 2
WINDOW = 512
DSA_TOPK = 256
ROPE_THETA = 10000.0
RMS_EPS = 1e-6
NEG = -1e30
BIG = 1e30
TINY = 1e-30

LANES = 128
VMEM_LIMIT = 56 * 1024 * 1024
PAGES_PER_STEP = 8

_NT = (((1,), (1,)), ((), ()))
_BF = jnp.bfloat16


def _params(*sem):
    return pltpu.CompilerParams(dimension_semantics=sem, vmem_limit_bytes=VMEM_LIMIT)


def _rmsnorm_kernel(x_ref, g_ref, o_ref):
    x = x_ref[...]
    y = x * lax.rsqrt(jnp.mean(x * x, axis=-1, keepdims=True) + RMS_EPS)
    o_ref[...] = (y * g_ref[...]).astype(o_ref.dtype)


def rmsnorm(x, g, out_dtype, tm=256):
    M, D = x.shape
    return pl.pallas_call(
        _rmsnorm_kernel,
        grid=(M // tm,),
        in_specs=[pl.BlockSpec((tm, D), lambda i: (i, 0)),
                  pl.BlockSpec((1, D), lambda i: (0, 0))],
        out_specs=pl.BlockSpec((tm, D), lambda i: (i, 0)),
        out_shape=jax.ShapeDtypeStruct((M, D), out_dtype),
        compiler_params=_params("parallel"),
    )(x, g.reshape(1, D))


def _gateup_kernel(x_ref, wg_ref, wu_ref, o_ref):
    x = x_ref[...]
    g = jnp.dot(x, wg_ref[...].astype(_BF), preferred_element_type=jnp.float32)
    u = jnp.dot(x, wu_ref[...].astype(_BF), preferred_element_type=jnp.float32)
    o_ref[...] = (g * jax.nn.sigmoid(g) * u).astype(o_ref.dtype)


def ffn_gate_up(xn, wg, wu, tm, tn):
    M, D = xn.shape
    F = wg.shape[1]
    return pl.pallas_call(
        _gateup_kernel,
        grid=(M // tm, F // tn),
        in_specs=[pl.BlockSpec((tm, D), lambda i, j: (i, 0)),
                  pl.BlockSpec((D, tn), lambda i, j: (0, j)),
                  pl.BlockSpec((D, tn), lambda i, j: (0, j))],
        out_specs=pl.BlockSpec((tm, tn), lambda i, j: (i, j)),
        out_shape=jax.ShapeDtypeStruct((M, F), _BF),
        compiler_params=_params("parallel", "parallel"),
    )(xn, wg, wu)


def _resid_matmul_kernel(a_ref, w_ref, r_ref, o_ref, *, scale):
    acc = jnp.dot(a_ref[...], w_ref[...].astype(_BF), preferred_element_type=jnp.float32)
    o_ref[...] = r_ref[...] + scale * acc


def resid_matmul(a, w, resid, scale, tm, tn):
    M, K = a.shape
    N = w.shape[1]
    return pl.pallas_call(
        functools.partial(_resid_matmul_kernel, scale=scale),
        grid=(M // tm, N // tn),
        in_specs=[pl.BlockSpec((tm, K), lambda i, j: (i, 0), pipeline_mode=pl.Buffered(1)),
                  pl.BlockSpec((K, tn), lambda i, j: (0, j)),
                  pl.BlockSpec((tm, tn), lambda i, j: (i, j))],
        out_specs=pl.BlockSpec((tm, tn), lambda i, j: (i, j)),
        out_shape=jax.ShapeDtypeStruct((M, N), jnp.float32),
        compiler_params=_params("parallel", "parallel"),
    )(a, w, resid)


def _rotary(y, cos, sin):
    return y * cos + pltpu.roll(y, HEAD_DIM // 2, axis=1) * sin


def _inproj_kernel(flags_ref, x_ref, w_ref, cos_ref, sin_ref, o_ref, *, n_chunks):
    j = pl.program_id(1)
    acc = jnp.dot(x_ref[...], w_ref[...], preferred_element_type=jnp.float32)
    for c in range(n_chunks):
        sl = slice(c * LANES, (c + 1) * LANES)
        y = acc[:, sl]
        flag = flags_ref[j * n_chunks + c]

        @pl.when(flag == 1)
        def _():
            o_ref[:, sl] = _rotary(y, cos_ref[...], sin_ref[...])

        @pl.when(flag == 0)
        def _():
            o_ref[:, sl] = y


def in_project(u, w, rope_flags, cos, sin, tm, tn):
    M, D = u.shape
    N = w.shape[1]
    n_chunks = tn // LANES
    grid_spec = pltpu.PrefetchScalarGridSpec(
        num_scalar_prefetch=1,
        grid=(M // tm, N // tn),
        in_specs=[pl.BlockSpec((tm, D), lambda i, j, f: (i, 0)),
                  pl.BlockSpec((D, tn), lambda i, j, f: (0, j)),
                  pl.BlockSpec((tm, LANES), lambda i, j, f: (i, 0)),
                  pl.BlockSpec((tm, LANES), lambda i, j, f: (i, 0))],
        out_specs=pl.BlockSpec((tm, tn), lambda i, j, f: (i, j)),
    )
    return pl.pallas_call(
        functools.partial(_inproj_kernel, n_chunks=n_chunks),
        grid_spec=grid_spec,
        out_shape=jax.ShapeDtypeStruct((M, N), jnp.float32),
        compiler_params=_params("parallel", "parallel"),
    )(rope_flags, u, w, cos, sin)


def _merge_kernel(oa_ref, ob_ref, wa_ref, wb_ref, ga_ref, gb_ref, o_ref):
    ya = jnp.dot(oa_ref[...], wa_ref[...].astype(_BF), preferred_element_type=jnp.float32)
    yb = jnp.dot(ob_ref[...], wb_ref[...].astype(_BF), preferred_element_type=jnp.float32)
    m = jax.nn.sigmoid(ga_ref[...]) * ya + jax.nn.sigmoid(gb_ref[...]) * yb
    o_ref[...] = m.astype(o_ref.dtype)


def merge_branches(o_a, o_b, w_a, w_b, proj, mg_col, tm, tn):
    M, K = o_a.shape
    N = w_a.shape[1]
    ja = mg_col // tn
    jb = (mg_col + N) // tn
    return pl.pallas_call(
        _merge_kernel,
        grid=(M // tm, N // tn),
        in_specs=[pl.BlockSpec((tm, K), lambda i, j: (i, 0)),
                  pl.BlockSpec((tm, K), lambda i, j: (i, 0)),
                  pl.BlockSpec((K, tn), lambda i, j: (0, j)),
                  pl.BlockSpec((K, tn), lambda i, j: (0, j)),
                  pl.BlockSpec((tm, tn), lambda i, j: (i, ja + j)),
                  pl.BlockSpec((tm, tn), lambda i, j: (i, jb + j))],
        out_specs=pl.BlockSpec((tm, tn), lambda i, j: (i, j)),
        out_shape=jax.ShapeDtypeStruct((M, N), _BF),
        compiler_params=_params("parallel", "parallel"),
    )(o_a, o_b, w_a, w_b, proj, proj)


def _stack_heads(x, n_heads):
    return jnp.concatenate([x[:, h * HEAD_DIM:(h + 1) * HEAD_DIM] for h in range(n_heads)], axis=0)


def _flash_update(qs, k, v, mask, carry, n_rep):
    m, l, acc = carry
    rows, tk = qs.shape[0], k.shape[0]
    tq = rows // n_rep
    scale = HEAD_DIM ** -0.5
    s = lax.dot_general(qs, k, _NT, preferred_element_type=jnp.float32).reshape(n_rep, tq, tk)
    mask = mask[None]
    s = jnp.where(mask, s, NEG)
    m_new = jnp.maximum(m, jnp.max(s, axis=-1, keepdims=True))
    p = jnp.where(mask, jnp.exp((s - m_new) * scale), 0.0)
    alpha = jnp.exp((m - m_new) * scale)
    l = alpha * l + jnp.sum(p, axis=-1, keepdims=True)
    pv = jnp.dot(p.reshape(rows, tk).astype(_BF), v, preferred_element_type=jnp.float32)
    return m_new, l, alpha * acc + pv.reshape(n_rep, tq, HEAD_DIM)


def _flash_init(n_rep, tq):
    return (jnp.full((n_rep, tq, 1), NEG, jnp.float32), jnp.zeros((n_rep, tq, 1), jnp.float32),
            jnp.zeros((n_rep, tq, HEAD_DIM), jnp.float32))


def _flash_finish(carry):
    _, l, acc = carry
    return acc * (1.0 / jnp.maximum(l, TINY))


def _attend(qs, k_ref, v_ref, start, n_chunks, chunk, mask_fn, n_rep):
    rows = qs.shape[0]
    tq = rows // n_rep
    n_lane_tiles = chunk // LANES
    scale = HEAD_DIM ** -0.5

    def scores(c):
        sl = pl.ds(pl.multiple_of(start + c * chunk, LANES), chunk)
        return lax.dot_general(qs, k_ref[sl, :], _NT, preferred_element_type=jnp.float32), sl

    def lane_tile(s, j):
        return s[:, j * LANES:(j + 1) * LANES].reshape(n_rep, tq, LANES)

    def row_max(c, mx):
        s, _ = scores(c)
        for j in range(n_lane_tiles):
            mx = jnp.maximum(mx, jnp.where(mask_fn(c, j)[None], lane_tile(s, j), NEG))
        return mx

    mx = lax.fori_loop(0, n_chunks, row_max, jnp.full((n_rep, tq, LANES), NEG, jnp.float32))
    m = jnp.broadcast_to(jnp.max(mx, axis=-1, keepdims=True), mx.shape)

    def accumulate(c, acc):
        s, sl = scores(c)
        p = [jnp.where(mask_fn(c, j)[None], jnp.exp((lane_tile(s, j) - m) * scale), 0.0)
             .astype(_BF).reshape(rows, LANES) for j in range(n_lane_tiles)]
        return acc + jnp.dot(jnp.concatenate(p, axis=1), v_ref[sl, :], preferred_element_type=jnp.float32)

    acc = lax.fori_loop(0, n_chunks, accumulate, jnp.zeros((rows, 2 * HEAD_DIM), jnp.float32))
    out = acc[:, :HEAD_DIM] * (1.0 / jnp.maximum(acc[:, HEAD_DIM:], TINY))
    return out.reshape(n_rep, tq, HEAD_DIM)


def _compress_rows(load, pe, w1, w2, n_ch):
    h0 = h1 = None
    for s in range(CMP_STRIDE):
        a = load(s)
        a0 = (a + pe[s:s + 1, :]).astype(_BF)
        a1 = (a + pe[CMP_STRIDE + s:CMP_STRIDE + s + 1, :]).astype(_BF)
        d0 = jnp.dot(a0, w1(s), preferred_element_type=jnp.float32)
        d1 = jnp.dot(a1, w1(CMP_STRIDE + s), preferred_element_type=jnp.float32)
        h0 = d0 if h0 is None else h0 + d0
        h1 = d1 if h1 is None else h1 + d1
    h = h0 + pltpu.roll(h1, n_ch - 1, axis=0)
    return jnp.dot((h * jax.nn.sigmoid(h)).astype(_BF), w2, preferred_element_type=jnp.float32)


def _cmp_attend_and_choose(qs, kc, vc, cover, t_col, n_heads, n_s):
    rows = qs.shape[0]
    tq = rows // n_heads
    n_ch, lanes = cover.shape
    scale = HEAD_DIM ** -0.5
    s = lax.dot_general(qs, kc, _NT, preferred_element_type=jnp.float32).reshape(n_heads, tq, n_ch)
    end = lax.broadcasted_iota(jnp.int32, (tq, n_ch), 1) * CMP_STRIDE + (CMP_BLOCK - 1)
    cmask = (end <= t_col)[None]
    s = jnp.where(cmask, s, NEG)
    m = jnp.max(s, axis=-1, keepdims=True)
    p = jnp.where(cmask, jnp.exp((s - m) * scale), 0.0)
    p = p * (1.0 / jnp.maximum(jnp.sum(p, axis=-1, keepdims=True), TINY))
    o_cmp = jnp.dot(p.reshape(rows, n_ch).astype(_BF), vc, preferred_element_type=jnp.float32)
    imp = jnp.dot(jnp.sum(p, axis=0).astype(_BF), cover, preferred_element_type=jnp.float32)

    lane = lax.broadcasted_iota(jnp.int32, (tq, lanes), 1)
    lane_f = lane.astype(jnp.float32)
    jt = lax.shift_right_arithmetic(t_col, jnp.int32(SEL_BLOCK.bit_length() - 1))
    adm = lane <= jt
    forced = adm & ((lane == 0) | (lane > jt - N_LOCAL_SEL))
    work = jnp.where(forced, BIG, jnp.where(adm, imp, NEG))
    work = jnp.where(lane < n_s, work, -jnp.inf)
    sel = jnp.zeros((tq, lanes), jnp.bool_)
    for _ in range(min(N_SEL, n_s)):
        mx = jnp.max(work, axis=-1, keepdims=True)
        first = jnp.min(jnp.where(work == mx, lane_f, float(lanes)), axis=-1, keepdims=True)
        pick = lane_f == first
        sel = sel | pick
        work = jnp.where(pick, -jnp.inf, work)
    return o_cmp.reshape(n_heads, tq, HEAD_DIM), sel


def _topk_mask(score_ref, key_ref, mask_ref, tri_ref, n_tiles, n_top):
    tq = score_ref.shape[0]
    int_min = jnp.int32(-2 ** 31)

    def to_key(t, _):
        sl = pl.ds(pl.multiple_of(t * LANES, LANES), LANES)
        bits = lax.bitcast_convert_type(score_ref[:, sl], jnp.int32)
        key_ref[:, sl] = bits ^ ((bits >> 31) & jnp.int32(0x7FFFFFFF))
        return 0

    lax.fori_loop(0, n_tiles, to_key, 0)

    unroll = 8 if isinstance(n_tiles, int) else 1

    def count_ge(cand):
        def body(t, acc):
            sl = pl.ds(pl.multiple_of(t * LANES, LANES), LANES)
            return acc + jnp.where(key_ref[:, sl] >= cand, 1.0, 0.0)
        acc = lax.fori_loop(0, n_tiles, body, jnp.zeros((tq, LANES), jnp.float32), unroll=unroll)
        return jnp.sum(acc, axis=-1, keepdims=True)

    tau = jnp.zeros((tq, 1), jnp.int32)
    for bit in range(31, -1, -1):
        cand = tau | jnp.int32(-2 ** 31 if bit == 31 else 1 << bit)
        tau = jnp.where(count_ge(cand ^ int_min) >= float(n_top), cand, tau)
    thr = tau ^ int_min

    def count_gt(t, acc):
        sl = pl.ds(pl.multiple_of(t * LANES, LANES), LANES)
        return acc + jnp.where(key_ref[:, sl] > thr, 1.0, 0.0)

    n_gt = jnp.sum(lax.fori_loop(0, n_tiles, count_gt, jnp.zeros((tq, LANES), jnp.float32)),
                   axis=-1, keepdims=True)
    need = float(n_top) - n_gt
    ones = jnp.ones((LANES, LANES), _BF)

    def cut(t, eq_before):
        sl = pl.ds(pl.multiple_of(t * LANES, LANES), LANES)
        key = key_ref[:, sl]
        eq = jnp.where(key == thr, 1.0, 0.0)
        rank = (jnp.dot(eq.astype(_BF), tri_ref[...], preferred_element_type=jnp.float32)
                + jnp.dot(eq_before.astype(_BF), ones, preferred_element_type=jnp.float32))
        keep = (key > thr) | ((key == thr) & (rank <= need))
        mask_ref[:, sl] = jnp.where(keep, 1.0, 0.0)
        return eq_before + eq

    lax.fori_loop(0, n_tiles, cut, jnp.zeros((tq, LANES), jnp.float32))


def _indexer_scores(iqs, ik, iw, iw_lane, n_idx):
    tq = iqs.shape[0] // n_idx
    n = ik.shape[0]
    logits = lax.dot_general(iqs, ik, _NT, preferred_element_type=jnp.float32).reshape(n_idx, tq, n)
    logits = jnp.maximum(logits * IDX_DIM ** -0.5, 0.0)
    sc = jnp.zeros((tq, n), jnp.float32)
    for h in range(n_idx):
        sc = sc + logits[h] * iw[:, iw_lane + h:iw_lane + h + 1]
    return sc * n_idx ** -0.5


def _cover_matrix(n_ch, lanes, n_s):
    ci = np.arange(n_ch)[:, None] * CMP_STRIDE
    sj = np.arange(lanes)[None, :] * SEL_BLOCK
    return jnp.asarray((ci < sj + SEL_BLOCK) & (ci + CMP_BLOCK > sj) & (sj < n_s * SEL_BLOCK), _BF)


def _expand_matrix(lanes, n_keys):
    return jnp.asarray(np.arange(lanes)[:, None] == np.arange(n_keys)[None, :] // SEL_BLOCK, _BF)


def _tri_matrix():
    return jnp.asarray(np.arange(LANES)[:, None] <= np.arange(LANES)[None, :], _BF)


def _pad_rows(x, n):
    return jnp.concatenate([x, jnp.zeros((n - x.shape[0], x.shape[1]), x.dtype)], axis=0)


def _compress_kernel(x_ref, w1_ref, w2_ref, pe_ref, cos_ref, sin_ref, o_ref):
    n_ch = o_ref.shape[-2]
    y = _compress_rows(lambda s: x_ref[pl.ds(s, n_ch, stride=CMP_STRIDE), :], pe_ref[0],
                       lambda s: w1_ref[0, s], w2_ref[0], n_ch)

    @pl.when(pl.program_id(1) == 0)
    def _():
        o_ref[0, 0, 0] = _rotary(y, cos_ref[...], sin_ref[...]).astype(o_ref.dtype)

    @pl.when(pl.program_id(1) != 0)
    def _():
        o_ref[0, 0, 0] = y.astype(o_ref.dtype)


def compress_prompt(proj, n_batch, seq, cmp_col, w1, w2, pe, cos_end, sin_end):
    n_ch = seq // CMP_STRIDE
    col0 = cmp_col // HEAD_DIM
    return pl.pallas_call(
        _compress_kernel,
        grid=(n_batch, 2, N_KV_A),
        in_specs=[pl.BlockSpec((seq, HEAD_DIM), lambda b, kv, g: (b, col0 + kv * N_KV_A + g)),
                  pl.BlockSpec((1, CMP_BLOCK, HEAD_DIM, w1.shape[-1]), lambda b, kv, g: (kv, 0, 0, 0)),
                  pl.BlockSpec((1, w2.shape[1], HEAD_DIM), lambda b, kv, g: (kv, 0, 0)),
                  pl.BlockSpec((1, CMP_BLOCK, HEAD_DIM), lambda b, kv, g: (kv, 0, 0)),
                  pl.BlockSpec((n_ch, HEAD_DIM), lambda b, kv, g: (0, 0)),
                  pl.BlockSpec((n_ch, HEAD_DIM), lambda b, kv, g: (0, 0))],
        out_specs=pl.BlockSpec((1, 1, 1, n_ch, HEAD_DIM), lambda b, kv, g: (b, kv, g, 0, 0)),
        out_shape=jax.ShapeDtypeStruct((n_batch, 2, N_KV_A, n_ch, HEAD_DIM), _BF),
        compiler_params=_params("parallel", "parallel", "parallel"),
    )(proj, w1, w2, pe, cos_end, sin_end)


def _nsa_prompt_kernel(q_ref, kc_ref, vc_ref, selk_ref, selv_ref, wink_ref, winv_ref, gate_ref,
                       cover_ref, expand_ref, o_ref, selexp_ref, sk_ref, sv_ref, wk_ref, wv_ref,
                       *, n_heads, n_s, chunk):
    tq = q_ref.shape[0]
    seq = selexp_ref.shape[1]
    grp = pl.program_id(1)
    qi = pl.program_id(2)
    row0 = pl.multiple_of(qi * tq, tq)
    t_col = qi * tq + lax.broadcasted_iota(jnp.int32, (tq, 1), 0)

    @pl.when(qi == 0)
    def _():
        for ref in (sk_ref, sv_ref, wk_ref, wv_ref):
            ref[...] = jnp.zeros(ref.shape, ref.dtype)

    ones = jnp.ones((tq, HEAD_DIM), _BF)
    sk_ref[pl.ds(row0, tq), :] = selk_ref[...].astype(_BF)
    sv_ref[pl.ds(row0, tq), :] = jnp.concatenate([selv_ref[...].astype(_BF), ones], axis=1)
    wk_ref[pl.ds(row0, tq), :] = wink_ref[...].astype(_BF)
    wv_ref[pl.ds(row0, tq), :] = jnp.concatenate([winv_ref[...].astype(_BF), ones], axis=1)

    qs = _stack_heads(q_ref[...], n_heads).astype(_BF)
    o_cmp, sel = _cmp_attend_and_choose(qs, kc_ref[0, 0, 0], vc_ref[0, 0, 0], cover_ref[...], t_col, n_heads, n_s)
    chosen = jnp.dot(jnp.where(sel, 1.0, 0.0).astype(_BF), expand_ref[...], preferred_element_type=jnp.float32)
    causal = lax.broadcasted_iota(jnp.int32, (tq, seq), 1) <= t_col
    selexp_ref[...] = jnp.where(causal, chosen, 0.0)

    def sel_mask(c, j):
        return selexp_ref[:, pl.ds(pl.multiple_of(c * chunk + j * LANES, LANES), LANES)] > 0.5

    o_slc = _attend(qs, sk_ref, sv_ref, 0, (row0 + tq + chunk - 1) // chunk, chunk, sel_mask, n_heads)

    span = WINDOW + tq
    start = pl.multiple_of(jnp.maximum(row0 - WINDOW, 0), tq)
    d = t_col - (start + lax.broadcasted_iota(jnp.int32, (tq, span), 1))
    visible = (d >= 0) & (d < WINDOW)
    o_win = _attend(qs, wk_ref, wv_ref, start, 1, span, lambda c, j: visible[:, j * LANES:(j + 1) * LANES], n_heads)

    gates = jax.nn.sigmoid(gate_ref[...])
    gates = jnp.where(grp == 0, gates, pltpu.roll(gates, LANES - n_heads, axis=1))
    n_all = N_KV_A * n_heads
    for r in range(n_heads):
        o = (gates[:, r:r + 1] * o_cmp[r] + gates[:, n_all + r:n_all + r + 1] * o_slc[r]
             + gates[:, 2 * n_all + r:2 * n_all + r + 1] * o_win[r])
        o_ref[:, r * HEAD_DIM:(r + 1) * HEAD_DIM] = o.astype(o_ref.dtype)


def nsa_prompt(proj, kvc, lay, n_batch, seq, tq, chunk=512):
    n_heads = lay.n_heads_a // N_KV_A
    n_ch = kvc.shape[-2]
    n_s = -(-seq // SEL_BLOCK)
    assert tq == LANES and n_s <= LANES and seq % chunk == 0 and chunk % tq == 0
    assert WINDOW % tq == 0 and seq >= WINDOW + tq and lay.ga % LANES == 0 and N_KV_A == 2
    nq = seq // tq
    qw = n_heads * HEAD_DIM

    def tile(col):
        c0 = col // HEAD_DIM
        return pl.BlockSpec((tq, HEAD_DIM), lambda b, g, i: (b * nq + i, c0 + g))

    return pl.pallas_call(
        functools.partial(_nsa_prompt_kernel, n_heads=n_heads, n_s=n_s, chunk=chunk),
        grid=(n_batch, N_KV_A, nq),
        in_specs=[pl.BlockSpec((tq, qw), lambda b, g, i: (b * nq + i, lay.qa // qw + g)),
                  pl.BlockSpec((1, 1, 1, n_ch, HEAD_DIM), lambda b, g, i: (b, 0, g, 0, 0)),
                  pl.BlockSpec((1, 1, 1, n_ch, HEAD_DIM), lambda b, g, i: (b, 1, g, 0, 0)),
                  tile(lay.selk), tile(lay.selv), tile(lay.wink), tile(lay.winv),
                  pl.BlockSpec((tq, LANES), lambda b, g, i: (b * nq + i, lay.ga // LANES)),
                  pl.BlockSpec((n_ch, LANES), lambda b, g, i: (0, 0)),
                  pl.BlockSpec((LANES, seq), lambda b, g, i: (0, 0))],
        out_specs=pl.BlockSpec((tq, qw), lambda b, g, i: (b * nq + i, g)),
        out_shape=jax.ShapeDtypeStruct((n_batch * seq, N_KV_A * qw), _BF),
        scratch_shapes=[pltpu.VMEM((tq, seq), jnp.float32),
                        pltpu.VMEM((seq, HEAD_DIM), _BF), pltpu.VMEM((seq, 2 * HEAD_DIM), _BF),
                        pltpu.VMEM((seq, HEAD_DIM), _BF), pltpu.VMEM((seq, 2 * HEAD_DIM), _BF)],
        compiler_params=_params("parallel", "parallel", "arbitrary"),
    )(proj, kvc, kvc, proj, proj, proj, proj, proj, _cover_matrix(n_ch, LANES, n_s), _expand_matrix(LANES, seq))


def _dsa_prompt_kernel(iq_ref, iw_ref, q_ref, ik_ref, k_ref, v_ref, tri_ref, o_ref,
                       score_ref, key_ref, mask_ref, ikb_ref, kb_ref, vb_ref,
                       *, n_idx, n_heads, n_top, iw_lane, chunk):
    tq = iq_ref.shape[0]
    seq = mask_ref.shape[1]
    qi = pl.program_id(1)
    n_tiles = qi + 1
    row0 = pl.multiple_of(qi * tq, tq)
    t_col = qi * tq + lax.broadcasted_iota(jnp.int32, (tq, 1), 0)
    key_iota = lax.broadcasted_iota(jnp.int32, (tq, tq), 1)

    @pl.when(qi == 0)
    def _():
        for ref in (ikb_ref, kb_ref, vb_ref):
            ref[...] = jnp.zeros(ref.shape, ref.dtype)

    ikb_ref[pl.ds(row0, tq), :] = ik_ref[...].astype(_BF)
    kb_ref[pl.ds(row0, tq), :] = k_ref[...].astype(_BF)
    vb_ref[pl.ds(row0, tq), :] = jnp.concatenate([v_ref[...].astype(_BF), jnp.ones((tq, HEAD_DIM), _BF)], axis=1)
    iqs = _stack_heads(iq_ref[...], n_idx).astype(_BF)
    iw = iw_ref[...]

    def score_tile(kt, _):
        sl = pl.ds(pl.multiple_of(kt * tq, tq), tq)
        sc = _indexer_scores(iqs, ikb_ref[sl, :], iw, iw_lane, n_idx)
        score_ref[:, sl] = jnp.where(kt * tq + key_iota <= t_col, sc, NEG)
        return 0

    lax.fori_loop(0, n_tiles, score_tile, 0)
    mask_ref[...] = jnp.zeros(mask_ref.shape, mask_ref.dtype)
    _topk_mask(score_ref, key_ref, mask_ref, tri_ref, n_tiles, n_top)
    causal = lax.broadcasted_iota(jnp.int32, (tq, seq), 1) <= t_col
    mask_ref[...] = jnp.where(causal, mask_ref[...], 0.0)

    def dsa_mask(c, j):
        return mask_ref[:, pl.ds(pl.multiple_of(c * chunk + j * LANES, LANES), LANES)] > 0.5

    qs = _stack_heads(q_ref[...], n_heads).astype(_BF)
    o = _attend(qs, kb_ref, vb_ref, 0, (row0 + tq + chunk - 1) // chunk, chunk, dsa_mask, n_heads)
    for h in range(n_heads):
        o_ref[:, h * HEAD_DIM:(h + 1) * HEAD_DIM] = o[h].astype(o_ref.dtype)


def dsa_prompt(proj, lay, n_batch, seq, tq, chunk=512):
    assert tq == LANES and seq % chunk == 0 and chunk % tq == 0
    nq = seq // tq
    n_top = min(DSA_TOPK, seq // 4)
    iqw = lay.n_idx_heads * IDX_DIM
    qw = lay.n_heads_b * HEAD_DIM

    def tile(col):
        c0 = col // HEAD_DIM
        return pl.BlockSpec((tq, HEAD_DIM), lambda b, i: (b * nq + i, c0))

    return pl.pallas_call(
        functools.partial(_dsa_prompt_kernel, n_idx=lay.n_idx_heads, n_heads=lay.n_heads_b, n_top=n_top,
                          iw_lane=lay.iw % LANES, chunk=chunk),
        grid=(n_batch, nq),
        in_specs=[pl.BlockSpec((tq, iqw), lambda b, i: (b * nq + i, lay.iq // iqw)),
                  pl.BlockSpec((tq, LANES), lambda b, i: (b * nq + i, lay.iw // LANES)),
                  pl.BlockSpec((tq, qw), lambda b, i: (b * nq + i, lay.qb // qw)),
                  tile(lay.ik), tile(lay.dk), tile(lay.dv),
                  pl.BlockSpec((LANES, LANES), lambda b, i: (0, 0))],
        out_specs=pl.BlockSpec((tq, qw), lambda b, i: (b * nq + i, 0)),
        out_shape=jax.ShapeDtypeStruct((n_batch * seq, qw), _BF),
        scratch_shapes=[pltpu.VMEM((tq, seq), jnp.float32), pltpu.VMEM((tq, seq), jnp.int32),
                        pltpu.VMEM((tq, seq), jnp.float32),
                        pltpu.VMEM((seq, IDX_DIM), _BF), pltpu.VMEM((seq, HEAD_DIM), _BF),
                        pltpu.VMEM((seq, 2 * HEAD_DIM), _BF)],
        compiler_params=_params("parallel", "arbitrary"),
    )(proj, proj, proj, proj, proj, proj, _tri_matrix())


def _page_specs(block, second_block=0):
    tail = (0,) * (len(block) - 3)
    def spec(k):
        return pl.BlockSpec(block, lambda b, s, pt: (pt[b, s * PAGES_PER_STEP + k], 0, second_block) + tail)
    return [spec(k) for k in range(PAGES_PER_STEP)]


def _compress_sample_kernel(pt_ref, *refs):
    pages = refs[:PAGES_PER_STEP]
    w1_ref, w2_ref, pe_ref, cos_ref, sin_ref, o_ref, rows_ref = refs[PAGES_PER_STEP:]
    step = pl.program_id(1)
    page = pages[0].shape[1]
    n_slabs = rows_ref.shape[0]
    for k in range(PAGES_PER_STEP):
        start = pl.multiple_of((step * PAGES_PER_STEP + k) * page, page)
        for c in range(n_slabs):
            rows_ref.at[c][pl.ds(start, page), :] = pages[k][0, :, c // N_KV_A, c % N_KV_A, :]

    @pl.when(step == pl.num_programs(1) - 1)
    def _():
        n_ch = o_ref.shape[-2]
        for kv in range(2):
            for g in range(N_KV_A):
                c = kv * N_KV_A + g
                y = _compress_rows(lambda s: rows_ref.at[c][pl.ds(s, n_ch, stride=CMP_STRIDE), :],
                                   pe_ref[kv], lambda s: w1_ref[kv, s], w2_ref[kv], n_ch)
                if kv == 0:
                    y = _rotary(y, cos_ref[...], sin_ref[...])
                o_ref[0, kv, g] = y.astype(o_ref.dtype)


def compress_sample(cache, page_table, w1, w2, pe, cos_end, sin_end):
    n_batch, n_pages = page_table.shape
    page = cache.shape[1]
    n_ch = n_pages * page // CMP_STRIDE
    assert n_pages % PAGES_PER_STEP == 0
    const = lambda nd: (lambda b, s, pt: (0,) * nd)
    grid_spec = pltpu.PrefetchScalarGridSpec(
        num_scalar_prefetch=1,
        grid=(n_batch, n_pages // PAGES_PER_STEP),
        in_specs=_page_specs((1, page, 2, N_KV_A, HEAD_DIM), 0) + [
            pl.BlockSpec(w1.shape, const(4)), pl.BlockSpec(w2.shape, const(3)), pl.BlockSpec(pe.shape, const(3)),
            pl.BlockSpec((n_ch, HEAD_DIM), const(2)), pl.BlockSpec((n_ch, HEAD_DIM), const(2))],
        out_specs=pl.BlockSpec((1, 2, N_KV_A, n_ch, HEAD_DIM), lambda b, s, pt: (b, 0, 0, 0, 0)),
        scratch_shapes=[pltpu.VMEM((2 * N_KV_A, n_pages * page, HEAD_DIM), jnp.float32)],
    )
    return pl.pallas_call(
        _compress_sample_kernel,
        grid_spec=grid_spec,
        out_shape=jax.ShapeDtypeStruct((n_batch, 2, N_KV_A, n_ch, HEAD_DIM), _BF),
        compiler_params=_params("parallel", "arbitrary"),
    )(page_table, *([cache] * PAGES_PER_STEP), w1, w2, pe, cos_end, sin_end)


def _nsa_sample_kernel(pt_ref, *refs, n_heads, n_s, past_len):
    pages = refs[:PAGES_PER_STEP]
    (q_ref, gate_ref, kvc_ref, nsk_ref, nsv_ref, nwk_ref, nwv_ref, win_ref, cover_ref, expand_ref,
     o_ref, m_ref, l_ref, acc_ref, ocmp_ref, sel_ref) = refs[PAGES_PER_STEP:]
    step = pl.program_id(1)
    ts = q_ref.shape[0]
    qw = n_heads * HEAD_DIM
    n_grp = N_KV_A
    t_col = past_len + lax.broadcasted_iota(jnp.int32, (ts, 1), 0)

    def queries(g):
        return _stack_heads(q_ref[:, g * qw:(g + 1) * qw], n_heads).astype(_BF)

    def group_lanes(x, g):
        return x[:, g * HEAD_DIM:(g + 1) * HEAD_DIM]

    @pl.when(step == 0)
    def _():
        for g in range(n_grp):
            o_cmp, sel = _cmp_attend_and_choose(queries(g), kvc_ref[0, 0, g], kvc_ref[0, 1, g], cover_ref[...],
                                                t_col, n_heads, n_s)
            ocmp_ref[g] = o_cmp
            sel_ref[g] = jnp.where(sel, 1.0, 0.0)
            m_ref[g], l_ref[g], acc_ref[g] = _flash_init(n_heads, ts)

    for g in range(n_grp):
        chosen = jnp.dot(sel_ref[g].astype(_BF), expand_ref[...], preferred_element_type=jnp.float32) > 0.5
        k = jnp.concatenate([p[0, :, 0, g, :] for p in pages], axis=0).astype(_BF)
        v = jnp.concatenate([p[0, :, 1, g, :] for p in pages], axis=0).astype(_BF)
        m_ref[g], l_ref[g], acc_ref[g] = _flash_update(queries(g), k, v, chosen,
                                                       (m_ref[g], l_ref[g], acc_ref[g]), n_heads)

    @pl.when(step == pl.num_programs(1) - 1)
    def _():
        gates = jax.nn.sigmoid(gate_ref[...])
        n_all = n_grp * n_heads
        w_len = win_ref.shape[1]
        row = lax.broadcasted_iota(jnp.int32, (ts, LANES), 0)
        lane = lax.broadcasted_iota(jnp.int32, (ts, LANES), 1)
        new_causal = (lane <= row) & (lane < ts)
        wlane = lax.broadcasted_iota(jnp.int32, (ts, w_len + LANES), 1)
        k_pos = past_len - w_len + wlane
        d = t_col - k_pos
        win_mask = (d >= 0) & (d < WINDOW) & (k_pos >= 0) & (wlane < w_len + ts)
        new_blk = past_len // SEL_BLOCK
        for g in range(n_grp):
            qs = queries(g)
            chosen = sel_ref[g][:, new_blk:new_blk + 1] > 0.5
            k_new = _pad_rows(group_lanes(nsk_ref[...], g), LANES).astype(_BF)
            v_new = _pad_rows(group_lanes(nsv_ref[...], g), LANES).astype(_BF)
            o_slc = _flash_finish(_flash_update(qs, k_new, v_new, new_causal & chosen,
                                                (m_ref[g], l_ref[g], acc_ref[g]), n_heads))
            kw = jnp.concatenate([win_ref[0, :, 0, g, :], _pad_rows(group_lanes(nwk_ref[...], g), LANES)],
                                 axis=0).astype(_BF)
            vw = jnp.concatenate([win_ref[0, :, 1, g, :], _pad_rows(group_lanes(nwv_ref[...], g), LANES)],
                                 axis=0).astype(_BF)
            o_win = _flash_finish(_flash_update(qs, kw, vw, win_mask, _flash_init(n_heads, ts), n_heads))
            o_cmp = ocmp_ref[g]
            for r in range(n_heads):
                h = g * n_heads + r
                o = (gates[:, h:h + 1] * o_cmp[r] + gates[:, n_all + h:n_all + h + 1] * o_slc[r]
                     + gates[:, 2 * n_all + h:2 * n_all + h + 1] * o_win[r])
                o_ref[0, :, h * HEAD_DIM:(h + 1) * HEAD_DIM] = o.astype(o_ref.dtype)


def nsa_sample(proj, row0, kvc, cache, win_buf, page_table, lay, ts):
    n_batch, n_pages = page_table.shape
    page = cache.shape[1]
    past_len = n_pages * page
    n_heads = lay.n_heads_a // N_KV_A
    n_ch = kvc.shape[-2]
    n_s = -(-(past_len + ts) // SEL_BLOCK)
    sel_lanes = -(-n_s // LANES) * LANES
    kvw = N_KV_A * HEAD_DIM
    assert n_pages % PAGES_PER_STEP == 0 and row0 % ts == 0 and ts <= SEL_BLOCK and past_len % SEL_BLOCK == 0
    assert ts % 8 == 0 and lay.ga % LANES == 0
    assert (past_len + ts - CMP_BLOCK) // CMP_STRIDE + 1 <= n_ch and n_ch * CMP_STRIDE <= past_len
    r0 = row0 // ts
    keys_per_step = PAGES_PER_STEP * page
    qw = lay.n_heads_a * HEAD_DIM
    rows = lambda width, col: pl.BlockSpec((ts, width), lambda b, s, pt: (r0 + b, col // width))
    const = lambda nd: (lambda b, s, pt: (0,) * nd)
    grid_spec = pltpu.PrefetchScalarGridSpec(
        num_scalar_prefetch=1,
        grid=(n_batch, n_pages // PAGES_PER_STEP),
        in_specs=_page_specs((1, page, 2, N_KV_A, HEAD_DIM), 1) + [
            rows(qw, lay.qa), rows(LANES, lay.ga),
            pl.BlockSpec((1, 2, N_KV_A, n_ch, HEAD_DIM), lambda b, s, pt: (b, 0, 0, 0, 0)),
            rows(kvw, lay.selk), rows(kvw, lay.selv), rows(kvw, lay.wink), rows(kvw, lay.winv),
            pl.BlockSpec((1,) + win_buf.shape[1:], lambda b, s, pt: (b, 0, 0, 0, 0)),
            pl.BlockSpec((n_ch, sel_lanes), const(2)),
            pl.BlockSpec((sel_lanes, keys_per_step), lambda b, s, pt: (0, s))],
        out_specs=pl.BlockSpec((1, ts, qw), lambda b, s, pt: (b, 0, 0)),
        scratch_shapes=[pltpu.VMEM((N_KV_A, n_heads, ts, 1), jnp.float32),
                        pltpu.VMEM((N_KV_A, n_heads, ts, 1), jnp.float32),
                        pltpu.VMEM((N_KV_A, n_heads, ts, HEAD_DIM), jnp.float32),
                        pltpu.VMEM((N_KV_A, n_heads, ts, HEAD_DIM), jnp.float32),
                        pltpu.VMEM((N_KV_A, ts, sel_lanes), jnp.float32)],
    )
    return pl.pallas_call(
        functools.partial(_nsa_sample_kernel, n_heads=n_heads, n_s=n_s, past_len=past_len),
        grid_spec=grid_spec,
        out_shape=jax.ShapeDtypeStruct((n_batch, ts, qw), _BF),
        compiler_params=_params("parallel", "arbitrary"),
    )(page_table, *([cache] * PAGES_PER_STEP), proj, proj, kvc, proj, proj, proj, proj, win_buf,
      _cover_matrix(n_ch, sel_lanes, n_s), _expand_matrix(sel_lanes, past_len))


def _dsa_sample_select_kernel(pt_ref, *refs, n_idx, n_top, iw_lane, past_len):
    pages = refs[:PAGES_PER_STEP]
    iq_ref, iw_ref, nik_ref, tri_ref, mask_ref, score_ref, key_ref = refs[PAGES_PER_STEP:]
    step = pl.program_id(1)
    ts = iq_ref.shape[0]
    keys_per_step = PAGES_PER_STEP * pages[0].shape[1]
    iqs = _stack_heads(iq_ref[...], n_idx).astype(_BF)
    iw = iw_ref[...]
    ik = jnp.concatenate([p[0] for p in pages], axis=0).astype(_BF)
    start = pl.multiple_of(step * keys_per_step, keys_per_step)
    score_ref[:, pl.ds(start, keys_per_step)] = _indexer_scores(iqs, ik, iw, iw_lane, n_idx)

    @pl.when(step == pl.num_programs(1) - 1)
    def _():
        sc = _indexer_scores(iqs, _pad_rows(nik_ref[...], LANES).astype(_BF), iw, iw_lane, n_idx)
        row = lax.broadcasted_iota(jnp.int32, (ts, LANES), 0)
        lane = lax.broadcasted_iota(jnp.int32, (ts, LANES), 1)
        score_ref[:, past_len:past_len + LANES] = jnp.where(lane < ts, jnp.where(lane <= row, sc, NEG), -jnp.inf)
        _topk_mask(score_ref, key_ref, mask_ref.at[0], tri_ref, past_len // LANES + 1, n_top)


def dsa_sample_select(proj, row0, cache_idx, page_table, lay, ts):
    n_batch, n_pages = page_table.shape
    page = cache_idx.shape[1]
    past_len = n_pages * page
    assert n_pages % PAGES_PER_STEP == 0 and row0 % ts == 0 and ts <= LANES and past_len % LANES == 0
    n_top = min(DSA_TOPK, (past_len + ts) // 4)
    r0 = row0 // ts
    iqw = lay.n_idx_heads * IDX_DIM
    width = past_len + LANES
    rows = lambda w, col: pl.BlockSpec((ts, w), lambda b, s, pt: (r0 + b, col // w))
    grid_spec = pltpu.PrefetchScalarGridSpec(
        num_scalar_prefetch=1,
        grid=(n_batch, n_pages // PAGES_PER_STEP),
        in_specs=_page_specs((1, page, IDX_DIM), 0) + [
            rows(iqw, lay.iq), rows(LANES, lay.iw), rows(IDX_DIM, lay.ik),
            pl.BlockSpec((LANES, LANES), lambda b, s, pt: (0, 0))],
        out_specs=pl.BlockSpec((1, ts, width), lambda b, s, pt: (b, 0, 0)),
        scratch_shapes=[pltpu.VMEM((ts, width), jnp.float32), pltpu.VMEM((ts, width), jnp.int32)],
    )
    return pl.pallas_call(
        functools.partial(_dsa_sample_select_kernel, n_idx=lay.n_idx_heads, n_top=n_top,
                          iw_lane=lay.iw % LANES, past_len=past_len),
        grid_spec=grid_spec,
        out_shape=jax.ShapeDtypeStruct((n_batch, ts, width), jnp.float32),
        compiler_params=_params("parallel", "arbitrary"),
    )(page_table, *([cache_idx] * PAGES_PER_STEP), proj, proj, proj, _tri_matrix())


def _dsa_sample_attend_kernel(pt_ref, *refs, n_heads):
    pages = refs[:PAGES_PER_STEP]
    q_ref, mask_ref, nmask_ref, nk_ref, nv_ref, o_ref, m_ref, l_ref, acc_ref = refs[PAGES_PER_STEP:]
    step = pl.program_id(1)
    ts = q_ref.shape[0]
    qs = _stack_heads(q_ref[...], n_heads).astype(_BF)

    @pl.when(step == 0)
    def _():
        m_ref[...], l_ref[...], acc_ref[...] = _flash_init(n_heads, ts)

    k = jnp.concatenate([p[0, :, 0, :] for p in pages], axis=0).astype(_BF)
    v = jnp.concatenate([p[0, :, 1, :] for p in pages], axis=0).astype(_BF)
    m_ref[...], l_ref[...], acc_ref[...] = _flash_update(qs, k, v, mask_ref[0] > 0.5,
                                                         (m_ref[...], l_ref[...], acc_ref[...]), n_heads)

    @pl.when(step == pl.num_programs(1) - 1)
    def _():
        row = lax.broadcasted_iota(jnp.int32, (ts, LANES), 0)
        lane = lax.broadcasted_iota(jnp.int32, (ts, LANES), 1)
        mask = (nmask_ref[0] > 0.5) & (lane <= row) & (lane < ts)
        o = _flash_finish(_flash_update(qs, _pad_rows(nk_ref[...], LANES).astype(_BF),
                                        _pad_rows(nv_ref[...], LANES).astype(_BF), mask,
                                        (m_ref[...], l_ref[...], acc_ref[...]), n_heads))
        for h in range(n_heads):
            o_ref[0, :, h * HEAD_DIM:(h + 1) * HEAD_DIM] = o[h].astype(o_ref.dtype)


def dsa_sample_attend(proj, row0, mask, cache_kv, page_table, lay, ts):
    n_batch, n_pages = page_table.shape
    page = cache_kv.shape[1]
    past_len = n_pages * page
    keys_per_step = PAGES_PER_STEP * page
    r0 = row0 // ts
    qw = lay.n_heads_b * HEAD_DIM
    rows = lambda w, col: pl.BlockSpec((ts, w), lambda b, s, pt: (r0 + b, col // w))
    grid_spec = pltpu.PrefetchScalarGridSpec(
        num_scalar_prefetch=1,
        grid=(n_batch, n_pages // PAGES_PER_STEP),
        in_specs=_page_specs((1, page, 2, HEAD_DIM), 0) + [
            rows(qw, lay.qb),
            pl.BlockSpec((1, ts, keys_per_step), lambda b, s, pt: (b, 0, s)),
            pl.BlockSpec((1, ts, LANES), lambda b, s, pt: (b, 0, past_len // LANES)),
            rows(HEAD_DIM, lay.dk), rows(HEAD_DIM, lay.dv)],
        out_specs=pl.BlockSpec((1, ts, qw), lambda b, s, pt: (b, 0, 0)),
        scratch_shapes=[pltpu.VMEM((lay.n_heads_b, ts, 1), jnp.float32),
                        pltpu.VMEM((lay.n_heads_b, ts, 1), jnp.float32),
                        pltpu.VMEM((lay.n_heads_b, ts, HEAD_DIM), jnp.float32)],
    )
    return pl.pallas_call(
        functools.partial(_dsa_sample_attend_kernel, n_heads=lay.n_heads_b),
        grid_spec=grid_spec,
        out_shape=jax.ShapeDtypeStruct((n_batch, ts, qw), _BF),
        compiler_params=_params("parallel", "arbitrary"),
    )(page_table, *([cache_kv] * PAGES_PER_STEP), proj, mask, mask, proj, proj)


class Layout:
    def __init__(self, d_model, n_heads_a, n_heads_b, n_idx_heads):
        self.n_heads_a, self.n_heads_b, self.n_idx_heads = n_heads_a, n_heads_b, n_idx_heads
        src = np.cumsum([0, n_heads_a * HEAD_DIM, 6 * N_KV_A * HEAD_DIM, 3 * n_heads_a,
                         n_heads_b * HEAD_DIM, 2 * HEAD_DIM, n_idx_heads * IDX_DIM, n_idx_heads,
                         IDX_DIM, 2 * d_model])
        s_qa, s_kva, s_ga, s_qb, s_kvb, s_iq, s_iw, s_ik, s_mg, s_end = (int(v) for v in src)
        small = 3 * n_heads_a + n_idx_heads
        assert small <= LANES
        self.small_pad = LANES - small
        kvw = N_KV_A * HEAD_DIM
        self.pieces = [
            (s_qa, s_kva, True),
            (s_qb, s_kvb, True),
            (s_iq, s_iw, True),
            (s_kva, s_kva + 2 * kvw, False),
            (s_kva + 2 * kvw, s_kva + 3 * kvw, True),
            (s_kva + 3 * kvw, s_kva + 4 * kvw, False),
            (s_kva + 4 * kvw, s_kva + 5 * kvw, True),
            (s_kva + 5 * kvw, s_ga, False),
            (s_kvb, s_kvb + HEAD_DIM, True),
            (s_kvb + HEAD_DIM, s_iq, False),
            (s_ik, s_mg, True),
            (s_ga, s_qb, False),
            (s_iw, s_ik, False),
            None,
            (s_mg, s_end, False),
        ]
        off = 0
        starts = []
        for p in self.pieces:
            starts.append(off)
            off += self.small_pad if p is None else p[1] - p[0]
        (self.qa, self.qb, self.iq, self.cmp, self.selk, self.selv, self.wink, self.winv, self.dk,
         self.dv, self.ik, self.ga, self.iw, _, self.mg) = starts
        self.width = off
        assert self.width % LANES == 0
        flags = np.zeros(self.width // LANES, np.int32)
        for st, p in zip(starts, self.pieces):
            if p is not None and p[2]:
                assert st % LANES == 0 and (p[1] - p[0]) % LANES == 0
                flags[st // LANES:(st + p[1] - p[0]) // LANES] = 1
        self.rope_flags = flags

    def pack(self, w_in):
        cols = []
        for p in self.pieces:
            if p is None:
                cols.append(jnp.zeros((w_in.shape[0], self.small_pad), w_in.dtype))
            else:
                cols.append(w_in[:, p[0]:p[1]])
        return jnp.concatenate(cols, axis=1).astype(_BF)


def rope_tables(pos):
    half = HEAD_DIM // 2
    inv = ROPE_THETA ** (-jnp.arange(half, dtype=jnp.float32) / half)
    ang = pos.astype(jnp.float32)[:, None] * inv[None, :]
    cos, sin = jnp.cos(ang), jnp.sin(ang)
    return jnp.concatenate([cos, cos], axis=1), jnp.concatenate([-sin, sin], axis=1)


def _swiglu_half_step(h, g, wg, wu, wd):
    xn = rmsnorm(h, g, _BF)
    a = ffn_gate_up(xn, wg, wu, tm=1408, tn=256)
    return resid_matmul(a, wd, h, 0.5, tm=768, tn=256)


def kernel(x_prompt, x_sample, cache_nsa_kv, cache_nsa_win, cache_dsa_kv, cache_dsa_idx, page_table,
           g_norm, w_ffn_gate, w_ffn_up, w_ffn_down, w_in, w_cmp1, w_cmp2, cmp_pos,
           w_br_a, w_br_b, w_out, g_final):
    B, T, D = x_prompt.shape
    DB, Ts, _ = x_sample.shape
    depth = g_norm.shape[0]
    page = cache_nsa_kv.shape[2]
    past_len = page_table.shape[1] * page
    n_heads_a = w_br_a.shape[1] // HEAD_DIM
    n_heads_b = w_br_b.shape[1] // HEAD_DIM
    G = N_KV_A
    lay = Layout(D, n_heads_a, n_heads_b, n_heads_b // 2)
    Mp, Ms = B * T, DB * Ts

    pos_p = jnp.arange(T, dtype=jnp.int32)
    pos_s = past_len + jnp.arange(Ts, dtype=jnp.int32)
    cos, sin = rope_tables(jnp.concatenate([jnp.tile(pos_p, B), jnp.tile(pos_s, DB)]))
    rope_flags = jnp.asarray(lay.rope_flags)

    def block_end_tables(n_ch):
        return rope_tables(jnp.arange(n_ch, dtype=jnp.int32) * CMP_STRIDE + CMP_BLOCK - 1)

    h = jnp.concatenate([x_prompt.reshape(Mp, D), x_sample.reshape(Ms, D)], axis=0)
    outs = [[] for _ in range(8)]
    for l in range(depth):
        h = _swiglu_half_step(h, g_norm[l, 0], w_ffn_gate[l, 0], w_ffn_up[l, 0], w_ffn_down[l, 0])

        u = rmsnorm(h, g_norm[l, 1], _BF)
        proj = in_project(u, lay.pack(w_in[l]), rope_flags, cos, sin, tm=1408, tn=512)
        w1 = w_cmp1[l].reshape(2, CMP_BLOCK, HEAD_DIM, -1).astype(_BF)
        w2 = w_cmp2[l].astype(_BF)

        kvc_p = compress_prompt(proj, B, T, lay.cmp, w1, w2, cmp_pos[l], *block_end_tables(T // CMP_STRIDE))
        o_a_p = nsa_prompt(proj, kvc_p, lay, B, T, tq=LANES)
        o_b_p = dsa_prompt(proj, lay, B, T, tq=LANES)

        kvc_s = compress_sample(cache_nsa_kv[l], page_table, w1, w2, cmp_pos[l],
                                *block_end_tables(past_len // CMP_STRIDE))
        o_a_s = nsa_sample(proj, Mp, kvc_s, cache_nsa_kv[l], cache_nsa_win[l], page_table, lay, Ts)
        top_mask = dsa_sample_select(proj, Mp, cache_dsa_idx[l], page_table, lay, Ts)
        o_b_s = dsa_sample_attend(proj, Mp, top_mask, cache_dsa_kv[l], page_table, lay, Ts)

        o_a = jnp.concatenate([o_a_p, o_a_s.reshape(Ms, -1)], axis=0)
        o_b = jnp.concatenate([o_b_p, o_b_s.reshape(Ms, -1)], axis=0)
        m = merge_branches(o_a, o_b, w_br_a[l], w_br_b[l], proj, lay.mg, tm=768, tn=512)
        h = resid_matmul(m, w_out[l], h, 1.0, tm=768, tn=512)

        nsa_rows = proj[:, lay.cmp:lay.cmp + 4 * G * HEAD_DIM]
        win_rows = proj[:, lay.wink:lay.wink + 2 * G * HEAD_DIM]
        dsa_rows = proj[:, lay.dk:lay.dk + 2 * HEAD_DIM]
        ik_rows = proj[:, lay.ik:lay.ik + IDX_DIM]
        win_p = win_rows[:Mp].reshape(B, T, 2, G, HEAD_DIM)
        win_s = win_rows[Mp:].reshape(DB, Ts, 2, G, HEAD_DIM)
        outs[0].append(nsa_rows[:Mp].reshape(B, T, 4, G, HEAD_DIM))
        outs[1].append(win_p[:, T - min(WINDOW, T):])
        outs[2].append(dsa_rows[:Mp].reshape(B, T, 2, HEAD_DIM))
        outs[3].append(ik_rows[:Mp].reshape(B, T, IDX_DIM))
        outs[4].append(nsa_rows[Mp:].reshape(DB, Ts, 4, G, HEAD_DIM))
        outs[5].append(jnp.concatenate([cache_nsa_win[l], win_s], axis=1)[:, Ts:])
        outs[6].append(dsa_rows[Mp:].reshape(DB, Ts, 2, HEAD_DIM))
        outs[7].append(ik_rows[Mp:].reshape(DB, Ts, IDX_DIM))

        h = _swiglu_half_step(h, g_norm[l, 2], w_ffn_gate[l, 1], w_ffn_up[l, 1], w_ffn_down[l, 1])

    y = rmsnorm(h, g_final, jnp.float32)
    return (y[:Mp].reshape(B, T, D), y[Mp:].reshape(DB, Ts, D), *(jnp.stack(o) for o in outs))
```

```python
import functools

import jax
import jax.numpy as jnp
import numpy as np
from jax import lax
from jax.experimental import pallas as pl
from jax.experimental.pallas import tpu as pltpu

HEAD_DIM = 128
N_KV_A = 2
IDX_DIM = 128
CMP_BLOCK = 32
CMP_STRIDE = 16
SEL_BLOCK = 64
N_SEL = 16
N_LOCAL_SEL = 2
WINDOW = 512
DSA_TOPK = 256
ROPE_THETA = 10000.0
RMS_EPS = 1e-6
NEG = -1e30
BIG = 1e30
TINY = 1e-30

LANES = 128
VMEM_LIMIT = 56 * 1024 * 1024
PAGES_PER_STEP = 16

_NT = (((1,), (1,)), ((), ()))
_BF = jnp.bfloat16


def _params(*sem):
    return pltpu.CompilerParams(dimension_semantics=sem, vmem_limit_bytes=VMEM_LIMIT)


def _rmsnorm_kernel(x_ref, g_ref, o_ref):
    x = x_ref[...]
    y = x * lax.rsqrt(jnp.mean(x * x, axis=-1, keepdims=True) + RMS_EPS)
    o_ref[...] = (y * g_ref[...]).astype(o_ref.dtype)


def rmsnorm(x, g, out_dtype, row0=0, n_rows=None, tm=256):
    D = x.shape[1]
    n_rows = x.shape[0] - row0 if n_rows is None else n_rows
    assert row0 % tm == 0 and n_rows % tm == 0
    return pl.pallas_call(
        _rmsnorm_kernel,
        grid=(n_rows // tm,),
        in_specs=[pl.BlockSpec((tm, D), lambda i: (row0 // tm + i, 0)),
                  pl.BlockSpec((1, D), lambda i: (0, 0))],
        out_specs=pl.BlockSpec((tm, D), lambda i: (i, 0)),
        out_shape=jax.ShapeDtypeStruct((n_rows, D), out_dtype),
        compiler_params=_params("parallel"),
    )(x, g.reshape(1, D))


def _gateup_kernel(x_ref, wg_ref, wu_ref, o_ref):
    x = x_ref[...]
    g = jnp.dot(x, wg_ref[...].astype(_BF), preferred_element_type=jnp.float32)
    u = jnp.dot(x, wu_ref[...].astype(_BF), preferred_element_type=jnp.float32)
    o_ref[...] = (g * jax.nn.sigmoid(g) * u).astype(o_ref.dtype)


def ffn_gate_up(xn, wg, wu, tm, tn):
    M, D = xn.shape
    F = wg.shape[1]
    return pl.pallas_call(
        _gateup_kernel,
        grid=(M // tm, F // tn),
        in_specs=[pl.BlockSpec((tm, D), lambda i, j: (i, 0)),
                  pl.BlockSpec((D, tn), lambda i, j: (0, j)),
                  pl.BlockSpec((D, tn), lambda i, j: (0, j))],
        out_specs=pl.BlockSpec((tm, tn), lambda i, j: (i, j)),
        out_shape=jax.ShapeDtypeStruct((M, F), _BF),
        compiler_params=_params("parallel", "parallel"),
    )(xn, wg, wu)


def _resid_matmul_kernel(a_ref, w_ref, r_ref, o_ref, *, scale):
    acc = jnp.dot(a_ref[...], w_ref[...].astype(_BF), preferred_element_type=jnp.float32)
    o_ref[...] = r_ref[...] + scale * acc


def resid_matmul(a, w, resid, scale, tm, tn):
    M, K = a.shape
    N = w.shape[1]
    return pl.pallas_call(
        functools.partial(_resid_matmul_kernel, scale=scale),
        grid=(M // tm, N // tn),
        in_specs=[pl.BlockSpec((tm, K), lambda i, j: (i, 0), pipeline_mode=pl.Buffered(1)),
                  pl.BlockSpec((K, tn), lambda i, j: (0, j)),
                  pl.BlockSpec((tm, tn), lambda i, j: (i, j))],
        out_specs=pl.BlockSpec((tm, tn), lambda i, j: (i, j)),
        out_shape=jax.ShapeDtypeStruct((M, N), jnp.float32),
        compiler_params=_params("parallel", "parallel"),
    )(a, w, resid)


def _rotary(y, cos, sin):
    return y * cos + pltpu.roll(y, HEAD_DIM // 2, axis=1) * sin


def _inproj_kernel(flags_ref, x_ref, w_ref, cos_ref, sin_ref, o_ref, *row_refs, n_chunks, routes):
    j = pl.program_id(1)
    acc = jnp.dot(x_ref[...], w_ref[...], preferred_element_type=jnp.float32)
    for c in range(n_chunks):
        sl = slice(c * LANES, (c + 1) * LANES)
        y = acc[:, sl]
        flag = flags_ref[j * n_chunks + c]

        @pl.when(flag == 1)
        def _():
            o_ref[:, sl] = _rotary(y, cos_ref[...], sin_ref[...])

        @pl.when(flag == 0)
        def _():
            o_ref[:, sl] = y

    for chunk, out_idx, index in routes:
        @pl.when(j == chunk // n_chunks)
        def _():
            c = chunk % n_chunks
            row_refs[out_idx][(slice(None),) + index + (slice(None),)] = o_ref[:, c * LANES:(c + 1) * LANES]


def in_project(u, w, rope_flags, cos, sin, lay, tm, tn):
    M, D = u.shape
    N = w.shape[1]
    n_chunks = tn // LANES
    G = N_KV_A
    routes = ([(lay.cmp // LANES + k, 0, (k // G, k % G)) for k in range(4 * G)]
              + [(lay.wink // LANES + k, 1, (k // G, k % G)) for k in range(2 * G)]
              + [(lay.dk // LANES + k, 2, (k,)) for k in range(2)]
              + [(lay.ik // LANES, 3, ())])
    row_shapes = [(M, 4, G, HEAD_DIM), (M, 2, G, HEAD_DIM), (M, 2, HEAD_DIM), (M, IDX_DIM)]
    grid_spec = pltpu.PrefetchScalarGridSpec(
        num_scalar_prefetch=1,
        grid=(M // tm, N // tn),
        in_specs=[pl.BlockSpec((tm, D), lambda i, j, f: (i, 0), pipeline_mode=pl.Buffered(1)),
                  pl.BlockSpec((D, tn), lambda i, j, f: (0, j)),
                  pl.BlockSpec((tm, LANES), lambda i, j, f: (i, 0)),
                  pl.BlockSpec((tm, LANES), lambda i, j, f: (i, 0))],
        out_specs=[pl.BlockSpec((tm, tn), lambda i, j, f: (i, j))]
        + [pl.BlockSpec((tm,) + shp[1:], lambda i, j, f, nd=len(shp): (i,) + (0,) * (nd - 1)) for shp in row_shapes],
    )
    return pl.pallas_call(
        functools.partial(_inproj_kernel, n_chunks=n_chunks, routes=routes),
        grid_spec=grid_spec,
        out_shape=[jax.ShapeDtypeStruct((M, N), jnp.float32)]
        + [jax.ShapeDtypeStruct(shp, jnp.float32) for shp in row_shapes],
        compiler_params=_params("parallel", "arbitrary"),
    )(rope_flags, u, w, cos, sin)


def _merge_kernel(oa_ref, ob_ref, wa_ref, wb_ref, ga_ref, gb_ref, o_ref):
    ya = jnp.dot(oa_ref[...], wa_ref[...].astype(_BF), preferred_element_type=jnp.float32)
    yb = jnp.dot(ob_ref[...], wb_ref[...].astype(_BF), preferred_element_type=jnp.float32)
    m = jax.nn.sigmoid(ga_ref[...]) * ya + jax.nn.sigmoid(gb_ref[...]) * yb
    o_ref[...] = m.astype(o_ref.dtype)


def merge_branches(o_a, o_b, w_a, w_b, proj, mg_col, tm, tn):
    M, K = o_a.shape
    N = w_a.shape[1]
    ja = mg_col // tn
    jb = (mg_col + N) // tn
    return pl.pallas_call(
        _merge_kernel,
        grid=(M // tm, N // tn),
        in_specs=[pl.BlockSpec((tm, K), lambda i, j: (i, 0)),
                  pl.BlockSpec((tm, K), lambda i, j: (i, 0)),
                  pl.BlockSpec((K, tn), lambda i, j: (0, j)),
                  pl.BlockSpec((K, tn), lambda i, j: (0, j)),
                  pl.BlockSpec((tm, tn), lambda i, j: (i, ja + j)),
                  pl.BlockSpec((tm, tn), lambda i, j: (i, jb + j))],
        out_specs=pl.BlockSpec((tm, tn), lambda i, j: (i, j)),
        out_shape=jax.ShapeDtypeStruct((M, N), _BF),
        compiler_params=_params("parallel", "parallel"),
    )(o_a, o_b, w_a, w_b, proj, proj)


def _stack_heads(x, n_heads):
    return jnp.concatenate([x[:, h * HEAD_DIM:(h + 1) * HEAD_DIM] for h in range(n_heads)], axis=0)


def _flash_update(qs, k, v, mask, carry, n_rep):
    m, l, acc = carry
    rows, tk = qs.shape[0], k.shape[0]
    tq = rows // n_rep
    scale = HEAD_DIM ** -0.5
    s = lax.dot_general(qs, k, _NT, preferred_element_type=jnp.float32).reshape(n_rep, tq, tk)
    mask = mask[None]
    s = jnp.where(mask, s, NEG)
    m_new = jnp.maximum(m, jnp.max(s, axis=-1, keepdims=True))
    p = jnp.where(mask, jnp.exp((s - m_new) * scale), 0.0)
    alpha = jnp.exp((m - m_new) * scale)
    l = alpha * l + jnp.sum(p, axis=-1, keepdims=True)
    pv = jnp.dot(p.reshape(rows, tk).astype(_BF), v, preferred_element_type=jnp.float32)
    return m_new, l, alpha * acc + pv.reshape(n_rep, tq, HEAD_DIM)


def _flash_init(n_rep, tq):
    return (jnp.full((n_rep, tq, 1), NEG, jnp.float32), jnp.zeros((n_rep, tq, 1), jnp.float32),
            jnp.zeros((n_rep, tq, HEAD_DIM), jnp.float32))


def _flash_finish(carry):
    _, l, acc = carry
    return acc * (1.0 / jnp.maximum(l, TINY))


def _attend(qs, k_ref, v_ref, start, n_chunks, chunk, mask_fn, n_rep):
    rows = qs.shape[0]
    tq = rows // n_rep
    n_lane_tiles = chunk // LANES
    scale = HEAD_DIM ** -0.5

    def scores(c):
        sl = pl.ds(pl.multiple_of(start + c * chunk, LANES), chunk)
        return lax.dot_general(qs, k_ref[sl, :], _NT, preferred_element_type=jnp.float32), sl

    def lane_tile(s, j):
        return s[:, j * LANES:(j + 1) * LANES].reshape(n_rep, tq, LANES)

    def row_max(c, mx):
        s, _ = scores(c)
        for j in range(n_lane_tiles):
            mx = jnp.maximum(mx, jnp.where(mask_fn(c, j)[None], lane_tile(s, j), NEG))
        return mx

    mx = lax.fori_loop(0, n_chunks, row_max, jnp.full((n_rep, tq, LANES), NEG, jnp.float32))
    m = jnp.broadcast_to(jnp.max(mx, axis=-1, keepdims=True), mx.shape)

    def accumulate(c, acc):
        s, sl = scores(c)
        p = [jnp.where(mask_fn(c, j)[None], jnp.exp((lane_tile(s, j) - m) * scale), 0.0)
             .astype(_BF).reshape(rows, LANES) for j in range(n_lane_tiles)]
        return acc + jnp.dot(jnp.concatenate(p, axis=1), v_ref[sl, :], preferred_element_type=jnp.float32)

    acc = lax.fori_loop(0, n_chunks, accumulate, jnp.zeros((rows, 2 * HEAD_DIM), jnp.float32))
    out = acc[:, :HEAD_DIM] * (1.0 / jnp.maximum(acc[:, HEAD_DIM:], TINY))
    return out.reshape(n_rep, tq, HEAD_DIM)


def _compress_rows(load, pe, w1, w2, n_ch):
    rows = [load(s) for s in range(CMP_STRIDE)]
    half = [jnp.concatenate([(rows[s] + pe[r * CMP_STRIDE + s:r * CMP_STRIDE + s + 1, :]).astype(_BF)
                             for s in range(CMP_STRIDE)], axis=1) for r in range(2)]
    h0 = jnp.dot(half[0], w1(0), preferred_element_type=jnp.float32)
    h1 = jnp.dot(half[1], w1(1), preferred_element_type=jnp.float32)
    h = h0 + pltpu.roll(h1, n_ch - 1, axis=0)
    return jnp.dot((h * jax.nn.sigmoid(h)).astype(_BF), w2, preferred_element_type=jnp.float32)


def _cmp_attend_and_choose(qs, kc, vc, cover, t_col, n_heads, n_s):
    rows = qs.shape[0]
    tq = rows // n_heads
    n_ch, lanes = cover.shape
    scale = HEAD_DIM ** -0.5
    s = lax.dot_general(qs, kc, _NT, preferred_element_type=jnp.float32).reshape(n_heads, tq, n_ch)
    end = lax.broadcasted_iota(jnp.int32, (tq, n_ch), 1) * CMP_STRIDE + (CMP_BLOCK - 1)
    cmask = (end <= t_col)[None]
    s = jnp.where(cmask, s, NEG)
    m = jnp.max(s, axis=-1, keepdims=True)
    p = jnp.where(cmask, jnp.exp((s - m) * scale), 0.0)
    p = p * (1.0 / jnp.maximum(jnp.sum(p, axis=-1, keepdims=True), TINY))
    o_cmp = jnp.dot(p.reshape(rows, n_ch).astype(_BF), vc, preferred_element_type=jnp.float32)
    imp = jnp.dot(jnp.sum(p, axis=0).astype(_BF), cover, preferred_element_type=jnp.float32)

    lane = lax.broadcasted_iota(jnp.int32, (tq, lanes), 1)
    lane_f = lane.astype(jnp.float32)
    jt = lax.shift_right_arithmetic(t_col, jnp.int32(SEL_BLOCK.bit_length() - 1))
    adm = lane <= jt
    forced = adm & ((lane == 0) | (lane > jt - N_LOCAL_SEL))
    work = jnp.where(forced, BIG, jnp.where(adm, imp, NEG))
    work = jnp.where(lane < n_s, work, -jnp.inf)
    sel = jnp.zeros((tq, lanes), jnp.bool_)
    for _ in range(min(N_SEL, n_s)):
        mx = jnp.max(work, axis=-1, keepdims=True)
        first = jnp.min(jnp.where(work == mx, lane_f, float(lanes)), axis=-1, keepdims=True)
        pick = lane_f == first
        sel = sel | pick
        work = jnp.where(pick, -jnp.inf, work)
    return o_cmp.reshape(n_heads, tq, HEAD_DIM), sel


def _topk_mask(score_ref, key_ref, mask_ref, tri_ref, n_tiles, n_top):
    tq = score_ref.shape[0]
    int_min = jnp.int32(-2 ** 31)

    def to_key(t, _):
        sl = pl.ds(pl.multiple_of(t * LANES, LANES), LANES)
        bits = lax.bitcast_convert_type(score_ref[:, sl], jnp.int32)
        key_ref[:, sl] = bits ^ ((bits >> 31) & jnp.int32(0x7FFFFFFF))
        return 0

    lax.fori_loop(0, n_tiles, to_key, 0)

    unroll = 8 if isinstance(n_tiles, int) else 1

    def count_ge(cand):
        def body(t, acc):
            sl = pl.ds(pl.multiple_of(t * LANES, LANES), LANES)
            return acc + jnp.where(key_ref[:, sl] >= cand, 1.0, 0.0)
        acc = lax.fori_loop(0, n_tiles, body, jnp.zeros((tq, LANES), jnp.float32), unroll=unroll)
        return jnp.sum(acc, axis=-1, keepdims=True)

    tau = jnp.zeros((tq, 1), jnp.int32)
    for bit in range(31, -1, -1):
        cand = tau | jnp.int32(-2 ** 31 if bit == 31 else 1 << bit)
        tau = jnp.where(count_ge(cand ^ int_min) >= float(n_top), cand, tau)
    thr = tau ^ int_min

    def count_gt(t, acc):
        sl = pl.ds(pl.multiple_of(t * LANES, LANES), LANES)
        return acc + jnp.where(key_ref[:, sl] > thr, 1.0, 0.0)

    n_gt = jnp.sum(lax.fori_loop(0, n_tiles, count_gt, jnp.zeros((tq, LANES), jnp.float32)),
                   axis=-1, keepdims=True)
    need = float(n_top) - n_gt
    clean = jnp.max(jnp.abs(count_ge(thr) - float(n_top))) == 0.0

    @pl.when(clean)
    def _():
        def keep_ge(t, _):
            sl = pl.ds(pl.multiple_of(t * LANES, LANES), LANES)
            mask_ref[:, sl] = jnp.where(key_ref[:, sl] >= thr, 1.0, 0.0)
            return 0
        lax.fori_loop(0, n_tiles, keep_ge, 0, unroll=unroll)

    @pl.when(jnp.logical_not(clean))
    def _():
        ones = jnp.ones((LANES, LANES), _BF)

        def cut(t, eq_before):
            sl = pl.ds(pl.multiple_of(t * LANES, LANES), LANES)
            key = key_ref[:, sl]
            eq = jnp.where(key == thr, 1.0, 0.0)
            rank = (jnp.dot(eq.astype(_BF), tri_ref[...], preferred_element_type=jnp.float32)
                    + jnp.dot(eq_before.astype(_BF), ones, preferred_element_type=jnp.float32))
            keep = (key > thr) | ((key == thr) & (rank <= need))
            mask_ref[:, sl] = jnp.where(keep, 1.0, 0.0)
            return eq_before + eq

        lax.fori_loop(0, n_tiles, cut, jnp.zeros((tq, LANES), jnp.float32))


def _indexer_scores(iqs, ik, iw, iw_lane, n_idx):
    tq = iqs.shape[0] // n_idx
    n = ik.shape[0]
    logits = lax.dot_general(iqs, ik, _NT, preferred_element_type=jnp.float32).reshape(n_idx, tq, n)
    logits = jnp.maximum(logits * IDX_DIM ** -0.5, 0.0)
    sc = jnp.zeros((tq, n), jnp.float32)
    for h in range(n_idx):
        sc = sc + logits[h] * iw[:, iw_lane + h:iw_lane + h + 1]
    return sc * n_idx ** -0.5


def _cover_matrix(n_ch, lanes, n_s):
    ci = np.arange(n_ch)[:, None] * CMP_STRIDE
    sj = np.arange(lanes)[None, :] * SEL_BLOCK
    return jnp.asarray((ci < sj + SEL_BLOCK) & (ci + CMP_BLOCK > sj) & (sj < n_s * SEL_BLOCK), _BF)


def _expand_matrix(lanes, n_keys):
    return jnp.asarray(np.arange(lanes)[:, None] == np.arange(n_keys)[None, :] // SEL_BLOCK, _BF)


def _tri_matrix():
    return jnp.asarray(np.arange(LANES)[:, None] <= np.arange(LANES)[None, :], _BF)


def _pad_rows(x, n):
    return jnp.concatenate([x, jnp.zeros((n - x.shape[0], x.shape[1]), x.dtype)], axis=0)


def _compress_kernel(x_ref, w1_ref, w2_ref, pe_ref, cos_ref, sin_ref, o_ref):
    n_ch = o_ref.shape[-2]
    y = _compress_rows(lambda s: x_ref[pl.ds(s, n_ch, stride=CMP_STRIDE), :], pe_ref[0],
                       lambda r: w1_ref[0, r], w2_ref[0], n_ch)

    @pl.when(pl.program_id(1) == 0)
    def _():
        o_ref[0, 0, 0] = _rotary(y, cos_ref[...], sin_ref[...]).astype(o_ref.dtype)

    @pl.when(pl.program_id(1) != 0)
    def _():
        o_ref[0, 0, 0] = y.astype(o_ref.dtype)


def compress_prompt(proj, n_batch, seq, cmp_col, w1, w2, pe, cos_end, sin_end):
    n_ch = seq // CMP_STRIDE
    col0 = cmp_col // HEAD_DIM
    return pl.pallas_call(
        _compress_kernel,
        grid=(n_batch, 2, N_KV_A),
        in_specs=[pl.BlockSpec((seq, HEAD_DIM), lambda b, kv, g: (b, col0 + kv * N_KV_A + g)),
                  pl.BlockSpec((1,) + w1.shape[1:], lambda b, kv, g: (kv, 0, 0, 0)),
                  pl.BlockSpec((1, w2.shape[1], HEAD_DIM), lambda b, kv, g: (kv, 0, 0)),
                  pl.BlockSpec((1, CMP_BLOCK, HEAD_DIM), lambda b, kv, g: (kv, 0, 0)),
                  pl.BlockSpec((n_ch, HEAD_DIM), lambda b, kv, g: (0, 0)),
                  pl.BlockSpec((n_ch, HEAD_DIM), lambda b, kv, g: (0, 0))],
        out_specs=pl.BlockSpec((1, 1, 1, n_ch, HEAD_DIM), lambda b, kv, g: (b, kv, g, 0, 0)),
        out_shape=jax.ShapeDtypeStruct((n_batch, 2, N_KV_A, n_ch, HEAD_DIM), _BF),
        compiler_params=_params("parallel", "parallel", "parallel"),
    )(proj, w1, w2, pe, cos_end, sin_end)


def _nsa_prompt_kernel(q_ref, kc_ref, vc_ref, selk_ref, selv_ref, wink_ref, winv_ref, gate_ref,
                       cover_ref, expand_ref, o_ref, selexp_ref, sk_ref, sv_ref, wk_ref, wv_ref,
                       *, n_heads, n_s, chunk):
    tq = q_ref.shape[0]
    seq = selexp_ref.shape[1]
    grp = pl.program_id(1)
    qi = pl.program_id(2)
    row0 = pl.multiple_of(qi * tq, tq)
    t_col = qi * tq + lax.broadcasted_iota(jnp.int32, (tq, 1), 0)

    @pl.when(qi == 0)
    def _():
        for ref in (sk_ref, sv_ref, wk_ref, wv_ref):
            ref[...] = jnp.zeros(ref.shape, ref.dtype)

    ones = jnp.ones((tq, HEAD_DIM), _BF)
    sk_ref[pl.ds(row0, tq), :] = selk_ref[...].astype(_BF)
    sv_ref[pl.ds(row0, tq), :] = jnp.concatenate([selv_ref[...].astype(_BF), ones], axis=1)
    wk_ref[pl.ds(row0, tq), :] = wink_ref[...].astype(_BF)
    wv_ref[pl.ds(row0, tq), :] = jnp.concatenate([winv_ref[...].astype(_BF), ones], axis=1)

    qs = _stack_heads(q_ref[...], n_heads).astype(_BF)
    o_cmp, sel = _cmp_attend_and_choose(qs, kc_ref[0, 0, 0], vc_ref[0, 0, 0], cover_ref[...], t_col, n_heads, n_s)
    chosen = jnp.dot(jnp.where(sel, 1.0, 0.0).astype(_BF), expand_ref[...], preferred_element_type=jnp.float32)
    causal = lax.broadcasted_iota(jnp.int32, (tq, seq), 1) <= t_col
    selexp_ref[...] = jnp.where(causal, chosen, 0.0)

    def sel_mask(c, j):
        return selexp_ref[:, pl.ds(pl.multiple_of(c * chunk + j * LANES, LANES), LANES)] > 0.5

    o_slc = _attend(qs, sk_ref, sv_ref, 0, (row0 + tq + chunk - 1) // chunk, chunk, sel_mask, n_heads)

    span = WINDOW + tq
    start = pl.multiple_of(jnp.maximum(row0 - WINDOW, 0), tq)
    d = t_col - (start + lax.broadcasted_iota(jnp.int32, (tq, span), 1))
    visible = (d >= 0) & (d < WINDOW)
    o_win = _attend(qs, wk_ref, wv_ref, start, 1, span, lambda c, j: visible[:, j * LANES:(j + 1) * LANES], n_heads)

    gates = jax.nn.sigmoid(gate_ref[...])
    gates = jnp.where(grp == 0, gates, pltpu.roll(gates, LANES - n_heads, axis=1))
    n_all = N_KV_A * n_heads
    for r in range(n_heads):
        o = (gates[:, r:r + 1] * o_cmp[r] + gates[:, n_all + r:n_all + r + 1] * o_slc[r]
             + gates[:, 2 * n_all + r:2 * n_all + r + 1] * o_win[r])
        o_ref[:, r * HEAD_DIM:(r + 1) * HEAD_DIM] = o.astype(o_ref.dtype)


def nsa_prompt(proj, kvc, lay, n_batch, seq, tq, chunk=512):
    n_heads = lay.n_heads_a // N_KV_A
    n_ch = kvc.shape[-2]
    n_s = -(-seq // SEL_BLOCK)
    assert tq % LANES == 0 and n_s <= LANES and seq % chunk == 0 and chunk % tq == 0
    assert WINDOW % tq == 0 and seq >= WINDOW + tq and lay.ga % LANES == 0 and N_KV_A == 2
    nq = seq // tq
    qw = n_heads * HEAD_DIM

    def tile(col):
        c0 = col // HEAD_DIM
        return pl.BlockSpec((tq, HEAD_DIM), lambda b, g, i: (b * nq + i, c0 + g))

    return pl.pallas_call(
        functools.partial(_nsa_prompt_kernel, n_heads=n_heads, n_s=n_s, chunk=chunk),
        grid=(n_batch, N_KV_A, nq),
        in_specs=[pl.BlockSpec((tq, qw), lambda b, g, i: (b * nq + i, lay.qa // qw + g)),
                  pl.BlockSpec((1, 1, 1, n_ch, HEAD_DIM), lambda b, g, i: (b, 0, g, 0, 0)),
                  pl.BlockSpec((1, 1, 1, n_ch, HEAD_DIM), lambda b, g, i: (b, 1, g, 0, 0)),
                  tile(lay.selk), tile(lay.selv), tile(lay.wink), tile(lay.winv),
                  pl.BlockSpec((tq, LANES), lambda b, g, i: (b * nq + i, lay.ga // LANES)),
                  pl.BlockSpec((n_ch, LANES), lambda b, g, i: (0, 0)),
                  pl.BlockSpec((LANES, seq), lambda b, g, i: (0, 0))],
        out_specs=pl.BlockSpec((tq, qw), lambda b, g, i: (b * nq + i, g)),
        out_shape=jax.ShapeDtypeStruct((n_batch * seq, N_KV_A * qw), _BF),
        scratch_shapes=[pltpu.VMEM((tq, seq), jnp.float32),
                        pltpu.VMEM((seq, HEAD_DIM), _BF), pltpu.VMEM((seq, 2 * HEAD_DIM), _BF),
                        pltpu.VMEM((seq, HEAD_DIM), _BF), pltpu.VMEM((seq, 2 * HEAD_DIM), _BF)],
        compiler_params=_params("parallel", "parallel", "arbitrary"),
    )(proj, kvc, kvc, proj, proj, proj, proj, proj, _cover_matrix(n_ch, LANES, n_s), _expand_matrix(LANES, seq))


def _dsa_prompt_kernel(iq_ref, iw_ref, q_ref, ik_ref, k_ref, v_ref, tri_ref, o_ref,
                       score_ref, key_ref, mask_ref, ikb_ref, kb_ref, vb_ref,
                       *, n_idx, n_heads, n_top, iw_lane, chunk):
    tq = iq_ref.shape[0]
    seq = mask_ref.shape[1]
    qi = pl.program_id(1)
    n_tiles = qi + 1
    row0 = pl.multiple_of(qi * tq, tq)
    t_col = qi * tq + lax.broadcasted_iota(jnp.int32, (tq, 1), 0)
    key_iota = lax.broadcasted_iota(jnp.int32, (tq, tq), 1)

    @pl.when(qi == 0)
    def _():
        for ref in (ikb_ref, kb_ref, vb_ref):
            ref[...] = jnp.zeros(ref.shape, ref.dtype)

    ikb_ref[pl.ds(row0, tq), :] = ik_ref[...].astype(_BF)
    kb_ref[pl.ds(row0, tq), :] = k_ref[...].astype(_BF)
    vb_ref[pl.ds(row0, tq), :] = jnp.concatenate([v_ref[...].astype(_BF), jnp.ones((tq, HEAD_DIM), _BF)], axis=1)
    iqs = _stack_heads(iq_ref[...], n_idx).astype(_BF)
    iw = iw_ref[...]

    def score_tile(kt, _):
        sl = pl.ds(pl.multiple_of(kt * tq, tq), tq)
        sc = _indexer_scores(iqs, ikb_ref[sl, :], iw, iw_lane, n_idx)
        score_ref[:, sl] = jnp.where(kt * tq + key_iota <= t_col, sc, NEG)
        return 0

    lax.fori_loop(0, n_tiles, score_tile, 0)
    mask_ref[...] = jnp.zeros(mask_ref.shape, mask_ref.dtype)
    _topk_mask(score_ref, key_ref, mask_ref, tri_ref, n_tiles * (tq // LANES), n_top)
    causal = lax.broadcasted_iota(jnp.int32, (tq, seq), 1) <= t_col
    mask_ref[...] = jnp.where(causal, mask_ref[...], 0.0)

    def dsa_mask(c, j):
        return mask_ref[:, pl.ds(pl.multiple_of(c * chunk + j * LANES, LANES), LANES)] > 0.5

    qs = _stack_heads(q_ref[...], n_heads).astype(_BF)
    o = _attend(qs, kb_ref, vb_ref, 0, (row0 + tq + chunk - 1) // chunk, chunk, dsa_mask, n_heads)
    for h in range(n_heads):
        o_ref[:, h * HEAD_DIM:(h + 1) * HEAD_DIM] = o[h].astype(o_ref.dtype)


def dsa_prompt(proj, lay, n_batch, seq, tq, chunk=512):
    assert tq % LANES == 0 and seq % chunk == 0 and chunk % tq == 0
    nq = seq // tq
    n_top = min(DSA_TOPK, seq // 4)
    iqw = lay.n_idx_heads * IDX_DIM
    qw = lay.n_heads_b * HEAD_DIM

    def tile(col):
        c0 = col // HEAD_DIM
        return pl.BlockSpec((tq, HEAD_DIM), lambda b, i: (b * nq + i, c0))

    return pl.pallas_call(
        functools.partial(_dsa_prompt_kernel, n_idx=lay.n_idx_heads, n_heads=lay.n_heads_b, n_top=n_top,
                          iw_lane=lay.iw % LANES, chunk=chunk),
        grid=(n_batch, nq),
        in_specs=[pl.BlockSpec((tq, iqw), lambda b, i: (b * nq + i, lay.iq // iqw)),
                  pl.BlockSpec((tq, LANES), lambda b, i: (b * nq + i, lay.iw // LANES)),
                  pl.BlockSpec((tq, qw), lambda b, i: (b * nq + i, lay.qb // qw)),
                  tile(lay.ik), tile(lay.dk), tile(lay.dv),
                  pl.BlockSpec((LANES, LANES), lambda b, i: (0, 0))],
        out_specs=pl.BlockSpec((tq, qw), lambda b, i: (b * nq + i, 0)),
        out_shape=jax.ShapeDtypeStruct((n_batch * seq, qw), _BF),
        scratch_shapes=[pltpu.VMEM((tq, seq), jnp.float32), pltpu.VMEM((tq, seq), jnp.int32),
                        pltpu.VMEM((tq, seq), jnp.float32),
                        pltpu.VMEM((seq, IDX_DIM), _BF), pltpu.VMEM((seq, HEAD_DIM), _BF),
                        pltpu.VMEM((seq, 2 * HEAD_DIM), _BF)],
        compiler_params=_params("parallel", "arbitrary"),
    )(proj, proj, proj, proj, proj, proj, _tri_matrix())


def _page_specs(block, second_block=0):
    tail = (0,) * (len(block) - 3)
    def spec(k):
        return pl.BlockSpec(block, lambda b, s, pt: (pt[b, s * PAGES_PER_STEP + k], 0, second_block) + tail)
    return [spec(k) for k in range(PAGES_PER_STEP)]


def _compress_sample_kernel(pt_ref, *refs):
    pages = refs[:PAGES_PER_STEP]
    w1_ref, w2_ref, pe_ref, cos_ref, sin_ref, o_ref, rows_ref = refs[PAGES_PER_STEP:]
    step = pl.program_id(1)
    page = pages[0].shape[1]
    n_slabs = rows_ref.shape[0]
    for k in range(PAGES_PER_STEP):
        start = pl.multiple_of((step * PAGES_PER_STEP + k) * page, page)
        for c in range(n_slabs):
            rows_ref.at[c][pl.ds(start, page), :] = pages[k][0, :, c // N_KV_A, c % N_KV_A, :]

    @pl.when(step == pl.num_programs(1) - 1)
    def _():
        n_ch = o_ref.shape[-2]
        for kv in range(2):
            for g in range(N_KV_A):
                c = kv * N_KV_A + g
                y = _compress_rows(lambda s: rows_ref.at[c][pl.ds(s, n_ch, stride=CMP_STRIDE), :],
                                   pe_ref[kv], lambda r: w1_ref[kv, r], w2_ref[kv], n_ch)
                if kv == 0:
                    y = _rotary(y, cos_ref[...], sin_ref[...])
                o_ref[0, kv, g] = y.astype(o_ref.dtype)


def compress_sample(cache, page_table, w1, w2, pe, cos_end, sin_end):
    n_batch, n_pages = page_table.shape
    page = cache.shape[1]
    n_ch = n_pages * page // CMP_STRIDE
    assert n_pages % PAGES_PER_STEP == 0
    const = lambda nd: (lambda b, s, pt: (0,) * nd)
    grid_spec = pltpu.PrefetchScalarGridSpec(
        num_scalar_prefetch=1,
        grid=(n_batch, n_pages // PAGES_PER_STEP),
        in_specs=_page_specs((1, page, 2, N_KV_A, HEAD_DIM), 0) + [
            pl.BlockSpec(w1.shape, const(4)), pl.BlockSpec(w2.shape, const(3)), pl.BlockSpec(pe.shape, const(3)),
            pl.BlockSpec((n_ch, HEAD_DIM), const(2)), pl.BlockSpec((n_ch, HEAD_DIM), const(2))],
        out_specs=pl.BlockSpec((1, 2, N_KV_A, n_ch, HEAD_DIM), lambda b, s, pt: (b, 0, 0, 0, 0)),
        scratch_shapes=[pltpu.VMEM((2 * N_KV_A, n_pages * page, HEAD_DIM), jnp.float32)],
    )
    return pl.pallas_call(
        _compress_sample_kernel,
        grid_spec=grid_spec,
        out_shape=jax.ShapeDtypeStruct((n_batch, 2, N_KV_A, n_ch, HEAD_DIM), _BF),
        compiler_params=_params("parallel", "arbitrary"),
    )(page_table, *([cache] * PAGES_PER_STEP), w1, w2, pe, cos_end, sin_end)


def _nsa_sample_kernel(pt_ref, *refs, n_heads, n_s, past_len):
    pages = refs[:PAGES_PER_STEP]
    (q_ref, gate_ref, kvc_ref, nsk_ref, nsv_ref, nwk_ref, nwv_ref, win_ref, cover_ref, expand_ref,
     o_ref, m_ref, l_ref, acc_ref, ocmp_ref, sel_ref) = refs[PAGES_PER_STEP:]
    step = pl.program_id(1)
    ts = q_ref.shape[0]
    qw = n_heads * HEAD_DIM
    n_grp = N_KV_A
    t_col = past_len + lax.broadcasted_iota(jnp.int32, (ts, 1), 0)

    def queries(g):
        return _stack_heads(q_ref[:, g * qw:(g + 1) * qw], n_heads).astype(_BF)

    def group_lanes(x, g):
        return x[:, g * HEAD_DIM:(g + 1) * HEAD_DIM]

    @pl.when(step == 0)
    def _():
        for g in range(n_grp):
            o_cmp, sel = _cmp_attend_and_choose(queries(g), kvc_ref[0, 0, g], kvc_ref[0, 1, g], cover_ref[...],
                                                t_col, n_heads, n_s)
            ocmp_ref[g] = o_cmp
            sel_ref[g] = jnp.where(sel, 1.0, 0.0)
            m_ref[g], l_ref[g], acc_ref[g] = _flash_init(n_heads, ts)

    state = [(m_ref[g], l_ref[g], acc_ref[g]) for g in range(n_grp)]
    for g in range(n_grp):
        chosen = jnp.dot(sel_ref[g].astype(_BF), expand_ref[...], preferred_element_type=jnp.float32) > 0.5
        k = jnp.concatenate([p[0, :, 0, g, :] for p in pages], axis=0).astype(_BF)
        v = jnp.concatenate([p[0, :, 1, g, :] for p in pages], axis=0).astype(_BF)
        state[g] = _flash_update(queries(g), k, v, chosen, state[g], n_heads)
    for g in range(n_grp):
        m_ref[g], l_ref[g], acc_ref[g] = state[g]

    @pl.when(step == pl.num_programs(1) - 1)
    def _():
        gates = jax.nn.sigmoid(gate_ref[...])
        n_all = n_grp * n_heads
        w_len = win_ref.shape[1]
        row = lax.broadcasted_iota(jnp.int32, (ts, LANES), 0)
        lane = lax.broadcasted_iota(jnp.int32, (ts, LANES), 1)
        new_causal = (lane <= row) & (lane < ts)
        wlane = lax.broadcasted_iota(jnp.int32, (ts, w_len + LANES), 1)
        k_pos = past_len - w_len + wlane
        d = t_col - k_pos
        win_mask = (d >= 0) & (d < WINDOW) & (k_pos >= 0) & (wlane < w_len + ts)
        new_blk = past_len // SEL_BLOCK
        for g in range(n_grp):
            qs = queries(g)
            chosen = sel_ref[g][:, new_blk:new_blk + 1] > 0.5
            k_new = _pad_rows(group_lanes(nsk_ref[...], g), LANES).astype(_BF)
            v_new = _pad_rows(group_lanes(nsv_ref[...], g), LANES).astype(_BF)
            o_slc = _flash_finish(_flash_update(qs, k_new, v_new, new_causal & chosen,
                                                (m_ref[g], l_ref[g], acc_ref[g]), n_heads))
            kw = jnp.concatenate([win_ref[0, :, 0, g, :], _pad_rows(group_lanes(nwk_ref[...], g), LANES)],
                                 axis=0).astype(_BF)
            vw = jnp.concatenate([win_ref[0, :, 1, g, :], _pad_rows(group_lanes(nwv_ref[...], g), LANES)],
                                 axis=0).astype(_BF)
            o_win = _flash_finish(_flash_update(qs, kw, vw, win_mask, _flash_init(n_heads, ts), n_heads))
            o_cmp = ocmp_ref[g]
            for r in range(n_heads):
                h = g * n_heads + r
                o = (gates[:, h:h + 1] * o_cmp[r] + gates[:, n_all + h:n_all + h + 1] * o_slc[r]
                     + gates[:, 2 * n_all + h:2 * n_all + h + 1] * o_win[r])
                o_ref[0, :, h * HEAD_DIM:(h + 1) * HEAD_DIM] = o.astype(o_ref.dtype)


def nsa_sample(proj, row0, kvc, cache, win_buf, page_table, lay, ts):
    n_batch, n_pages = page_table.shape
    page = cache.shape[1]
    past_len = n_pages * page
    n_heads = lay.n_heads_a // N_KV_A
    n_ch = kvc.shape[-2]
    n_s = -(-(past_len + ts) // SEL_BLOCK)
    sel_lanes = -(-n_s // LANES) * LANES
    kvw = N_KV_A * HEAD_DIM
    assert n_pages % PAGES_PER_STEP == 0 and row0 % ts == 0 and ts <= SEL_BLOCK and past_len % SEL_BLOCK == 0
    assert ts % 8 == 0 and lay.ga % LANES == 0
    assert (past_len + ts - CMP_BLOCK) // CMP_STRIDE + 1 <= n_ch and n_ch * CMP_STRIDE <= past_len
    r0 = row0 // ts
    keys_per_step = PAGES_PER_STEP * page
    qw = lay.n_heads_a * HEAD_DIM
    rows = lambda width, col: pl.BlockSpec((ts, width), lambda b, s, pt: (r0 + b, col // width))
    const = lambda nd: (lambda b, s, pt: (0,) * nd)
    grid_spec = pltpu.PrefetchScalarGridSpec(
        num_scalar_prefetch=1,
        grid=(n_batch, n_pages // PAGES_PER_STEP),
        in_specs=_page_specs((1, page, 2, N_KV_A, HEAD_DIM), 1) + [
            rows(qw, lay.qa), rows(LANES, lay.ga),
            pl.BlockSpec((1, 2, N_KV_A, n_ch, HEAD_DIM), lambda b, s, pt: (b, 0, 0, 0, 0)),
            rows(kvw, lay.selk), rows(kvw, lay.selv), rows(kvw, lay.wink), rows(kvw, lay.winv),
            pl.BlockSpec((1,) + win_buf.shape[1:], lambda b, s, pt: (b, 0, 0, 0, 0)),
            pl.BlockSpec((n_ch, sel_lanes), const(2)),
            pl.BlockSpec((sel_lanes, keys_per_step), lambda b, s, pt: (0, s))],
        out_specs=pl.BlockSpec((1, ts, qw), lambda b, s, pt: (b, 0, 0)),
        scratch_shapes=[pltpu.VMEM((N_KV_A, n_heads, ts, 1), jnp.float32),
                        pltpu.VMEM((N_KV_A, n_heads, ts, 1), jnp.float32),
                        pltpu.VMEM((N_KV_A, n_heads, ts, HEAD_DIM), jnp.float32),
                        pltpu.VMEM((N_KV_A, n_heads, ts, HEAD_DIM), jnp.float32),
                        pltpu.VMEM((N_KV_A, ts, sel_lanes), jnp.float32)],
    )
    return pl.pallas_call(
        functools.partial(_nsa_sample_kernel, n_heads=n_heads, n_s=n_s, past_len=past_len),
        grid_spec=grid_spec,
        out_shape=jax.ShapeDtypeStruct((n_batch, ts, qw), _BF),
        compiler_params=_params("parallel", "arbitrary"),
    )(page_table, *([cache] * PAGES_PER_STEP), proj, proj, kvc, proj, proj, proj, proj, win_buf,
      _cover_matrix(n_ch, sel_lanes, n_s), _expand_matrix(sel_lanes, past_len))


def _dsa_sample_select_kernel(pt_ref, *refs, n_idx, n_top, iw_lane, past_len):
    pages = refs[:PAGES_PER_STEP]
    iq_ref, iw_ref, nik_ref, tri_ref, mask_ref, score_ref, key_ref = refs[PAGES_PER_STEP:]
    step = pl.program_id(1)
    ts = iq_ref.shape[0]
    keys_per_step = PAGES_PER_STEP * pages[0].shape[1]
    iqs = _stack_heads(iq_ref[...], n_idx).astype(_BF)
    iw = iw_ref[...]
    ik = jnp.concatenate([p[0] for p in pages], axis=0).astype(_BF)
    start = pl.multiple_of(step * keys_per_step, keys_per_step)
    score_ref[:, pl.ds(start, keys_per_step)] = _indexer_scores(iqs, ik, iw, iw_lane, n_idx)

    @pl.when(step == pl.num_programs(1) - 1)
    def _():
        sc = _indexer_scores(iqs, _pad_rows(nik_ref[...], LANES).astype(_BF), iw, iw_lane, n_idx)
        row = lax.broadcasted_iota(jnp.int32, (ts, LANES), 0)
        lane = lax.broadcasted_iota(jnp.int32, (ts, LANES), 1)
        score_ref[:, past_len:past_len + LANES] = jnp.where(lane < ts, jnp.where(lane <= row, sc, NEG), -jnp.inf)
        _topk_mask(score_ref, key_ref, mask_ref.at[0], tri_ref, past_len // LANES + 1, n_top)


def dsa_sample_select(proj, row0, cache_idx, page_table, lay, ts):
    n_batch, n_pages = page_table.shape
    page = cache_idx.shape[1]
    past_len = n_pages * page
    assert n_pages % PAGES_PER_STEP == 0 and row0 % ts == 0 and ts <= LANES and past_len % LANES == 0
    n_top = min(DSA_TOPK, (past_len + ts) // 4)
    r0 = row0 // ts
    iqw = lay.n_idx_heads * IDX_DIM
    width = past_len + LANES
    rows = lambda w, col: pl.BlockSpec((ts, w), lambda b, s, pt: (r0 + b, col // w))
    grid_spec = pltpu.PrefetchScalarGridSpec(
        num_scalar_prefetch=1,
        grid=(n_batch, n_pages // PAGES_PER_STEP),
        in_specs=_page_specs((1, page, IDX_DIM), 0) + [
            rows(iqw, lay.iq), rows(LANES, lay.iw), rows(IDX_DIM, lay.ik),
            pl.BlockSpec((LANES, LANES), lambda b, s, pt: (0, 0))],
        out_specs=pl.BlockSpec((1, ts, width), lambda b, s, pt: (b, 0, 0)),
        scratch_shapes=[pltpu.VMEM((ts, width), jnp.float32), pltpu.VMEM((ts, width), jnp.int32)],
    )
    return pl.pallas_call(
        functools.partial(_dsa_sample_select_kernel, n_idx=lay.n_idx_heads, n_top=n_top,
                          iw_lane=lay.iw % LANES, past_len=past_len),
        grid_spec=grid_spec,
        out_shape=jax.ShapeDtypeStruct((n_batch, ts, width), jnp.float32),
        compiler_params=_params("parallel", "arbitrary"),
    )(page_table, *([cache_idx] * PAGES_PER_STEP), proj, proj, proj, _tri_matrix())


def _dsa_sample_attend_kernel(pt_ref, *refs, n_heads):
    pages = refs[:PAGES_PER_STEP]
    q_ref, mask_ref, nmask_ref, nk_ref, nv_ref, o_ref, m_ref, l_ref, acc_ref = refs[PAGES_PER_STEP:]
    step = pl.program_id(1)
    ts = q_ref.shape[0]
    qs = _stack_heads(q_ref[...], n_heads).astype(_BF)

    @pl.when(step == 0)
    def _():
        m_ref[...], l_ref[...], acc_ref[...] = _flash_init(n_heads, ts)

    k = jnp.concatenate([p[0, :, 0, :] for p in pages], axis=0).astype(_BF)
    v = jnp.concatenate([p[0, :, 1, :] for p in pages], axis=0).astype(_BF)
    m_ref[...], l_ref[...], acc_ref[...] = _flash_update(qs, k, v, mask_ref[0] > 0.5,
                                                         (m_ref[...], l_ref[...], acc_ref[...]), n_heads)

    @pl.when(step == pl.num_programs(1) - 1)
    def _():
        row = lax.broadcasted_iota(jnp.int32, (ts, LANES), 0)
        lane = lax.broadcasted_iota(jnp.int32, (ts, LANES), 1)
        mask = (nmask_ref[0] > 0.5) & (lane <= row) & (lane < ts)
        o = _flash_finish(_flash_update(qs, _pad_rows(nk_ref[...], LANES).astype(_BF),
                                        _pad_rows(nv_ref[...], LANES).astype(_BF), mask,
                                        (m_ref[...], l_ref[...], acc_ref[...]), n_heads))
        for h in range(n_heads):
            o_ref[0, :, h * HEAD_DIM:(h + 1) * HEAD_DIM] = o[h].astype(o_ref.dtype)


def dsa_sample_attend(proj, row0, mask, cache_kv, page_table, lay, ts):
    n_batch, n_pages = page_table.shape
    page = cache_kv.shape[1]
    past_len = n_pages * page
    keys_per_step = PAGES_PER_STEP * page
    r0 = row0 // ts
    qw = lay.n_heads_b * HEAD_DIM
    rows = lambda w, col: pl.BlockSpec((ts, w), lambda b, s, pt: (r0 + b, col // w))
    grid_spec = pltpu.PrefetchScalarGridSpec(
        num_scalar_prefetch=1,
        grid=(n_batch, n_pages // PAGES_PER_STEP),
        in_specs=_page_specs((1, page, 2, HEAD_DIM), 0) + [
            rows(qw, lay.qb),
            pl.BlockSpec((1, ts, keys_per_step), lambda b, s, pt: (b, 0, s)),
            pl.BlockSpec((1, ts, LANES), lambda b, s, pt: (b, 0, past_len // LANES)),
            rows(HEAD_DIM, lay.dk), rows(HEAD_DIM, lay.dv)],
        out_specs=pl.BlockSpec((1, ts, qw), lambda b, s, pt: (b, 0, 0)),
        scratch_shapes=[pltpu.VMEM((lay.n_heads_b, ts, 1), jnp.float32),
                        pltpu.VMEM((lay.n_heads_b, ts, 1), jnp.float32),
                        pltpu.VMEM((lay.n_heads_b, ts, HEAD_DIM), jnp.float32)],
    )
    return pl.pallas_call(
        functools.partial(_dsa_sample_attend_kernel, n_heads=lay.n_heads_b),
        grid_spec=grid_spec,
        out_shape=jax.ShapeDtypeStruct((n_batch, ts, qw), _BF),
        compiler_params=_params("parallel", "arbitrary"),
    )(page_table, *([cache_kv] * PAGES_PER_STEP), proj, mask, mask, proj, proj)


class Layout:
    def __init__(self, d_model, n_heads_a, n_heads_b, n_idx_heads):
        self.n_heads_a, self.n_heads_b, self.n_idx_heads = n_heads_a, n_heads_b, n_idx_heads
        src = np.cumsum([0, n_heads_a * HEAD_DIM, 6 * N_KV_A * HEAD_DIM, 3 * n_heads_a,
                         n_heads_b * HEAD_DIM, 2 * HEAD_DIM, n_idx_heads * IDX_DIM, n_idx_heads,
                         IDX_DIM, 2 * d_model])
        s_qa, s_kva, s_ga, s_qb, s_kvb, s_iq, s_iw, s_ik, s_mg, s_end = (int(v) for v in src)
        small = 3 * n_heads_a + n_idx_heads
        assert small <= LANES
        self.small_pad = LANES - small
        kvw = N_KV_A * HEAD_DIM
        self.pieces = [
            (s_qa, s_kva, True),
            (s_qb, s_kvb, True),
            (s_iq, s_iw, True),
            (s_kva, s_kva + 2 * kvw, False),
            (s_kva + 2 * kvw, s_kva + 3 * kvw, True),
            (s_kva + 3 * kvw, s_kva + 4 * kvw, False),
            (s_kva + 4 * kvw, s_kva + 5 * kvw, True),
            (s_kva + 5 * kvw, s_ga, False),
            (s_kvb, s_kvb + HEAD_DIM, True),
            (s_kvb + HEAD_DIM, s_iq, False),
            (s_ik, s_mg, True),
            (s_ga, s_qb, False),
            (s_iw, s_ik, False),
            None,
            (s_mg, s_end, False),
        ]
        off = 0
        starts = []
        for p in self.pieces:
            starts.append(off)
            off += self.small_pad if p is None else p[1] - p[0]
        (self.qa, self.qb, self.iq, self.cmp, self.selk, self.selv, self.wink, self.winv, self.dk,
         self.dv, self.ik, self.ga, self.iw, _, self.mg) = starts
        self.width = off
        assert self.width % LANES == 0
        flags = np.zeros(self.width // LANES, np.int32)
        for st, p in zip(starts, self.pieces):
            if p is not None and p[2]:
                assert st % LANES == 0 and (p[1] - p[0]) % LANES == 0
                flags[st // LANES:(st + p[1] - p[0]) // LANES] = 1
        self.rope_flags = flags

    def pack(self, w_in):
        cols = []
        for p in self.pieces:
            if p is None:
                cols.append(jnp.zeros((w_in.shape[0], self.small_pad), w_in.dtype))
            else:
                cols.append(w_in[:, p[0]:p[1]])
        return jnp.concatenate(cols, axis=1).astype(_BF)


def rope_tables(pos):
    half = HEAD_DIM // 2
    inv = ROPE_THETA ** (-jnp.arange(half, dtype=jnp.float32) / half)
    ang = pos.astype(jnp.float32)[:, None] * inv[None, :]
    cos, sin = jnp.cos(ang), jnp.sin(ang)
    return jnp.concatenate([cos, cos], axis=1), jnp.concatenate([-sin, sin], axis=1)


def _swiglu_half_step(h, g, wg, wu, wd):
    xn = rmsnorm(h, g, _BF)
    a = ffn_gate_up(xn, wg, wu, tm=1408, tn=256)
    return resid_matmul(a, wd, h, 0.5, tm=768, tn=256)


def kernel(x_prompt, x_sample, cache_nsa_kv, cache_nsa_win, cache_dsa_kv, cache_dsa_idx, page_table,
           g_norm, w_ffn_gate, w_ffn_up, w_ffn_down, w_in, w_cmp1, w_cmp2, cmp_pos,
           w_br_a, w_br_b, w_out, g_final):
    B, T, D = x_prompt.shape
    DB, Ts, _ = x_sample.shape
    depth = g_norm.shape[0]
    page = cache_nsa_kv.shape[2]
    past_len = page_table.shape[1] * page
    n_heads_a = w_br_a.shape[1] // HEAD_DIM
    n_heads_b = w_br_b.shape[1] // HEAD_DIM
    G = N_KV_A
    lay = Layout(D, n_heads_a, n_heads_b, n_heads_b // 2)
    Mp, Ms = B * T, DB * Ts

    pos_p = jnp.arange(T, dtype=jnp.int32)
    pos_s = past_len + jnp.arange(Ts, dtype=jnp.int32)
    cos, sin = rope_tables(jnp.concatenate([jnp.tile(pos_p, B), jnp.tile(pos_s, DB)]))
    rope_flags = jnp.asarray(lay.rope_flags)

    def block_end_tables(n_ch):
        return rope_tables(jnp.arange(n_ch, dtype=jnp.int32) * CMP_STRIDE + CMP_BLOCK - 1)

    h = jnp.concatenate([x_prompt.reshape(Mp, D), x_sample.reshape(Ms, D)], axis=0)
    outs = [[] for _ in range(8)]
    for l in range(depth):
        h = _swiglu_half_step(h, g_norm[l, 0], w_ffn_gate[l, 0], w_ffn_up[l, 0], w_ffn_down[l, 0])

        u = rmsnorm(h, g_norm[l, 1], _BF)
        proj, nsa_rows, win_rows, dsa_rows, ik_rows = in_project(u, lay.pack(w_in[l]), rope_flags, cos, sin, lay,
                                                                 tm=1408, tn=512)
        w1 = w_cmp1[l].reshape(2, 2, CMP_STRIDE * HEAD_DIM, -1).astype(_BF)
        w2 = w_cmp2[l].astype(_BF)

        kvc_p = compress_prompt(proj, B, T, lay.cmp, w1, w2, cmp_pos[l], *block_end_tables(T // CMP_STRIDE))
        o_a_p = nsa_prompt(proj, kvc_p, lay, B, T, tq=2 * LANES)
        o_b_p = dsa_prompt(proj, lay, B, T, tq=2 * LANES)

        kvc_s = compress_sample(cache_nsa_kv[l], page_table, w1, w2, cmp_pos[l],
                                *block_end_tables(past_len // CMP_STRIDE))
        o_a_s = nsa_sample(proj, Mp, kvc_s, cache_nsa_kv[l], cache_nsa_win[l], page_table, lay, Ts)
        top_mask = dsa_sample_select(proj, Mp, cache_dsa_idx[l], page_table, lay, Ts)
        o_b_s = dsa_sample_attend(proj, Mp, top_mask, cache_dsa_kv[l], page_table, lay, Ts)

        o_a = jnp.concatenate([o_a_p, o_a_s.reshape(Ms, -1)], axis=0)
        o_b = jnp.concatenate([o_b_p, o_b_s.reshape(Ms, -1)], axis=0)
        m = merge_branches(o_a, o_b, w_br_a[l], w_br_b[l], proj, lay.mg, tm=768, tn=512)
        h = resid_matmul(m, w_out[l], h, 1.0, tm=768, tn=512)

        win_p = win_rows[:Mp].reshape(B, T, 2, G, HEAD_DIM)
        win_s = win_rows[Mp:].reshape(DB, Ts, 2, G, HEAD_DIM)
        outs[0].append(nsa_rows[:Mp].reshape(B, T, 4, G, HEAD_DIM))
        outs[1].append(win_p[:, T - min(WINDOW, T):])
        outs[2].append(dsa_rows[:Mp].reshape(B, T, 2, HEAD_DIM))
        outs[3].append(ik_rows[:Mp].reshape(B, T, IDX_DIM))
        outs[4].append(nsa_rows[Mp:].reshape(DB, Ts, 4, G, HEAD_DIM))
        outs[5].append(jnp.concatenate([cache_nsa_win[l], win_s], axis=1)[:, Ts:])
        outs[6].append(dsa_rows[Mp:].reshape(DB, Ts, 2, HEAD_DIM))
        outs[7].append(ik_rows[Mp:].reshape(DB, Ts, IDX_DIM))

        h = _swiglu_half_step(h, g_norm[l, 2], w_ffn_gate[l, 1], w_ffn_up[l, 1], w_ffn_down[l, 1])

    y_p = rmsnorm(h, g_final, jnp.float32, 0, Mp)
    y_s = rmsnorm(h, g_final, jnp.float32, Mp, Ms)
    return (y_p.reshape(B, T, D), y_s.reshape(DB, Ts, D), *(jnp.stack(o) for o in outs))
```

```python
import functools

import jax
import jax.numpy as jnp
import numpy as np
from jax import lax
from jax.experimental import pallas as pl
from jax.experimental.pallas import tpu as pltpu

HEAD_DIM = 128
N_KV_A = 2
IDX_DIM = 128
CMP_BLOCK = 32
CMP_STRIDE = 16
SEL_BLOCK = 64
N_SEL = 16
N_LOCAL_SEL = 2
WINDOW = 512
DSA_TOPK = 256
ROPE_THETA = 10000.0
RMS_EPS = 1e-6
NEG = -1e30
BIG = 1e30
TINY = 1e-30

LANES = 128
VMEM_LIMIT = 56 * 1024 * 1024
PAGES_PER_STEP = 32

_NT = (((1,), (1,)), ((), ()))
_BF = jnp.bfloat16


def _params(*sem):
    return pltpu.CompilerParams(dimension_semantics=sem, vmem_limit_bytes=VMEM_LIMIT)


def _rmsnorm_kernel(x_ref, g_ref, o_ref):
    x = x_ref[...]
    y = x * lax.rsqrt(jnp.mean(x * x, axis=-1, keepdims=True) + RMS_EPS)
    o_ref[...] = (y * g_ref[...]).astype(o_ref.dtype)


def rmsnorm(x, g, out_dtype, row0=0, n_rows=None, tm=256):
    D = x.shape[1]
    n_rows = x.shape[0] - row0 if n_rows is None else n_rows
    assert row0 % tm == 0 and n_rows % tm == 0
    return pl.pallas_call(
        _rmsnorm_kernel,
        grid=(n_rows // tm,),
        in_specs=[pl.BlockSpec((tm, D), lambda i: (row0 // tm + i, 0)),
                  pl.BlockSpec((1, D), lambda i: (0, 0))],
        out_specs=pl.BlockSpec((tm, D), lambda i: (i, 0)),
        out_shape=jax.ShapeDtypeStruct((n_rows, D), out_dtype),
        compiler_params=_params("parallel"),
    )(x, g.reshape(1, D))


def _gateup_kernel(x_ref, wg_ref, wu_ref, o_ref):
    x = x_ref[...]
    g = jnp.dot(x, wg_ref[...].astype(_BF), preferred_element_type=jnp.float32)
    u = jnp.dot(x, wu_ref[...].astype(_BF), preferred_element_type=jnp.float32)
    o_ref[...] = (g * jax.nn.sigmoid(g) * u).astype(o_ref.dtype)


def _weight_spec(w, w_index, rows, tn):
    assert w.ndim == len(w_index) + 2 and w.shape[-2] == rows
    return pl.BlockSpec((None,) * len(w_index) + (rows, tn), lambda i, j: tuple(w_index) + (0, j))


def ffn_gate_up(xn, wg, wu, w_index, tm, tn):
    M, D = xn.shape
    F = wg.shape[-1]
    return pl.pallas_call(
        _gateup_kernel,
        grid=(M // tm, F // tn),
        in_specs=[pl.BlockSpec((tm, D), lambda i, j: (i, 0)),
                  _weight_spec(wg, w_index, D, tn),
                  _weight_spec(wu, w_index, D, tn)],
        out_specs=pl.BlockSpec((tm, tn), lambda i, j: (i, j)),
        out_shape=jax.ShapeDtypeStruct((M, F), _BF),
        compiler_params=_params("parallel", "parallel"),
    )(xn, wg, wu)


def _resid_matmul_kernel(a_ref, w_ref, r_ref, o_ref, *, scale):
    acc = jnp.dot(a_ref[...], w_ref[...].astype(_BF), preferred_element_type=jnp.float32)
    o_ref[...] = r_ref[...] + scale * acc


def resid_matmul(a, w, w_index, resid, scale, tm, tn):
    M, K = a.shape
    N = w.shape[-1]
    return pl.pallas_call(
        functools.partial(_resid_matmul_kernel, scale=scale),
        grid=(M // tm, N // tn),
        in_specs=[pl.BlockSpec((tm, K), lambda i, j: (i, 0), pipeline_mode=pl.Buffered(1)),
                  _weight_spec(w, w_index, K, tn),
                  pl.BlockSpec((tm, tn), lambda i, j: (i, j))],
        out_specs=pl.BlockSpec((tm, tn), lambda i, j: (i, j)),
        out_shape=jax.ShapeDtypeStruct((M, N), jnp.float32),
        compiler_params=_params("parallel", "parallel"),
    )(a, w, resid)


def _rotary(y, cos, sin):
    return y * cos + pltpu.roll(y, HEAD_DIM // 2, axis=1) * sin


def _inproj_kernel(flags_ref, x_ref, w_ref, cos_ref, sin_ref, o_ref, *row_refs, n_chunks, routes):
    j = pl.program_id(1)
    acc = jnp.dot(x_ref[...], w_ref[...], preferred_element_type=jnp.float32)
    for c in range(n_chunks):
        sl = slice(c * LANES, (c + 1) * LANES)
        y = acc[:, sl]
        flag = flags_ref[j * n_chunks + c]

        @pl.when(flag == 1)
        def _():
            o_ref[:, sl] = _rotary(y, cos_ref[...], sin_ref[...])

        @pl.when(flag == 0)
        def _():
            o_ref[:, sl] = y

    for chunk, out_idx, index in routes:
        @pl.when(j == chunk // n_chunks)
        def _():
            c = chunk % n_chunks
            row_refs[out_idx][(slice(None),) + index + (slice(None),)] = o_ref[:, c * LANES:(c + 1) * LANES]


def in_project(u, w, rope_flags, cos, sin, lay, tm, tn):
    M, D = u.shape
    N = w.shape[1]
    n_chunks = tn // LANES
    G = N_KV_A
    routes = ([(lay.cmp // LANES + k, 0, (k // G, k % G)) for k in range(4 * G)]
              + [(lay.wink // LANES + k, 1, (k // G, k % G)) for k in range(2 * G)]
              + [(lay.dk // LANES + k, 2, (k,)) for k in range(2)]
              + [(lay.ik // LANES, 3, ())])
    row_shapes = [(M, 4, G, HEAD_DIM), (M, 2, G, HEAD_DIM), (M, 2, HEAD_DIM), (M, IDX_DIM)]
    grid_spec = pltpu.PrefetchScalarGridSpec(
        num_scalar_prefetch=1,
        grid=(M // tm, N // tn),
        in_specs=[pl.BlockSpec((tm, D), lambda i, j, f: (i, 0), pipeline_mode=pl.Buffered(1)),
                  pl.BlockSpec((D, tn), lambda i, j, f: (0, j)),
                  pl.BlockSpec((tm, LANES), lambda i, j, f: (i, 0)),
                  pl.BlockSpec((tm, LANES), lambda i, j, f: (i, 0))],
        out_specs=[pl.BlockSpec((tm, tn), lambda i, j, f: (i, j))]
        + [pl.BlockSpec((tm,) + shp[1:], lambda i, j, f, nd=len(shp): (i,) + (0,) * (nd - 1)) for shp in row_shapes],
    )
    return pl.pallas_call(
        functools.partial(_inproj_kernel, n_chunks=n_chunks, routes=routes),
        grid_spec=grid_spec,
        out_shape=[jax.ShapeDtypeStruct((M, N), jnp.float32)]
        + [jax.ShapeDtypeStruct(shp, jnp.float32) for shp in row_shapes],
        compiler_params=_params("parallel", "arbitrary"),
    )(rope_flags, u, w, cos, sin)


def _merge_kernel(oa_ref, ob_ref, wa_ref, wb_ref, ga_ref, gb_ref, o_ref):
    ya = jnp.dot(oa_ref[...], wa_ref[...].astype(_BF), preferred_element_type=jnp.float32)
    yb = jnp.dot(ob_ref[...], wb_ref[...].astype(_BF), preferred_element_type=jnp.float32)
    m = jax.nn.sigmoid(ga_ref[...]) * ya + jax.nn.sigmoid(gb_ref[...]) * yb
    o_ref[...] = m.astype(o_ref.dtype)


def merge_branches(o_a, o_b, w_a, w_b, w_index, proj, mg_col, tm, tn):
    M, K = o_a.shape
    N = w_a.shape[-1]
    ja = mg_col // tn
    jb = (mg_col + N) // tn
    return pl.pallas_call(
        _merge_kernel,
        grid=(M // tm, N // tn),
        in_specs=[pl.BlockSpec((tm, K), lambda i, j: (i, 0)),
                  pl.BlockSpec((tm, K), lambda i, j: (i, 0)),
                  _weight_spec(w_a, w_index, K, tn),
                  _weight_spec(w_b, w_index, K, tn),
                  pl.BlockSpec((tm, tn), lambda i, j: (i, ja + j)),
                  pl.BlockSpec((tm, tn), lambda i, j: (i, jb + j))],
        out_specs=pl.BlockSpec((tm, tn), lambda i, j: (i, j)),
        out_shape=jax.ShapeDtypeStruct((M, N), _BF),
        compiler_params=_params("parallel", "parallel"),
    )(o_a, o_b, w_a, w_b, proj, proj)


def _stack_heads(x, n_heads):
    return jnp.concatenate([x[:, h * HEAD_DIM:(h + 1) * HEAD_DIM] for h in range(n_heads)], axis=0)


def _flash_update(qs, k, v, mask, carry, n_rep):
    m, l, acc = carry
    rows, tk = qs.shape[0], k.shape[0]
    tq = rows // n_rep
    scale = HEAD_DIM ** -0.5
    s = lax.dot_general(qs, k, _NT, preferred_element_type=jnp.float32).reshape(n_rep, tq, tk)
    mask = mask[None]
    s = jnp.where(mask, s, NEG)
    m_new = jnp.maximum(m, jnp.max(s, axis=-1, keepdims=True))
    p = jnp.where(mask, jnp.exp((s - m_new) * scale), 0.0)
    alpha = jnp.exp((m - m_new) * scale)
    l = alpha * l + jnp.sum(p, axis=-1, keepdims=True)
    pv = jnp.dot(p.reshape(rows, tk).astype(_BF), v, preferred_element_type=jnp.float32)
    return m_new, l, alpha * acc + pv.reshape(n_rep, tq, HEAD_DIM)


def _flash_init(n_rep, tq):
    return (jnp.full((n_rep, tq, 1), NEG, jnp.float32), jnp.zeros((n_rep, tq, 1), jnp.float32),
            jnp.zeros((n_rep, tq, HEAD_DIM), jnp.float32))


def _flash_finish(carry):
    _, l, acc = carry
    return acc * (1.0 / jnp.maximum(l, TINY))


def _attend(qs, k_ref, v_ref, start, n_chunks, chunk, mask_fn, n_rep):
    rows = qs.shape[0]
    tq = rows // n_rep
    n_lane_tiles = chunk // LANES
    scale = HEAD_DIM ** -0.5

    def scores(c):
        sl = pl.ds(pl.multiple_of(start + c * chunk, LANES), chunk)
        return lax.dot_general(qs, k_ref[sl, :], _NT, preferred_element_type=jnp.float32), sl

    def lane_tile(s, j):
        return s[:, j * LANES:(j + 1) * LANES].reshape(n_rep, tq, LANES)

    def row_max(c, mx):
        s, _ = scores(c)
        for j in range(n_lane_tiles):
            mx = jnp.maximum(mx, jnp.where(mask_fn(c, j)[None], lane_tile(s, j), NEG))
        return mx

    mx = lax.fori_loop(0, n_chunks, row_max, jnp.full((n_rep, tq, LANES), NEG, jnp.float32))
    m = jnp.broadcast_to(jnp.max(mx, axis=-1, keepdims=True), mx.shape)

    def accumulate(c, acc):
        s, sl = scores(c)
        p = [jnp.where(mask_fn(c, j)[None], jnp.exp((lane_tile(s, j) - m) * scale), 0.0)
             .astype(_BF).reshape(rows, LANES) for j in range(n_lane_tiles)]
        return acc + jnp.dot(jnp.concatenate(p, axis=1), v_ref[sl, :], preferred_element_type=jnp.float32)

    acc = lax.fori_loop(0, n_chunks, accumulate, jnp.zeros((rows, 2 * HEAD_DIM), jnp.float32))
    out = acc[:, :HEAD_DIM] * (1.0 / jnp.maximum(acc[:, HEAD_DIM:], TINY))
    return out.reshape(n_rep, tq, HEAD_DIM)


def _compress_rows(load, pe, w1, w2, n_ch):
    rows = [load(s) for s in range(CMP_STRIDE)]
    half = [jnp.concatenate([(rows[s] + pe[r * CMP_STRIDE + s:r * CMP_STRIDE + s + 1, :]).astype(_BF)
                             for s in range(CMP_STRIDE)], axis=1) for r in range(2)]
    h0 = jnp.dot(half[0], w1(0), preferred_element_type=jnp.float32)
    h1 = jnp.dot(half[1], w1(1), preferred_element_type=jnp.float32)
    h = h0 + pltpu.roll(h1, n_ch - 1, axis=0)
    return jnp.dot((h * jax.nn.sigmoid(h)).astype(_BF), w2, preferred_element_type=jnp.float32)


def _cmp_attend_and_choose(qs, kc, vc, cover, t_col, n_heads, n_s):
    rows = qs.shape[0]
    tq = rows // n_heads
    n_ch, lanes = cover.shape
    scale = HEAD_DIM ** -0.5
    s = lax.dot_general(qs, kc, _NT, preferred_element_type=jnp.float32).reshape(n_heads, tq, n_ch)
    end = lax.broadcasted_iota(jnp.int32, (tq, n_ch), 1) * CMP_STRIDE + (CMP_BLOCK - 1)
    cmask = (end <= t_col)[None]
    s = jnp.where(cmask, s, NEG)
    m = jnp.max(s, axis=-1, keepdims=True)
    p = jnp.where(cmask, jnp.exp((s - m) * scale), 0.0)
    p = p * (1.0 / jnp.maximum(jnp.sum(p, axis=-1, keepdims=True), TINY))
    o_cmp = jnp.dot(p.reshape(rows, n_ch).astype(_BF), vc, preferred_element_type=jnp.float32)
    imp = jnp.dot(jnp.sum(p, axis=0).astype(_BF), cover, preferred_element_type=jnp.float32)

    lane = lax.broadcasted_iota(jnp.int32, (tq, lanes), 1)
    lane_f = lane.astype(jnp.float32)
    jt = lax.shift_right_arithmetic(t_col, jnp.int32(SEL_BLOCK.bit_length() - 1))
    adm = lane <= jt
    forced = adm & ((lane == 0) | (lane > jt - N_LOCAL_SEL))
    work = jnp.where(forced, BIG, jnp.where(adm, imp, NEG))
    work = jnp.where(lane < n_s, work, -jnp.inf)
    sel = jnp.zeros((tq, lanes), jnp.bool_)
    for _ in range(min(N_SEL, n_s)):
        mx = jnp.max(work, axis=-1, keepdims=True)
        first = jnp.min(jnp.where(work == mx, lane_f, float(lanes)), axis=-1, keepdims=True)
        pick = lane_f == first
        sel = sel | pick
        work = jnp.where(pick, -jnp.inf, work)
    return o_cmp.reshape(n_heads, tq, HEAD_DIM), sel


def _topk_mask(score_ref, key_ref, mask_ref, tri_ref, n_tiles, n_top):
    tq = score_ref.shape[0]
    int_min = jnp.int32(-2 ** 31)

    def to_key(t, _):
        sl = pl.ds(pl.multiple_of(t * LANES, LANES), LANES)
        bits = lax.bitcast_convert_type(score_ref[:, sl], jnp.int32)
        key_ref[:, sl] = bits ^ ((bits >> 31) & jnp.int32(0x7FFFFFFF))
        return 0

    lax.fori_loop(0, n_tiles, to_key, 0)

    unroll = 8 if isinstance(n_tiles, int) else 1

    def count_ge(cand):
        def body(t, acc):
            sl = pl.ds(pl.multiple_of(t * LANES, LANES), LANES)
            return acc + jnp.where(key_ref[:, sl] >= cand, 1.0, 0.0)
        acc = lax.fori_loop(0, n_tiles, body, jnp.zeros((tq, LANES), jnp.float32), unroll=unroll)
        return jnp.sum(acc, axis=-1, keepdims=True)

    tau = jnp.zeros((tq, 1), jnp.int32)
    for bit in range(31, -1, -1):
        cand = tau | jnp.int32(-2 ** 31 if bit == 31 else 1 << bit)
        tau = jnp.where(count_ge(cand ^ int_min) >= float(n_top), cand, tau)
    thr = tau ^ int_min

    def count_gt(t, acc):
        sl = pl.ds(pl.multiple_of(t * LANES, LANES), LANES)
        return acc + jnp.where(key_ref[:, sl] > thr, 1.0, 0.0)

    n_gt = jnp.sum(lax.fori_loop(0, n_tiles, count_gt, jnp.zeros((tq, LANES), jnp.float32)),
                   axis=-1, keepdims=True)
    need = float(n_top) - n_gt
    clean = jnp.max(jnp.abs(count_ge(thr) - float(n_top))) == 0.0

    @pl.when(clean)
    def _():
        def keep_ge(t, _):
            sl = pl.ds(pl.multiple_of(t * LANES, LANES), LANES)
            mask_ref[:, sl] = jnp.where(key_ref[:, sl] >= thr, 1.0, 0.0)
            return 0
        lax.fori_loop(0, n_tiles, keep_ge, 0, unroll=unroll)

    @pl.when(jnp.logical_not(clean))
    def _():
        ones = jnp.ones((LANES, LANES), _BF)

        def cut(t, eq_before):
            sl = pl.ds(pl.multiple_of(t * LANES, LANES), LANES)
            key = key_ref[:, sl]
            eq = jnp.where(key == thr, 1.0, 0.0)
            rank = (jnp.dot(eq.astype(_BF), tri_ref[...], preferred_element_type=jnp.float32)
                    + jnp.dot(eq_before.astype(_BF), ones, preferred_element_type=jnp.float32))
            keep = (key > thr) | ((key == thr) & (rank <= need))
            mask_ref[:, sl] = jnp.where(keep, 1.0, 0.0)
            return eq_before + eq

        lax.fori_loop(0, n_tiles, cut, jnp.zeros((tq, LANES), jnp.float32))


def _indexer_scores(iqs, ik, iw, iw_lane, n_idx):
    tq = iqs.shape[0] // n_idx
    n = ik.shape[0]
    logits = lax.dot_general(iqs, ik, _NT, preferred_element_type=jnp.float32).reshape(n_idx, tq, n)
    logits = jnp.maximum(logits * IDX_DIM ** -0.5, 0.0)
    sc = jnp.zeros((tq, n), jnp.float32)
    for h in range(n_idx):
        sc = sc + logits[h] * iw[:, iw_lane + h:iw_lane + h + 1]
    return sc * n_idx ** -0.5


def _cover_matrix(n_ch, lanes, n_s):
    ci = np.arange(n_ch)[:, None] * CMP_STRIDE
    sj = np.arange(lanes)[None, :] * SEL_BLOCK
    return jnp.asarray((ci < sj + SEL_BLOCK) & (ci + CMP_BLOCK > sj) & (sj < n_s * SEL_BLOCK), _BF)


def _expand_matrix(lanes, n_keys):
    return jnp.asarray(np.arange(lanes)[:, None] == np.arange(n_keys)[None, :] // SEL_BLOCK, _BF)


def _tri_matrix():
    return jnp.asarray(np.arange(LANES)[:, None] <= np.arange(LANES)[None, :], _BF)


def _pad_rows(x, n):
    return jnp.concatenate([x, jnp.zeros((n - x.shape[0], x.shape[1]), x.dtype)], axis=0)


def _compress_kernel(x_ref, w1_ref, w2_ref, pe_ref, cos_ref, sin_ref, o_ref):
    n_ch = o_ref.shape[-2]
    y = _compress_rows(lambda s: x_ref[pl.ds(s, n_ch, stride=CMP_STRIDE), :], pe_ref[0],
                       lambda r: w1_ref[0, r], w2_ref[0], n_ch)

    @pl.when(pl.program_id(1) == 0)
    def _():
        o_ref[0, 0, 0] = _rotary(y, cos_ref[...], sin_ref[...]).astype(o_ref.dtype)

    @pl.when(pl.program_id(1) != 0)
    def _():
        o_ref[0, 0, 0] = y.astype(o_ref.dtype)


def compress_prompt(proj, n_batch, seq, cmp_col, w1, w2, pe, cos_end, sin_end):
    n_ch = seq // CMP_STRIDE
    col0 = cmp_col // HEAD_DIM
    return pl.pallas_call(
        _compress_kernel,
        grid=(n_batch, 2, N_KV_A),
        in_specs=[pl.BlockSpec((seq, HEAD_DIM), lambda b, kv, g: (b, col0 + kv * N_KV_A + g)),
                  pl.BlockSpec((1,) + w1.shape[1:], lambda b, kv, g: (kv, 0, 0, 0)),
                  pl.BlockSpec((1, w2.shape[1], HEAD_DIM), lambda b, kv, g: (kv, 0, 0)),
                  pl.BlockSpec((1, CMP_BLOCK, HEAD_DIM), lambda b, kv, g: (kv, 0, 0)),
                  pl.BlockSpec((n_ch, HEAD_DIM), lambda b, kv, g: (0, 0)),
                  pl.BlockSpec((n_ch, HEAD_DIM), lambda b, kv, g: (0, 0))],
        out_specs=pl.BlockSpec((1, 1, 1, n_ch, HEAD_DIM), lambda b, kv, g: (b, kv, g, 0, 0)),
        out_shape=jax.ShapeDtypeStruct((n_batch, 2, N_KV_A, n_ch, HEAD_DIM), _BF),
        compiler_params=_params("parallel", "parallel", "parallel"),
    )(proj, w1, w2, pe, cos_end, sin_end)


def _nsa_prompt_kernel(q_ref, kc_ref, vc_ref, selk_ref, selv_ref, wink_ref, winv_ref, gate_ref,
                       cover_ref, expand_ref, o_ref, selexp_ref, sk_ref, sv_ref, wk_ref, wv_ref,
                       *, n_heads, n_s, chunk):
    tq = q_ref.shape[0]
    seq = selexp_ref.shape[1]
    grp = pl.program_id(1)
    qi = pl.program_id(2)
    row0 = pl.multiple_of(qi * tq, tq)
    t_col = qi * tq + lax.broadcasted_iota(jnp.int32, (tq, 1), 0)

    @pl.when(qi == 0)
    def _():
        for ref in (sk_ref, sv_ref, wk_ref, wv_ref):
            ref[...] = jnp.zeros(ref.shape, ref.dtype)

    ones = jnp.ones((tq, HEAD_DIM), _BF)
    sk_ref[pl.ds(row0, tq), :] = selk_ref[...].astype(_BF)
    sv_ref[pl.ds(row0, tq), :] = jnp.concatenate([selv_ref[...].astype(_BF), ones], axis=1)
    wk_ref[pl.ds(row0, tq), :] = wink_ref[...].astype(_BF)
    wv_ref[pl.ds(row0, tq), :] = jnp.concatenate([winv_ref[...].astype(_BF), ones], axis=1)

    qs = _stack_heads(q_ref[...], n_heads).astype(_BF)
    o_cmp, sel = _cmp_attend_and_choose(qs, kc_ref[0, 0, 0], vc_ref[0, 0, 0], cover_ref[...], t_col, n_heads, n_s)
    chosen = jnp.dot(jnp.where(sel, 1.0, 0.0).astype(_BF), expand_ref[...], preferred_element_type=jnp.float32)
    causal = lax.broadcasted_iota(jnp.int32, (tq, seq), 1) <= t_col
    selexp_ref[...] = jnp.where(causal, chosen, 0.0)

    def sel_mask(c, j):
        return selexp_ref[:, pl.ds(pl.multiple_of(c * chunk + j * LANES, LANES), LANES)] > 0.5

    o_slc = _attend(qs, sk_ref, sv_ref, 0, (row0 + tq + chunk - 1) // chunk, chunk, sel_mask, n_heads)

    span = WINDOW + tq
    start = pl.multiple_of(jnp.maximum(row0 - WINDOW, 0), tq)
    d = t_col - (start + lax.broadcasted_iota(jnp.int32, (tq, span), 1))
    visible = (d >= 0) & (d < WINDOW)
    o_win = _attend(qs, wk_ref, wv_ref, start, 1, span, lambda c, j: visible[:, j * LANES:(j + 1) * LANES], n_heads)

    gates = jax.nn.sigmoid(gate_ref[...])
    gates = jnp.where(grp == 0, gates, pltpu.roll(gates, LANES - n_heads, axis=1))
    n_all = N_KV_A * n_heads
    for r in range(n_heads):
        o = (gates[:, r:r + 1] * o_cmp[r] + gates[:, n_all + r:n_all + r + 1] * o_slc[r]
             + gates[:, 2 * n_all + r:2 * n_all + r + 1] * o_win[r])
        o_ref[:, r * HEAD_DIM:(r + 1) * HEAD_DIM] = o.astype(o_ref.dtype)


def nsa_prompt(proj, kvc, lay, n_batch, seq, tq, chunk=512):
    n_heads = lay.n_heads_a // N_KV_A
    n_ch = kvc.shape[-2]
    n_s = -(-seq // SEL_BLOCK)
    assert tq % LANES == 0 and n_s <= LANES and seq % chunk == 0 and chunk % tq == 0
    assert WINDOW % tq == 0 and seq >= WINDOW + tq and lay.ga % LANES == 0 and N_KV_A == 2
    nq = seq // tq
    qw = n_heads * HEAD_DIM

    def tile(col):
        c0 = col // HEAD_DIM
        return pl.BlockSpec((tq, HEAD_DIM), lambda b, g, i: (b * nq + i, c0 + g))

    return pl.pallas_call(
        functools.partial(_nsa_prompt_kernel, n_heads=n_heads, n_s=n_s, chunk=chunk),
        grid=(n_batch, N_KV_A, nq),
        in_specs=[pl.BlockSpec((tq, qw), lambda b, g, i: (b * nq + i, lay.qa // qw + g)),
                  pl.BlockSpec((1, 1, 1, n_ch, HEAD_DIM), lambda b, g, i: (b, 0, g, 0, 0)),
                  pl.BlockSpec((1, 1, 1, n_ch, HEAD_DIM), lambda b, g, i: (b, 1, g, 0, 0)),
                  tile(lay.selk), tile(lay.selv), tile(lay.wink), tile(lay.winv),
                  pl.BlockSpec((tq, LANES), lambda b, g, i: (b * nq + i, lay.ga // LANES)),
                  pl.BlockSpec((n_ch, LANES), lambda b, g, i: (0, 0)),
                  pl.BlockSpec((LANES, seq), lambda b, g, i: (0, 0))],
        out_specs=pl.BlockSpec((tq, qw), lambda b, g, i: (b * nq + i, g)),
        out_shape=jax.ShapeDtypeStruct((n_batch * seq, N_KV_A * qw), _BF),
        scratch_shapes=[pltpu.VMEM((tq, seq), jnp.float32),
                        pltpu.VMEM((seq, HEAD_DIM), _BF), pltpu.VMEM((seq, 2 * HEAD_DIM), _BF),
                        pltpu.VMEM((seq, HEAD_DIM), _BF), pltpu.VMEM((seq, 2 * HEAD_DIM), _BF)],
        compiler_params=_params("parallel", "parallel", "arbitrary"),
    )(proj, kvc, kvc, proj, proj, proj, proj, proj, _cover_matrix(n_ch, LANES, n_s), _expand_matrix(LANES, seq))


def _dsa_prompt_kernel(iq_ref, iw_ref, q_ref, ik_ref, k_ref, v_ref, tri_ref, o_ref,
                       score_ref, key_ref, mask_ref, ikb_ref, kb_ref, vb_ref,
                       *, n_idx, n_heads, n_top, iw_lane, chunk):
    tq = iq_ref.shape[0]
    seq = mask_ref.shape[1]
    qi = pl.program_id(1)
    n_tiles = qi + 1
    row0 = pl.multiple_of(qi * tq, tq)
    t_col = qi * tq + lax.broadcasted_iota(jnp.int32, (tq, 1), 0)
    key_iota = lax.broadcasted_iota(jnp.int32, (tq, tq), 1)

    @pl.when(qi == 0)
    def _():
        for ref in (ikb_ref, kb_ref, vb_ref):
            ref[...] = jnp.zeros(ref.shape, ref.dtype)

    ikb_ref[pl.ds(row0, tq), :] = ik_ref[...].astype(_BF)
    kb_ref[pl.ds(row0, tq), :] = k_ref[...].astype(_BF)
    vb_ref[pl.ds(row0, tq), :] = jnp.concatenate([v_ref[...].astype(_BF), jnp.ones((tq, HEAD_DIM), _BF)], axis=1)
    iqs = _stack_heads(iq_ref[...], n_idx).astype(_BF)
    iw = iw_ref[...]

    def score_tile(kt, _):
        sl = pl.ds(pl.multiple_of(kt * tq, tq), tq)
        sc = _indexer_scores(iqs, ikb_ref[sl, :], iw, iw_lane, n_idx)
        score_ref[:, sl] = jnp.where(kt * tq + key_iota <= t_col, sc, NEG)
        return 0

    lax.fori_loop(0, n_tiles, score_tile, 0)
    mask_ref[...] = jnp.zeros(mask_ref.shape, mask_ref.dtype)
    _topk_mask(score_ref, key_ref, mask_ref, tri_ref, n_tiles * (tq // LANES), n_top)
    causal = lax.broadcasted_iota(jnp.int32, (tq, seq), 1) <= t_col
    mask_ref[...] = jnp.where(causal, mask_ref[...], 0.0)

    def dsa_mask(c, j):
        return mask_ref[:, pl.ds(pl.multiple_of(c * chunk + j * LANES, LANES), LANES)] > 0.5

    qs = _stack_heads(q_ref[...], n_heads).astype(_BF)
    o = _attend(qs, kb_ref, vb_ref, 0, (row0 + tq + chunk - 1) // chunk, chunk, dsa_mask, n_heads)
    for h in range(n_heads):
        o_ref[:, h * HEAD_DIM:(h + 1) * HEAD_DIM] = o[h].astype(o_ref.dtype)


def dsa_prompt(proj, lay, n_batch, seq, tq, chunk=512):
    assert tq % LANES == 0 and seq % chunk == 0 and chunk % tq == 0
    nq = seq // tq
    n_top = min(DSA_TOPK, seq // 4)
    iqw = lay.n_idx_heads * IDX_DIM
    qw = lay.n_heads_b * HEAD_DIM

    def tile(col):
        c0 = col // HEAD_DIM
        return pl.BlockSpec((tq, HEAD_DIM), lambda b, i: (b * nq + i, c0))

    return pl.pallas_call(
        functools.partial(_dsa_prompt_kernel, n_idx=lay.n_idx_heads, n_heads=lay.n_heads_b, n_top=n_top,
                          iw_lane=lay.iw % LANES, chunk=chunk),
        grid=(n_batch, nq),
        in_specs=[pl.BlockSpec((tq, iqw), lambda b, i: (b * nq + i, lay.iq // iqw)),
                  pl.BlockSpec((tq, LANES), lambda b, i: (b * nq + i, lay.iw // LANES)),
                  pl.BlockSpec((tq, qw), lambda b, i: (b * nq + i, lay.qb // qw)),
                  tile(lay.ik), tile(lay.dk), tile(lay.dv),
                  pl.BlockSpec((LANES, LANES), lambda b, i: (0, 0))],
        out_specs=pl.BlockSpec((tq, qw), lambda b, i: (b * nq + i, 0)),
        out_shape=jax.ShapeDtypeStruct((n_batch * seq, qw), _BF),
        scratch_shapes=[pltpu.VMEM((tq, seq), jnp.float32), pltpu.VMEM((tq, seq), jnp.int32),
                        pltpu.VMEM((tq, seq), jnp.float32),
                        pltpu.VMEM((seq, IDX_DIM), _BF), pltpu.VMEM((seq, HEAD_DIM), _BF),
                        pltpu.VMEM((seq, 2 * HEAD_DIM), _BF)],
        compiler_params=_params("parallel", "arbitrary"),
    )(proj, proj, proj, proj, proj, proj, _tri_matrix())


def _page_specs(block, second_block=0):
    tail = (0,) * (len(block) - 3)
    def spec(k):
        return pl.BlockSpec(block, lambda b, s, pt: (pt[b, s * PAGES_PER_STEP + k], 0, second_block) + tail)
    return [spec(k) for k in range(PAGES_PER_STEP)]


def _compress_sample_kernel(pt_ref, *refs):
    pages = refs[:PAGES_PER_STEP]
    w1_ref, w2_ref, pe_ref, cos_ref, sin_ref, o_ref, rows_ref = refs[PAGES_PER_STEP:]
    step = pl.program_id(1)
    page = pages[0].shape[1]
    n_slabs = rows_ref.shape[0]
    for k in range(PAGES_PER_STEP):
        start = pl.multiple_of((step * PAGES_PER_STEP + k) * page, page)
        for c in range(n_slabs):
            rows_ref.at[c][pl.ds(start, page), :] = pages[k][0, :, c // N_KV_A, c % N_KV_A, :]

    @pl.when(step == pl.num_programs(1) - 1)
    def _():
        n_ch = o_ref.shape[-2]
        for kv in range(2):
            for g in range(N_KV_A):
                c = kv * N_KV_A + g
                y = _compress_rows(lambda s: rows_ref.at[c][pl.ds(s, n_ch, stride=CMP_STRIDE), :],
                                   pe_ref[kv], lambda r: w1_ref[kv, r], w2_ref[kv], n_ch)
                if kv == 0:
                    y = _rotary(y, cos_ref[...], sin_ref[...])
                o_ref[0, kv, g] = y.astype(o_ref.dtype)


def compress_sample(cache, page_table, w1, w2, pe, cos_end, sin_end):
    n_batch, n_pages = page_table.shape
    page = cache.shape[1]
    n_ch = n_pages * page // CMP_STRIDE
    assert n_pages % PAGES_PER_STEP == 0
    const = lambda nd: (lambda b, s, pt: (0,) * nd)
    grid_spec = pltpu.PrefetchScalarGridSpec(
        num_scalar_prefetch=1,
        grid=(n_batch, n_pages // PAGES_PER_STEP),
        in_specs=_page_specs((1, page, 2, N_KV_A, HEAD_DIM), 0) + [
            pl.BlockSpec(w1.shape, const(4)), pl.BlockSpec(w2.shape, const(3)), pl.BlockSpec(pe.shape, const(3)),
            pl.BlockSpec((n_ch, HEAD_DIM), const(2)), pl.BlockSpec((n_ch, HEAD_DIM), const(2))],
        out_specs=pl.BlockSpec((1, 2, N_KV_A, n_ch, HEAD_DIM), lambda b, s, pt: (b, 0, 0, 0, 0)),
        scratch_shapes=[pltpu.VMEM((2 * N_KV_A, n_pages * page, HEAD_DIM), jnp.float32)],
    )
    return pl.pallas_call(
        _compress_sample_kernel,
        grid_spec=grid_spec,
        out_shape=jax.ShapeDtypeStruct((n_batch, 2, N_KV_A, n_ch, HEAD_DIM), _BF),
        compiler_params=_params("parallel", "arbitrary"),
    )(page_table, *([cache] * PAGES_PER_STEP), w1, w2, pe, cos_end, sin_end)


def _nsa_sample_kernel(pt_ref, *refs, n_heads, n_s, past_len):
    pages = refs[:PAGES_PER_STEP]
    (q_ref, gate_ref, kvc_ref, nsk_ref, nsv_ref, nwk_ref, nwv_ref, win_ref, cover_ref, expand_ref,
     o_ref, m_ref, l_ref, acc_ref, ocmp_ref, sel_ref) = refs[PAGES_PER_STEP:]
    step = pl.program_id(1)
    ts = q_ref.shape[0]
    qw = n_heads * HEAD_DIM
    n_grp = N_KV_A
    t_col = past_len + lax.broadcasted_iota(jnp.int32, (ts, 1), 0)

    def queries(g):
        return _stack_heads(q_ref[:, g * qw:(g + 1) * qw], n_heads).astype(_BF)

    def group_lanes(x, g):
        return x[:, g * HEAD_DIM:(g + 1) * HEAD_DIM]

    @pl.when(step == 0)
    def _():
        for g in range(n_grp):
            o_cmp, sel = _cmp_attend_and_choose(queries(g), kvc_ref[0, 0, g], kvc_ref[0, 1, g], cover_ref[...],
                                                t_col, n_heads, n_s)
            ocmp_ref[g] = o_cmp
            sel_ref[g] = jnp.where(sel, 1.0, 0.0)
            m_ref[g], l_ref[g], acc_ref[g] = _flash_init(n_heads, ts)

    state = [(m_ref[g], l_ref[g], acc_ref[g]) for g in range(n_grp)]
    for g in range(n_grp):
        chosen = jnp.dot(sel_ref[g].astype(_BF), expand_ref[...], preferred_element_type=jnp.float32) > 0.5
        k = jnp.concatenate([p[0, :, 0, g, :] for p in pages], axis=0).astype(_BF)
        v = jnp.concatenate([p[0, :, 1, g, :] for p in pages], axis=0).astype(_BF)
        state[g] = _flash_update(queries(g), k, v, chosen, state[g], n_heads)
    for g in range(n_grp):
        m_ref[g], l_ref[g], acc_ref[g] = state[g]

    @pl.when(step == pl.num_programs(1) - 1)
    def _():
        gates = jax.nn.sigmoid(gate_ref[...])
        n_all = n_grp * n_heads
        w_len = win_ref.shape[1]
        row = lax.broadcasted_iota(jnp.int32, (ts, LANES), 0)
        lane = lax.broadcasted_iota(jnp.int32, (ts, LANES), 1)
        new_causal = (lane <= row) & (lane < ts)
        wlane = lax.broadcasted_iota(jnp.int32, (ts, w_len + LANES), 1)
        k_pos = past_len - w_len + wlane
        d = t_col - k_pos
        win_mask = (d >= 0) & (d < WINDOW) & (k_pos >= 0) & (wlane < w_len + ts)
        new_blk = past_len // SEL_BLOCK
        for g in range(n_grp):
            qs = queries(g)
            chosen = sel_ref[g][:, new_blk:new_blk + 1] > 0.5
            k_new = _pad_rows(group_lanes(nsk_ref[...], g), LANES).astype(_BF)
            v_new = _pad_rows(group_lanes(nsv_ref[...], g), LANES).astype(_BF)
            o_slc = _flash_finish(_flash_update(qs, k_new, v_new, new_causal & chosen,
                                                (m_ref[g], l_ref[g], acc_ref[g]), n_heads))
            kw = jnp.concatenate([win_ref[0, :, 0, g, :], _pad_rows(group_lanes(nwk_ref[...], g), LANES)],
                                 axis=0).astype(_BF)
            vw = jnp.concatenate([win_ref[0, :, 1, g, :], _pad_rows(group_lanes(nwv_ref[...], g), LANES)],
                                 axis=0).astype(_BF)
            o_win = _flash_finish(_flash_update(qs, kw, vw, win_mask, _flash_init(n_heads, ts), n_heads))
            o_cmp = ocmp_ref[g]
            for r in range(n_heads):
                h = g * n_heads + r
                o = (gates[:, h:h + 1] * o_cmp[r] + gates[:, n_all + h:n_all + h + 1] * o_slc[r]
                     + gates[:, 2 * n_all + h:2 * n_all + h + 1] * o_win[r])
                o_ref[0, :, h * HEAD_DIM:(h + 1) * HEAD_DIM] = o.astype(o_ref.dtype)


def nsa_sample(proj, row0, kvc, cache, win_buf, page_table, lay, ts):
    n_batch, n_pages = page_table.shape
    page = cache.shape[1]
    past_len = n_pages * page
    n_heads = lay.n_heads_a // N_KV_A
    n_ch = kvc.shape[-2]
    n_s = -(-(past_len + ts) // SEL_BLOCK)
    sel_lanes = -(-n_s // LANES) * LANES
    kvw = N_KV_A * HEAD_DIM
    assert n_pages % PAGES_PER_STEP == 0 and row0 % ts == 0 and ts <= SEL_BLOCK and past_len % SEL_BLOCK == 0
    assert ts % 8 == 0 and lay.ga % LANES == 0
    assert (past_len + ts - CMP_BLOCK) // CMP_STRIDE + 1 <= n_ch and n_ch * CMP_STRIDE <= past_len
    r0 = row0 // ts
    keys_per_step = PAGES_PER_STEP * page
    qw = lay.n_heads_a * HEAD_DIM
    rows = lambda width, col: pl.BlockSpec((ts, width), lambda b, s, pt: (r0 + b, col // width))
    const = lambda nd: (lambda b, s, pt: (0,) * nd)
    grid_spec = pltpu.PrefetchScalarGridSpec(
        num_scalar_prefetch=1,
        grid=(n_batch, n_pages // PAGES_PER_STEP),
        in_specs=_page_specs((1, page, 2, N_KV_A, HEAD_DIM), 1) + [
            rows(qw, lay.qa), rows(LANES, lay.ga),
            pl.BlockSpec((1, 2, N_KV_A, n_ch, HEAD_DIM), lambda b, s, pt: (b, 0, 0, 0, 0)),
            rows(kvw, lay.selk), rows(kvw, lay.selv), rows(kvw, lay.wink), rows(kvw, lay.winv),
            pl.BlockSpec((1,) + win_buf.shape[1:], lambda b, s, pt: (b, 0, 0, 0, 0)),
            pl.BlockSpec((n_ch, sel_lanes), const(2)),
            pl.BlockSpec((sel_lanes, keys_per_step), lambda b, s, pt: (0, s))],
        out_specs=pl.BlockSpec((1, ts, qw), lambda b, s, pt: (b, 0, 0)),
        scratch_shapes=[pltpu.VMEM((N_KV_A, n_heads, ts, 1), jnp.float32),
                        pltpu.VMEM((N_KV_A, n_heads, ts, 1), jnp.float32),
                        pltpu.VMEM((N_KV_A, n_heads, ts, HEAD_DIM), jnp.float32),
                        pltpu.VMEM((N_KV_A, n_heads, ts, HEAD_DIM), jnp.float32),
                        pltpu.VMEM((N_KV_A, ts, sel_lanes), jnp.float32)],
    )
    return pl.pallas_call(
        functools.partial(_nsa_sample_kernel, n_heads=n_heads, n_s=n_s, past_len=past_len),
        grid_spec=grid_spec,
        out_shape=jax.ShapeDtypeStruct((n_batch, ts, qw), _BF),
        compiler_params=_params("parallel", "arbitrary"),
    )(page_table, *([cache] * PAGES_PER_STEP), proj, proj, kvc, proj, proj, proj, proj, win_buf,
      _cover_matrix(n_ch, sel_lanes, n_s), _expand_matrix(sel_lanes, past_len))


def _dsa_sample_select_kernel(pt_ref, *refs, n_idx, n_top, iw_lane, past_len):
    pages = refs[:PAGES_PER_STEP]
    iq_ref, iw_ref, nik_ref, tri_ref, mask_ref, score_ref, key_ref = refs[PAGES_PER_STEP:]
    step = pl.program_id(1)
    ts = iq_ref.shape[0]
    keys_per_step = PAGES_PER_STEP * pages[0].shape[1]
    iqs = _stack_heads(iq_ref[...], n_idx).astype(_BF)
    iw = iw_ref[...]
    ik = jnp.concatenate([p[0] for p in pages], axis=0).astype(_BF)
    start = pl.multiple_of(step * keys_per_step, keys_per_step)
    score_ref[:, pl.ds(start, keys_per_step)] = _indexer_scores(iqs, ik, iw, iw_lane, n_idx)

    @pl.when(step == pl.num_programs(1) - 1)
    def _():
        sc = _indexer_scores(iqs, _pad_rows(nik_ref[...], LANES).astype(_BF), iw, iw_lane, n_idx)
        row = lax.broadcasted_iota(jnp.int32, (ts, LANES), 0)
        lane = lax.broadcasted_iota(jnp.int32, (ts, LANES), 1)
        score_ref[:, past_len:past_len + LANES] = jnp.where(lane < ts, jnp.where(lane <= row, sc, NEG), -jnp.inf)
        _topk_mask(score_ref, key_ref, mask_ref.at[0], tri_ref, past_len // LANES + 1, n_top)


def dsa_sample_select(proj, row0, cache_idx, page_table, lay, ts):
    n_batch, n_pages = page_table.shape
    page = cache_idx.shape[1]
    past_len = n_pages * page
    assert n_pages % PAGES_PER_STEP == 0 and row0 % ts == 0 and ts <= LANES and past_len % LANES == 0
    n_top = min(DSA_TOPK, (past_len + ts) // 4)
    r0 = row0 // ts
    iqw = lay.n_idx_heads * IDX_DIM
    width = past_len + LANES
    rows = lambda w, col: pl.BlockSpec((ts, w), lambda b, s, pt: (r0 + b, col // w))
    grid_spec = pltpu.PrefetchScalarGridSpec(
        num_scalar_prefetch=1,
        grid=(n_batch, n_pages // PAGES_PER_STEP),
        in_specs=_page_specs((1, page, IDX_DIM), 0) + [
            rows(iqw, lay.iq), rows(LANES, lay.iw), rows(IDX_DIM, lay.ik),
            pl.BlockSpec((LANES, LANES), lambda b, s, pt: (0, 0))],
        out_specs=pl.BlockSpec((1, ts, width), lambda b, s, pt: (b, 0, 0)),
        scratch_shapes=[pltpu.VMEM((ts, width), jnp.float32), pltpu.VMEM((ts, width), jnp.int32)],
    )
    return pl.pallas_call(
        functools.partial(_dsa_sample_select_kernel, n_idx=lay.n_idx_heads, n_top=n_top,
                          iw_lane=lay.iw % LANES, past_len=past_len),
        grid_spec=grid_spec,
        out_shape=jax.ShapeDtypeStruct((n_batch, ts, width), jnp.float32),
        compiler_params=_params("parallel", "arbitrary"),
    )(page_table, *([cache_idx] * PAGES_PER_STEP), proj, proj, proj, _tri_matrix())


def _dsa_sample_attend_kernel(pt_ref, *refs, n_heads):
    pages = refs[:PAGES_PER_STEP]
    q_ref, mask_ref, nmask_ref, nk_ref, nv_ref, o_ref, m_ref, l_ref, acc_ref = refs[PAGES_PER_STEP:]
    step = pl.program_id(1)
    ts = q_ref.shape[0]
    qs = _stack_heads(q_ref[...], n_heads).astype(_BF)

    @pl.when(step == 0)
    def _():
        m_ref[...], l_ref[...], acc_ref[...] = _flash_init(n_heads, ts)

    k = jnp.concatenate([p[0, :, 0, :] for p in pages], axis=0).astype(_BF)
    v = jnp.concatenate([p[0, :, 1, :] for p in pages], axis=0).astype(_BF)
    m_ref[...], l_ref[...], acc_ref[...] = _flash_update(qs, k, v, mask_ref[0] > 0.5,
                                                         (m_ref[...], l_ref[...], acc_ref[...]), n_heads)

    @pl.when(step == pl.num_programs(1) - 1)
    def _():
        row = lax.broadcasted_iota(jnp.int32, (ts, LANES), 0)
        lane = lax.broadcasted_iota(jnp.int32, (ts, LANES), 1)
        mask = (nmask_ref[0] > 0.5) & (lane <= row) & (lane < ts)
        o = _flash_finish(_flash_update(qs, _pad_rows(nk_ref[...], LANES).astype(_BF),
                                        _pad_rows(nv_ref[...], LANES).astype(_BF), mask,
                                        (m_ref[...], l_ref[...], acc_ref[...]), n_heads))
        for h in range(n_heads):
            o_ref[0, :, h * HEAD_DIM:(h + 1) * HEAD_DIM] = o[h].astype(o_ref.dtype)


def dsa_sample_attend(proj, row0, mask, cache_kv, page_table, lay, ts):
    n_batch, n_pages = page_table.shape
    page = cache_kv.shape[1]
    past_len = n_pages * page
    keys_per_step = PAGES_PER_STEP * page
    r0 = row0 // ts
    qw = lay.n_heads_b * HEAD_DIM
    rows = lambda w, col: pl.BlockSpec((ts, w), lambda b, s, pt: (r0 + b, col // w))
    grid_spec = pltpu.PrefetchScalarGridSpec(
        num_scalar_prefetch=1,
        grid=(n_batch, n_pages // PAGES_PER_STEP),
        in_specs=_page_specs((1, page, 2, HEAD_DIM), 0) + [
            rows(qw, lay.qb),
            pl.BlockSpec((1, ts, keys_per_step), lambda b, s, pt: (b, 0, s)),
            pl.BlockSpec((1, ts, LANES), lambda b, s, pt: (b, 0, past_len // LANES)),
            rows(HEAD_DIM, lay.dk), rows(HEAD_DIM, lay.dv)],
        out_specs=pl.BlockSpec((1, ts, qw), lambda b, s, pt: (b, 0, 0)),
        scratch_shapes=[pltpu.VMEM((lay.n_heads_b, ts, 1), jnp.float32),
                        pltpu.VMEM((lay.n_heads_b, ts, 1), jnp.float32),
                        pltpu.VMEM((lay.n_heads_b, ts, HEAD_DIM), jnp.float32)],
    )
    return pl.pallas_call(
        functools.partial(_dsa_sample_attend_kernel, n_heads=lay.n_heads_b),
        grid_spec=grid_spec,
        out_shape=jax.ShapeDtypeStruct((n_batch, ts, qw), _BF),
        compiler_params=_params("parallel", "arbitrary"),
    )(page_table, *([cache_kv] * PAGES_PER_STEP), proj, mask, mask, proj, proj)


class Layout:
    def __init__(self, d_model, n_heads_a, n_heads_b, n_idx_heads):
        self.n_heads_a, self.n_heads_b, self.n_idx_heads = n_heads_a, n_heads_b, n_idx_heads
        src = np.cumsum([0, n_heads_a * HEAD_DIM, 6 * N_KV_A * HEAD_DIM, 3 * n_heads_a,
                         n_heads_b * HEAD_DIM, 2 * HEAD_DIM, n_idx_heads * IDX_DIM, n_idx_heads,
                         IDX_DIM, 2 * d_model])
        s_qa, s_kva, s_ga, s_qb, s_kvb, s_iq, s_iw, s_ik, s_mg, s_end = (int(v) for v in src)
        small = 3 * n_heads_a + n_idx_heads
        assert small <= LANES
        self.small_pad = LANES - small
        kvw = N_KV_A * HEAD_DIM
        self.pieces = [
            (s_qa, s_kva, True),
            (s_qb, s_kvb, True),
            (s_iq, s_iw, True),
            (s_kva, s_kva + 2 * kvw, False),
            (s_kva + 2 * kvw, s_kva + 3 * kvw, True),
            (s_kva + 3 * kvw, s_kva + 4 * kvw, False),
            (s_kva + 4 * kvw, s_kva + 5 * kvw, True),
            (s_kva + 5 * kvw, s_ga, False),
            (s_kvb, s_kvb + HEAD_DIM, True),
            (s_kvb + HEAD_DIM, s_iq, False),
            (s_ik, s_mg, True),
            (s_ga, s_qb, False),
            (s_iw, s_ik, False),
            None,
            (s_mg, s_end, False),
        ]
        off = 0
        starts = []
        for p in self.pieces:
            starts.append(off)
            off += self.small_pad if p is None else p[1] - p[0]
        (self.qa, self.qb, self.iq, self.cmp, self.selk, self.selv, self.wink, self.winv, self.dk,
         self.dv, self.ik, self.ga, self.iw, _, self.mg) = starts
        self.width = off
        assert self.width % LANES == 0
        flags = np.zeros(self.width // LANES, np.int32)
        for st, p in zip(starts, self.pieces):
            if p is not None and p[2]:
                assert st % LANES == 0 and (p[1] - p[0]) % LANES == 0
                flags[st // LANES:(st + p[1] - p[0]) // LANES] = 1
        self.rope_flags = flags

    def pack(self, w_in):
        cols = []
        for p in self.pieces:
            if p is None:
                cols.append(jnp.zeros((w_in.shape[0], self.small_pad), w_in.dtype))
            else:
                cols.append(w_in[:, p[0]:p[1]])
        return jnp.concatenate(cols, axis=1).astype(_BF)


def rope_tables(pos):
    half = HEAD_DIM // 2
    inv = ROPE_THETA ** (-jnp.arange(half, dtype=jnp.float32) / half)
    ang = pos.astype(jnp.float32)[:, None] * inv[None, :]
    cos, sin = jnp.cos(ang), jnp.sin(ang)
    return jnp.concatenate([cos, cos], axis=1), jnp.concatenate([-sin, sin], axis=1)


def _swiglu_half_step(h, g, wg, wu, wd, w_index):
    xn = rmsnorm(h, g, _BF)
    a = ffn_gate_up(xn, wg, wu, w_index, tm=1408, tn=256)
    return resid_matmul(a, wd, w_index, h, 0.5, tm=768, tn=256)


def kernel(x_prompt, x_sample, cache_nsa_kv, cache_nsa_win, cache_dsa_kv, cache_dsa_idx, page_table,
           g_norm, w_ffn_gate, w_ffn_up, w_ffn_down, w_in, w_cmp1, w_cmp2, cmp_pos,
           w_br_a, w_br_b, w_out, g_final):
    B, T, D = x_prompt.shape
    DB, Ts, _ = x_sample.shape
    depth = g_norm.shape[0]
    page = cache_nsa_kv.shape[2]
    past_len = page_table.shape[1] * page
    n_heads_a = w_br_a.shape[1] // HEAD_DIM
    n_heads_b = w_br_b.shape[1] // HEAD_DIM
    G = N_KV_A
    lay = Layout(D, n_heads_a, n_heads_b, n_heads_b // 2)
    Mp, Ms = B * T, DB * Ts

    pos_p = jnp.arange(T, dtype=jnp.int32)
    pos_s = past_len + jnp.arange(Ts, dtype=jnp.int32)
    cos, sin = rope_tables(jnp.concatenate([jnp.tile(pos_p, B), jnp.tile(pos_s, DB)]))
    rope_flags = jnp.asarray(lay.rope_flags)

    def block_end_tables(n_ch):
        return rope_tables(jnp.arange(n_ch, dtype=jnp.int32) * CMP_STRIDE + CMP_BLOCK - 1)

    h = jnp.concatenate([x_prompt.reshape(Mp, D), x_sample.reshape(Ms, D)], axis=0)
    outs = [[] for _ in range(8)]
    for l in range(depth):
        h = _swiglu_half_step(h, g_norm[l, 0], w_ffn_gate, w_ffn_up, w_ffn_down, (l, 0))

        u = rmsnorm(h, g_norm[l, 1], _BF)
        proj, nsa_rows, win_rows, dsa_rows, ik_rows = in_project(u, lay.pack(w_in[l]), rope_flags, cos, sin, lay,
                                                                 tm=1408, tn=512)
        w1 = w_cmp1[l].reshape(2, 2, CMP_STRIDE * HEAD_DIM, -1).astype(_BF)
        w2 = w_cmp2[l].astype(_BF)

        kvc_p = compress_prompt(proj, B, T, lay.cmp, w1, w2, cmp_pos[l], *block_end_tables(T // CMP_STRIDE))
        o_a_p = nsa_prompt(proj, kvc_p, lay, B, T, tq=2 * LANES)
        o_b_p = dsa_prompt(proj, lay, B, T, tq=2 * LANES)

        kvc_s = compress_sample(cache_nsa_kv[l], page_table, w1, w2, cmp_pos[l],
                                *block_end_tables(past_len // CMP_STRIDE))
        o_a_s = nsa_sample(proj, Mp, kvc_s, cache_nsa_kv[l], cache_nsa_win[l], page_table, lay, Ts)
        top_mask = dsa_sample_select(proj, Mp, cache_dsa_idx[l], page_table, lay, Ts)
        o_b_s = dsa_sample_attend(proj, Mp, top_mask, cache_dsa_kv[l], page_table, lay, Ts)

        o_a = jnp.concatenate([o_a_p, o_a_s.reshape(Ms, -1)], axis=0)
        o_b = jnp.concatenate([o_b_p, o_b_s.reshape(Ms, -1)], axis=0)
        m = merge_branches(o_a, o_b, w_br_a, w_br_b, (l,), proj, lay.mg, tm=768, tn=512)
        h = resid_matmul(m, w_out, (l,), h, 1.0, tm=768, tn=512)

        win_p = win_rows[:Mp].reshape(B, T, 2, G, HEAD_DIM)
        win_s = win_rows[Mp:].reshape(DB, Ts, 2, G, HEAD_DIM)
        outs[0].append(nsa_rows[:Mp].reshape(B, T, 4, G, HEAD_DIM))
        outs[1].append(win_p[:, T - min(WINDOW, T):])
        outs[2].append(dsa_rows[:Mp].reshape(B, T, 2, HEAD_DIM))
        outs[3].append(ik_rows[:Mp].reshape(B, T, IDX_DIM))
        outs[4].append(nsa_rows[Mp:].reshape(DB, Ts, 4, G, HEAD_DIM))
        outs[5].append(jnp.concatenate([cache_nsa_win[l], win_s], axis=1)[:, Ts:])
        outs[6].append(dsa_rows[Mp:].reshape(DB, Ts, 2, HEAD_DIM))
        outs[7].append(ik_rows[Mp:].reshape(DB, Ts, IDX_DIM))

        h = _swiglu_half_step(h, g_norm[l, 2], w_ffn_gate, w_ffn_up, w_ffn_down, (l, 1))

    y_p = rmsnorm(h, g_final, jnp.float32, 0, Mp)
    y_s = rmsnorm(h, g_final, jnp.float32, Mp, Ms)
    return (y_p.reshape(B, T, D), y_s.reshape(DB, Ts, D), *(jnp.stack(o) for o in outs))
```

```python
import functools

import jax
import jax.numpy as jnp
import numpy as np
from jax import lax
from jax.experimental import pallas as pl
from jax.experimental.pallas import tpu as pltpu

HEAD_DIM = 128
N_KV_A = 2
IDX_DIM = 128
CMP_BLOCK = 32
CMP_STRIDE = 16
SEL_BLOCK = 64
N_SEL = 16
N_LOCAL_SEL = 2
WINDOW = 512
DSA_TOPK = 256
ROPE_THETA = 10000.0
RMS_EPS = 1e-6
NEG = -1e30
BIG = 1e30
TINY = 1e-30

LANES = 128
VMEM_LIMIT = 56 * 1024 * 1024
PAGES_PER_STEP = 32

_NT = (((1,), (1,)), ((), ()))
_BF = jnp.bfloat16


def _params(*sem):
    return pltpu.CompilerParams(dimension_semantics=sem, vmem_limit_bytes=VMEM_LIMIT)


def _rmsnorm_kernel(x_ref, g_ref, o_ref):
    x = x_ref[...]
    y = x * lax.rsqrt(jnp.mean(x * x, axis=-1, keepdims=True) + RMS_EPS)
    o_ref[...] = (y * g_ref[...]).astype(o_ref.dtype)


def rmsnorm(x, g, out_dtype, row0=0, n_rows=None, tm=256):
    D = x.shape[1]
    n_rows = x.shape[0] - row0 if n_rows is None else n_rows
    assert row0 % tm == 0 and n_rows % tm == 0
    return pl.pallas_call(
        _rmsnorm_kernel,
        grid=(n_rows // tm,),
        in_specs=[pl.BlockSpec((tm, D), lambda i: (row0 // tm + i, 0)),
                  pl.BlockSpec((1, D), lambda i: (0, 0))],
        out_specs=pl.BlockSpec((tm, D), lambda i: (i, 0)),
        out_shape=jax.ShapeDtypeStruct((n_rows, D), out_dtype),
        compiler_params=_params("parallel"),
    )(x, g.reshape(1, D))


def _gateup_kernel(x_ref, wg_ref, wu_ref, o_ref):
    x = x_ref[...]
    g = jnp.dot(x, wg_ref[...].astype(_BF), preferred_element_type=jnp.float32)
    u = jnp.dot(x, wu_ref[...].astype(_BF), preferred_element_type=jnp.float32)
    o_ref[...] = (g * jax.nn.sigmoid(g) * u).astype(o_ref.dtype)


def _weight_spec(w, w_index, rows, tn):
    assert w.ndim == len(w_index) + 2 and w.shape[-2] == rows
    return pl.BlockSpec((None,) * len(w_index) + (rows, tn), lambda i, j: tuple(w_index) + (0, j))


def ffn_gate_up(xn, wg, wu, w_index, tm, tn):
    M, D = xn.shape
    F = wg.shape[-1]
    return pl.pallas_call(
        _gateup_kernel,
        grid=(M // tm, F // tn),
        in_specs=[pl.BlockSpec((tm, D), lambda i, j: (i, 0)),
                  _weight_spec(wg, w_index, D, tn),
                  _weight_spec(wu, w_index, D, tn)],
        out_specs=pl.BlockSpec((tm, tn), lambda i, j: (i, j)),
        out_shape=jax.ShapeDtypeStruct((M, F), _BF),
        compiler_params=_params("parallel", "parallel"),
    )(xn, wg, wu)


def _resid_matmul_kernel(a_ref, w_ref, r_ref, o_ref, *, scale):
    acc = jnp.dot(a_ref[...], w_ref[...].astype(_BF), preferred_element_type=jnp.float32)
    o_ref[...] = r_ref[...] + scale * acc


def resid_matmul(a, w, w_index, resid, scale, tm, tn):
    M, K = a.shape
    N = w.shape[-1]
    return pl.pallas_call(
        functools.partial(_resid_matmul_kernel, scale=scale),
        grid=(M // tm, N // tn),
        in_specs=[pl.BlockSpec((tm, K), lambda i, j: (i, 0), pipeline_mode=pl.Buffered(1)),
                  _weight_spec(w, w_index, K, tn),
                  pl.BlockSpec((tm, tn), lambda i, j: (i, j))],
        out_specs=pl.BlockSpec((tm, tn), lambda i, j: (i, j)),
        out_shape=jax.ShapeDtypeStruct((M, N), jnp.float32),
        compiler_params=_params("parallel", "parallel"),
    )(a, w, resid)


def _rotary(y, cos, sin):
    return y * cos + pltpu.roll(y, HEAD_DIM // 2, axis=1) * sin


def _inproj_kernel(flags_ref, x_ref, w_ref, cos_ref, sin_ref, o_ref, *row_refs, n_chunks, routes):
    j = pl.program_id(1)
    acc = jnp.dot(x_ref[...], w_ref[...], preferred_element_type=jnp.float32)
    for c in range(n_chunks):
        sl = slice(c * LANES, (c + 1) * LANES)
        y = acc[:, sl]
        flag = flags_ref[j * n_chunks + c]

        @pl.when(flag == 1)
        def _():
            o_ref[:, sl] = _rotary(y, cos_ref[...], sin_ref[...])

        @pl.when(flag == 0)
        def _():
            o_ref[:, sl] = y

    for chunk, out_idx, index in routes:
        @pl.when(j == chunk // n_chunks)
        def _():
            c = chunk % n_chunks
            row_refs[out_idx][(slice(None),) + index + (slice(None),)] = o_ref[:, c * LANES:(c + 1) * LANES]


def in_project(u, w, rope_flags, cos, sin, lay, tm, tn):
    M, D = u.shape
    N = w.shape[1]
    n_chunks = tn // LANES
    G = N_KV_A
    routes = ([(lay.cmp // LANES + k, 0, (k // G, k % G)) for k in range(4 * G)]
              + [(lay.wink // LANES + k, 1, (k // G, k % G)) for k in range(2 * G)]
              + [(lay.dk // LANES + k, 2, (k,)) for k in range(2)]
              + [(lay.ik // LANES, 3, ())])
    row_shapes = [(M, 4, G, HEAD_DIM), (M, 2, G, HEAD_DIM), (M, 2, HEAD_DIM), (M, IDX_DIM)]
    grid_spec = pltpu.PrefetchScalarGridSpec(
        num_scalar_prefetch=1,
        grid=(M // tm, N // tn),
        in_specs=[pl.BlockSpec((tm, D), lambda i, j, f: (i, 0), pipeline_mode=pl.Buffered(1)),
                  pl.BlockSpec((D, tn), lambda i, j, f: (0, j)),
                  pl.BlockSpec((tm, LANES), lambda i, j, f: (i, 0)),
                  pl.BlockSpec((tm, LANES), lambda i, j, f: (i, 0))],
        out_specs=[pl.BlockSpec((tm, tn), lambda i, j, f: (i, j))]
        + [pl.BlockSpec((tm,) + shp[1:], lambda i, j, f, nd=len(shp): (i,) + (0,) * (nd - 1)) for shp in row_shapes],
    )
    return pl.pallas_call(
        functools.partial(_inproj_kernel, n_chunks=n_chunks, routes=routes),
        grid_spec=grid_spec,
        out_shape=[jax.ShapeDtypeStruct((M, N), jnp.float32)]
        + [jax.ShapeDtypeStruct(shp, jnp.float32) for shp in row_shapes],
        compiler_params=_params("parallel", "arbitrary"),
    )(rope_flags, u, w, cos, sin)


def _merge_kernel(oa_ref, ob_ref, wa_ref, wb_ref, ga_ref, gb_ref, o_ref):
    ya = jnp.dot(oa_ref[...], wa_ref[...].astype(_BF), preferred_element_type=jnp.float32)
    yb = jnp.dot(ob_ref[...], wb_ref[...].astype(_BF), preferred_element_type=jnp.float32)
    m = jax.nn.sigmoid(ga_ref[...]) * ya + jax.nn.sigmoid(gb_ref[...]) * yb
    o_ref[...] = m.astype(o_ref.dtype)


def merge_branches(o_a, o_b, w_a, w_b, w_index, proj, mg_col, tm, tn):
    M, K = o_a.shape
    N = w_a.shape[-1]
    ja = mg_col // tn
    jb = (mg_col + N) // tn
    return pl.pallas_call(
        _merge_kernel,
        grid=(M // tm, N // tn),
        in_specs=[pl.BlockSpec((tm, K), lambda i, j: (i, 0), pipeline_mode=pl.Buffered(1)),
                  pl.BlockSpec((tm, K), lambda i, j: (i, 0), pipeline_mode=pl.Buffered(1)),
                  _weight_spec(w_a, w_index, K, tn),
                  _weight_spec(w_b, w_index, K, tn),
                  pl.BlockSpec((tm, tn), lambda i, j: (i, ja + j)),
                  pl.BlockSpec((tm, tn), lambda i, j: (i, jb + j))],
        out_specs=pl.BlockSpec((tm, tn), lambda i, j: (i, j)),
        out_shape=jax.ShapeDtypeStruct((M, N), _BF),
        compiler_params=_params("parallel", "parallel"),
    )(o_a, o_b, w_a, w_b, proj, proj)


def _stack_heads(x, n_heads):
    return jnp.concatenate([x[:, h * HEAD_DIM:(h + 1) * HEAD_DIM] for h in range(n_heads)], axis=0)


def _flash_update(qs, k, v, mask, carry, n_rep):
    m, l, acc = carry
    rows, tk = qs.shape[0], k.shape[0]
    tq = rows // n_rep
    scale = HEAD_DIM ** -0.5
    s = lax.dot_general(qs, k, _NT, preferred_element_type=jnp.float32).reshape(n_rep, tq, tk)
    mask = mask[None]
    s = jnp.where(mask, s, NEG)
    m_new = jnp.maximum(m, jnp.max(s, axis=-1, keepdims=True))
    p = jnp.where(mask, jnp.exp((s - m_new) * scale), 0.0)
    alpha = jnp.exp((m - m_new) * scale)
    l = alpha * l + jnp.sum(p, axis=-1, keepdims=True)
    pv = jnp.dot(p.reshape(rows, tk).astype(_BF), v, preferred_element_type=jnp.float32)
    return m_new, l, alpha * acc + pv.reshape(n_rep, tq, HEAD_DIM)


def _flash_init(n_rep, tq):
    return (jnp.full((n_rep, tq, 1), NEG, jnp.float32), jnp.zeros((n_rep, tq, 1), jnp.float32),
            jnp.zeros((n_rep, tq, HEAD_DIM), jnp.float32))


def _flash_finish(carry):
    _, l, acc = carry
    return acc * (1.0 / jnp.maximum(l, TINY))


def _attend(qs, k_ref, v_ref, start, n_chunks, chunk, mask_fn, n_rep):
    rows = qs.shape[0]
    tq = rows // n_rep
    n_lane_tiles = chunk // LANES
    scale = HEAD_DIM ** -0.5

    def scores(c):
        sl = pl.ds(pl.multiple_of(start + c * chunk, LANES), chunk)
        return lax.dot_general(qs, k_ref[sl, :], _NT, preferred_element_type=jnp.float32), sl

    def lane_tile(s, j):
        return s[:, j * LANES:(j + 1) * LANES].reshape(n_rep, tq, LANES)

    def row_max(c, mx):
        s, _ = scores(c)
        for j in range(n_lane_tiles):
            mx = jnp.maximum(mx, jnp.where(mask_fn(c, j)[None], lane_tile(s, j), NEG))
        return mx

    mx = lax.fori_loop(0, n_chunks, row_max, jnp.full((n_rep, tq, LANES), NEG, jnp.float32))
    m = jnp.broadcast_to(jnp.max(mx, axis=-1, keepdims=True), mx.shape)

    def accumulate(c, acc):
        s, sl = scores(c)
        p = [jnp.where(mask_fn(c, j)[None], jnp.exp((lane_tile(s, j) - m) * scale), 0.0)
             .astype(_BF).reshape(rows, LANES) for j in range(n_lane_tiles)]
        return acc + jnp.dot(jnp.concatenate(p, axis=1), v_ref[sl, :], preferred_element_type=jnp.float32)

    acc = lax.fori_loop(0, n_chunks, accumulate, jnp.zeros((rows, 2 * HEAD_DIM), jnp.float32))
    out = acc[:, :HEAD_DIM] * (1.0 / jnp.maximum(acc[:, HEAD_DIM:], TINY))
    return out.reshape(n_rep, tq, HEAD_DIM)


def _compress_rows(load, pe, w1, w2, n_ch):
    rows = [load(s) for s in range(CMP_STRIDE)]
    half = [jnp.concatenate([(rows[s] + pe[r * CMP_STRIDE + s:r * CMP_STRIDE + s + 1, :]).astype(_BF)
                             for s in range(CMP_STRIDE)], axis=1) for r in range(2)]
    h0 = jnp.dot(half[0], w1(0), preferred_element_type=jnp.float32)
    h1 = jnp.dot(half[1], w1(1), preferred_element_type=jnp.float32)
    h = h0 + pltpu.roll(h1, n_ch - 1, axis=0)
    return jnp.dot((h * jax.nn.sigmoid(h)).astype(_BF), w2, preferred_element_type=jnp.float32)


def _cmp_attend_and_choose(qs, kc, vc, cover, t_col, n_heads, n_s):
    rows = qs.shape[0]
    tq = rows // n_heads
    n_ch, lanes = cover.shape
    scale = HEAD_DIM ** -0.5
    s = lax.dot_general(qs, kc, _NT, preferred_element_type=jnp.float32).reshape(n_heads, tq, n_ch)
    end = lax.broadcasted_iota(jnp.int32, (tq, n_ch), 1) * CMP_STRIDE + (CMP_BLOCK - 1)
    cmask = (end <= t_col)[None]
    s = jnp.where(cmask, s, NEG)
    m = jnp.max(s, axis=-1, keepdims=True)
    p = jnp.where(cmask, jnp.exp((s - m) * scale), 0.0)
    p = p * (1.0 / jnp.maximum(jnp.sum(p, axis=-1, keepdims=True), TINY))
    o_cmp = jnp.dot(p.reshape(rows, n_ch).astype(_BF), vc, preferred_element_type=jnp.float32)
    imp = jnp.dot(jnp.sum(p, axis=0).astype(_BF), cover, preferred_element_type=jnp.float32)

    lane = lax.broadcasted_iota(jnp.int32, (tq, lanes), 1)
    lane_f = lane.astype(jnp.float32)
    jt = lax.shift_right_arithmetic(t_col, jnp.int32(SEL_BLOCK.bit_length() - 1))
    adm = lane <= jt
    forced = adm & ((lane == 0) | (lane > jt - N_LOCAL_SEL))
    work = jnp.where(forced, BIG, jnp.where(adm, imp, NEG))
    work = jnp.where(lane < n_s, work, -jnp.inf)
    sel = jnp.zeros((tq, lanes), jnp.bool_)
    for _ in range(min(N_SEL, n_s)):
        mx = jnp.max(work, axis=-1, keepdims=True)
        first = jnp.min(jnp.where(work == mx, lane_f, float(lanes)), axis=-1, keepdims=True)
        pick = lane_f == first
        sel = sel | pick
        work = jnp.where(pick, -jnp.inf, work)
    return o_cmp.reshape(n_heads, tq, HEAD_DIM), sel


def _topk_mask(score_ref, key_ref, mask_ref, tri_ref, n_tiles, n_top):
    tq = score_ref.shape[0]
    int_min = jnp.int32(-2 ** 31)

    def to_key(t, _):
        sl = pl.ds(pl.multiple_of(t * LANES, LANES), LANES)
        bits = lax.bitcast_convert_type(score_ref[:, sl], jnp.int32)
        key_ref[:, sl] = bits ^ ((bits >> 31) & jnp.int32(0x7FFFFFFF))
        return 0

    lax.fori_loop(0, n_tiles, to_key, 0)

    unroll = 8 if isinstance(n_tiles, int) else 1

    def count_ge(cand):
        def body(t, acc):
            sl = pl.ds(pl.multiple_of(t * LANES, LANES), LANES)
            return acc + jnp.where(key_ref[:, sl] >= cand, 1.0, 0.0)
        acc = lax.fori_loop(0, n_tiles, body, jnp.zeros((tq, LANES), jnp.float32), unroll=unroll)
        return jnp.sum(acc, axis=-1, keepdims=True)

    tau = jnp.zeros((tq, 1), jnp.int32)
    for bit in range(31, -1, -1):
        cand = tau | jnp.int32(-2 ** 31 if bit == 31 else 1 << bit)
        tau = jnp.where(count_ge(cand ^ int_min) >= float(n_top), cand, tau)
    thr = tau ^ int_min

    def count_gt(t, acc):
        sl = pl.ds(pl.multiple_of(t * LANES, LANES), LANES)
        return acc + jnp.where(key_ref[:, sl] > thr, 1.0, 0.0)

    n_gt = jnp.sum(lax.fori_loop(0, n_tiles, count_gt, jnp.zeros((tq, LANES), jnp.float32)),
                   axis=-1, keepdims=True)
    need = float(n_top) - n_gt
    clean = jnp.max(jnp.abs(count_ge(thr) - float(n_top))) == 0.0

    @pl.when(clean)
    def _():
        def keep_ge(t, _):
            sl = pl.ds(pl.multiple_of(t * LANES, LANES), LANES)
            mask_ref[:, sl] = jnp.where(key_ref[:, sl] >= thr, 1.0, 0.0)
            return 0
        lax.fori_loop(0, n_tiles, keep_ge, 0, unroll=unroll)

    @pl.when(jnp.logical_not(clean))
    def _():
        ones = jnp.ones((LANES, LANES), _BF)

        def cut(t, eq_before):
            sl = pl.ds(pl.multiple_of(t * LANES, LANES), LANES)
            key = key_ref[:, sl]
            eq = jnp.where(key == thr, 1.0, 0.0)
            rank = (jnp.dot(eq.astype(_BF), tri_ref[...], preferred_element_type=jnp.float32)
                    + jnp.dot(eq_before.astype(_BF), ones, preferred_element_type=jnp.float32))
            keep = (key > thr) | ((key == thr) & (rank <= need))
            mask_ref[:, sl] = jnp.where(keep, 1.0, 0.0)
            return eq_before + eq

        lax.fori_loop(0, n_tiles, cut, jnp.zeros((tq, LANES), jnp.float32))


def _indexer_scores(iqs, ik, iw, iw_lane, n_idx):
    tq = iqs.shape[0] // n_idx
    n = ik.shape[0]
    logits = lax.dot_general(iqs, ik, _NT, preferred_element_type=jnp.float32).reshape(n_idx, tq, n)
    logits = jnp.maximum(logits * IDX_DIM ** -0.5, 0.0)
    sc = jnp.zeros((tq, n), jnp.float32)
    for h in range(n_idx):
        sc = sc + logits[h] * iw[:, iw_lane + h:iw_lane + h + 1]
    return sc * n_idx ** -0.5


def _cover_matrix(n_ch, lanes, n_s):
    ci = np.arange(n_ch)[:, None] * CMP_STRIDE
    sj = np.arange(lanes)[None, :] * SEL_BLOCK
    return jnp.asarray((ci < sj + SEL_BLOCK) & (ci + CMP_BLOCK > sj) & (sj < n_s * SEL_BLOCK), _BF)


def _expand_matrix(lanes, n_keys):
    return jnp.asarray(np.arange(lanes)[:, None] == np.arange(n_keys)[None, :] // SEL_BLOCK, _BF)


def _tri_matrix():
    return jnp.asarray(np.arange(LANES)[:, None] <= np.arange(LANES)[None, :], _BF)


def _pad_rows(x, n):
    return jnp.concatenate([x, jnp.zeros((n - x.shape[0], x.shape[1]), x.dtype)], axis=0)


def _compress_kernel(x_ref, w1_ref, w2_ref, pe_ref, cos_ref, sin_ref, o_ref):
    n_ch = o_ref.shape[-2]
    y = _compress_rows(lambda s: x_ref[pl.ds(s, n_ch, stride=CMP_STRIDE), :], pe_ref[0],
                       lambda r: w1_ref[0, r], w2_ref[0], n_ch)

    @pl.when(pl.program_id(1) == 0)
    def _():
        o_ref[0, 0, 0] = _rotary(y, cos_ref[...], sin_ref[...]).astype(o_ref.dtype)

    @pl.when(pl.program_id(1) != 0)
    def _():
        o_ref[0, 0, 0] = y.astype(o_ref.dtype)


def compress_prompt(proj, n_batch, seq, cmp_col, w1, w2, pe, cos_end, sin_end):
    n_ch = seq // CMP_STRIDE
    col0 = cmp_col // HEAD_DIM
    return pl.pallas_call(
        _compress_kernel,
        grid=(n_batch, 2, N_KV_A),
        in_specs=[pl.BlockSpec((seq, HEAD_DIM), lambda b, kv, g: (b, col0 + kv * N_KV_A + g)),
                  pl.BlockSpec((1,) + w1.shape[1:], lambda b, kv, g: (kv, 0, 0, 0)),
                  pl.BlockSpec((1, w2.shape[1], HEAD_DIM), lambda b, kv, g: (kv, 0, 0)),
                  pl.BlockSpec((1, CMP_BLOCK, HEAD_DIM), lambda b, kv, g: (kv, 0, 0)),
                  pl.BlockSpec((n_ch, HEAD_DIM), lambda b, kv, g: (0, 0)),
                  pl.BlockSpec((n_ch, HEAD_DIM), lambda b, kv, g: (0, 0))],
        out_specs=pl.BlockSpec((1, 1, 1, n_ch, HEAD_DIM), lambda b, kv, g: (b, kv, g, 0, 0)),
        out_shape=jax.ShapeDtypeStruct((n_batch, 2, N_KV_A, n_ch, HEAD_DIM), _BF),
        compiler_params=_params("parallel", "parallel", "parallel"),
    )(proj, w1, w2, pe, cos_end, sin_end)


def _nsa_prompt_kernel(q_ref, kc_ref, vc_ref, selk_ref, selv_ref, wink_ref, winv_ref, gate_ref,
                       cover_ref, expand_ref, o_ref, selexp_ref, sk_ref, sv_ref, wk_ref, wv_ref,
                       *, n_heads, n_s, chunk):
    tq = q_ref.shape[0]
    seq = selexp_ref.shape[1]
    grp = pl.program_id(1)
    qi = pl.program_id(2)
    row0 = pl.multiple_of(qi * tq, tq)
    t_col = qi * tq + lax.broadcasted_iota(jnp.int32, (tq, 1), 0)

    @pl.when(qi == 0)
    def _():
        for ref in (sk_ref, sv_ref, wk_ref, wv_ref):
            ref[...] = jnp.zeros(ref.shape, ref.dtype)

    ones = jnp.ones((tq, HEAD_DIM), _BF)
    sk_ref[pl.ds(row0, tq), :] = selk_ref[...].astype(_BF)
    sv_ref[pl.ds(row0, tq), :] = jnp.concatenate([selv_ref[...].astype(_BF), ones], axis=1)
    wk_ref[pl.ds(row0, tq), :] = wink_ref[...].astype(_BF)
    wv_ref[pl.ds(row0, tq), :] = jnp.concatenate([winv_ref[...].astype(_BF), ones], axis=1)

    qs = _stack_heads(q_ref[...], n_heads).astype(_BF)
    o_cmp, sel = _cmp_attend_and_choose(qs, kc_ref[0, 0, 0], vc_ref[0, 0, 0], cover_ref[...], t_col, n_heads, n_s)
    chosen = jnp.dot(jnp.where(sel, 1.0, 0.0).astype(_BF), expand_ref[...], preferred_element_type=jnp.float32)
    causal = lax.broadcasted_iota(jnp.int32, (tq, seq), 1) <= t_col
    selexp_ref[...] = jnp.where(causal, chosen, 0.0)

    def sel_mask(c, j):
        return selexp_ref[:, pl.ds(pl.multiple_of(c * chunk + j * LANES, LANES), LANES)] > 0.5

    o_slc = _attend(qs, sk_ref, sv_ref, 0, (row0 + tq + chunk - 1) // chunk, chunk, sel_mask, n_heads)

    span = WINDOW + tq
    start = pl.multiple_of(jnp.maximum(row0 - WINDOW, 0), tq)
    d = t_col - (start + lax.broadcasted_iota(jnp.int32, (tq, span), 1))
    visible = (d >= 0) & (d < WINDOW)
    o_win = _attend(qs, wk_ref, wv_ref, start, 1, span, lambda c, j: visible[:, j * LANES:(j + 1) * LANES], n_heads)

    gates = jax.nn.sigmoid(gate_ref[...])
    gates = jnp.where(grp == 0, gates, pltpu.roll(gates, LANES - n_heads, axis=1))
    n_all = N_KV_A * n_heads
    for r in range(n_heads):
        o = (gates[:, r:r + 1] * o_cmp[r] + gates[:, n_all + r:n_all + r + 1] * o_slc[r]
             + gates[:, 2 * n_all + r:2 * n_all + r + 1] * o_win[r])
        o_ref[:, r * HEAD_DIM:(r + 1) * HEAD_DIM] = o.astype(o_ref.dtype)


def nsa_prompt(proj, kvc, lay, n_batch, seq, tq, chunk=512):
    n_heads = lay.n_heads_a // N_KV_A
    n_ch = kvc.shape[-2]
    n_s = -(-seq // SEL_BLOCK)
    assert tq % LANES == 0 and n_s <= LANES and seq % chunk == 0 and chunk % tq == 0
    assert WINDOW % tq == 0 and seq >= WINDOW + tq and lay.ga % LANES == 0 and N_KV_A == 2
    nq = seq // tq
    qw = n_heads * HEAD_DIM

    def tile(col):
        c0 = col // HEAD_DIM
        return pl.BlockSpec((tq, HEAD_DIM), lambda b, g, i: (b * nq + i, c0 + g))

    return pl.pallas_call(
        functools.partial(_nsa_prompt_kernel, n_heads=n_heads, n_s=n_s, chunk=chunk),
        grid=(n_batch, N_KV_A, nq),
        in_specs=[pl.BlockSpec((tq, qw), lambda b, g, i: (b * nq + i, lay.qa // qw + g)),
                  pl.BlockSpec((1, 1, 1, n_ch, HEAD_DIM), lambda b, g, i: (b, 0, g, 0, 0)),
                  pl.BlockSpec((1, 1, 1, n_ch, HEAD_DIM), lambda b, g, i: (b, 1, g, 0, 0)),
                  tile(lay.selk), tile(lay.selv), tile(lay.wink), tile(lay.winv),
                  pl.BlockSpec((tq, LANES), lambda b, g, i: (b * nq + i, lay.ga // LANES)),
                  pl.BlockSpec((n_ch, LANES), lambda b, g, i: (0, 0)),
                  pl.BlockSpec((LANES, seq), lambda b, g, i: (0, 0))],
        out_specs=pl.BlockSpec((tq, qw), lambda b, g, i: (b * nq + i, g)),
        out_shape=jax.ShapeDtypeStruct((n_batch * seq, N_KV_A * qw), _BF),
        scratch_shapes=[pltpu.VMEM((tq, seq), jnp.float32),
                        pltpu.VMEM((seq, HEAD_DIM), _BF), pltpu.VMEM((seq, 2 * HEAD_DIM), _BF),
                        pltpu.VMEM((seq, HEAD_DIM), _BF), pltpu.VMEM((seq, 2 * HEAD_DIM), _BF)],
        compiler_params=_params("parallel", "parallel", "arbitrary"),
    )(proj, kvc, kvc, proj, proj, proj, proj, proj, _cover_matrix(n_ch, LANES, n_s), _expand_matrix(LANES, seq))


def _dsa_prompt_kernel(iq_ref, iw_ref, q_ref, ik_ref, k_ref, v_ref, tri_ref, o_ref,
                       score_ref, key_ref, mask_ref, ikb_ref, kb_ref, vb_ref,
                       *, n_idx, n_heads, n_top, iw_lane, chunk):
    tq = iq_ref.shape[0]
    seq = mask_ref.shape[1]
    qi = pl.program_id(1)
    n_tiles = qi + 1
    row0 = pl.multiple_of(qi * tq, tq)
    t_col = qi * tq + lax.broadcasted_iota(jnp.int32, (tq, 1), 0)
    key_iota = lax.broadcasted_iota(jnp.int32, (tq, tq), 1)

    @pl.when(qi == 0)
    def _():
        for ref in (ikb_ref, kb_ref, vb_ref):
            ref[...] = jnp.zeros(ref.shape, ref.dtype)

    ikb_ref[pl.ds(row0, tq), :] = ik_ref[...].astype(_BF)
    kb_ref[pl.ds(row0, tq), :] = k_ref[...].astype(_BF)
    vb_ref[pl.ds(row0, tq), :] = jnp.concatenate([v_ref[...].astype(_BF), jnp.ones((tq, HEAD_DIM), _BF)], axis=1)
    iqs = _stack_heads(iq_ref[...], n_idx).astype(_BF)
    iw = iw_ref[...]

    def score_tile(kt, _):
        sl = pl.ds(pl.multiple_of(kt * tq, tq), tq)
        sc = _indexer_scores(iqs, ikb_ref[sl, :], iw, iw_lane, n_idx)
        score_ref[:, sl] = jnp.where(kt * tq + key_iota <= t_col, sc, NEG)
        return 0

    lax.fori_loop(0, n_tiles, score_tile, 0)
    mask_ref[...] = jnp.zeros(mask_ref.shape, mask_ref.dtype)
    _topk_mask(score_ref, key_ref, mask_ref, tri_ref, n_tiles * (tq // LANES), n_top)
    causal = lax.broadcasted_iota(jnp.int32, (tq, seq), 1) <= t_col
    mask_ref[...] = jnp.where(causal, mask_ref[...], 0.0)

    def dsa_mask(c, j):
        return mask_ref[:, pl.ds(pl.multiple_of(c * chunk + j * LANES, LANES), LANES)] > 0.5

    qs = _stack_heads(q_ref[...], n_heads).astype(_BF)
    o = _attend(qs, kb_ref, vb_ref, 0, (row0 + tq + chunk - 1) // chunk, chunk, dsa_mask, n_heads)
    for h in range(n_heads):
        o_ref[:, h * HEAD_DIM:(h + 1) * HEAD_DIM] = o[h].astype(o_ref.dtype)


def dsa_prompt(proj, lay, n_batch, seq, tq, chunk=512):
    assert tq % LANES == 0 and seq % chunk == 0 and chunk % tq == 0
    nq = seq // tq
    n_top = min(DSA_TOPK, seq // 4)
    iqw = lay.n_idx_heads * IDX_DIM
    qw = lay.n_heads_b * HEAD_DIM

    def tile(col):
        c0 = col // HEAD_DIM
        return pl.BlockSpec((tq, HEAD_DIM), lambda b, i: (b * nq + i, c0))

    return pl.pallas_call(
        functools.partial(_dsa_prompt_kernel, n_idx=lay.n_idx_heads, n_heads=lay.n_heads_b, n_top=n_top,
                          iw_lane=lay.iw % LANES, chunk=chunk),
        grid=(n_batch, nq),
        in_specs=[pl.BlockSpec((tq, iqw), lambda b, i: (b * nq + i, lay.iq // iqw)),
                  pl.BlockSpec((tq, LANES), lambda b, i: (b * nq + i, lay.iw // LANES)),
                  pl.BlockSpec((tq, qw), lambda b, i: (b * nq + i, lay.qb // qw)),
                  tile(lay.ik), tile(lay.dk), tile(lay.dv),
                  pl.BlockSpec((LANES, LANES), lambda b, i: (0, 0))],
        out_specs=pl.BlockSpec((tq, qw), lambda b, i: (b * nq + i, 0)),
        out_shape=jax.ShapeDtypeStruct((n_batch * seq, qw), _BF),
        scratch_shapes=[pltpu.VMEM((tq, seq), jnp.float32), pltpu.VMEM((tq, seq), jnp.int32),
                        pltpu.VMEM((tq, seq), jnp.float32),
                        pltpu.VMEM((seq, IDX_DIM), _BF), pltpu.VMEM((seq, HEAD_DIM), _BF),
                        pltpu.VMEM((seq, 2 * HEAD_DIM), _BF)],
        compiler_params=_params("parallel", "arbitrary"),
    )(proj, proj, proj, proj, proj, proj, _tri_matrix())


def _page_specs(block, second_block=0):
    tail = (0,) * (len(block) - 3)
    def spec(k):
        return pl.BlockSpec(block, lambda b, s, pt: (pt[b, s * PAGES_PER_STEP + k], 0, second_block) + tail)
    return [spec(k) for k in range(PAGES_PER_STEP)]


def _compress_sample_kernel(pt_ref, *refs):
    pages = refs[:PAGES_PER_STEP]
    w1_ref, w2_ref, pe_ref, cos_ref, sin_ref, o_ref, rows_ref = refs[PAGES_PER_STEP:]
    step = pl.program_id(1)
    page = pages[0].shape[1]
    n_slabs = rows_ref.shape[0]
    for k in range(PAGES_PER_STEP):
        start = pl.multiple_of((step * PAGES_PER_STEP + k) * page, page)
        for c in range(n_slabs):
            rows_ref.at[c][pl.ds(start, page), :] = pages[k][0, :, c // N_KV_A, c % N_KV_A, :]

    @pl.when(step == pl.num_programs(1) - 1)
    def _():
        n_ch = o_ref.shape[-2]
        for kv in range(2):
            for g in range(N_KV_A):
                c = kv * N_KV_A + g
                y = _compress_rows(lambda s: rows_ref.at[c][pl.ds(s, n_ch, stride=CMP_STRIDE), :],
                                   pe_ref[kv], lambda r: w1_ref[kv, r], w2_ref[kv], n_ch)
                if kv == 0:
                    y = _rotary(y, cos_ref[...], sin_ref[...])
                o_ref[0, kv, g] = y.astype(o_ref.dtype)


def compress_sample(cache, page_table, w1, w2, pe, cos_end, sin_end):
    n_batch, n_pages = page_table.shape
    page = cache.shape[1]
    n_ch = n_pages * page // CMP_STRIDE
    assert n_pages % PAGES_PER_STEP == 0
    const = lambda nd: (lambda b, s, pt: (0,) * nd)
    grid_spec = pltpu.PrefetchScalarGridSpec(
        num_scalar_prefetch=1,
        grid=(n_batch, n_pages // PAGES_PER_STEP),
        in_specs=_page_specs((1, page, 2, N_KV_A, HEAD_DIM), 0) + [
            pl.BlockSpec(w1.shape, const(4)), pl.BlockSpec(w2.shape, const(3)), pl.BlockSpec(pe.shape, const(3)),
            pl.BlockSpec((n_ch, HEAD_DIM), const(2)), pl.BlockSpec((n_ch, HEAD_DIM), const(2))],
        out_specs=pl.BlockSpec((1, 2, N_KV_A, n_ch, HEAD_DIM), lambda b, s, pt: (b, 0, 0, 0, 0)),
        scratch_shapes=[pltpu.VMEM((2 * N_KV_A, n_pages * page, HEAD_DIM), jnp.float32)],
    )
    return pl.pallas_call(
        _compress_sample_kernel,
        grid_spec=grid_spec,
        out_shape=jax.ShapeDtypeStruct((n_batch, 2, N_KV_A, n_ch, HEAD_DIM), _BF),
        compiler_params=_params("parallel", "arbitrary"),
    )(page_table, *([cache] * PAGES_PER_STEP), w1, w2, pe, cos_end, sin_end)


def _nsa_sample_kernel(pt_ref, *refs, n_heads, n_s, past_len):
    pages = refs[:PAGES_PER_STEP]
    (q_ref, gate_ref, kvc_ref, nsk_ref, nsv_ref, nwk_ref, nwv_ref, win_ref, cover_ref, expand_ref,
     o_ref, m_ref, l_ref, acc_ref, ocmp_ref, sel_ref) = refs[PAGES_PER_STEP:]
    step = pl.program_id(1)
    ts = q_ref.shape[0]
    qw = n_heads * HEAD_DIM
    n_grp = N_KV_A
    t_col = past_len + lax.broadcasted_iota(jnp.int32, (ts, 1), 0)

    def queries(g):
        return _stack_heads(q_ref[:, g * qw:(g + 1) * qw], n_heads).astype(_BF)

    def group_lanes(x, g):
        return x[:, g * HEAD_DIM:(g + 1) * HEAD_DIM]

    @pl.when(step == 0)
    def _():
        for g in range(n_grp):
            o_cmp, sel = _cmp_attend_and_choose(queries(g), kvc_ref[0, 0, g], kvc_ref[0, 1, g], cover_ref[...],
                                                t_col, n_heads, n_s)
            ocmp_ref[g] = o_cmp
            sel_ref[g] = jnp.where(sel, 1.0, 0.0)
            m_ref[g], l_ref[g], acc_ref[g] = _flash_init(n_heads, ts)

    state = [(m_ref[g], l_ref[g], acc_ref[g]) for g in range(n_grp)]
    for g in range(n_grp):
        chosen = jnp.dot(sel_ref[g].astype(_BF), expand_ref[...], preferred_element_type=jnp.float32) > 0.5
        k = jnp.concatenate([p[0, :, 0, g, :] for p in pages], axis=0).astype(_BF)
        v = jnp.concatenate([p[0, :, 1, g, :] for p in pages], axis=0).astype(_BF)
        state[g] = _flash_update(queries(g), k, v, chosen, state[g], n_heads)
    for g in range(n_grp):
        m_ref[g], l_ref[g], acc_ref[g] = state[g]

    @pl.when(step == pl.num_programs(1) - 1)
    def _():
        gates = jax.nn.sigmoid(gate_ref[...])
        n_all = n_grp * n_heads
        w_len = win_ref.shape[1]
        row = lax.broadcasted_iota(jnp.int32, (ts, LANES), 0)
        lane = lax.broadcasted_iota(jnp.int32, (ts, LANES), 1)
        new_causal = (lane <= row) & (lane < ts)
        wlane = lax.broadcasted_iota(jnp.int32, (ts, w_len + LANES), 1)
        k_pos = past_len - w_len + wlane
        d = t_col - k_pos
        win_mask = (d >= 0) & (d < WINDOW) & (k_pos >= 0) & (wlane < w_len + ts)
        new_blk = past_len // SEL_BLOCK
        for g in range(n_grp):
            qs = queries(g)
            chosen = sel_ref[g][:, new_blk:new_blk + 1] > 0.5
            k_new = _pad_rows(group_lanes(nsk_ref[...], g), LANES).astype(_BF)
            v_new = _pad_rows(group_lanes(nsv_ref[...], g), LANES).astype(_BF)
            o_slc = _flash_finish(_flash_update(qs, k_new, v_new, new_causal & chosen,
                                                (m_ref[g], l_ref[g], acc_ref[g]), n_heads))
            kw = jnp.concatenate([win_ref[0, :, 0, g, :], _pad_rows(group_lanes(nwk_ref[...], g), LANES)],
                                 axis=0).astype(_BF)
            vw = jnp.concatenate([win_ref[0, :, 1, g, :], _pad_rows(group_lanes(nwv_ref[...], g), LANES)],
                                 axis=0).astype(_BF)
            o_win = _flash_finish(_flash_update(qs, kw, vw, win_mask, _flash_init(n_heads, ts), n_heads))
            o_cmp = ocmp_ref[g]
            for r in range(n_heads):
                h = g * n_heads + r
                o = (gates[:, h:h + 1] * o_cmp[r] + gates[:, n_all + h:n_all + h + 1] * o_slc[r]
                     + gates[:, 2 * n_all + h:2 * n_all + h + 1] * o_win[r])
                o_ref[0, :, h * HEAD_DIM:(h + 1) * HEAD_DIM] = o.astype(o_ref.dtype)


def nsa_sample(proj, row0, kvc, cache, win_buf, page_table, lay, ts):
    n_batch, n_pages = page_table.shape
    page = cache.shape[1]
    past_len = n_pages * page
    n_heads = lay.n_heads_a // N_KV_A
    n_ch = kvc.shape[-2]
    n_s = -(-(past_len + ts) // SEL_BLOCK)
    sel_lanes = -(-n_s // LANES) * LANES
    kvw = N_KV_A * HEAD_DIM
    assert n_pages % PAGES_PER_STEP == 0 and row0 % ts == 0 and ts <= SEL_BLOCK and past_len % SEL_BLOCK == 0
    assert ts % 8 == 0 and lay.ga % LANES == 0
    assert (past_len + ts - CMP_BLOCK) // CMP_STRIDE + 1 <= n_ch and n_ch * CMP_STRIDE <= past_len
    r0 = row0 // ts
    keys_per_step = PAGES_PER_STEP * page
    qw = lay.n_heads_a * HEAD_DIM
    rows = lambda width, col: pl.BlockSpec((ts, width), lambda b, s, pt: (r0 + b, col // width))
    const = lambda nd: (lambda b, s, pt: (0,) * nd)
    grid_spec = pltpu.PrefetchScalarGridSpec(
        num_scalar_prefetch=1,
        grid=(n_batch, n_pages // PAGES_PER_STEP),
        in_specs=_page_specs((1, page, 2, N_KV_A, HEAD_DIM), 1) + [
            rows(qw, lay.qa), rows(LANES, lay.ga),
            pl.BlockSpec((1, 2, N_KV_A, n_ch, HEAD_DIM), lambda b, s, pt: (b, 0, 0, 0, 0)),
            rows(kvw, lay.selk), rows(kvw, lay.selv), rows(kvw, lay.wink), rows(kvw, lay.winv),
            pl.BlockSpec((1,) + win_buf.shape[1:], lambda b, s, pt: (b, 0, 0, 0, 0)),
            pl.BlockSpec((n_ch, sel_lanes), const(2)),
            pl.BlockSpec((sel_lanes, keys_per_step), lambda b, s, pt: (0, s))],
        out_specs=pl.BlockSpec((1, ts, qw), lambda b, s, pt: (b, 0, 0)),
        scratch_shapes=[pltpu.VMEM((N_KV_A, n_heads, ts, 1), jnp.float32),
                        pltpu.VMEM((N_KV_A, n_heads, ts, 1), jnp.float32),
                        pltpu.VMEM((N_KV_A, n_heads, ts, HEAD_DIM), jnp.float32),
                        pltpu.VMEM((N_KV_A, n_heads, ts, HEAD_DIM), jnp.float32),
                        pltpu.VMEM((N_KV_A, ts, sel_lanes), jnp.float32)],
    )
    return pl.pallas_call(
        functools.partial(_nsa_sample_kernel, n_heads=n_heads, n_s=n_s, past_len=past_len),
        grid_spec=grid_spec,
        out_shape=jax.ShapeDtypeStruct((n_batch, ts, qw), _BF),
        compiler_params=_params("parallel", "arbitrary"),
    )(page_table, *([cache] * PAGES_PER_STEP), proj, proj, kvc, proj, proj, proj, proj, win_buf,
      _cover_matrix(n_ch, sel_lanes, n_s), _expand_matrix(sel_lanes, past_len))


def _dsa_sample_select_kernel(pt_ref, *refs, n_idx, n_top, iw_lane, past_len):
    pages = refs[:PAGES_PER_STEP]
    iq_ref, iw_ref, nik_ref, tri_ref, mask_ref, score_ref, key_ref = refs[PAGES_PER_STEP:]
    step = pl.program_id(1)
    ts = iq_ref.shape[0]
    keys_per_step = PAGES_PER_STEP * pages[0].shape[1]
    iqs = _stack_heads(iq_ref[...], n_idx).astype(_BF)
    iw = iw_ref[...]
    ik = jnp.concatenate([p[0] for p in pages], axis=0).astype(_BF)
    start = pl.multiple_of(step * keys_per_step, keys_per_step)
    score_ref[:, pl.ds(start, keys_per_step)] = _indexer_scores(iqs, ik, iw, iw_lane, n_idx)

    @pl.when(step == pl.num_programs(1) - 1)
    def _():
        sc = _indexer_scores(iqs, _pad_rows(nik_ref[...], LANES).astype(_BF), iw, iw_lane, n_idx)
        row = lax.broadcasted_iota(jnp.int32, (ts, LANES), 0)
        lane = lax.broadcasted_iota(jnp.int32, (ts, LANES), 1)
        score_ref[:, past_len:past_len + LANES] = jnp.where(lane < ts, jnp.where(lane <= row, sc, NEG), -jnp.inf)
        _topk_mask(score_ref, key_ref, mask_ref.at[0], tri_ref, past_len // LANES + 1, n_top)


def dsa_sample_select(proj, row0, cache_idx, page_table, lay, ts):
    n_batch, n_pages = page_table.shape
    page = cache_idx.shape[1]
    past_len = n_pages * page
    assert n_pages % PAGES_PER_STEP == 0 and row0 % ts == 0 and ts <= LANES and past_len % LANES == 0
    n_top = min(DSA_TOPK, (past_len + ts) // 4)
    r0 = row0 // ts
    iqw = lay.n_idx_heads * IDX_DIM
    width = past_len + LANES
    rows = lambda w, col: pl.BlockSpec((ts, w), lambda b, s, pt: (r0 + b, col // w))
    grid_spec = pltpu.PrefetchScalarGridSpec(
        num_scalar_prefetch=1,
        grid=(n_batch, n_pages // PAGES_PER_STEP),
        in_specs=_page_specs((1, page, IDX_DIM), 0) + [
            rows(iqw, lay.iq), rows(LANES, lay.iw), rows(IDX_DIM, lay.ik),
            pl.BlockSpec((LANES, LANES), lambda b, s, pt: (0, 0))],
        out_specs=pl.BlockSpec((1, ts, width), lambda b, s, pt: (b, 0, 0)),
        scratch_shapes=[pltpu.VMEM((ts, width), jnp.float32), pltpu.VMEM((ts, width), jnp.int32)],
    )
    return pl.pallas_call(
        functools.partial(_dsa_sample_select_kernel, n_idx=lay.n_idx_heads, n_top=n_top,
                          iw_lane=lay.iw % LANES, past_len=past_len),
        grid_spec=grid_spec,
        out_shape=jax.ShapeDtypeStruct((n_batch, ts, width), jnp.float32),
        compiler_params=_params("parallel", "arbitrary"),
    )(page_table, *([cache_idx] * PAGES_PER_STEP), proj, proj, proj, _tri_matrix())


def _dsa_sample_attend_kernel(pt_ref, *refs, n_heads):
    pages = refs[:PAGES_PER_STEP]
    q_ref, mask_ref, nmask_ref, nk_ref, nv_ref, o_ref, m_ref, l_ref, acc_ref = refs[PAGES_PER_STEP:]
    step = pl.program_id(1)
    ts = q_ref.shape[0]
    qs = _stack_heads(q_ref[...], n_heads).astype(_BF)

    @pl.when(step == 0)
    def _():
        m_ref[...], l_ref[...], acc_ref[...] = _flash_init(n_heads, ts)

    k = jnp.concatenate([p[0, :, 0, :] for p in pages], axis=0).astype(_BF)
    v = jnp.concatenate([p[0, :, 1, :] for p in pages], axis=0).astype(_BF)
    m_ref[...], l_ref[...], acc_ref[...] = _flash_update(qs, k, v, mask_ref[0] > 0.5,
                                                         (m_ref[...], l_ref[...], acc_ref[...]), n_heads)

    @pl.when(step == pl.num_programs(1) - 1)
    def _():
        row = lax.broadcasted_iota(jnp.int32, (ts, LANES), 0)
        lane = lax.broadcasted_iota(jnp.int32, (ts, LANES), 1)
        mask = (nmask_ref[0] > 0.5) & (lane <= row) & (lane < ts)
        o = _flash_finish(_flash_update(qs, _pad_rows(nk_ref[...], LANES).astype(_BF),
                                        _pad_rows(nv_ref[...], LANES).astype(_BF), mask,
                                        (m_ref[...], l_ref[...], acc_ref[...]), n_heads))
        for h in range(n_heads):
            o_ref[0, :, h * HEAD_DIM:(h + 1) * HEAD_DIM] = o[h].astype(o_ref.dtype)


def dsa_sample_attend(proj, row0, mask, cache_kv, page_table, lay, ts):
    n_batch, n_pages = page_table.shape
    page = cache_kv.shape[1]
    past_len = n_pages * page
    keys_per_step = PAGES_PER_STEP * page
    r0 = row0 // ts
    qw = lay.n_heads_b * HEAD_DIM
    rows = lambda w, col: pl.BlockSpec((ts, w), lambda b, s, pt: (r0 + b, col // w))
    grid_spec = pltpu.PrefetchScalarGridSpec(
        num_scalar_prefetch=1,
        grid=(n_batch, n_pages // PAGES_PER_STEP),
        in_specs=_page_specs((1, page, 2, HEAD_DIM), 0) + [
            rows(qw, lay.qb),
            pl.BlockSpec((1, ts, keys_per_step), lambda b, s, pt: (b, 0, s)),
            pl.BlockSpec((1, ts, LANES), lambda b, s, pt: (b, 0, past_len // LANES)),
            rows(HEAD_DIM, lay.dk), rows(HEAD_DIM, lay.dv)],
        out_specs=pl.BlockSpec((1, ts, qw), lambda b, s, pt: (b, 0, 0)),
        scratch_shapes=[pltpu.VMEM((lay.n_heads_b, ts, 1), jnp.float32),
                        pltpu.VMEM((lay.n_heads_b, ts, 1), jnp.float32),
                        pltpu.VMEM((lay.n_heads_b, ts, HEAD_DIM), jnp.float32)],
    )
    return pl.pallas_call(
        functools.partial(_dsa_sample_attend_kernel, n_heads=lay.n_heads_b),
        grid_spec=grid_spec,
        out_shape=jax.ShapeDtypeStruct((n_batch, ts, qw), _BF),
        compiler_params=_params("parallel", "arbitrary"),
    )(page_table, *([cache_kv] * PAGES_PER_STEP), proj, mask, mask, proj, proj)


class Layout:
    def __init__(self, d_model, n_heads_a, n_heads_b, n_idx_heads):
        self.n_heads_a, self.n_heads_b, self.n_idx_heads = n_heads_a, n_heads_b, n_idx_heads
        src = np.cumsum([0, n_heads_a * HEAD_DIM, 6 * N_KV_A * HEAD_DIM, 3 * n_heads_a,
                         n_heads_b * HEAD_DIM, 2 * HEAD_DIM, n_idx_heads * IDX_DIM, n_idx_heads,
                         IDX_DIM, 2 * d_model])
        s_qa, s_kva, s_ga, s_qb, s_kvb, s_iq, s_iw, s_ik, s_mg, s_end = (int(v) for v in src)
        small = 3 * n_heads_a + n_idx_heads
        assert small <= LANES
        self.small_pad = LANES - small
        kvw = N_KV_A * HEAD_DIM
        self.pieces = [
            (s_qa, s_kva, True),
            (s_qb, s_kvb, True),
            (s_iq, s_iw, True),
            (s_kva, s_kva + 2 * kvw, False),
            (s_kva + 2 * kvw, s_kva + 3 * kvw, True),
            (s_kva + 3 * kvw, s_kva + 4 * kvw, False),
            (s_kva + 4 * kvw, s_kva + 5 * kvw, True),
            (s_kva + 5 * kvw, s_ga, False),
            (s_kvb, s_kvb + HEAD_DIM, True),
            (s_kvb + HEAD_DIM, s_iq, False),
            (s_ik, s_mg, True),
            (s_ga, s_qb, False),
            (s_iw, s_ik, False),
            None,
            (s_mg, s_end, False),
        ]
        off = 0
        starts = []
        for p in self.pieces:
            starts.append(off)
            off += self.small_pad if p is None else p[1] - p[0]
        (self.qa, self.qb, self.iq, self.cmp, self.selk, self.selv, self.wink, self.winv, self.dk,
         self.dv, self.ik, self.ga, self.iw, _, self.mg) = starts
        self.width = off
        assert self.width % LANES == 0
        flags = np.zeros(self.width // LANES, np.int32)
        for st, p in zip(starts, self.pieces):
            if p is not None and p[2]:
                assert st % LANES == 0 and (p[1] - p[0]) % LANES == 0
                flags[st // LANES:(st + p[1] - p[0]) // LANES] = 1
        self.rope_flags = flags

    def pack(self, w_in):
        cols = []
        for p in self.pieces:
            if p is None:
                cols.append(jnp.zeros((w_in.shape[0], self.small_pad), w_in.dtype))
            else:
                cols.append(w_in[:, p[0]:p[1]])
        return jnp.concatenate(cols, axis=1).astype(_BF)


def rope_tables(pos):
    half = HEAD_DIM // 2
    inv = ROPE_THETA ** (-jnp.arange(half, dtype=jnp.float32) / half)
    ang = pos.astype(jnp.float32)[:, None] * inv[None, :]
    cos, sin = jnp.cos(ang), jnp.sin(ang)
    return jnp.concatenate([cos, cos], axis=1), jnp.concatenate([-sin, sin], axis=1)


def _swiglu_half_step(h, g, wg, wu, wd, w_index):
    xn = rmsnorm(h, g, _BF)
    a = ffn_gate_up(xn, wg, wu, w_index, tm=1408, tn=256)
    return resid_matmul(a, wd, w_index, h, 0.5, tm=1056, tn=256)


def kernel(x_prompt, x_sample, cache_nsa_kv, cache_nsa_win, cache_dsa_kv, cache_dsa_idx, page_table,
           g_norm, w_ffn_gate, w_ffn_up, w_ffn_down, w_in, w_cmp1, w_cmp2, cmp_pos,
           w_br_a, w_br_b, w_out, g_final):
    B, T, D = x_prompt.shape
    DB, Ts, _ = x_sample.shape
    depth = g_norm.shape[0]
    page = cache_nsa_kv.shape[2]
    past_len = page_table.shape[1] * page
    n_heads_a = w_br_a.shape[1] // HEAD_DIM
    n_heads_b = w_br_b.shape[1] // HEAD_DIM
    G = N_KV_A
    lay = Layout(D, n_heads_a, n_heads_b, n_heads_b // 2)
    Mp, Ms = B * T, DB * Ts

    pos_p = jnp.arange(T, dtype=jnp.int32)
    pos_s = past_len + jnp.arange(Ts, dtype=jnp.int32)
    cos, sin = rope_tables(jnp.concatenate([jnp.tile(pos_p, B), jnp.tile(pos_s, DB)]))
    rope_flags = jnp.asarray(lay.rope_flags)

    def block_end_tables(n_ch):
        return rope_tables(jnp.arange(n_ch, dtype=jnp.int32) * CMP_STRIDE + CMP_BLOCK - 1)

    h = jnp.concatenate([x_prompt.reshape(Mp, D), x_sample.reshape(Ms, D)], axis=0)
    outs = [[] for _ in range(8)]
    for l in range(depth):
        h = _swiglu_half_step(h, g_norm[l, 0], w_ffn_gate, w_ffn_up, w_ffn_down, (l, 0))

        u = rmsnorm(h, g_norm[l, 1], _BF)
        proj, nsa_rows, win_rows, dsa_rows, ik_rows = in_project(u, lay.pack(w_in[l]), rope_flags, cos, sin, lay,
                                                                 tm=1408, tn=512)
        w1 = w_cmp1[l].reshape(2, 2, CMP_STRIDE * HEAD_DIM, -1).astype(_BF)
        w2 = w_cmp2[l].astype(_BF)

        kvc_p = compress_prompt(proj, B, T, lay.cmp, w1, w2, cmp_pos[l], *block_end_tables(T // CMP_STRIDE))
        o_a_p = nsa_prompt(proj, kvc_p, lay, B, T, tq=2 * LANES)
        o_b_p = dsa_prompt(proj, lay, B, T, tq=2 * LANES)

        kvc_s = compress_sample(cache_nsa_kv[l], page_table, w1, w2, cmp_pos[l],
                                *block_end_tables(past_len // CMP_STRIDE))
        o_a_s = nsa_sample(proj, Mp, kvc_s, cache_nsa_kv[l], cache_nsa_win[l], page_table, lay, Ts)
        top_mask = dsa_sample_select(proj, Mp, cache_dsa_idx[l], page_table, lay, Ts)
        o_b_s = dsa_sample_attend(proj, Mp, top_mask, cache_dsa_kv[l], page_table, lay, Ts)

        o_a = jnp.concatenate([o_a_p, o_a_s.reshape(Ms, -1)], axis=0)
        o_b = jnp.concatenate([o_b_p, o_b_s.reshape(Ms, -1)], axis=0)
        m = merge_branches(o_a, o_b, w_br_a, w_br_b, (l,), proj, lay.mg, tm=1408, tn=512)
        h = resid_matmul(m, w_out, (l,), h, 1.0, tm=1408, tn=512)

        win_p = win_rows[:Mp].reshape(B, T, 2, G, HEAD_DIM)
        win_s = win_rows[Mp:].reshape(DB, Ts, 2, G, HEAD_DIM)
        outs[0].append(nsa_rows[:Mp].reshape(B, T, 4, G, HEAD_DIM))
        outs[1].append(win_p[:, T - min(WINDOW, T):])
        outs[2].append(dsa_rows[:Mp].reshape(B, T, 2, HEAD_DIM))
        outs[3].append(ik_rows[:Mp].reshape(B, T, IDX_DIM))
        outs[4].append(nsa_rows[Mp:].reshape(DB, Ts, 4, G, HEAD_DIM))
        outs[5].append(jnp.concatenate([cache_nsa_win[l], win_s], axis=1)[:, Ts:])
        outs[6].append(dsa_rows[Mp:].reshape(DB, Ts, 2, HEAD_DIM))
        outs[7].append(ik_rows[Mp:].reshape(DB, Ts, IDX_DIM))

        h = _swiglu_half_step(h, g_norm[l, 2], w_ffn_gate, w_ffn_up, w_ffn_down, (l, 1))

    y_p = rmsnorm(h, g_final, jnp.float32, 0, Mp)
    y_s = rmsnorm(h, g_final, jnp.float32, Mp, Ms)
    return (y_p.reshape(B, T, D), y_s.reshape(DB, Ts, D), *(jnp.stack(o) for o in outs))
```

```python
import functools

import jax
import jax.numpy as jnp
import numpy as np
from jax import lax
from jax.experimental import pallas as pl
from jax.experimental.pallas import tpu as pltpu

HEAD_DIM = 128
N_KV_A = 2
IDX_DIM = 128
CMP_BLOCK = 32
CMP_STRIDE = 16
SEL_BLOCK = 64
N_SEL = 16
N_LOCAL_SEL = 2
WINDOW = 512
DSA_TOPK = 256
ROPE_THETA = 10000.0
RMS_EPS = 1e-6
NEG = -1e30
BIG = 1e30
TINY = 1e-30

LANES = 128
VMEM_LIMIT = 56 * 1024 * 1024
PAGES_PER_STEP = 32

_NT = (((1,), (1,)), ((), ()))
_BF = jnp.bfloat16


ROW_TILE_WIDE = 1408
ROW_TILE_DOWN = 1056
Q_TILE = 2 * LANES


def _params(*sem):
    return pltpu.CompilerParams(dimension_semantics=sem, vmem_limit_bytes=VMEM_LIMIT)


def _row_tile(m, limit):
    return max(t for t in range(16, limit + 1, 16) if m % t == 0)


def _rmsnorm_kernel(x_ref, g_ref, o_ref):
    x = x_ref[...]
    y = x * lax.rsqrt(jnp.mean(x * x, axis=-1, keepdims=True) + RMS_EPS)
    o_ref[...] = (y * g_ref[...]).astype(o_ref.dtype)


def rmsnorm(x, g, out_dtype, row0=0, n_rows=None, tm=256):
    D = x.shape[1]
    n_rows = x.shape[0] - row0 if n_rows is None else n_rows
    assert row0 % tm == 0 and n_rows % tm == 0
    return pl.pallas_call(
        _rmsnorm_kernel,
        grid=(n_rows // tm,),
        in_specs=[pl.BlockSpec((tm, D), lambda i: (row0 // tm + i, 0)),
                  pl.BlockSpec((1, D), lambda i: (0, 0))],
        out_specs=pl.BlockSpec((tm, D), lambda i: (i, 0)),
        out_shape=jax.ShapeDtypeStruct((n_rows, D), out_dtype),
        compiler_params=_params("parallel"),
    )(x, g.reshape(1, D))


def _gateup_kernel(x_ref, wg_ref, wu_ref, o_ref):
    x = x_ref[...]
    g = jnp.dot(x, wg_ref[...].astype(_BF), preferred_element_type=jnp.float32)
    u = jnp.dot(x, wu_ref[...].astype(_BF), preferred_element_type=jnp.float32)
    o_ref[...] = (g * jax.nn.sigmoid(g) * u).astype(o_ref.dtype)


def _weight_spec(w, w_index, rows, tn):
    assert w.ndim == len(w_index) + 2 and w.shape[-2] == rows
    return pl.BlockSpec((None,) * len(w_index) + (rows, tn), lambda i, j: tuple(w_index) + (0, j))


def ffn_gate_up(xn, wg, wu, w_index, tm, tn):
    M, D = xn.shape
    F = wg.shape[-1]
    return pl.pallas_call(
        _gateup_kernel,
        grid=(M // tm, F // tn),
        in_specs=[pl.BlockSpec((tm, D), lambda i, j: (i, 0)),
                  _weight_spec(wg, w_index, D, tn),
                  _weight_spec(wu, w_index, D, tn)],
        out_specs=pl.BlockSpec((tm, tn), lambda i, j: (i, j)),
        out_shape=jax.ShapeDtypeStruct((M, F), _BF),
        compiler_params=_params("parallel", "parallel"),
    )(xn, wg, wu)


def _resid_matmul_kernel(a_ref, w_ref, r_ref, o_ref, *, scale):
    acc = jnp.dot(a_ref[...], w_ref[...].astype(_BF), preferred_element_type=jnp.float32)
    o_ref[...] = r_ref[...] + scale * acc


def resid_matmul(a, w, w_index, resid, scale, tm, tn):
    M, K = a.shape
    N = w.shape[-1]
    return pl.pallas_call(
        functools.partial(_resid_matmul_kernel, scale=scale),
        grid=(M // tm, N // tn),
        in_specs=[pl.BlockSpec((tm, K), lambda i, j: (i, 0), pipeline_mode=pl.Buffered(1)),
                  _weight_spec(w, w_index, K, tn),
                  pl.BlockSpec((tm, tn), lambda i, j: (i, j))],
        out_specs=pl.BlockSpec((tm, tn), lambda i, j: (i, j)),
        out_shape=jax.ShapeDtypeStruct((M, N), jnp.float32),
        compiler_params=_params("parallel", "parallel"),
    )(a, w, resid)


def _rotary(y, cos, sin):
    return y * cos + pltpu.roll(y, HEAD_DIM // 2, axis=1) * sin


def _inproj_kernel(flags_ref, x_ref, w_ref, cos_ref, sin_ref, o_ref, *row_refs, n_chunks, routes):
    j = pl.program_id(1)
    acc = jnp.dot(x_ref[...], w_ref[...], preferred_element_type=jnp.float32)
    for c in range(n_chunks):
        sl = slice(c * LANES, (c + 1) * LANES)
        y = acc[:, sl]
        flag = flags_ref[j * n_chunks + c]

        @pl.when(flag == 1)
        def _():
            o_ref[:, sl] = _rotary(y, cos_ref[...], sin_ref[...])

        @pl.when(flag == 0)
        def _():
            o_ref[:, sl] = y

    for chunk, out_idx, index in routes:
        @pl.when(j == chunk // n_chunks)
        def _():
            c = chunk % n_chunks
            row_refs[out_idx][(slice(None),) + index + (slice(None),)] = o_ref[:, c * LANES:(c + 1) * LANES]


def in_project(u, w, rope_flags, cos, sin, lay, tm, tn):
    M, D = u.shape
    N = w.shape[1]
    n_chunks = tn // LANES
    G = N_KV_A
    routes = ([(lay.cmp // LANES + k, 0, (k // G, k % G)) for k in range(4 * G)]
              + [(lay.wink // LANES + k, 1, (k // G, k % G)) for k in range(2 * G)]
              + [(lay.dk // LANES + k, 2, (k,)) for k in range(2)]
              + [(lay.ik // LANES, 3, ())])
    row_shapes = [(M, 4, G, HEAD_DIM), (M, 2, G, HEAD_DIM), (M, 2, HEAD_DIM), (M, IDX_DIM)]
    grid_spec = pltpu.PrefetchScalarGridSpec(
        num_scalar_prefetch=1,
        grid=(M // tm, N // tn),
        in_specs=[pl.BlockSpec((tm, D), lambda i, j, f: (i, 0), pipeline_mode=pl.Buffered(1)),
                  pl.BlockSpec((D, tn), lambda i, j, f: (0, j)),
                  pl.BlockSpec((tm, LANES), lambda i, j, f: (i, 0)),
                  pl.BlockSpec((tm, LANES), lambda i, j, f: (i, 0))],
        out_specs=[pl.BlockSpec((tm, tn), lambda i, j, f: (i, j))]
        + [pl.BlockSpec((tm,) + shp[1:], lambda i, j, f, nd=len(shp): (i,) + (0,) * (nd - 1)) for shp in row_shapes],
    )
    return pl.pallas_call(
        functools.partial(_inproj_kernel, n_chunks=n_chunks, routes=routes),
        grid_spec=grid_spec,
        out_shape=[jax.ShapeDtypeStruct((M, N), jnp.float32)]
        + [jax.ShapeDtypeStruct(shp, jnp.float32) for shp in row_shapes],
        compiler_params=_params("parallel", "arbitrary"),
    )(rope_flags, u, w, cos, sin)


def _merge_kernel(oa_ref, ob_ref, wa_ref, wb_ref, ga_ref, gb_ref, o_ref):
    ya = jnp.dot(oa_ref[...], wa_ref[...].astype(_BF), preferred_element_type=jnp.float32)
    yb = jnp.dot(ob_ref[...], wb_ref[...].astype(_BF), preferred_element_type=jnp.float32)
    m = jax.nn.sigmoid(ga_ref[...]) * ya + jax.nn.sigmoid(gb_ref[...]) * yb
    o_ref[...] = m.astype(o_ref.dtype)


def merge_branches(o_a, o_b, w_a, w_b, w_index, proj, mg_col, tm, tn):
    M, K = o_a.shape
    N = w_a.shape[-1]
    ja = mg_col // tn
    jb = (mg_col + N) // tn
    return pl.pallas_call(
        _merge_kernel,
        grid=(M // tm, N // tn),
        in_specs=[pl.BlockSpec((tm, K), lambda i, j: (i, 0), pipeline_mode=pl.Buffered(1)),
                  pl.BlockSpec((tm, K), lambda i, j: (i, 0), pipeline_mode=pl.Buffered(1)),
                  _weight_spec(w_a, w_index, K, tn),
                  _weight_spec(w_b, w_index, K, tn),
                  pl.BlockSpec((tm, tn), lambda i, j: (i, ja + j)),
                  pl.BlockSpec((tm, tn), lambda i, j: (i, jb + j))],
        out_specs=pl.BlockSpec((tm, tn), lambda i, j: (i, j)),
        out_shape=jax.ShapeDtypeStruct((M, N), _BF),
        compiler_params=_params("parallel", "parallel"),
    )(o_a, o_b, w_a, w_b, proj, proj)


def _stack_heads(x, n_heads):
    return jnp.concatenate([x[:, h * HEAD_DIM:(h + 1) * HEAD_DIM] for h in range(n_heads)], axis=0)


def _flash_update(qs, k, v, mask, carry, n_rep):
    m, l, acc = carry
    rows, tk = qs.shape[0], k.shape[0]
    tq = rows // n_rep
    scale = HEAD_DIM ** -0.5
    s = lax.dot_general(qs, k, _NT, preferred_element_type=jnp.float32).reshape(n_rep, tq, tk)
    mask = mask[None]
    s = jnp.where(mask, s, NEG)
    m_new = jnp.maximum(m, jnp.max(s, axis=-1, keepdims=True))
    p = jnp.where(mask, jnp.exp((s - m_new) * scale), 0.0)
    alpha = jnp.exp((m - m_new) * scale)
    l = alpha * l + jnp.sum(p, axis=-1, keepdims=True)
    pv = jnp.dot(p.reshape(rows, tk).astype(_BF), v, preferred_element_type=jnp.float32)
    return m_new, l, alpha * acc + pv.reshape(n_rep, tq, HEAD_DIM)


def _flash_init(n_rep, tq):
    return (jnp.full((n_rep, tq, 1), NEG, jnp.float32), jnp.zeros((n_rep, tq, 1), jnp.float32),
            jnp.zeros((n_rep, tq, HEAD_DIM), jnp.float32))


def _flash_finish(carry):
    _, l, acc = carry
    return acc * (1.0 / jnp.maximum(l, TINY))


def _attend(qs, k_ref, v_ref, start, n_chunks, chunk, mask_fn, n_rep):
    rows = qs.shape[0]
    tq = rows // n_rep
    n_lane_tiles = chunk // LANES
    scale = HEAD_DIM ** -0.5

    def scores(c):
        sl = pl.ds(pl.multiple_of(start + c * chunk, LANES), chunk)
        return lax.dot_general(qs, k_ref[sl, :], _NT, preferred_element_type=jnp.float32), sl

    def lane_tile(s, j):
        return s[:, j * LANES:(j + 1) * LANES].reshape(n_rep, tq, LANES)

    def row_max(c, mx):
        s, _ = scores(c)
        for j in range(n_lane_tiles):
            mx = jnp.maximum(mx, jnp.where(mask_fn(c, j)[None], lane_tile(s, j), NEG))
        return mx

    mx = lax.fori_loop(0, n_chunks, row_max, jnp.full((n_rep, tq, LANES), NEG, jnp.float32))
    m = jnp.broadcast_to(jnp.max(mx, axis=-1, keepdims=True), mx.shape)

    def accumulate(c, acc):
        s, sl = scores(c)
        p = [jnp.where(mask_fn(c, j)[None], jnp.exp((lane_tile(s, j) - m) * scale), 0.0)
             .astype(_BF).reshape(rows, LANES) for j in range(n_lane_tiles)]
        return acc + jnp.dot(jnp.concatenate(p, axis=1), v_ref[sl, :], preferred_element_type=jnp.float32)

    acc = lax.fori_loop(0, n_chunks, accumulate, jnp.zeros((rows, 2 * HEAD_DIM), jnp.float32))
    out = acc[:, :HEAD_DIM] * (1.0 / jnp.maximum(acc[:, HEAD_DIM:], TINY))
    return out.reshape(n_rep, tq, HEAD_DIM)


def _compress_rows(load, pe, w1, w2, n_ch):
    rows = [load(s) for s in range(CMP_STRIDE)]
    half = [jnp.concatenate([(rows[s] + pe[r * CMP_STRIDE + s:r * CMP_STRIDE + s + 1, :]).astype(_BF)
                             for s in range(CMP_STRIDE)], axis=1) for r in range(2)]
    h0 = jnp.dot(half[0], w1(0), preferred_element_type=jnp.float32)
    h1 = jnp.dot(half[1], w1(1), preferred_element_type=jnp.float32)
    h = h0 + pltpu.roll(h1, n_ch - 1, axis=0)
    return jnp.dot((h * jax.nn.sigmoid(h)).astype(_BF), w2, preferred_element_type=jnp.float32)


def _cmp_attend_and_choose(qs, kc, vc, cover, t_col, n_heads, n_s):
    rows = qs.shape[0]
    tq = rows // n_heads
    n_ch, lanes = cover.shape
    scale = HEAD_DIM ** -0.5
    s = lax.dot_general(qs, kc, _NT, preferred_element_type=jnp.float32).reshape(n_heads, tq, n_ch)
    end = lax.broadcasted_iota(jnp.int32, (tq, n_ch), 1) * CMP_STRIDE + (CMP_BLOCK - 1)
    cmask = (end <= t_col)[None]
    s = jnp.where(cmask, s, NEG)
    m = jnp.max(s, axis=-1, keepdims=True)
    p = jnp.where(cmask, jnp.exp((s - m) * scale), 0.0)
    p = p * (1.0 / jnp.maximum(jnp.sum(p, axis=-1, keepdims=True), TINY))
    o_cmp = jnp.dot(p.reshape(rows, n_ch).astype(_BF), vc, preferred_element_type=jnp.float32)
    imp = jnp.dot(jnp.sum(p, axis=0).astype(_BF), cover, preferred_element_type=jnp.float32)

    lane = lax.broadcasted_iota(jnp.int32, (tq, lanes), 1)
    lane_f = lane.astype(jnp.float32)
    jt = lax.shift_right_arithmetic(t_col, jnp.int32(SEL_BLOCK.bit_length() - 1))
    adm = lane <= jt
    forced = adm & ((lane == 0) | (lane > jt - N_LOCAL_SEL))
    work = jnp.where(forced, BIG, jnp.where(adm, imp, NEG))
    work = jnp.where(lane < n_s, work, -jnp.inf)
    sel = jnp.zeros((tq, lanes), jnp.bool_)
    for _ in range(min(N_SEL, n_s)):
        mx = jnp.max(work, axis=-1, keepdims=True)
        first = jnp.min(jnp.where(work == mx, lane_f, float(lanes)), axis=-1, keepdims=True)
        pick = lane_f == first
        sel = sel | pick
        work = jnp.where(pick, -jnp.inf, work)
    return o_cmp.reshape(n_heads, tq, HEAD_DIM), sel


def _topk_mask(score_ref, key_ref, mask_ref, tri_ref, n_tiles, n_top):
    tq = score_ref.shape[0]
    int_min = jnp.int32(-2 ** 31)

    def to_key(t, _):
        sl = pl.ds(pl.multiple_of(t * LANES, LANES), LANES)
        bits = lax.bitcast_convert_type(score_ref[:, sl], jnp.int32)
        key_ref[:, sl] = bits ^ ((bits >> 31) & jnp.int32(0x7FFFFFFF))
        return 0

    lax.fori_loop(0, n_tiles, to_key, 0)

    unroll = 8 if isinstance(n_tiles, int) else 1

    def count_ge(cand):
        def body(t, acc):
            sl = pl.ds(pl.multiple_of(t * LANES, LANES), LANES)
            return acc + jnp.where(key_ref[:, sl] >= cand, 1.0, 0.0)
        acc = lax.fori_loop(0, n_tiles, body, jnp.zeros((tq, LANES), jnp.float32), unroll=unroll)
        return jnp.sum(acc, axis=-1, keepdims=True)

    tau = jnp.zeros((tq, 1), jnp.int32)
    for bit in range(31, -1, -1):
        cand = tau | jnp.int32(-2 ** 31 if bit == 31 else 1 << bit)
        tau = jnp.where(count_ge(cand ^ int_min) >= float(n_top), cand, tau)
    thr = tau ^ int_min

    clean = jnp.max(jnp.abs(count_ge(thr) - float(n_top))) == 0.0

    @pl.when(clean)
    def _():
        def keep_ge(t, _):
            sl = pl.ds(pl.multiple_of(t * LANES, LANES), LANES)
            mask_ref[:, sl] = jnp.where(key_ref[:, sl] >= thr, 1.0, 0.0)
            return 0
        lax.fori_loop(0, n_tiles, keep_ge, 0, unroll=unroll)

    @pl.when(jnp.logical_not(clean))
    def _():
        ones = jnp.ones((LANES, LANES), _BF)

        def count_gt(t, acc):
            sl = pl.ds(pl.multiple_of(t * LANES, LANES), LANES)
            return acc + jnp.where(key_ref[:, sl] > thr, 1.0, 0.0)

        n_gt = jnp.sum(lax.fori_loop(0, n_tiles, count_gt, jnp.zeros((tq, LANES), jnp.float32)),
                       axis=-1, keepdims=True)
        need = float(n_top) - n_gt

        def cut(t, eq_before):
            sl = pl.ds(pl.multiple_of(t * LANES, LANES), LANES)
            key = key_ref[:, sl]
            eq = jnp.where(key == thr, 1.0, 0.0)
            rank = (jnp.dot(eq.astype(_BF), tri_ref[...], preferred_element_type=jnp.float32)
                    + jnp.dot(eq_before.astype(_BF), ones, preferred_element_type=jnp.float32))
            keep = (key > thr) | ((key == thr) & (rank <= need))
            mask_ref[:, sl] = jnp.where(keep, 1.0, 0.0)
            return eq_before + eq

        lax.fori_loop(0, n_tiles, cut, jnp.zeros((tq, LANES), jnp.float32))


def _indexer_scores(iqs, ik, iw, iw_lane, n_idx):
    tq = iqs.shape[0] // n_idx
    n = ik.shape[0]
    logits = lax.dot_general(iqs, ik, _NT, preferred_element_type=jnp.float32).reshape(n_idx, tq, n)
    logits = jnp.maximum(logits * IDX_DIM ** -0.5, 0.0)
    sc = jnp.zeros((tq, n), jnp.float32)
    for h in range(n_idx):
        sc = sc + logits[h] * iw[:, iw_lane + h:iw_lane + h + 1]
    return sc * n_idx ** -0.5


def _cover_matrix(n_ch, lanes, n_s):
    ci = np.arange(n_ch)[:, None] * CMP_STRIDE
    sj = np.arange(lanes)[None, :] * SEL_BLOCK
    return jnp.asarray((ci < sj + SEL_BLOCK) & (ci + CMP_BLOCK > sj) & (sj < n_s * SEL_BLOCK), _BF)


def _expand_matrix(lanes, n_keys):
    return jnp.asarray(np.arange(lanes)[:, None] == np.arange(n_keys)[None, :] // SEL_BLOCK, _BF)


def _tri_matrix():
    return jnp.asarray(np.arange(LANES)[:, None] <= np.arange(LANES)[None, :], _BF)


def _pad_rows(x, n):
    return jnp.concatenate([x, jnp.zeros((n - x.shape[0], x.shape[1]), x.dtype)], axis=0)


def _compress_kernel(x_ref, w1_ref, w2_ref, pe_ref, cos_ref, sin_ref, o_ref):
    n_ch = o_ref.shape[-2]
    y = _compress_rows(lambda s: x_ref[pl.ds(s, n_ch, stride=CMP_STRIDE), :], pe_ref[0],
                       lambda r: w1_ref[0, r], w2_ref[0], n_ch)

    @pl.when(pl.program_id(1) == 0)
    def _():
        o_ref[0, 0, 0] = _rotary(y, cos_ref[...], sin_ref[...]).astype(o_ref.dtype)

    @pl.when(pl.program_id(1) != 0)
    def _():
        o_ref[0, 0, 0] = y.astype(o_ref.dtype)


def compress_prompt(proj, n_batch, seq, cmp_col, w1, w2, pe, cos_end, sin_end):
    n_ch = seq // CMP_STRIDE
    col0 = cmp_col // HEAD_DIM
    return pl.pallas_call(
        _compress_kernel,
        grid=(n_batch, 2, N_KV_A),
        in_specs=[pl.BlockSpec((seq, HEAD_DIM), lambda b, kv, g: (b, col0 + kv * N_KV_A + g)),
                  pl.BlockSpec((1,) + w1.shape[1:], lambda b, kv, g: (kv, 0, 0, 0)),
                  pl.BlockSpec((1, w2.shape[1], HEAD_DIM), lambda b, kv, g: (kv, 0, 0)),
                  pl.BlockSpec((1, CMP_BLOCK, HEAD_DIM), lambda b, kv, g: (kv, 0, 0)),
                  pl.BlockSpec((n_ch, HEAD_DIM), lambda b, kv, g: (0, 0)),
                  pl.BlockSpec((n_ch, HEAD_DIM), lambda b, kv, g: (0, 0))],
        out_specs=pl.BlockSpec((1, 1, 1, n_ch, HEAD_DIM), lambda b, kv, g: (b, kv, g, 0, 0)),
        out_shape=jax.ShapeDtypeStruct((n_batch, 2, N_KV_A, n_ch, HEAD_DIM), _BF),
        compiler_params=_params("parallel", "parallel", "parallel"),
    )(proj, w1, w2, pe, cos_end, sin_end)


def _nsa_prompt_kernel(q_ref, kc_ref, vc_ref, selk_ref, selv_ref, wink_ref, winv_ref, gate_ref,
                       cover_ref, expand_ref, o_ref, selexp_ref, sk_ref, sv_ref, wk_ref, wv_ref,
                       *, n_heads, n_s, chunk):
    tq = q_ref.shape[0]
    seq = selexp_ref.shape[-1]
    n_grp = N_KV_A
    qi = pl.program_id(1)
    row0 = pl.multiple_of(qi * tq, tq)
    t_col = qi * tq + lax.broadcasted_iota(jnp.int32, (tq, 1), 0)
    qw = n_heads * HEAD_DIM

    @pl.when(qi == 0)
    def _():
        for ref in (sk_ref, sv_ref, wk_ref, wv_ref):
            ref[...] = jnp.zeros(ref.shape, ref.dtype)

    ones = jnp.ones((tq, HEAD_DIM), _BF)
    causal = lax.broadcasted_iota(jnp.int32, (tq, seq), 1) <= t_col
    qs, o_cmp = [], []
    for g in range(n_grp):
        lanes = slice(g * HEAD_DIM, (g + 1) * HEAD_DIM)
        sk_ref[g, pl.ds(row0, tq), :] = selk_ref[:, lanes].astype(_BF)
        sv_ref[g, pl.ds(row0, tq), :] = jnp.concatenate([selv_ref[:, lanes].astype(_BF), ones], axis=1)
        wk_ref[g, pl.ds(row0, tq), :] = wink_ref[:, lanes].astype(_BF)
        wv_ref[g, pl.ds(row0, tq), :] = jnp.concatenate([winv_ref[:, lanes].astype(_BF), ones], axis=1)
        qs.append(_stack_heads(q_ref[:, g * qw:(g + 1) * qw], n_heads).astype(_BF))
    sels = []
    for g in range(n_grp):
        o, sel = _cmp_attend_and_choose(qs[g], kc_ref[0, 0, g], vc_ref[0, 0, g], cover_ref[...], t_col, n_heads, n_s)
        o_cmp.append(o)
        sels.append(sel)
    for g in range(n_grp):
        chosen = jnp.dot(jnp.where(sels[g], 1.0, 0.0).astype(_BF), expand_ref[...],
                         preferred_element_type=jnp.float32)
        selexp_ref[g] = jnp.where(causal, chosen, 0.0)

    span = WINDOW + tq
    start = pl.multiple_of(jnp.maximum(row0 - WINDOW, 0), tq)
    d = t_col - (start + lax.broadcasted_iota(jnp.int32, (tq, span), 1))
    visible = (d >= 0) & (d < WINDOW)
    gates = jax.nn.sigmoid(gate_ref[...])
    n_all = n_grp * n_heads
    for g in range(n_grp):
        def sel_mask(c, j):
            return selexp_ref[g, :, pl.ds(pl.multiple_of(c * chunk + j * LANES, LANES), LANES)] > 0.5

        o_slc = _attend(qs[g], sk_ref.at[g], sv_ref.at[g], 0, (row0 + tq + chunk - 1) // chunk, chunk, sel_mask,
                        n_heads)
        o_win = _attend(qs[g], wk_ref.at[g], wv_ref.at[g], start, 1, span,
                        lambda c, j: visible[:, j * LANES:(j + 1) * LANES], n_heads)
        for r in range(n_heads):
            h = g * n_heads + r
            o = (gates[:, h:h + 1] * o_cmp[g][r] + gates[:, n_all + h:n_all + h + 1] * o_slc[r]
                 + gates[:, 2 * n_all + h:2 * n_all + h + 1] * o_win[r])
            o_ref[:, h * HEAD_DIM:(h + 1) * HEAD_DIM] = o.astype(o_ref.dtype)


def nsa_prompt(proj, kvc, lay, n_batch, seq, tq, chunk=512):
    n_heads = lay.n_heads_a // N_KV_A
    n_ch = kvc.shape[-2]
    n_s = -(-seq // SEL_BLOCK)
    assert tq % LANES == 0 and n_s <= LANES and seq % chunk == 0 and chunk % tq == 0
    assert WINDOW % tq == 0 and seq >= WINDOW + tq and lay.ga % LANES == 0
    nq = seq // tq
    qw = lay.n_heads_a * HEAD_DIM
    kvw = N_KV_A * HEAD_DIM

    def tile(col):
        return pl.BlockSpec((tq, kvw), lambda b, i: (b * nq + i, col // kvw))

    return pl.pallas_call(
        functools.partial(_nsa_prompt_kernel, n_heads=n_heads, n_s=n_s, chunk=chunk),
        grid=(n_batch, nq),
        in_specs=[pl.BlockSpec((tq, qw), lambda b, i: (b * nq + i, lay.qa // qw)),
                  pl.BlockSpec((1, 1, N_KV_A, n_ch, HEAD_DIM), lambda b, i: (b, 0, 0, 0, 0)),
                  pl.BlockSpec((1, 1, N_KV_A, n_ch, HEAD_DIM), lambda b, i: (b, 1, 0, 0, 0)),
                  tile(lay.selk), tile(lay.selv), tile(lay.wink), tile(lay.winv),
                  pl.BlockSpec((tq, LANES), lambda b, i: (b * nq + i, lay.ga // LANES)),
                  pl.BlockSpec((n_ch, LANES), lambda b, i: (0, 0)),
                  pl.BlockSpec((LANES, seq), lambda b, i: (0, 0))],
        out_specs=pl.BlockSpec((tq, qw), lambda b, i: (b * nq + i, 0)),
        out_shape=jax.ShapeDtypeStruct((n_batch * seq, qw), _BF),
        scratch_shapes=[pltpu.VMEM((N_KV_A, tq, seq), jnp.float32),
                        pltpu.VMEM((N_KV_A, seq, HEAD_DIM), _BF), pltpu.VMEM((N_KV_A, seq, 2 * HEAD_DIM), _BF),
                        pltpu.VMEM((N_KV_A, seq, HEAD_DIM), _BF), pltpu.VMEM((N_KV_A, seq, 2 * HEAD_DIM), _BF)],
        compiler_params=_params("parallel", "arbitrary"),
    )(proj, kvc, kvc, proj, proj, proj, proj, proj, _cover_matrix(n_ch, LANES, n_s), _expand_matrix(LANES, seq))


def _dsa_prompt_kernel(iq_ref, iw_ref, q_ref, ik_ref, k_ref, v_ref, tri_ref, o_ref,
                       score_ref, key_ref, mask_ref, ikb_ref, kb_ref, vb_ref,
                       *, n_idx, n_heads, n_top, iw_lane, chunk):
    tq = iq_ref.shape[0]
    seq = mask_ref.shape[1]
    qi = pl.program_id(1)
    n_tiles = qi + 1
    row0 = pl.multiple_of(qi * tq, tq)
    t_col = qi * tq + lax.broadcasted_iota(jnp.int32, (tq, 1), 0)
    key_iota = lax.broadcasted_iota(jnp.int32, (tq, tq), 1)

    @pl.when(qi == 0)
    def _():
        for ref in (ikb_ref, kb_ref, vb_ref):
            ref[...] = jnp.zeros(ref.shape, ref.dtype)

    ikb_ref[pl.ds(row0, tq), :] = ik_ref[...].astype(_BF)
    kb_ref[pl.ds(row0, tq), :] = k_ref[...].astype(_BF)
    vb_ref[pl.ds(row0, tq), :] = jnp.concatenate([v_ref[...].astype(_BF), jnp.ones((tq, HEAD_DIM), _BF)], axis=1)
    iqs = _stack_heads(iq_ref[...], n_idx).astype(_BF)
    iw = iw_ref[...]

    def score_tile(kt, _):
        sl = pl.ds(pl.multiple_of(kt * tq, tq), tq)
        sc = _indexer_scores(iqs, ikb_ref[sl, :], iw, iw_lane, n_idx)
        score_ref[:, sl] = jnp.where(kt * tq + key_iota <= t_col, sc, NEG)
        return 0

    lax.fori_loop(0, n_tiles, score_tile, 0)
    mask_ref[...] = jnp.zeros(mask_ref.shape, mask_ref.dtype)
    _topk_mask(score_ref, key_ref, mask_ref, tri_ref, n_tiles * (tq // LANES), n_top)
    causal = lax.broadcasted_iota(jnp.int32, (tq, seq), 1) <= t_col
    mask_ref[...] = jnp.where(causal, mask_ref[...], 0.0)

    def dsa_mask(c, j):
        return mask_ref[:, pl.ds(pl.multiple_of(c * chunk + j * LANES, LANES), LANES)] > 0.5

    qs = _stack_heads(q_ref[...], n_heads).astype(_BF)
    o = _attend(qs, kb_ref, vb_ref, 0, (row0 + tq + chunk - 1) // chunk, chunk, dsa_mask, n_heads)
    for h in range(n_heads):
        o_ref[:, h * HEAD_DIM:(h + 1) * HEAD_DIM] = o[h].astype(o_ref.dtype)


def dsa_prompt(proj, lay, n_batch, seq, tq, chunk=512):
    assert tq % LANES == 0 and seq % chunk == 0 and chunk % tq == 0
    nq = seq // tq
    n_top = min(DSA_TOPK, seq // 4)
    iqw = lay.n_idx_heads * IDX_DIM
    qw = lay.n_heads_b * HEAD_DIM

    def tile(col):
        c0 = col // HEAD_DIM
        return pl.BlockSpec((tq, HEAD_DIM), lambda b, i: (b * nq + i, c0))

    return pl.pallas_call(
        functools.partial(_dsa_prompt_kernel, n_idx=lay.n_idx_heads, n_heads=lay.n_heads_b, n_top=n_top,
                          iw_lane=lay.iw % LANES, chunk=chunk),
        grid=(n_batch, nq),
        in_specs=[pl.BlockSpec((tq, iqw), lambda b, i: (b * nq + i, lay.iq // iqw)),
                  pl.BlockSpec((tq, LANES), lambda b, i: (b * nq + i, lay.iw // LANES)),
                  pl.BlockSpec((tq, qw), lambda b, i: (b * nq + i, lay.qb // qw)),
                  tile(lay.ik), tile(lay.dk), tile(lay.dv),
                  pl.BlockSpec((LANES, LANES), lambda b, i: (0, 0))],
        out_specs=pl.BlockSpec((tq, qw), lambda b, i: (b * nq + i, 0)),
        out_shape=jax.ShapeDtypeStruct((n_batch * seq, qw), _BF),
        scratch_shapes=[pltpu.VMEM((tq, seq), jnp.float32), pltpu.VMEM((tq, seq), jnp.int32),
                        pltpu.VMEM((tq, seq), jnp.float32),
                        pltpu.VMEM((seq, IDX_DIM), _BF), pltpu.VMEM((seq, HEAD_DIM), _BF),
                        pltpu.VMEM((seq, 2 * HEAD_DIM), _BF)],
        compiler_params=_params("parallel", "arbitrary"),
    )(proj, proj, proj, proj, proj, proj, _tri_matrix())


def _page_specs(block, second_block=0):
    tail = (0,) * (len(block) - 3)
    def spec(k):
        return pl.BlockSpec(block, lambda b, s, pt: (pt[b, s * PAGES_PER_STEP + k], 0, second_block) + tail)
    return [spec(k) for k in range(PAGES_PER_STEP)]


def _compress_sample_kernel(pt_ref, *refs):
    pages = refs[:PAGES_PER_STEP]
    w1_ref, w2_ref, pe_ref, cos_ref, sin_ref, o_ref, rows_ref = refs[PAGES_PER_STEP:]
    step = pl.program_id(1)
    page = pages[0].shape[1]
    n_slabs = rows_ref.shape[0]
    for k in range(PAGES_PER_STEP):
        start = pl.multiple_of((step * PAGES_PER_STEP + k) * page, page)
        for c in range(n_slabs):
            rows_ref.at[c][pl.ds(start, page), :] = pages[k][0, :, c // N_KV_A, c % N_KV_A, :]

    @pl.when(step == pl.num_programs(1) - 1)
    def _():
        n_ch = o_ref.shape[-2]
        for kv in range(2):
            for g in range(N_KV_A):
                c = kv * N_KV_A + g
                y = _compress_rows(lambda s: rows_ref.at[c][pl.ds(s, n_ch, stride=CMP_STRIDE), :],
                                   pe_ref[kv], lambda r: w1_ref[kv, r], w2_ref[kv], n_ch)
                if kv == 0:
                    y = _rotary(y, cos_ref[...], sin_ref[...])
                o_ref[0, kv, g] = y.astype(o_ref.dtype)


def compress_sample(cache, page_table, w1, w2, pe, cos_end, sin_end):
    n_batch, n_pages = page_table.shape
    page = cache.shape[1]
    n_ch = n_pages * page // CMP_STRIDE
    assert n_pages % PAGES_PER_STEP == 0
    const = lambda nd: (lambda b, s, pt: (0,) * nd)
    grid_spec = pltpu.PrefetchScalarGridSpec(
        num_scalar_prefetch=1,
        grid=(n_batch, n_pages // PAGES_PER_STEP),
        in_specs=_page_specs((1, page, 2, N_KV_A, HEAD_DIM), 0) + [
            pl.BlockSpec(w1.shape, const(4)), pl.BlockSpec(w2.shape, const(3)), pl.BlockSpec(pe.shape, const(3)),
            pl.BlockSpec((n_ch, HEAD_DIM), const(2)), pl.BlockSpec((n_ch, HEAD_DIM), const(2))],
        out_specs=pl.BlockSpec((1, 2, N_KV_A, n_ch, HEAD_DIM), lambda b, s, pt: (b, 0, 0, 0, 0)),
        scratch_shapes=[pltpu.VMEM((2 * N_KV_A, n_pages * page, HEAD_DIM), jnp.float32)],
    )
    return pl.pallas_call(
        _compress_sample_kernel,
        grid_spec=grid_spec,
        out_shape=jax.ShapeDtypeStruct((n_batch, 2, N_KV_A, n_ch, HEAD_DIM), _BF),
        compiler_params=_params("parallel", "arbitrary"),
    )(page_table, *([cache] * PAGES_PER_STEP), w1, w2, pe, cos_end, sin_end)


def _nsa_sample_kernel(pt_ref, *refs, n_heads, n_s, past_len):
    pages = refs[:PAGES_PER_STEP]
    (q_ref, gate_ref, kvc_ref, nsk_ref, nsv_ref, nwk_ref, nwv_ref, win_ref, cover_ref, expand_ref,
     o_ref, m_ref, l_ref, acc_ref, ocmp_ref, sel_ref) = refs[PAGES_PER_STEP:]
    step = pl.program_id(1)
    ts = q_ref.shape[0]
    qw = n_heads * HEAD_DIM
    n_grp = N_KV_A
    t_col = past_len + lax.broadcasted_iota(jnp.int32, (ts, 1), 0)

    def queries(g):
        return _stack_heads(q_ref[:, g * qw:(g + 1) * qw], n_heads).astype(_BF)

    def group_lanes(x, g):
        return x[:, g * HEAD_DIM:(g + 1) * HEAD_DIM]

    @pl.when(step == 0)
    def _():
        for g in range(n_grp):
            o_cmp, sel = _cmp_attend_and_choose(queries(g), kvc_ref[0, 0, g], kvc_ref[0, 1, g], cover_ref[...],
                                                t_col, n_heads, n_s)
            ocmp_ref[g] = o_cmp
            sel_ref[g] = jnp.where(sel, 1.0, 0.0)
            m_ref[g], l_ref[g], acc_ref[g] = _flash_init(n_heads, ts)

    state = [(m_ref[g], l_ref[g], acc_ref[g]) for g in range(n_grp)]
    for g in range(n_grp):
        chosen = jnp.dot(sel_ref[g].astype(_BF), expand_ref[...], preferred_element_type=jnp.float32) > 0.5
        k = jnp.concatenate([p[0, :, 0, g, :] for p in pages], axis=0).astype(_BF)
        v = jnp.concatenate([p[0, :, 1, g, :] for p in pages], axis=0).astype(_BF)
        state[g] = _flash_update(queries(g), k, v, chosen, state[g], n_heads)
    for g in range(n_grp):
        m_ref[g], l_ref[g], acc_ref[g] = state[g]

    @pl.when(step == pl.num_programs(1) - 1)
    def _():
        gates = jax.nn.sigmoid(gate_ref[...])
        n_all = n_grp * n_heads
        w_len = win_ref.shape[1]
        row = lax.broadcasted_iota(jnp.int32, (ts, LANES), 0)
        lane = lax.broadcasted_iota(jnp.int32, (ts, LANES), 1)
        new_causal = (lane <= row) & (lane < ts)
        wlane = lax.broadcasted_iota(jnp.int32, (ts, w_len + LANES), 1)
        k_pos = past_len - w_len + wlane
        d = t_col - k_pos
        win_mask = (d >= 0) & (d < WINDOW) & (k_pos >= 0) & (wlane < w_len + ts)
        new_blk = past_len // SEL_BLOCK
        for g in range(n_grp):
            qs = queries(g)
            chosen = sel_ref[g][:, new_blk:new_blk + 1] > 0.5
            k_new = _pad_rows(group_lanes(nsk_ref[...], g), LANES).astype(_BF)
            v_new = _pad_rows(group_lanes(nsv_ref[...], g), LANES).astype(_BF)
            o_slc = _flash_finish(_flash_update(qs, k_new, v_new, new_causal & chosen,
                                                (m_ref[g], l_ref[g], acc_ref[g]), n_heads))
            kw = jnp.concatenate([win_ref[0, :, 0, g, :], _pad_rows(group_lanes(nwk_ref[...], g), LANES)],
                                 axis=0).astype(_BF)
            vw = jnp.concatenate([win_ref[0, :, 1, g, :], _pad_rows(group_lanes(nwv_ref[...], g), LANES)],
                                 axis=0).astype(_BF)
            o_win = _flash_finish(_flash_update(qs, kw, vw, win_mask, _flash_init(n_heads, ts), n_heads))
            o_cmp = ocmp_ref[g]
            for r in range(n_heads):
                h = g * n_heads + r
                o = (gates[:, h:h + 1] * o_cmp[r] + gates[:, n_all + h:n_all + h + 1] * o_slc[r]
                     + gates[:, 2 * n_all + h:2 * n_all + h + 1] * o_win[r])
                o_ref[0, :, h * HEAD_DIM:(h + 1) * HEAD_DIM] = o.astype(o_ref.dtype)


def nsa_sample(proj, row0, kvc, cache, win_buf, page_table, lay, ts):
    n_batch, n_pages = page_table.shape
    page = cache.shape[1]
    past_len = n_pages * page
    n_heads = lay.n_heads_a // N_KV_A
    n_ch = kvc.shape[-2]
    n_s = -(-(past_len + ts) // SEL_BLOCK)
    sel_lanes = -(-n_s // LANES) * LANES
    kvw = N_KV_A * HEAD_DIM
    assert n_pages % PAGES_PER_STEP == 0 and row0 % ts == 0 and ts <= SEL_BLOCK and past_len % SEL_BLOCK == 0
    assert ts % 8 == 0 and lay.ga % LANES == 0
    assert (past_len + ts - CMP_BLOCK) // CMP_STRIDE + 1 <= n_ch and n_ch * CMP_STRIDE <= past_len
    r0 = row0 // ts
    keys_per_step = PAGES_PER_STEP * page
    qw = lay.n_heads_a * HEAD_DIM
    rows = lambda width, col: pl.BlockSpec((ts, width), lambda b, s, pt: (r0 + b, col // width))
    const = lambda nd: (lambda b, s, pt: (0,) * nd)
    grid_spec = pltpu.PrefetchScalarGridSpec(
        num_scalar_prefetch=1,
        grid=(n_batch, n_pages // PAGES_PER_STEP),
        in_specs=_page_specs((1, page, 2, N_KV_A, HEAD_DIM), 1) + [
            rows(qw, lay.qa), rows(LANES, lay.ga),
            pl.BlockSpec((1, 2, N_KV_A, n_ch, HEAD_DIM), lambda b, s, pt: (b, 0, 0, 0, 0)),
            rows(kvw, lay.selk), rows(kvw, lay.selv), rows(kvw, lay.wink), rows(kvw, lay.winv),
            pl.BlockSpec((1,) + win_buf.shape[1:], lambda b, s, pt: (b, 0, 0, 0, 0)),
            pl.BlockSpec((n_ch, sel_lanes), const(2)),
            pl.BlockSpec((sel_lanes, keys_per_step), lambda b, s, pt: (0, s))],
        out_specs=pl.BlockSpec((1, ts, qw), lambda b, s, pt: (b, 0, 0)),
        scratch_shapes=[pltpu.VMEM((N_KV_A, n_heads, ts, 1), jnp.float32),
                        pltpu.VMEM((N_KV_A, n_heads, ts, 1), jnp.float32),
                        pltpu.VMEM((N_KV_A, n_heads, ts, HEAD_DIM), jnp.float32),
                        pltpu.VMEM((N_KV_A, n_heads, ts, HEAD_DIM), jnp.float32),
                        pltpu.VMEM((N_KV_A, ts, sel_lanes), jnp.float32)],
    )
    return pl.pallas_call(
        functools.partial(_nsa_sample_kernel, n_heads=n_heads, n_s=n_s, past_len=past_len),
        grid_spec=grid_spec,
        out_shape=jax.ShapeDtypeStruct((n_batch, ts, qw), _BF),
        compiler_params=_params("parallel", "arbitrary"),
    )(page_table, *([cache] * PAGES_PER_STEP), proj, proj, kvc, proj, proj, proj, proj, win_buf,
      _cover_matrix(n_ch, sel_lanes, n_s), _expand_matrix(sel_lanes, past_len))


def _dsa_sample_select_kernel(pt_ref, *refs, n_idx, n_top, iw_lane, past_len):
    pages = refs[:PAGES_PER_STEP]
    iq_ref, iw_ref, nik_ref, tri_ref, mask_ref, score_ref, key_ref = refs[PAGES_PER_STEP:]
    step = pl.program_id(1)
    ts = iq_ref.shape[0]
    keys_per_step = PAGES_PER_STEP * pages[0].shape[1]
    iqs = _stack_heads(iq_ref[...], n_idx).astype(_BF)
    iw = iw_ref[...]
    ik = jnp.concatenate([p[0] for p in pages], axis=0).astype(_BF)
    start = pl.multiple_of(step * keys_per_step, keys_per_step)
    score_ref[:, pl.ds(start, keys_per_step)] = _indexer_scores(iqs, ik, iw, iw_lane, n_idx)

    @pl.when(step == pl.num_programs(1) - 1)
    def _():
        sc = _indexer_scores(iqs, _pad_rows(nik_ref[...], LANES).astype(_BF), iw, iw_lane, n_idx)
        row = lax.broadcasted_iota(jnp.int32, (ts, LANES), 0)
        lane = lax.broadcasted_iota(jnp.int32, (ts, LANES), 1)
        score_ref[:, past_len:past_len + LANES] = jnp.where(lane < ts, jnp.where(lane <= row, sc, NEG), -jnp.inf)
        _topk_mask(score_ref, key_ref, mask_ref.at[0], tri_ref, past_len // LANES + 1, n_top)


def dsa_sample_select(proj, row0, cache_idx, page_table, lay, ts):
    n_batch, n_pages = page_table.shape
    page = cache_idx.shape[1]
    past_len = n_pages * page
    assert n_pages % PAGES_PER_STEP == 0 and row0 % ts == 0 and ts <= LANES and past_len % LANES == 0
    n_top = min(DSA_TOPK, (past_len + ts) // 4)
    r0 = row0 // ts
    iqw = lay.n_idx_heads * IDX_DIM
    width = past_len + LANES
    rows = lambda w, col: pl.BlockSpec((ts, w), lambda b, s, pt: (r0 + b, col // w))
    grid_spec = pltpu.PrefetchScalarGridSpec(
        num_scalar_prefetch=1,
        grid=(n_batch, n_pages // PAGES_PER_STEP),
        in_specs=_page_specs((1, page, IDX_DIM), 0) + [
            rows(iqw, lay.iq), rows(LANES, lay.iw), rows(IDX_DIM, lay.ik),
            pl.BlockSpec((LANES, LANES), lambda b, s, pt: (0, 0))],
        out_specs=pl.BlockSpec((1, ts, width), lambda b, s, pt: (b, 0, 0)),
        scratch_shapes=[pltpu.VMEM((ts, width), jnp.float32), pltpu.VMEM((ts, width), jnp.int32)],
    )
    return pl.pallas_call(
        functools.partial(_dsa_sample_select_kernel, n_idx=lay.n_idx_heads, n_top=n_top,
                          iw_lane=lay.iw % LANES, past_len=past_len),
        grid_spec=grid_spec,
        out_shape=jax.ShapeDtypeStruct((n_batch, ts, width), jnp.float32),
        compiler_params=_params("parallel", "arbitrary"),
    )(page_table, *([cache_idx] * PAGES_PER_STEP), proj, proj, proj, _tri_matrix())


def _dsa_sample_attend_kernel(pt_ref, *refs, n_heads):
    pages = refs[:PAGES_PER_STEP]
    q_ref, mask_ref, nmask_ref, nk_ref, nv_ref, o_ref, m_ref, l_ref, acc_ref = refs[PAGES_PER_STEP:]
    step = pl.program_id(1)
    ts = q_ref.shape[0]
    qs = _stack_heads(q_ref[...], n_heads).astype(_BF)

    @pl.when(step == 0)
    def _():
        m_ref[...], l_ref[...], acc_ref[...] = _flash_init(n_heads, ts)

    k = jnp.concatenate([p[0, :, 0, :] for p in pages], axis=0).astype(_BF)
    v = jnp.concatenate([p[0, :, 1, :] for p in pages], axis=0).astype(_BF)
    m_ref[...], l_ref[...], acc_ref[...] = _flash_update(qs, k, v, mask_ref[0] > 0.5,
                                                         (m_ref[...], l_ref[...], acc_ref[...]), n_heads)

    @pl.when(step == pl.num_programs(1) - 1)
    def _():
        row = lax.broadcasted_iota(jnp.int32, (ts, LANES), 0)
        lane = lax.broadcasted_iota(jnp.int32, (ts, LANES), 1)
        mask = (nmask_ref[0] > 0.5) & (lane <= row) & (lane < ts)
        o = _flash_finish(_flash_update(qs, _pad_rows(nk_ref[...], LANES).astype(_BF),
                                        _pad_rows(nv_ref[...], LANES).astype(_BF), mask,
                                        (m_ref[...], l_ref[...], acc_ref[...]), n_heads))
        for h in range(n_heads):
            o_ref[0, :, h * HEAD_DIM:(h + 1) * HEAD_DIM] = o[h].astype(o_ref.dtype)


def dsa_sample_attend(proj, row0, mask, cache_kv, page_table, lay, ts):
    n_batch, n_pages = page_table.shape
    page = cache_kv.shape[1]
    past_len = n_pages * page
    keys_per_step = PAGES_PER_STEP * page
    r0 = row0 // ts
    qw = lay.n_heads_b * HEAD_DIM
    rows = lambda w, col: pl.BlockSpec((ts, w), lambda b, s, pt: (r0 + b, col // w))
    grid_spec = pltpu.PrefetchScalarGridSpec(
        num_scalar_prefetch=1,
        grid=(n_batch, n_pages // PAGES_PER_STEP),
        in_specs=_page_specs((1, page, 2, HEAD_DIM), 0) + [
            rows(qw, lay.qb),
            pl.BlockSpec((1, ts, keys_per_step), lambda b, s, pt: (b, 0, s)),
            pl.BlockSpec((1, ts, LANES), lambda b, s, pt: (b, 0, past_len // LANES)),
            rows(HEAD_DIM, lay.dk), rows(HEAD_DIM, lay.dv)],
        out_specs=pl.BlockSpec((1, ts, qw), lambda b, s, pt: (b, 0, 0)),
        scratch_shapes=[pltpu.VMEM((lay.n_heads_b, ts, 1), jnp.float32),
                        pltpu.VMEM((lay.n_heads_b, ts, 1), jnp.float32),
                        pltpu.VMEM((lay.n_heads_b, ts, HEAD_DIM), jnp.float32)],
    )
    return pl.pallas_call(
        functools.partial(_dsa_sample_attend_kernel, n_heads=lay.n_heads_b),
        grid_spec=grid_spec,
        out_shape=jax.ShapeDtypeStruct((n_batch, ts, qw), _BF),
        compiler_params=_params("parallel", "arbitrary"),
    )(page_table, *([cache_kv] * PAGES_PER_STEP), proj, mask, mask, proj, proj)


class Layout:
    def __init__(self, d_model, n_heads_a, n_heads_b, n_idx_heads):
        self.n_heads_a, self.n_heads_b, self.n_idx_heads = n_heads_a, n_heads_b, n_idx_heads
        src = np.cumsum([0, n_heads_a * HEAD_DIM, 6 * N_KV_A * HEAD_DIM, 3 * n_heads_a,
                         n_heads_b * HEAD_DIM, 2 * HEAD_DIM, n_idx_heads * IDX_DIM, n_idx_heads,
                         IDX_DIM, 2 * d_model])
        s_qa, s_kva, s_ga, s_qb, s_kvb, s_iq, s_iw, s_ik, s_mg, s_end = (int(v) for v in src)
        small = 3 * n_heads_a + n_idx_heads
        assert small <= LANES
        self.small_pad = LANES - small
        kvw = N_KV_A * HEAD_DIM
        self.pieces = [
            (s_qa, s_kva, True),
            (s_qb, s_kvb, True),
            (s_iq, s_iw, True),
            (s_kva, s_kva + 2 * kvw, False),
            (s_kva + 2 * kvw, s_kva + 3 * kvw, True),
            (s_kva + 3 * kvw, s_kva + 4 * kvw, False),
            (s_kva + 4 * kvw, s_kva + 5 * kvw, True),
            (s_kva + 5 * kvw, s_ga, False),
            (s_kvb, s_kvb + HEAD_DIM, True),
            (s_kvb + HEAD_DIM, s_iq, False),
            (s_ik, s_mg, True),
            (s_ga, s_qb, False),
            (s_iw, s_ik, False),
            None,
            (s_mg, s_end, False),
        ]
        off = 0
        starts = []
        for p in self.pieces:
            starts.append(off)
            off += self.small_pad if p is None else p[1] - p[0]
        (self.qa, self.qb, self.iq, self.cmp, self.selk, self.selv, self.wink, self.winv, self.dk,
         self.dv, self.ik, self.ga, self.iw, _, self.mg) = starts
        self.width = off
        assert self.width % LANES == 0
        flags = np.zeros(self.width // LANES, np.int32)
        for st, p in zip(starts, self.pieces):
            if p is not None and p[2]:
                assert st % LANES == 0 and (p[1] - p[0]) % LANES == 0
                flags[st // LANES:(st + p[1] - p[0]) // LANES] = 1
        self.rope_flags = flags

    def pack(self, w_in):
        cols = []
        for p in self.pieces:
            if p is None:
                cols.append(jnp.zeros((w_in.shape[0], self.small_pad), w_in.dtype))
            else:
                cols.append(w_in[:, p[0]:p[1]])
        return jnp.concatenate(cols, axis=1).astype(_BF)


def rope_tables(pos):
    half = HEAD_DIM // 2
    inv = ROPE_THETA ** (-jnp.arange(half, dtype=jnp.float32) / half)
    ang = pos.astype(jnp.float32)[:, None] * inv[None, :]
    cos, sin = jnp.cos(ang), jnp.sin(ang)
    return jnp.concatenate([cos, cos], axis=1), jnp.concatenate([-sin, sin], axis=1)


def _swiglu_half_step(h, g, wg, wu, wd, w_index):
    xn = rmsnorm(h, g, _BF)
    m = h.shape[0]
    a = ffn_gate_up(xn, wg, wu, w_index, tm=_row_tile(m, ROW_TILE_WIDE), tn=2 * LANES)
    return resid_matmul(a, wd, w_index, h, 0.5, tm=_row_tile(m, ROW_TILE_DOWN), tn=2 * LANES)


def kernel(x_prompt, x_sample, cache_nsa_kv, cache_nsa_win, cache_dsa_kv, cache_dsa_idx, page_table,
           g_norm, w_ffn_gate, w_ffn_up, w_ffn_down, w_in, w_cmp1, w_cmp2, cmp_pos,
           w_br_a, w_br_b, w_out, g_final):
    B, T, D = x_prompt.shape
    DB, Ts, _ = x_sample.shape
    depth = g_norm.shape[0]
    page = cache_nsa_kv.shape[2]
    past_len = page_table.shape[1] * page
    n_heads_a = w_br_a.shape[1] // HEAD_DIM
    n_heads_b = w_br_b.shape[1] // HEAD_DIM
    G = N_KV_A
    lay = Layout(D, n_heads_a, n_heads_b, n_heads_b // 2)
    Mp, Ms = B * T, DB * Ts
    tm_wide = _row_tile(Mp + Ms, ROW_TILE_WIDE)

    pos_p = jnp.arange(T, dtype=jnp.int32)
    pos_s = past_len + jnp.arange(Ts, dtype=jnp.int32)
    cos, sin = rope_tables(jnp.concatenate([jnp.tile(pos_p, B), jnp.tile(pos_s, DB)]))
    rope_flags = jnp.asarray(lay.rope_flags)

    def block_end_tables(n_ch):
        return rope_tables(jnp.arange(n_ch, dtype=jnp.int32) * CMP_STRIDE + CMP_BLOCK - 1)

    h = jnp.concatenate([x_prompt.reshape(Mp, D), x_sample.reshape(Ms, D)], axis=0)
    outs = [[] for _ in range(8)]
    for l in range(depth):
        h = _swiglu_half_step(h, g_norm[l, 0], w_ffn_gate, w_ffn_up, w_ffn_down, (l, 0))

        u = rmsnorm(h, g_norm[l, 1], _BF)
        proj, nsa_rows, win_rows, dsa_rows, ik_rows = in_project(u, lay.pack(w_in[l]), rope_flags, cos, sin, lay,
                                                                 tm=tm_wide, tn=4 * LANES)
        w1 = w_cmp1[l].reshape(2, 2, CMP_STRIDE * HEAD_DIM, -1).astype(_BF)
        w2 = w_cmp2[l].astype(_BF)

        kvc_p = compress_prompt(proj, B, T, lay.cmp, w1, w2, cmp_pos[l], *block_end_tables(T // CMP_STRIDE))
        o_a_p = nsa_prompt(proj, kvc_p, lay, B, T, tq=Q_TILE)
        o_b_p = dsa_prompt(proj, lay, B, T, tq=Q_TILE)

        kvc_s = compress_sample(cache_nsa_kv[l], page_table, w1, w2, cmp_pos[l],
                                *block_end_tables(past_len // CMP_STRIDE))
        o_a_s = nsa_sample(proj, Mp, kvc_s, cache_nsa_kv[l], cache_nsa_win[l], page_table, lay, Ts)
        top_mask = dsa_sample_select(proj, Mp, cache_dsa_idx[l], page_table, lay, Ts)
        o_b_s = dsa_sample_attend(proj, Mp, top_mask, cache_dsa_kv[l], page_table, lay, Ts)

        o_a = jnp.concatenate([o_a_p, o_a_s.reshape(Ms, -1)], axis=0)
        o_b = jnp.concatenate([o_b_p, o_b_s.reshape(Ms, -1)], axis=0)
        m = merge_branches(o_a, o_b, w_br_a, w_br_b, (l,), proj, lay.mg, tm=tm_wide, tn=4 * LANES)
        h = resid_matmul(m, w_out, (l,), h, 1.0, tm=tm_wide, tn=4 * LANES)

        win_p = win_rows[:Mp].reshape(B, T, 2, G, HEAD_DIM)
        win_s = win_rows[Mp:].reshape(DB, Ts, 2, G, HEAD_DIM)
        outs[0].append(nsa_rows[:Mp].reshape(B, T, 4, G, HEAD_DIM))
        outs[1].append(win_p[:, T - min(WINDOW, T):])
        outs[2].append(dsa_rows[:Mp].reshape(B, T, 2, HEAD_DIM))
        outs[3].append(ik_rows[:Mp].reshape(B, T, IDX_DIM))
        outs[4].append(nsa_rows[Mp:].reshape(DB, Ts, 4, G, HEAD_DIM))
        outs[5].append(jnp.concatenate([cache_nsa_win[l], win_s], axis=1)[:, Ts:])
        outs[6].append(dsa_rows[Mp:].reshape(DB, Ts, 2, HEAD_DIM))
        outs[7].append(ik_rows[Mp:].reshape(DB, Ts, IDX_DIM))

        h = _swiglu_half_step(h, g_norm[l, 2], w_ffn_gate, w_ffn_up, w_ffn_down, (l, 1))

    y_p = rmsnorm(h, g_final, jnp.float32, 0, Mp)
    y_s = rmsnorm(h, g_final, jnp.float32, Mp, Ms)
    return (y_p.reshape(B, T, D), y_s.reshape(DB, Ts, D), *(jnp.stack(o) for o in outs))
```

```python
import functools

import jax
import jax.numpy as jnp
import numpy as np
from jax import lax
from jax.experimental import pallas as pl
from jax.experimental.pallas import tpu as pltpu

HEAD_DIM = 128
N_KV_A = 2
IDX_DIM = 128
CMP_BLOCK = 32
CMP_STRIDE = 16
SEL_BLOCK = 64
N_SEL = 16
N_LOCAL_SEL = 2
WINDOW = 512
DSA_TOPK = 256
ROPE_THETA = 10000.0
RMS_EPS = 1e-6
NEG = -1e30
BIG = 1e30
TINY = 1e-30

LANES = 128
VMEM_LIMIT = 56 * 1024 * 1024
PAGES_PER_STEP = 32

_NT = (((1,), (1,)), ((), ()))
_BF = jnp.bfloat16


ROW_TILE_WIDE = 1408
ROW_TILE_DOWN = 1056
Q_TILE = 2 * LANES


def _params(*sem):
    return pltpu.CompilerParams(dimension_semantics=sem, vmem_limit_bytes=VMEM_LIMIT)


def _row_tile(m, limit):
    return max(t for t in range(16, limit + 1, 16) if m % t == 0)


def _rmsnorm_kernel(x_ref, g_ref, o_ref):
    x = x_ref[...]
    y = x * lax.rsqrt(jnp.mean(x * x, axis=-1, keepdims=True) + RMS_EPS)
    o_ref[...] = (y * g_ref[...]).astype(o_ref.dtype)


def rmsnorm(x, g, out_dtype, row0=0, n_rows=None, tm=256):
    D = x.shape[1]
    n_rows = x.shape[0] - row0 if n_rows is None else n_rows
    assert row0 % tm == 0 and n_rows % tm == 0
    return pl.pallas_call(
        _rmsnorm_kernel,
        grid=(n_rows // tm,),
        in_specs=[pl.BlockSpec((tm, D), lambda i: (row0 // tm + i, 0)),
                  pl.BlockSpec((1, D), lambda i: (0, 0))],
        out_specs=pl.BlockSpec((tm, D), lambda i: (i, 0)),
        out_shape=jax.ShapeDtypeStruct((n_rows, D), out_dtype),
        compiler_params=_params("parallel"),
    )(x, g.reshape(1, D))


def _gateup_kernel(x_ref, wg_ref, wu_ref, o_ref):
    x = x_ref[...]
    g = jnp.dot(x, wg_ref[...].astype(_BF), preferred_element_type=jnp.float32)
    u = jnp.dot(x, wu_ref[...].astype(_BF), preferred_element_type=jnp.float32)
    o_ref[...] = (g * jax.nn.sigmoid(g) * u).astype(o_ref.dtype)


def _weight_spec(w, w_index, rows, tn):
    assert w.ndim == len(w_index) + 2 and w.shape[-2] == rows
    return pl.BlockSpec((None,) * len(w_index) + (rows, tn), lambda i, j: tuple(w_index) + (0, j))


def ffn_gate_up(xn, wg, wu, w_index, tm, tn):
    M, D = xn.shape
    F = wg.shape[-1]
    return pl.pallas_call(
        _gateup_kernel,
        grid=(M // tm, F // tn),
        in_specs=[pl.BlockSpec((tm, D), lambda i, j: (i, 0)),
                  _weight_spec(wg, w_index, D, tn),
                  _weight_spec(wu, w_index, D, tn)],
        out_specs=pl.BlockSpec((tm, tn), lambda i, j: (i, j)),
        out_shape=jax.ShapeDtypeStruct((M, F), _BF),
        compiler_params=_params("parallel", "parallel"),
    )(xn, wg, wu)


def _resid_matmul_kernel(a_ref, w_ref, r_ref, o_ref, *, scale):
    acc = jnp.dot(a_ref[...], w_ref[...].astype(_BF), preferred_element_type=jnp.float32)
    o_ref[...] = r_ref[...] + scale * acc


def resid_matmul(a, w, w_index, resid, scale, tm, tn):
    M, K = a.shape
    N = w.shape[-1]
    return pl.pallas_call(
        functools.partial(_resid_matmul_kernel, scale=scale),
        grid=(M // tm, N // tn),
        in_specs=[pl.BlockSpec((tm, K), lambda i, j: (i, 0), pipeline_mode=pl.Buffered(1)),
                  _weight_spec(w, w_index, K, tn),
                  pl.BlockSpec((tm, tn), lambda i, j: (i, j))],
        out_specs=pl.BlockSpec((tm, tn), lambda i, j: (i, j)),
        out_shape=jax.ShapeDtypeStruct((M, N), jnp.float32),
        compiler_params=_params("parallel", "parallel"),
    )(a, w, resid)


def _rotary(y, cos, sin):
    return y * cos + pltpu.roll(y, HEAD_DIM // 2, axis=1) * sin


def _inproj_kernel(flags_ref, x_ref, w_ref, cos_ref, sin_ref, o_ref, *row_refs, n_chunks, routes):
    j = pl.program_id(1)
    acc = jnp.dot(x_ref[...], w_ref[...], preferred_element_type=jnp.float32)
    for c in range(n_chunks):
        sl = slice(c * LANES, (c + 1) * LANES)
        y = acc[:, sl]
        flag = flags_ref[j * n_chunks + c]

        @pl.when(flag == 1)
        def _():
            o_ref[:, sl] = _rotary(y, cos_ref[...], sin_ref[...])

        @pl.when(flag == 0)
        def _():
            o_ref[:, sl] = y

    tm = o_ref.shape[0]
    for chunk, out_idx, kind, n_kinds in routes:
        @pl.when(j == chunk // n_chunks)
        def _():
            c = chunk % n_chunks
            row_refs[out_idx][pl.ds(kind, tm, stride=n_kinds), :] = o_ref[:, c * LANES:(c + 1) * LANES]


def in_project(u, w, rope_flags, cos, sin, lay, tm, tn):
    M, D = u.shape
    N = w.shape[1]
    n_chunks = tn // LANES
    G = N_KV_A
    kinds = [4 * G, 2 * G, 2, 1]
    routes = ([(lay.cmp // LANES + k, 0, k, kinds[0]) for k in range(kinds[0])]
              + [(lay.wink // LANES + k, 1, k, kinds[1]) for k in range(kinds[1])]
              + [(lay.dk // LANES + k, 2, k, kinds[2]) for k in range(kinds[2])]
              + [(lay.ik // LANES, 3, 0, 1)])
    row_shapes = [(M, 4, G, HEAD_DIM), (M, 2, G, HEAD_DIM), (M, 2, HEAD_DIM), (M, IDX_DIM)]
    grid_spec = pltpu.PrefetchScalarGridSpec(
        num_scalar_prefetch=1,
        grid=(M // tm, N // tn),
        in_specs=[pl.BlockSpec((tm, D), lambda i, j, f: (i, 0), pipeline_mode=pl.Buffered(1)),
                  pl.BlockSpec((D, tn), lambda i, j, f: (0, j)),
                  pl.BlockSpec((tm, LANES), lambda i, j, f: (i, 0)),
                  pl.BlockSpec((tm, LANES), lambda i, j, f: (i, 0))],
        out_specs=[pl.BlockSpec((tm, tn), lambda i, j, f: (i, j))]
        + [pl.BlockSpec((tm * nk, LANES), lambda i, j, f: (i, 0)) for nk in kinds],
    )
    proj, *rows = pl.pallas_call(
        functools.partial(_inproj_kernel, n_chunks=n_chunks, routes=routes),
        grid_spec=grid_spec,
        out_shape=[jax.ShapeDtypeStruct((M, N), jnp.float32)]
        + [jax.ShapeDtypeStruct((M * nk, LANES), jnp.float32) for nk in kinds],
        compiler_params=_params("parallel", "arbitrary"),
    )(rope_flags, u, w, cos, sin)
    return (proj, *(r.reshape(shp) for r, shp in zip(rows, row_shapes)))


def _merge_kernel(oa_ref, ob_ref, wa_ref, wb_ref, ga_ref, gb_ref, o_ref):
    ya = jnp.dot(oa_ref[...], wa_ref[...].astype(_BF), preferred_element_type=jnp.float32)
    yb = jnp.dot(ob_ref[...], wb_ref[...].astype(_BF), preferred_element_type=jnp.float32)
    m = jax.nn.sigmoid(ga_ref[...]) * ya + jax.nn.sigmoid(gb_ref[...]) * yb
    o_ref[...] = m.astype(o_ref.dtype)


def merge_branches(o_a, o_b, w_a, w_b, w_index, proj, mg_col, tm, tn):
    M, K = o_a.shape
    N = w_a.shape[-1]
    ja = mg_col // tn
    jb = (mg_col + N) // tn
    return pl.pallas_call(
        _merge_kernel,
        grid=(M // tm, N // tn),
        in_specs=[pl.BlockSpec((tm, K), lambda i, j: (i, 0), pipeline_mode=pl.Buffered(1)),
                  pl.BlockSpec((tm, K), lambda i, j: (i, 0), pipeline_mode=pl.Buffered(1)),
                  _weight_spec(w_a, w_index, K, tn),
                  _weight_spec(w_b, w_index, K, tn),
                  pl.BlockSpec((tm, tn), lambda i, j: (i, ja + j)),
                  pl.BlockSpec((tm, tn), lambda i, j: (i, jb + j))],
        out_specs=pl.BlockSpec((tm, tn), lambda i, j: (i, j)),
        out_shape=jax.ShapeDtypeStruct((M, N), _BF),
        compiler_params=_params("parallel", "parallel"),
    )(o_a, o_b, w_a, w_b, proj, proj)


def _stack_heads(x, n_heads):
    return jnp.concatenate([x[:, h * HEAD_DIM:(h + 1) * HEAD_DIM] for h in range(n_heads)], axis=0)


def _flash_update(qs, k, v, mask, carry, n_rep):
    m, l, acc = carry
    rows, tk = qs.shape[0], k.shape[0]
    tq = rows // n_rep
    scale = HEAD_DIM ** -0.5
    s = lax.dot_general(qs, k, _NT, preferred_element_type=jnp.float32).reshape(n_rep, tq, tk)
    mask = mask[None]
    s = jnp.where(mask, s, NEG)
    m_new = jnp.maximum(m, jnp.max(s, axis=-1, keepdims=True))
    p = jnp.where(mask, jnp.exp((s - m_new) * scale), 0.0)
    alpha = jnp.exp((m - m_new) * scale)
    l = alpha * l + jnp.sum(p, axis=-1, keepdims=True)
    pv = jnp.dot(p.reshape(rows, tk).astype(_BF), v, preferred_element_type=jnp.float32)
    return m_new, l, alpha * acc + pv.reshape(n_rep, tq, HEAD_DIM)


def _flash_init(n_rep, tq):
    return (jnp.full((n_rep, tq, 1), NEG, jnp.float32), jnp.zeros((n_rep, tq, 1), jnp.float32),
            jnp.zeros((n_rep, tq, HEAD_DIM), jnp.float32))


def _flash_finish(carry):
    _, l, acc = carry
    return acc * (1.0 / jnp.maximum(l, TINY))


def _attend(qs, k_ref, v_ref, start, n_chunks, chunk, mask_fn, n_rep):
    rows = qs.shape[0]
    tq = rows // n_rep
    n_lane_tiles = chunk // LANES
    scale = HEAD_DIM ** -0.5

    def scores(c):
        sl = pl.ds(pl.multiple_of(start + c * chunk, LANES), chunk)
        return lax.dot_general(qs, k_ref[sl, :], _NT, preferred_element_type=jnp.float32), sl

    def lane_tile(s, j):
        return s[:, j * LANES:(j + 1) * LANES].reshape(n_rep, tq, LANES)

    def row_max(c, mx):
        s, _ = scores(c)
        for j in range(n_lane_tiles):
            mx = jnp.maximum(mx, jnp.where(mask_fn(c, j)[None], lane_tile(s, j), NEG))
        return mx

    mx = lax.fori_loop(0, n_chunks, row_max, jnp.full((n_rep, tq, LANES), NEG, jnp.float32))
    m = jnp.broadcast_to(jnp.max(mx, axis=-1, keepdims=True), mx.shape)

    def accumulate(c, acc):
        s, sl = scores(c)
        p = [jnp.where(mask_fn(c, j)[None], jnp.exp((lane_tile(s, j) - m) * scale), 0.0)
             .astype(_BF).reshape(rows, LANES) for j in range(n_lane_tiles)]
        return acc + jnp.dot(jnp.concatenate(p, axis=1), v_ref[sl, :], preferred_element_type=jnp.float32)

    acc = lax.fori_loop(0, n_chunks, accumulate, jnp.zeros((rows, 2 * HEAD_DIM), jnp.float32))
    out = acc[:, :HEAD_DIM] * (1.0 / jnp.maximum(acc[:, HEAD_DIM:], TINY))
    return out.reshape(n_rep, tq, HEAD_DIM)


def _compress_rows(load, pe, w1, w2, n_ch):
    rows = [load(s) for s in range(CMP_STRIDE)]
    half = [jnp.concatenate([(rows[s] + pe[r * CMP_STRIDE + s:r * CMP_STRIDE + s + 1, :]).astype(_BF)
                             for s in range(CMP_STRIDE)], axis=1) for r in range(2)]
    h0 = jnp.dot(half[0], w1(0), preferred_element_type=jnp.float32)
    h1 = jnp.dot(half[1], w1(1), preferred_element_type=jnp.float32)
    h = h0 + pltpu.roll(h1, n_ch - 1, axis=0)
    return jnp.dot((h * jax.nn.sigmoid(h)).astype(_BF), w2, preferred_element_type=jnp.float32)


def _cmp_attend_and_choose(qs, kc, vc, cover, t_col, n_heads, n_s):
    rows = qs.shape[0]
    tq = rows // n_heads
    n_ch, lanes = cover.shape
    scale = HEAD_DIM ** -0.5
    s = lax.dot_general(qs, kc, _NT, preferred_element_type=jnp.float32).reshape(n_heads, tq, n_ch)
    end = lax.broadcasted_iota(jnp.int32, (tq, n_ch), 1) * CMP_STRIDE + (CMP_BLOCK - 1)
    cmask = (end <= t_col)[None]
    s = jnp.where(cmask, s, NEG)
    m = jnp.max(s, axis=-1, keepdims=True)
    p = jnp.where(cmask, jnp.exp((s - m) * scale), 0.0)
    p = p * (1.0 / jnp.maximum(jnp.sum(p, axis=-1, keepdims=True), TINY))
    o_cmp = jnp.dot(p.reshape(rows, n_ch).astype(_BF), vc, preferred_element_type=jnp.float32)
    imp = jnp.dot(jnp.sum(p, axis=0).astype(_BF), cover, preferred_element_type=jnp.float32)

    lane = lax.broadcasted_iota(jnp.int32, (tq, lanes), 1)
    lane_f = lane.astype(jnp.float32)
    jt = lax.shift_right_arithmetic(t_col, jnp.int32(SEL_BLOCK.bit_length() - 1))
    adm = lane <= jt
    forced = adm & ((lane == 0) | (lane > jt - N_LOCAL_SEL))
    work = jnp.where(forced, BIG, jnp.where(adm, imp, NEG))
    work = jnp.where(lane < n_s, work, -jnp.inf)
    sel = jnp.zeros((tq, lanes), jnp.bool_)
    for _ in range(min(N_SEL, n_s)):
        mx = jnp.max(work, axis=-1, keepdims=True)
        first = jnp.min(jnp.where(work == mx, lane_f, float(lanes)), axis=-1, keepdims=True)
        pick = lane_f == first
        sel = sel | pick
        work = jnp.where(pick, -jnp.inf, work)
    return o_cmp.reshape(n_heads, tq, HEAD_DIM), sel


def _topk_mask(score_ref, key_ref, mask_ref, tri_ref, n_tiles, n_top):
    tq = score_ref.shape[0]
    int_min = jnp.int32(-2 ** 31)

    def to_key(t, _):
        sl = pl.ds(pl.multiple_of(t * LANES, LANES), LANES)
        bits = lax.bitcast_convert_type(score_ref[:, sl], jnp.int32)
        key_ref[:, sl] = bits ^ ((bits >> 31) & jnp.int32(0x7FFFFFFF))
        return 0

    lax.fori_loop(0, n_tiles, to_key, 0)

    unroll = 8 if isinstance(n_tiles, int) else 1

    def count_ge(cand):
        def body(t, acc):
            sl = pl.ds(pl.multiple_of(t * LANES, LANES), LANES)
            return acc + jnp.where(key_ref[:, sl] >= cand, 1.0, 0.0)
        acc = lax.fori_loop(0, n_tiles, body, jnp.zeros((tq, LANES), jnp.float32), unroll=unroll)
        return jnp.sum(acc, axis=-1, keepdims=True)

    tau = jnp.zeros((tq, 1), jnp.int32)
    for bit in range(31, -1, -1):
        cand = tau | jnp.int32(-2 ** 31 if bit == 31 else 1 << bit)
        tau = jnp.where(count_ge(cand ^ int_min) >= float(n_top), cand, tau)
    thr = tau ^ int_min

    clean = jnp.max(jnp.abs(count_ge(thr) - float(n_top))) == 0.0

    @pl.when(clean)
    def _():
        def keep_ge(t, _):
            sl = pl.ds(pl.multiple_of(t * LANES, LANES), LANES)
            mask_ref[:, sl] = jnp.where(key_ref[:, sl] >= thr, 1.0, 0.0)
            return 0
        lax.fori_loop(0, n_tiles, keep_ge, 0, unroll=unroll)

    @pl.when(jnp.logical_not(clean))
    def _():
        ones = jnp.ones((LANES, LANES), _BF)

        def count_gt(t, acc):
            sl = pl.ds(pl.multiple_of(t * LANES, LANES), LANES)
            return acc + jnp.where(key_ref[:, sl] > thr, 1.0, 0.0)

        n_gt = jnp.sum(lax.fori_loop(0, n_tiles, count_gt, jnp.zeros((tq, LANES), jnp.float32)),
                       axis=-1, keepdims=True)
        need = float(n_top) - n_gt

        def cut(t, eq_before):
            sl = pl.ds(pl.multiple_of(t * LANES, LANES), LANES)
            key = key_ref[:, sl]
            eq = jnp.where(key == thr, 1.0, 0.0)
            rank = (jnp.dot(eq.astype(_BF), tri_ref[...], preferred_element_type=jnp.float32)
                    + jnp.dot(eq_before.astype(_BF), ones, preferred_element_type=jnp.float32))
            keep = (key > thr) | ((key == thr) & (rank <= need))
            mask_ref[:, sl] = jnp.where(keep, 1.0, 0.0)
            return eq_before + eq

        lax.fori_loop(0, n_tiles, cut, jnp.zeros((tq, LANES), jnp.float32))


def _indexer_scores(iqs, ik, iw, iw_lane, n_idx):
    tq = iqs.shape[0] // n_idx
    n = ik.shape[0]
    logits = lax.dot_general(iqs, ik, _NT, preferred_element_type=jnp.float32).reshape(n_idx, tq, n)
    logits = jnp.maximum(logits * IDX_DIM ** -0.5, 0.0)
    sc = jnp.zeros((tq, n), jnp.float32)
    for h in range(n_idx):
        sc = sc + logits[h] * iw[:, iw_lane + h:iw_lane + h + 1]
    return sc * n_idx ** -0.5


def _cover_matrix(n_ch, lanes, n_s):
    ci = np.arange(n_ch)[:, None] * CMP_STRIDE
    sj = np.arange(lanes)[None, :] * SEL_BLOCK
    return jnp.asarray((ci < sj + SEL_BLOCK) & (ci + CMP_BLOCK > sj) & (sj < n_s * SEL_BLOCK), _BF)


def _expand_matrix(lanes, n_keys):
    return jnp.asarray(np.arange(lanes)[:, None] == np.arange(n_keys)[None, :] // SEL_BLOCK, _BF)


def _tri_matrix():
    return jnp.asarray(np.arange(LANES)[:, None] <= np.arange(LANES)[None, :], _BF)


def _pad_rows(x, n):
    return jnp.concatenate([x, jnp.zeros((n - x.shape[0], x.shape[1]), x.dtype)], axis=0)


def _compress_kernel(x_ref, w1_ref, w2_ref, pe_ref, cos_ref, sin_ref, o_ref):
    n_ch = o_ref.shape[-2]
    y = _compress_rows(lambda s: x_ref[pl.ds(s, n_ch, stride=CMP_STRIDE), :], pe_ref[0],
                       lambda r: w1_ref[0, r], w2_ref[0], n_ch)

    @pl.when(pl.program_id(1) == 0)
    def _():
        o_ref[0, 0, 0] = _rotary(y, cos_ref[...], sin_ref[...]).astype(o_ref.dtype)

    @pl.when(pl.program_id(1) != 0)
    def _():
        o_ref[0, 0, 0] = y.astype(o_ref.dtype)


def compress_prompt(proj, n_batch, seq, cmp_col, w1, w2, pe, cos_end, sin_end):
    n_ch = seq // CMP_STRIDE
    col0 = cmp_col // HEAD_DIM
    return pl.pallas_call(
        _compress_kernel,
        grid=(n_batch, 2, N_KV_A),
        in_specs=[pl.BlockSpec((seq, HEAD_DIM), lambda b, kv, g: (b, col0 + kv * N_KV_A + g)),
                  pl.BlockSpec((1,) + w1.shape[1:], lambda b, kv, g: (kv, 0, 0, 0)),
                  pl.BlockSpec((1, w2.shape[1], HEAD_DIM), lambda b, kv, g: (kv, 0, 0)),
                  pl.BlockSpec((1, CMP_BLOCK, HEAD_DIM), lambda b, kv, g: (kv, 0, 0)),
                  pl.BlockSpec((n_ch, HEAD_DIM), lambda b, kv, g: (0, 0)),
                  pl.BlockSpec((n_ch, HEAD_DIM), lambda b, kv, g: (0, 0))],
        out_specs=pl.BlockSpec((1, 1, 1, n_ch, HEAD_DIM), lambda b, kv, g: (b, kv, g, 0, 0)),
        out_shape=jax.ShapeDtypeStruct((n_batch, 2, N_KV_A, n_ch, HEAD_DIM), _BF),
        compiler_params=_params("parallel", "parallel", "parallel"),
    )(proj, w1, w2, pe, cos_end, sin_end)


def _nsa_prompt_kernel(q_ref, kc_ref, vc_ref, selk_ref, selv_ref, wink_ref, winv_ref, gate_ref,
                       cover_ref, expand_ref, o_ref, selexp_ref, sk_ref, sv_ref, wk_ref, wv_ref,
                       *, n_heads, n_s, chunk):
    tq = q_ref.shape[0]
    seq = selexp_ref.shape[-1]
    n_grp = N_KV_A
    qi = pl.program_id(1)
    row0 = pl.multiple_of(qi * tq, tq)
    t_col = qi * tq + lax.broadcasted_iota(jnp.int32, (tq, 1), 0)
    qw = n_heads * HEAD_DIM

    @pl.when(qi == 0)
    def _():
        for ref in (sk_ref, sv_ref, wk_ref, wv_ref):
            ref[...] = jnp.zeros(ref.shape, ref.dtype)

    ones = jnp.ones((tq, HEAD_DIM), _BF)
    causal = lax.broadcasted_iota(jnp.int32, (tq, seq), 1) <= t_col
    qs, o_cmp = [], []
    for g in range(n_grp):
        lanes = slice(g * HEAD_DIM, (g + 1) * HEAD_DIM)
        sk_ref[g, pl.ds(row0, tq), :] = selk_ref[:, lanes].astype(_BF)
        sv_ref[g, pl.ds(row0, tq), :] = jnp.concatenate([selv_ref[:, lanes].astype(_BF), ones], axis=1)
        wk_ref[g, pl.ds(row0, tq), :] = wink_ref[:, lanes].astype(_BF)
        wv_ref[g, pl.ds(row0, tq), :] = jnp.concatenate([winv_ref[:, lanes].astype(_BF), ones], axis=1)
        qs.append(_stack_heads(q_ref[:, g * qw:(g + 1) * qw], n_heads).astype(_BF))
    sels = []
    for g in range(n_grp):
        o, sel = _cmp_attend_and_choose(qs[g], kc_ref[0, 0, g], vc_ref[0, 0, g], cover_ref[...], t_col, n_heads, n_s)
        o_cmp.append(o)
        sels.append(sel)
    for g in range(n_grp):
        chosen = jnp.dot(jnp.where(sels[g], 1.0, 0.0).astype(_BF), expand_ref[...],
                         preferred_element_type=jnp.float32)
        selexp_ref[g] = jnp.where(causal, chosen, 0.0)

    span = WINDOW + tq
    start = pl.multiple_of(jnp.maximum(row0 - WINDOW, 0), tq)
    d = t_col - (start + lax.broadcasted_iota(jnp.int32, (tq, span), 1))
    visible = (d >= 0) & (d < WINDOW)
    gates = jax.nn.sigmoid(gate_ref[...])
    n_all = n_grp * n_heads
    for g in range(n_grp):
        def sel_mask(c, j):
            return selexp_ref[g, :, pl.ds(pl.multiple_of(c * chunk + j * LANES, LANES), LANES)] > 0.5

        o_slc = _attend(qs[g], sk_ref.at[g], sv_ref.at[g], 0, (row0 + tq + chunk - 1) // chunk, chunk, sel_mask,
                        n_heads)
        o_win = _attend(qs[g], wk_ref.at[g], wv_ref.at[g], start, 1, span,
                        lambda c, j: visible[:, j * LANES:(j + 1) * LANES], n_heads)
        for r in range(n_heads):
            h = g * n_heads + r
            o = (gates[:, h:h + 1] * o_cmp[g][r] + gates[:, n_all + h:n_all + h + 1] * o_slc[r]
                 + gates[:, 2 * n_all + h:2 * n_all + h + 1] * o_win[r])
            o_ref[:, h * HEAD_DIM:(h + 1) * HEAD_DIM] = o.astype(o_ref.dtype)


def nsa_prompt(proj, kvc, lay, n_batch, seq, tq, chunk=512):
    n_heads = lay.n_heads_a // N_KV_A
    n_ch = kvc.shape[-2]
    n_s = -(-seq // SEL_BLOCK)
    assert tq % LANES == 0 and n_s <= LANES and seq % chunk == 0 and chunk % tq == 0
    assert WINDOW % tq == 0 and seq >= WINDOW + tq and lay.ga % LANES == 0
    nq = seq // tq
    qw = lay.n_heads_a * HEAD_DIM
    kvw = N_KV_A * HEAD_DIM

    def tile(col):
        return pl.BlockSpec((tq, kvw), lambda b, i: (b * nq + i, col // kvw))

    return pl.pallas_call(
        functools.partial(_nsa_prompt_kernel, n_heads=n_heads, n_s=n_s, chunk=chunk),
        grid=(n_batch, nq),
        in_specs=[pl.BlockSpec((tq, qw), lambda b, i: (b * nq + i, lay.qa // qw)),
                  pl.BlockSpec((1, 1, N_KV_A, n_ch, HEAD_DIM), lambda b, i: (b, 0, 0, 0, 0)),
                  pl.BlockSpec((1, 1, N_KV_A, n_ch, HEAD_DIM), lambda b, i: (b, 1, 0, 0, 0)),
                  tile(lay.selk), tile(lay.selv), tile(lay.wink), tile(lay.winv),
                  pl.BlockSpec((tq, LANES), lambda b, i: (b * nq + i, lay.ga // LANES)),
                  pl.BlockSpec((n_ch, LANES), lambda b, i: (0, 0)),
                  pl.BlockSpec((LANES, seq), lambda b, i: (0, 0))],
        out_specs=pl.BlockSpec((tq, qw), lambda b, i: (b * nq + i, 0)),
        out_shape=jax.ShapeDtypeStruct((n_batch * seq, qw), _BF),
        scratch_shapes=[pltpu.VMEM((N_KV_A, tq, seq), jnp.float32),
                        pltpu.VMEM((N_KV_A, seq, HEAD_DIM), _BF), pltpu.VMEM((N_KV_A, seq, 2 * HEAD_DIM), _BF),
                        pltpu.VMEM((N_KV_A, seq, HEAD_DIM), _BF), pltpu.VMEM((N_KV_A, seq, 2 * HEAD_DIM), _BF)],
        compiler_params=_params("parallel", "arbitrary"),
    )(proj, kvc, kvc, proj, proj, proj, proj, proj, _cover_matrix(n_ch, LANES, n_s), _expand_matrix(LANES, seq))


def _dsa_prompt_kernel(iq_ref, iw_ref, q_ref, ik_ref, k_ref, v_ref, tri_ref, o_ref,
                       score_ref, key_ref, mask_ref, ikb_ref, kb_ref, vb_ref,
                       *, n_idx, n_heads, n_top, iw_lane, chunk):
    tq = iq_ref.shape[0]
    seq = mask_ref.shape[1]
    qi = pl.program_id(1)
    n_tiles = qi + 1
    row0 = pl.multiple_of(qi * tq, tq)
    t_col = qi * tq + lax.broadcasted_iota(jnp.int32, (tq, 1), 0)
    key_iota = lax.broadcasted_iota(jnp.int32, (tq, tq), 1)

    @pl.when(qi == 0)
    def _():
        for ref in (ikb_ref, kb_ref, vb_ref):
            ref[...] = jnp.zeros(ref.shape, ref.dtype)

    ikb_ref[pl.ds(row0, tq), :] = ik_ref[...].astype(_BF)
    kb_ref[pl.ds(row0, tq), :] = k_ref[...].astype(_BF)
    vb_ref[pl.ds(row0, tq), :] = jnp.concatenate([v_ref[...].astype(_BF), jnp.ones((tq, HEAD_DIM), _BF)], axis=1)
    iqs = _stack_heads(iq_ref[...], n_idx).astype(_BF)
    iw = iw_ref[...]

    def score_tile(kt, _):
        sl = pl.ds(pl.multiple_of(kt * tq, tq), tq)
        sc = _indexer_scores(iqs, ikb_ref[sl, :], iw, iw_lane, n_idx)
        score_ref[:, sl] = jnp.where(kt * tq + key_iota <= t_col, sc, NEG)
        return 0

    lax.fori_loop(0, n_tiles, score_tile, 0)
    mask_ref[...] = jnp.zeros(mask_ref.shape, mask_ref.dtype)
    _topk_mask(score_ref, key_ref, mask_ref, tri_ref, n_tiles * (tq // LANES), n_top)
    causal = lax.broadcasted_iota(jnp.int32, (tq, seq), 1) <= t_col
    mask_ref[...] = jnp.where(causal, mask_ref[...], 0.0)

    def dsa_mask(c, j):
        return mask_ref[:, pl.ds(pl.multiple_of(c * chunk + j * LANES, LANES), LANES)] > 0.5

    qs = _stack_heads(q_ref[...], n_heads).astype(_BF)
    o = _attend(qs, kb_ref, vb_ref, 0, (row0 + tq + chunk - 1) // chunk, chunk, dsa_mask, n_heads)
    for h in range(n_heads):
        o_ref[:, h * HEAD_DIM:(h + 1) * HEAD_DIM] = o[h].astype(o_ref.dtype)


def dsa_prompt(proj, lay, n_batch, seq, tq, chunk=512):
    assert tq % LANES == 0 and seq % chunk == 0 and chunk % tq == 0
    nq = seq // tq
    n_top = min(DSA_TOPK, seq // 4)
    iqw = lay.n_idx_heads * IDX_DIM
    qw = lay.n_heads_b * HEAD_DIM

    def tile(col):
        c0 = col // HEAD_DIM
        return pl.BlockSpec((tq, HEAD_DIM), lambda b, i: (b * nq + i, c0))

    return pl.pallas_call(
        functools.partial(_dsa_prompt_kernel, n_idx=lay.n_idx_heads, n_heads=lay.n_heads_b, n_top=n_top,
                          iw_lane=lay.iw % LANES, chunk=chunk),
        grid=(n_batch, nq),
        in_specs=[pl.BlockSpec((tq, iqw), lambda b, i: (b * nq + i, lay.iq // iqw)),
                  pl.BlockSpec((tq, LANES), lambda b, i: (b * nq + i, lay.iw // LANES)),
                  pl.BlockSpec((tq, qw), lambda b, i: (b * nq + i, lay.qb // qw)),
                  tile(lay.ik), tile(lay.dk), tile(lay.dv),
                  pl.BlockSpec((LANES, LANES), lambda b, i: (0, 0))],
        out_specs=pl.BlockSpec((tq, qw), lambda b, i: (b * nq + i, 0)),
        out_shape=jax.ShapeDtypeStruct((n_batch * seq, qw), _BF),
        scratch_shapes=[pltpu.VMEM((tq, seq), jnp.float32), pltpu.VMEM((tq, seq), jnp.int32),
                        pltpu.VMEM((tq, seq), jnp.float32),
                        pltpu.VMEM((seq, IDX_DIM), _BF), pltpu.VMEM((seq, HEAD_DIM), _BF),
                        pltpu.VMEM((seq, 2 * HEAD_DIM), _BF)],
        compiler_params=_params("parallel", "arbitrary"),
    )(proj, proj, proj, proj, proj, proj, _tri_matrix())


def _page_specs(rows):
    def spec(k):
        return pl.BlockSpec((1, rows, LANES), lambda b, s, pt: (pt[b, s * PAGES_PER_STEP + k], 0, 0))
    return [spec(k) for k in range(PAGES_PER_STEP)]


def _rows_view(x, lead):
    return x.reshape(x.shape[:lead] + (-1, x.shape[-1]))


def _kind_rows(ref, kind, n_kinds, n_rows):
    return ref.at[0][pl.ds(kind, n_rows, stride=n_kinds), :]


def _compress_sample_kernel(pt_ref, *refs):
    pages = refs[:PAGES_PER_STEP]
    w1_ref, w2_ref, pe_ref, cos_ref, sin_ref, o_ref, rows_ref = refs[PAGES_PER_STEP:]
    step = pl.program_id(1)
    n_kinds = 4 * N_KV_A
    n_slabs = rows_ref.shape[0]
    page = pages[0].shape[1] // n_kinds
    for k in range(PAGES_PER_STEP):
        start = pl.multiple_of((step * PAGES_PER_STEP + k) * page, page)
        for c in range(n_slabs):
            rows_ref.at[c][pl.ds(start, page), :] = _kind_rows(pages[k], c, n_kinds, page)

    @pl.when(step == pl.num_programs(1) - 1)
    def _():
        n_ch = o_ref.shape[-2]
        for kv in range(2):
            for g in range(N_KV_A):
                c = kv * N_KV_A + g
                y = _compress_rows(lambda s: rows_ref.at[c][pl.ds(s, n_ch, stride=CMP_STRIDE), :],
                                   pe_ref[kv], lambda r: w1_ref[kv, r], w2_ref[kv], n_ch)
                if kv == 0:
                    y = _rotary(y, cos_ref[...], sin_ref[...])
                o_ref[0, kv, g] = y.astype(o_ref.dtype)


def compress_sample(cache, page_table, w1, w2, pe, cos_end, sin_end):
    n_batch, n_pages = page_table.shape
    page = cache.shape[1]
    n_ch = n_pages * page // CMP_STRIDE
    pool = _rows_view(cache, 1)
    assert n_pages % PAGES_PER_STEP == 0
    const = lambda nd: (lambda b, s, pt: (0,) * nd)
    grid_spec = pltpu.PrefetchScalarGridSpec(
        num_scalar_prefetch=1,
        grid=(n_batch, n_pages // PAGES_PER_STEP),
        in_specs=_page_specs(pool.shape[1]) + [
            pl.BlockSpec(w1.shape, const(4)), pl.BlockSpec(w2.shape, const(3)), pl.BlockSpec(pe.shape, const(3)),
            pl.BlockSpec((n_ch, HEAD_DIM), const(2)), pl.BlockSpec((n_ch, HEAD_DIM), const(2))],
        out_specs=pl.BlockSpec((1, 2, N_KV_A, n_ch, HEAD_DIM), lambda b, s, pt: (b, 0, 0, 0, 0)),
        scratch_shapes=[pltpu.VMEM((2 * N_KV_A, n_pages * page, HEAD_DIM), jnp.float32)],
    )
    return pl.pallas_call(
        _compress_sample_kernel,
        grid_spec=grid_spec,
        out_shape=jax.ShapeDtypeStruct((n_batch, 2, N_KV_A, n_ch, HEAD_DIM), _BF),
        compiler_params=_params("parallel", "arbitrary"),
    )(page_table, *([pool] * PAGES_PER_STEP), w1, w2, pe, cos_end, sin_end)


def _nsa_sample_kernel(pt_ref, *refs, n_heads, n_s, past_len):
    pages = refs[:PAGES_PER_STEP]
    (q_ref, gate_ref, kvc_ref, nsk_ref, nsv_ref, nwk_ref, nwv_ref, win_ref, cover_ref, expand_ref,
     o_ref, m_ref, l_ref, acc_ref, ocmp_ref, sel_ref) = refs[PAGES_PER_STEP:]
    step = pl.program_id(1)
    ts = q_ref.shape[0]
    qw = n_heads * HEAD_DIM
    n_grp = N_KV_A
    n_kinds = 4 * n_grp
    page = pages[0].shape[1] // n_kinds
    t_col = past_len + lax.broadcasted_iota(jnp.int32, (ts, 1), 0)

    def queries(g):
        return _stack_heads(q_ref[:, g * qw:(g + 1) * qw], n_heads).astype(_BF)

    def group_lanes(x, g):
        return x[:, g * HEAD_DIM:(g + 1) * HEAD_DIM]

    @pl.when(step == 0)
    def _():
        for g in range(n_grp):
            o_cmp, sel = _cmp_attend_and_choose(queries(g), kvc_ref[0, 0, g], kvc_ref[0, 1, g], cover_ref[...],
                                                t_col, n_heads, n_s)
            ocmp_ref[g] = o_cmp
            sel_ref[g] = jnp.where(sel, 1.0, 0.0)
            m_ref[g], l_ref[g], acc_ref[g] = _flash_init(n_heads, ts)

    state = [(m_ref[g], l_ref[g], acc_ref[g]) for g in range(n_grp)]
    for g in range(n_grp):
        chosen = jnp.dot(sel_ref[g].astype(_BF), expand_ref[...], preferred_element_type=jnp.float32) > 0.5
        k = jnp.concatenate([_kind_rows(p, 2 * n_grp + g, n_kinds, page) for p in pages], axis=0).astype(_BF)
        v = jnp.concatenate([_kind_rows(p, 3 * n_grp + g, n_kinds, page) for p in pages], axis=0).astype(_BF)
        state[g] = _flash_update(queries(g), k, v, chosen, state[g], n_heads)
    for g in range(n_grp):
        m_ref[g], l_ref[g], acc_ref[g] = state[g]

    @pl.when(step == pl.num_programs(1) - 1)
    def _():
        gates = jax.nn.sigmoid(gate_ref[...])
        n_all = n_grp * n_heads
        w_len = win_ref.shape[1] // (2 * n_grp)
        row = lax.broadcasted_iota(jnp.int32, (ts, LANES), 0)
        lane = lax.broadcasted_iota(jnp.int32, (ts, LANES), 1)
        new_causal = (lane <= row) & (lane < ts)
        wlane = lax.broadcasted_iota(jnp.int32, (ts, w_len + LANES), 1)
        k_pos = past_len - w_len + wlane
        d = t_col - k_pos
        win_mask = (d >= 0) & (d < WINDOW) & (k_pos >= 0) & (wlane < w_len + ts)
        new_blk = past_len // SEL_BLOCK
        for g in range(n_grp):
            qs = queries(g)
            chosen = sel_ref[g][:, new_blk:new_blk + 1] > 0.5
            k_new = _pad_rows(group_lanes(nsk_ref[...], g), LANES).astype(_BF)
            v_new = _pad_rows(group_lanes(nsv_ref[...], g), LANES).astype(_BF)
            o_slc = _flash_finish(_flash_update(qs, k_new, v_new, new_causal & chosen,
                                                (m_ref[g], l_ref[g], acc_ref[g]), n_heads))
            kw = jnp.concatenate([_kind_rows(win_ref, g, 2 * n_grp, w_len),
                                  _pad_rows(group_lanes(nwk_ref[...], g), LANES)], axis=0).astype(_BF)
            vw = jnp.concatenate([_kind_rows(win_ref, n_grp + g, 2 * n_grp, w_len),
                                  _pad_rows(group_lanes(nwv_ref[...], g), LANES)], axis=0).astype(_BF)
            o_win = _flash_finish(_flash_update(qs, kw, vw, win_mask, _flash_init(n_heads, ts), n_heads))
            o_cmp = ocmp_ref[g]
            for r in range(n_heads):
                h = g * n_heads + r
                o = (gates[:, h:h + 1] * o_cmp[r] + gates[:, n_all + h:n_all + h + 1] * o_slc[r]
                     + gates[:, 2 * n_all + h:2 * n_all + h + 1] * o_win[r])
                o_ref[0, :, h * HEAD_DIM:(h + 1) * HEAD_DIM] = o.astype(o_ref.dtype)


def nsa_sample(proj, row0, kvc, cache, win_buf, page_table, lay, ts):
    n_batch, n_pages = page_table.shape
    page = cache.shape[1]
    past_len = n_pages * page
    pool = _rows_view(cache, 1)
    win_rows = _rows_view(win_buf, 1)
    n_heads = lay.n_heads_a // N_KV_A
    n_ch = kvc.shape[-2]
    n_s = -(-(past_len + ts) // SEL_BLOCK)
    sel_lanes = -(-n_s // LANES) * LANES
    kvw = N_KV_A * HEAD_DIM
    assert n_pages % PAGES_PER_STEP == 0 and row0 % ts == 0 and ts <= SEL_BLOCK and past_len % SEL_BLOCK == 0
    assert ts % 8 == 0 and lay.ga % LANES == 0
    assert (past_len + ts - CMP_BLOCK) // CMP_STRIDE + 1 <= n_ch and n_ch * CMP_STRIDE <= past_len
    r0 = row0 // ts
    keys_per_step = PAGES_PER_STEP * page
    qw = lay.n_heads_a * HEAD_DIM
    rows = lambda width, col: pl.BlockSpec((ts, width), lambda b, s, pt: (r0 + b, col // width))
    const = lambda nd: (lambda b, s, pt: (0,) * nd)
    grid_spec = pltpu.PrefetchScalarGridSpec(
        num_scalar_prefetch=1,
        grid=(n_batch, n_pages // PAGES_PER_STEP),
        in_specs=_page_specs(pool.shape[1]) + [
            rows(qw, lay.qa), rows(LANES, lay.ga),
            pl.BlockSpec((1, 2, N_KV_A, n_ch, HEAD_DIM), lambda b, s, pt: (b, 0, 0, 0, 0)),
            rows(kvw, lay.selk), rows(kvw, lay.selv), rows(kvw, lay.wink), rows(kvw, lay.winv),
            pl.BlockSpec((1,) + win_rows.shape[1:], lambda b, s, pt: (b, 0, 0)),
            pl.BlockSpec((n_ch, sel_lanes), const(2)),
            pl.BlockSpec((sel_lanes, keys_per_step), lambda b, s, pt: (0, s))],
        out_specs=pl.BlockSpec((1, ts, qw), lambda b, s, pt: (b, 0, 0)),
        scratch_shapes=[pltpu.VMEM((N_KV_A, n_heads, ts, 1), jnp.float32),
                        pltpu.VMEM((N_KV_A, n_heads, ts, 1), jnp.float32),
                        pltpu.VMEM((N_KV_A, n_heads, ts, HEAD_DIM), jnp.float32),
                        pltpu.VMEM((N_KV_A, n_heads, ts, HEAD_DIM), jnp.float32),
                        pltpu.VMEM((N_KV_A, ts, sel_lanes), jnp.float32)],
    )
    return pl.pallas_call(
        functools.partial(_nsa_sample_kernel, n_heads=n_heads, n_s=n_s, past_len=past_len),
        grid_spec=grid_spec,
        out_shape=jax.ShapeDtypeStruct((n_batch, ts, qw), _BF),
        compiler_params=_params("parallel", "arbitrary"),
    )(page_table, *([pool] * PAGES_PER_STEP), proj, proj, kvc, proj, proj, proj, proj, win_rows,
      _cover_matrix(n_ch, sel_lanes, n_s), _expand_matrix(sel_lanes, past_len))


def _dsa_sample_select_kernel(pt_ref, *refs, n_idx, n_top, iw_lane, past_len):
    pages = refs[:PAGES_PER_STEP]
    iq_ref, iw_ref, nik_ref, tri_ref, mask_ref, score_ref, key_ref = refs[PAGES_PER_STEP:]
    step = pl.program_id(1)
    ts = iq_ref.shape[0]
    keys_per_step = PAGES_PER_STEP * pages[0].shape[1]
    iqs = _stack_heads(iq_ref[...], n_idx).astype(_BF)
    iw = iw_ref[...]
    ik = jnp.concatenate([p[0] for p in pages], axis=0).astype(_BF)
    start = pl.multiple_of(step * keys_per_step, keys_per_step)
    score_ref[:, pl.ds(start, keys_per_step)] = _indexer_scores(iqs, ik, iw, iw_lane, n_idx)

    @pl.when(step == pl.num_programs(1) - 1)
    def _():
        sc = _indexer_scores(iqs, _pad_rows(nik_ref[...], LANES).astype(_BF), iw, iw_lane, n_idx)
        row = lax.broadcasted_iota(jnp.int32, (ts, LANES), 0)
        lane = lax.broadcasted_iota(jnp.int32, (ts, LANES), 1)
        score_ref[:, past_len:past_len + LANES] = jnp.where(lane < ts, jnp.where(lane <= row, sc, NEG), -jnp.inf)
        _topk_mask(score_ref, key_ref, mask_ref.at[0], tri_ref, past_len // LANES + 1, n_top)


def dsa_sample_select(proj, row0, cache_idx, page_table, lay, ts):
    n_batch, n_pages = page_table.shape
    page = cache_idx.shape[1]
    past_len = n_pages * page
    assert n_pages % PAGES_PER_STEP == 0 and row0 % ts == 0 and ts <= LANES and past_len % LANES == 0
    n_top = min(DSA_TOPK, (past_len + ts) // 4)
    r0 = row0 // ts
    iqw = lay.n_idx_heads * IDX_DIM
    width = past_len + LANES
    rows = lambda w, col: pl.BlockSpec((ts, w), lambda b, s, pt: (r0 + b, col // w))
    grid_spec = pltpu.PrefetchScalarGridSpec(
        num_scalar_prefetch=1,
        grid=(n_batch, n_pages // PAGES_PER_STEP),
        in_specs=_page_specs(page) + [
            rows(iqw, lay.iq), rows(LANES, lay.iw), rows(IDX_DIM, lay.ik),
            pl.BlockSpec((LANES, LANES), lambda b, s, pt: (0, 0))],
        out_specs=pl.BlockSpec((1, ts, width), lambda b, s, pt: (b, 0, 0)),
        scratch_shapes=[pltpu.VMEM((ts, width), jnp.float32), pltpu.VMEM((ts, width), jnp.int32)],
    )
    return pl.pallas_call(
        functools.partial(_dsa_sample_select_kernel, n_idx=lay.n_idx_heads, n_top=n_top,
                          iw_lane=lay.iw % LANES, past_len=past_len),
        grid_spec=grid_spec,
        out_shape=jax.ShapeDtypeStruct((n_batch, ts, width), jnp.float32),
        compiler_params=_params("parallel", "arbitrary"),
    )(page_table, *([cache_idx] * PAGES_PER_STEP), proj, proj, proj, _tri_matrix())


def _dsa_sample_attend_kernel(pt_ref, *refs, n_heads):
    pages = refs[:PAGES_PER_STEP]
    q_ref, mask_ref, nmask_ref, nk_ref, nv_ref, o_ref, m_ref, l_ref, acc_ref = refs[PAGES_PER_STEP:]
    step = pl.program_id(1)
    ts = q_ref.shape[0]
    qs = _stack_heads(q_ref[...], n_heads).astype(_BF)

    @pl.when(step == 0)
    def _():
        m_ref[...], l_ref[...], acc_ref[...] = _flash_init(n_heads, ts)

    page = pages[0].shape[1] // 2
    k = jnp.concatenate([_kind_rows(p, 0, 2, page) for p in pages], axis=0).astype(_BF)
    v = jnp.concatenate([_kind_rows(p, 1, 2, page) for p in pages], axis=0).astype(_BF)
    m_ref[...], l_ref[...], acc_ref[...] = _flash_update(qs, k, v, mask_ref[0] > 0.5,
                                                         (m_ref[...], l_ref[...], acc_ref[...]), n_heads)

    @pl.when(step == pl.num_programs(1) - 1)
    def _():
        row = lax.broadcasted_iota(jnp.int32, (ts, LANES), 0)
        lane = lax.broadcasted_iota(jnp.int32, (ts, LANES), 1)
        mask = (nmask_ref[0] > 0.5) & (lane <= row) & (lane < ts)
        o = _flash_finish(_flash_update(qs, _pad_rows(nk_ref[...], LANES).astype(_BF),
                                        _pad_rows(nv_ref[...], LANES).astype(_BF), mask,
                                        (m_ref[...], l_ref[...], acc_ref[...]), n_heads))
        for h in range(n_heads):
            o_ref[0, :, h * HEAD_DIM:(h + 1) * HEAD_DIM] = o[h].astype(o_ref.dtype)


def dsa_sample_attend(proj, row0, mask, cache_kv, page_table, lay, ts):
    n_batch, n_pages = page_table.shape
    page = cache_kv.shape[1]
    past_len = n_pages * page
    pool = _rows_view(cache_kv, 1)
    keys_per_step = PAGES_PER_STEP * page
    r0 = row0 // ts
    qw = lay.n_heads_b * HEAD_DIM
    rows = lambda w, col: pl.BlockSpec((ts, w), lambda b, s, pt: (r0 + b, col // w))
    grid_spec = pltpu.PrefetchScalarGridSpec(
        num_scalar_prefetch=1,
        grid=(n_batch, n_pages // PAGES_PER_STEP),
        in_specs=_page_specs(pool.shape[1]) + [
            rows(qw, lay.qb),
            pl.BlockSpec((1, ts, keys_per_step), lambda b, s, pt: (b, 0, s)),
            pl.BlockSpec((1, ts, LANES), lambda b, s, pt: (b, 0, past_len // LANES)),
            rows(HEAD_DIM, lay.dk), rows(HEAD_DIM, lay.dv)],
        out_specs=pl.BlockSpec((1, ts, qw), lambda b, s, pt: (b, 0, 0)),
        scratch_shapes=[pltpu.VMEM((lay.n_heads_b, ts, 1), jnp.float32),
                        pltpu.VMEM((lay.n_heads_b, ts, 1), jnp.float32),
                        pltpu.VMEM((lay.n_heads_b, ts, HEAD_DIM), jnp.float32)],
    )
    return pl.pallas_call(
        functools.partial(_dsa_sample_attend_kernel, n_heads=lay.n_heads_b),
        grid_spec=grid_spec,
        out_shape=jax.ShapeDtypeStruct((n_batch, ts, qw), _BF),
        compiler_params=_params("parallel", "arbitrary"),
    )(page_table, *([pool] * PAGES_PER_STEP), proj, mask, mask, proj, proj)


class Layout:
    def __init__(self, d_model, n_heads_a, n_heads_b, n_idx_heads):
        self.n_heads_a, self.n_heads_b, self.n_idx_heads = n_heads_a, n_heads_b, n_idx_heads
        src = np.cumsum([0, n_heads_a * HEAD_DIM, 6 * N_KV_A * HEAD_DIM, 3 * n_heads_a,
                         n_heads_b * HEAD_DIM, 2 * HEAD_DIM, n_idx_heads * IDX_DIM, n_idx_heads,
                         IDX_DIM, 2 * d_model])
        s_qa, s_kva, s_ga, s_qb, s_kvb, s_iq, s_iw, s_ik, s_mg, s_end = (int(v) for v in src)
        small = 3 * n_heads_a + n_idx_heads
        assert small <= LANES
        self.small_pad = LANES - small
        kvw = N_KV_A * HEAD_DIM
        self.pieces = [
            (s_qa, s_kva, True),
            (s_qb, s_kvb, True),
            (s_iq, s_iw, True),
            (s_kva, s_kva + 2 * kvw, False),
            (s_kva + 2 * kvw, s_kva + 3 * kvw, True),
            (s_kva + 3 * kvw, s_kva + 4 * kvw, False),
            (s_kva + 4 * kvw, s_kva + 5 * kvw, True),
            (s_kva + 5 * kvw, s_ga, False),
            (s_kvb, s_kvb + HEAD_DIM, True),
            (s_kvb + HEAD_DIM, s_iq, False),
            (s_ik, s_mg, True),
            (s_ga, s_qb, False),
            (s_iw, s_ik, False),
            None,
            (s_mg, s_end, False),
        ]
        off = 0
        starts = []
        for p in self.pieces:
            starts.append(off)
            off += self.small_pad if p is None else p[1] - p[0]
        (self.qa, self.qb, self.iq, self.cmp, self.selk, self.selv, self.wink, self.winv, self.dk,
         self.dv, self.ik, self.ga, self.iw, _, self.mg) = starts
        self.width = off
        assert self.width % LANES == 0
        flags = np.zeros(self.width // LANES, np.int32)
        for st, p in zip(starts, self.pieces):
            if p is not None and p[2]:
                assert st % LANES == 0 and (p[1] - p[0]) % LANES == 0
                flags[st // LANES:(st + p[1] - p[0]) // LANES] = 1
        self.rope_flags = flags

    def pack(self, w_in):
        cols = []
        for p in self.pieces:
            if p is None:
                cols.append(jnp.zeros((w_in.shape[0], self.small_pad), w_in.dtype))
            else:
                cols.append(w_in[:, p[0]:p[1]])
        return jnp.concatenate(cols, axis=1).astype(_BF)


def rope_tables(pos):
    half = HEAD_DIM // 2
    inv = ROPE_THETA ** (-jnp.arange(half, dtype=jnp.float32) / half)
    ang = pos.astype(jnp.float32)[:, None] * inv[None, :]
    cos, sin = jnp.cos(ang), jnp.sin(ang)
    return jnp.concatenate([cos, cos], axis=1), jnp.concatenate([-sin, sin], axis=1)


def _swiglu_half_step(h, g, wg, wu, wd, w_index):
    xn = rmsnorm(h, g, _BF)
    m = h.shape[0]
    a = ffn_gate_up(xn, wg, wu, w_index, tm=_row_tile(m, ROW_TILE_WIDE), tn=2 * LANES)
    return resid_matmul(a, wd, w_index, h, 0.5, tm=_row_tile(m, ROW_TILE_DOWN), tn=2 * LANES)


def kernel(x_prompt, x_sample, cache_nsa_kv, cache_nsa_win, cache_dsa_kv, cache_dsa_idx, page_table,
           g_norm, w_ffn_gate, w_ffn_up, w_ffn_down, w_in, w_cmp1, w_cmp2, cmp_pos,
           w_br_a, w_br_b, w_out, g_final):
    B, T, D = x_prompt.shape
    DB, Ts, _ = x_sample.shape
    depth = g_norm.shape[0]
    page = cache_nsa_kv.shape[2]
    past_len = page_table.shape[1] * page
    n_heads_a = w_br_a.shape[1] // HEAD_DIM
    n_heads_b = w_br_b.shape[1] // HEAD_DIM
    G = N_KV_A
    lay = Layout(D, n_heads_a, n_heads_b, n_heads_b // 2)
    Mp, Ms = B * T, DB * Ts
    tm_wide = _row_tile(Mp + Ms, ROW_TILE_WIDE)

    pos_p = jnp.arange(T, dtype=jnp.int32)
    pos_s = past_len + jnp.arange(Ts, dtype=jnp.int32)
    cos, sin = rope_tables(jnp.concatenate([jnp.tile(pos_p, B), jnp.tile(pos_s, DB)]))
    rope_flags = jnp.asarray(lay.rope_flags)

    def block_end_tables(n_ch):
        return rope_tables(jnp.arange(n_ch, dtype=jnp.int32) * CMP_STRIDE + CMP_BLOCK - 1)

    h = jnp.concatenate([x_prompt.reshape(Mp, D), x_sample.reshape(Ms, D)], axis=0)
    outs = [[] for _ in range(8)]
    for l in range(depth):
        h = _swiglu_half_step(h, g_norm[l, 0], w_ffn_gate, w_ffn_up, w_ffn_down, (l, 0))

        u = rmsnorm(h, g_norm[l, 1], _BF)
        proj, nsa_rows, win_rows, dsa_rows, ik_rows = in_project(u, lay.pack(w_in[l]), rope_flags, cos, sin, lay,
                                                                 tm=tm_wide, tn=4 * LANES)
        w1 = w_cmp1[l].reshape(2, 2, CMP_STRIDE * HEAD_DIM, -1).astype(_BF)
        w2 = w_cmp2[l].astype(_BF)

        kvc_p = compress_prompt(proj, B, T, lay.cmp, w1, w2, cmp_pos[l], *block_end_tables(T // CMP_STRIDE))
        o_a_p = nsa_prompt(proj, kvc_p, lay, B, T, tq=Q_TILE)
        o_b_p = dsa_prompt(proj, lay, B, T, tq=Q_TILE)

        kvc_s = compress_sample(cache_nsa_kv[l], page_table, w1, w2, cmp_pos[l],
                                *block_end_tables(past_len // CMP_STRIDE))
        o_a_s = nsa_sample(proj, Mp, kvc_s, cache_nsa_kv[l], cache_nsa_win[l], page_table, lay, Ts)
        top_mask = dsa_sample_select(proj, Mp, cache_dsa_idx[l], page_table, lay, Ts)
        o_b_s = dsa_sample_attend(proj, Mp, top_mask, cache_dsa_kv[l], page_table, lay, Ts)

        o_a = jnp.concatenate([o_a_p, o_a_s.reshape(Ms, -1)], axis=0)
        o_b = jnp.concatenate([o_b_p, o_b_s.reshape(Ms, -1)], axis=0)
        m = merge_branches(o_a, o_b, w_br_a, w_br_b, (l,), proj, lay.mg, tm=tm_wide, tn=4 * LANES)
        h = resid_matmul(m, w_out, (l,), h, 1.0, tm=tm_wide, tn=4 * LANES)

        win_p = win_rows[:Mp].reshape(B, T, 2, G, HEAD_DIM)
        win_s = win_rows[Mp:].reshape(DB, Ts, 2, G, HEAD_DIM)
        outs[0].append(nsa_rows[:Mp].reshape(B, T, 4, G, HEAD_DIM))
        outs[1].append(win_p[:, T - min(WINDOW, T):])
        outs[2].append(dsa_rows[:Mp].reshape(B, T, 2, HEAD_DIM))
        outs[3].append(ik_rows[:Mp].reshape(B, T, IDX_DIM))
        outs[4].append(nsa_rows[Mp:].reshape(DB, Ts, 4, G, HEAD_DIM))
        outs[5].append(jnp.concatenate([cache_nsa_win[l], win_s], axis=1)[:, Ts:])
        outs[6].append(dsa_rows[Mp:].reshape(DB, Ts, 2, HEAD_DIM))
        outs[7].append(ik_rows[Mp:].reshape(DB, Ts, IDX_DIM))

        h = _swiglu_half_step(h, g_norm[l, 2], w_ffn_gate, w_ffn_up, w_ffn_down, (l, 1))

    y_p = rmsnorm(h, g_final, jnp.float32, 0, Mp)
    y_s = rmsnorm(h, g_final, jnp.float32, Mp, Ms)
    return (y_p.reshape(B, T, D), y_s.reshape(DB, Ts, D), *(jnp.stack(o) for o in outs))
```

```python
import functools

import jax
import jax.numpy as jnp
import numpy as np
from jax import lax
from jax.experimental import pallas as pl
from jax.experimental.pallas import tpu as pltpu

HEAD_DIM = 128
N_KV_A = 2
IDX_DIM = 128
CMP_BLOCK = 32
CMP_STRIDE = 16
SEL_BLOCK = 64
N_SEL = 16
N_LOCAL_SEL = 2
WINDOW = 512
DSA_TOPK = 256
ROPE_THETA = 10000.0
RMS_EPS = 1e-6
NEG = -1e30
BIG = 1e30
TINY = 1e-30

LANES = 128
VMEM_LIMIT = 56 * 1024 * 1024
PAGES_PER_STEP = 32

_NT = (((1,), (1,)), ((), ()))
_BF = jnp.bfloat16


ROW_TILE_WIDE = 1408
ROW_TILE_DOWN = 1056
Q_TILE = 2 * LANES


def _params(*sem):
    return pltpu.CompilerParams(dimension_semantics=sem, vmem_limit_bytes=VMEM_LIMIT)


def _row_tile(m, limit):
    return max(t for t in range(16, limit + 1, 16) if m % t == 0)


def _rmsnorm_kernel(x_ref, g_ref, o_ref):
    x = x_ref[...]
    y = x * lax.rsqrt(jnp.mean(x * x, axis=-1, keepdims=True) + RMS_EPS)
    o_ref[...] = (y * g_ref[...]).astype(o_ref.dtype)


def rmsnorm(x, g, out_dtype, row0=0, n_rows=None, tm=256):
    D = x.shape[1]
    n_rows = x.shape[0] - row0 if n_rows is None else n_rows
    assert row0 % tm == 0 and n_rows % tm == 0
    return pl.pallas_call(
        _rmsnorm_kernel,
        grid=(n_rows // tm,),
        in_specs=[pl.BlockSpec((tm, D), lambda i: (row0 // tm + i, 0)),
                  pl.BlockSpec((1, D), lambda i: (0, 0))],
        out_specs=pl.BlockSpec((tm, D), lambda i: (i, 0)),
        out_shape=jax.ShapeDtypeStruct((n_rows, D), out_dtype),
        compiler_params=_params("parallel"),
    )(x, g.reshape(1, D))


def _gateup_kernel(x_ref, wg_ref, wu_ref, o_ref):
    x = x_ref[...]
    g = jnp.dot(x, wg_ref[...].astype(_BF), preferred_element_type=jnp.float32)
    u = jnp.dot(x, wu_ref[...].astype(_BF), preferred_element_type=jnp.float32)
    o_ref[...] = (g * jax.nn.sigmoid(g) * u).astype(o_ref.dtype)


def _weight_spec(w, w_index, rows, tn):
    assert w.ndim == len(w_index) + 2 and w.shape[-2] == rows
    return pl.BlockSpec((None,) * len(w_index) + (rows, tn), lambda i, j: tuple(w_index) + (0, j))


def ffn_gate_up(xn, wg, wu, w_index, tm, tn):
    M, D = xn.shape
    F = wg.shape[-1]
    return pl.pallas_call(
        _gateup_kernel,
        grid=(M // tm, F // tn),
        in_specs=[pl.BlockSpec((tm, D), lambda i, j: (i, 0)),
                  _weight_spec(wg, w_index, D, tn),
                  _weight_spec(wu, w_index, D, tn)],
        out_specs=pl.BlockSpec((tm, tn), lambda i, j: (i, j)),
        out_shape=jax.ShapeDtypeStruct((M, F), _BF),
        compiler_params=_params("parallel", "parallel"),
    )(xn, wg, wu)


def _resid_matmul_kernel(a_ref, w_ref, r_ref, o_ref, *, scale):
    acc = jnp.dot(a_ref[...], w_ref[...].astype(_BF), preferred_element_type=jnp.float32)
    o_ref[...] = r_ref[...] + scale * acc


def resid_matmul(a, w, w_index, resid, scale, tm, tn):
    M, K = a.shape
    N = w.shape[-1]
    return pl.pallas_call(
        functools.partial(_resid_matmul_kernel, scale=scale),
        grid=(M // tm, N // tn),
        in_specs=[pl.BlockSpec((tm, K), lambda i, j: (i, 0), pipeline_mode=pl.Buffered(1)),
                  _weight_spec(w, w_index, K, tn),
                  pl.BlockSpec((tm, tn), lambda i, j: (i, j))],
        out_specs=pl.BlockSpec((tm, tn), lambda i, j: (i, j)),
        out_shape=jax.ShapeDtypeStruct((M, N), jnp.float32),
        compiler_params=_params("parallel", "parallel"),
    )(a, w, resid)


def _rotary(y, cos, sin):
    return y * cos + pltpu.roll(y, HEAD_DIM // 2, axis=1) * sin


def _inproj_kernel(flags_ref, x_ref, w_ref, cos_ref, sin_ref, o_ref, *row_refs, n_chunks, routes):
    j = pl.program_id(1)
    acc = jnp.dot(x_ref[...], w_ref[...], preferred_element_type=jnp.float32)
    for c in range(n_chunks):
        sl = slice(c * LANES, (c + 1) * LANES)
        y = acc[:, sl]
        flag = flags_ref[j * n_chunks + c]

        @pl.when(flag == 1)
        def _():
            o_ref[:, sl] = _rotary(y, cos_ref[...], sin_ref[...])

        @pl.when(flag == 0)
        def _():
            o_ref[:, sl] = y

    tm = o_ref.shape[0]
    for chunk, out_idx, kind, n_kinds in routes:
        @pl.when(j == chunk // n_chunks)
        def _():
            c = chunk % n_chunks
            row_refs[out_idx][pl.ds(kind, tm, stride=n_kinds), :] = o_ref[:, c * LANES:(c + 1) * LANES]


def in_project(u, w, rope_flags, cos, sin, lay, tm, tn):
    M, D = u.shape
    N = w.shape[1]
    n_chunks = tn // LANES
    G = N_KV_A
    kinds = [4 * G, 2 * G, 2, 1]
    routes = ([(lay.cmp // LANES + k, 0, k, kinds[0]) for k in range(kinds[0])]
              + [(lay.wink // LANES + k, 1, k, kinds[1]) for k in range(kinds[1])]
              + [(lay.dk // LANES + k, 2, k, kinds[2]) for k in range(kinds[2])]
              + [(lay.ik // LANES, 3, 0, 1)])
    row_shapes = [(M, 4, G, HEAD_DIM), (M, 2, G, HEAD_DIM), (M, 2, HEAD_DIM), (M, IDX_DIM)]
    grid_spec = pltpu.PrefetchScalarGridSpec(
        num_scalar_prefetch=1,
        grid=(M // tm, N // tn),
        in_specs=[pl.BlockSpec((tm, D), lambda i, j, f: (i, 0), pipeline_mode=pl.Buffered(1)),
                  pl.BlockSpec((D, tn), lambda i, j, f: (0, j)),
                  pl.BlockSpec((tm, LANES), lambda i, j, f: (i, 0)),
                  pl.BlockSpec((tm, LANES), lambda i, j, f: (i, 0))],
        out_specs=[pl.BlockSpec((tm, tn), lambda i, j, f: (i, j))]
        + [pl.BlockSpec((tm * nk, LANES), lambda i, j, f: (i, 0)) for nk in kinds],
    )
    proj, *rows = pl.pallas_call(
        functools.partial(_inproj_kernel, n_chunks=n_chunks, routes=routes),
        grid_spec=grid_spec,
        out_shape=[jax.ShapeDtypeStruct((M, N), jnp.float32)]
        + [jax.ShapeDtypeStruct((M * nk, LANES), jnp.float32) for nk in kinds],
        compiler_params=_params("parallel", "arbitrary"),
    )(rope_flags, u, w, cos, sin)
    return (proj, *(r.reshape(shp) for r, shp in zip(rows, row_shapes)))


def _merge_kernel(oa_ref, ob_ref, wa_ref, wb_ref, ga_ref, gb_ref, o_ref):
    ya = jnp.dot(oa_ref[...], wa_ref[...].astype(_BF), preferred_element_type=jnp.float32)
    yb = jnp.dot(ob_ref[...], wb_ref[...].astype(_BF), preferred_element_type=jnp.float32)
    m = jax.nn.sigmoid(ga_ref[...]) * ya + jax.nn.sigmoid(gb_ref[...]) * yb
    o_ref[...] = m.astype(o_ref.dtype)


def merge_branches(o_a, o_b, w_a, w_b, w_index, proj, mg_col, tm, tn):
    M, K = o_a.shape
    N = w_a.shape[-1]
    ja = mg_col // tn
    jb = (mg_col + N) // tn
    return pl.pallas_call(
        _merge_kernel,
        grid=(M // tm, N // tn),
        in_specs=[pl.BlockSpec((tm, K), lambda i, j: (i, 0), pipeline_mode=pl.Buffered(1)),
                  pl.BlockSpec((tm, K), lambda i, j: (i, 0), pipeline_mode=pl.Buffered(1)),
                  _weight_spec(w_a, w_index, K, tn),
                  _weight_spec(w_b, w_index, K, tn),
                  pl.BlockSpec((tm, tn), lambda i, j: (i, ja + j)),
                  pl.BlockSpec((tm, tn), lambda i, j: (i, jb + j))],
        out_specs=pl.BlockSpec((tm, tn), lambda i, j: (i, j)),
        out_shape=jax.ShapeDtypeStruct((M, N), _BF),
        compiler_params=_params("parallel", "parallel"),
    )(o_a, o_b, w_a, w_b, proj, proj)


def _stack_heads(x, n_heads):
    return jnp.concatenate([x[:, h * HEAD_DIM:(h + 1) * HEAD_DIM] for h in range(n_heads)], axis=0)


def _flash_update(qs, k, v, mask, carry, n_rep):
    m, l, acc = carry
    rows, tk = qs.shape[0], k.shape[0]
    tq = rows // n_rep
    scale = HEAD_DIM ** -0.5
    s = lax.dot_general(qs, k, _NT, preferred_element_type=jnp.float32).reshape(n_rep, tq, tk)
    mask = mask[None]
    s = jnp.where(mask, s, NEG)
    m_new = jnp.maximum(m, jnp.max(s, axis=-1, keepdims=True))
    p = jnp.where(mask, jnp.exp((s - m_new) * scale), 0.0)
    alpha = jnp.exp((m - m_new) * scale)
    l = alpha * l + jnp.sum(p, axis=-1, keepdims=True)
    pv = jnp.dot(p.reshape(rows, tk).astype(_BF), v, preferred_element_type=jnp.float32)
    return m_new, l, alpha * acc + pv.reshape(n_rep, tq, HEAD_DIM)


def _flash_init(n_rep, tq):
    return (jnp.full((n_rep, tq, 1), NEG, jnp.float32), jnp.zeros((n_rep, tq, 1), jnp.float32),
            jnp.zeros((n_rep, tq, HEAD_DIM), jnp.float32))


def _flash_finish(carry):
    _, l, acc = carry
    return acc * (1.0 / jnp.maximum(l, TINY))


def _attend(qs, k_ref, v_ref, start, n_chunks, chunk, mask_fn, n_rep):
    rows = qs.shape[0]
    tq = rows // n_rep
    n_lane_tiles = chunk // LANES
    scale = HEAD_DIM ** -0.5

    def scores(c):
        sl = pl.ds(pl.multiple_of(start + c * chunk, LANES), chunk)
        return lax.dot_general(qs, k_ref[sl, :], _NT, preferred_element_type=jnp.float32), sl

    def lane_tile(s, j):
        return s[:, j * LANES:(j + 1) * LANES].reshape(n_rep, tq, LANES)

    def row_max(c, mx):
        s, _ = scores(c)
        for j in range(n_lane_tiles):
            mx = jnp.maximum(mx, jnp.where(mask_fn(c, j)[None], lane_tile(s, j), NEG))
        return mx

    mx = lax.fori_loop(0, n_chunks, row_max, jnp.full((n_rep, tq, LANES), NEG, jnp.float32))
    m = jnp.broadcast_to(jnp.max(mx, axis=-1, keepdims=True), mx.shape)

    def accumulate(c, acc):
        s, sl = scores(c)
        p = [jnp.where(mask_fn(c, j)[None], jnp.exp((lane_tile(s, j) - m) * scale), 0.0)
             .astype(_BF).reshape(rows, LANES) for j in range(n_lane_tiles)]
        return acc + jnp.dot(jnp.concatenate(p, axis=1), v_ref[sl, :], preferred_element_type=jnp.float32)

    acc = lax.fori_loop(0, n_chunks, accumulate, jnp.zeros((rows, 2 * HEAD_DIM), jnp.float32))
    out = acc[:, :HEAD_DIM] * (1.0 / jnp.maximum(acc[:, HEAD_DIM:], TINY))
    return out.reshape(n_rep, tq, HEAD_DIM)


def _compress_rows(load, pe, w1, w2, n_ch):
    rows = [load(s) for s in range(CMP_STRIDE)]
    half = [jnp.concatenate([(rows[s] + pe[r * CMP_STRIDE + s:r * CMP_STRIDE + s + 1, :]).astype(_BF)
                             for s in range(CMP_STRIDE)], axis=1) for r in range(2)]
    h0 = jnp.dot(half[0], w1(0), preferred_element_type=jnp.float32)
    h1 = jnp.dot(half[1], w1(1), preferred_element_type=jnp.float32)
    h = h0 + pltpu.roll(h1, n_ch - 1, axis=0)
    return jnp.dot((h * jax.nn.sigmoid(h)).astype(_BF), w2, preferred_element_type=jnp.float32)


def _cmp_attend_and_choose(qs, kc, vc, cover, t_col, n_heads, n_s):
    rows = qs.shape[0]
    tq = rows // n_heads
    n_ch, lanes = cover.shape
    scale = HEAD_DIM ** -0.5
    s = lax.dot_general(qs, kc, _NT, preferred_element_type=jnp.float32).reshape(n_heads, tq, n_ch)
    end = lax.broadcasted_iota(jnp.int32, (tq, n_ch), 1) * CMP_STRIDE + (CMP_BLOCK - 1)
    cmask = (end <= t_col)[None]
    s = jnp.where(cmask, s, NEG)
    m = jnp.max(s, axis=-1, keepdims=True)
    p = jnp.where(cmask, jnp.exp((s - m) * scale), 0.0)
    p = p * (1.0 / jnp.maximum(jnp.sum(p, axis=-1, keepdims=True), TINY))
    o_cmp = jnp.dot(p.reshape(rows, n_ch).astype(_BF), vc, preferred_element_type=jnp.float32)
    imp = jnp.dot(jnp.sum(p, axis=0).astype(_BF), cover, preferred_element_type=jnp.float32)

    lane = lax.broadcasted_iota(jnp.int32, (tq, lanes), 1)
    lane_f = lane.astype(jnp.float32)
    jt = lax.shift_right_arithmetic(t_col, jnp.int32(SEL_BLOCK.bit_length() - 1))
    adm = lane <= jt
    forced = adm & ((lane == 0) | (lane > jt - N_LOCAL_SEL))
    work = jnp.where(forced, BIG, jnp.where(adm, imp, NEG))
    work = jnp.where(lane < n_s, work, -jnp.inf)
    sel = jnp.zeros((tq, lanes), jnp.bool_)
    for _ in range(min(N_SEL, n_s)):
        mx = jnp.max(work, axis=-1, keepdims=True)
        first = jnp.min(jnp.where(work == mx, lane_f, float(lanes)), axis=-1, keepdims=True)
        pick = lane_f == first
        sel = sel | pick
        work = jnp.where(pick, -jnp.inf, work)
    return o_cmp.reshape(n_heads, tq, HEAD_DIM), sel


def _topk_mask(score_ref, key_ref, mask_ref, tri_ref, n_tiles, n_top):
    tq = score_ref.shape[0]
    int_min = jnp.int32(-2 ** 31)

    def to_key(t, _):
        sl = pl.ds(pl.multiple_of(t * LANES, LANES), LANES)
        bits = lax.bitcast_convert_type(score_ref[:, sl], jnp.int32)
        key_ref[:, sl] = bits ^ ((bits >> 31) & jnp.int32(0x7FFFFFFF))
        return 0

    lax.fori_loop(0, n_tiles, to_key, 0)

    unroll = 8 if isinstance(n_tiles, int) else 1

    def count_ge(cand):
        def body(t, acc):
            sl = pl.ds(pl.multiple_of(t * LANES, LANES), LANES)
            return acc + jnp.where(key_ref[:, sl] >= cand, 1.0, 0.0)
        acc = lax.fori_loop(0, n_tiles, body, jnp.zeros((tq, LANES), jnp.float32), unroll=unroll)
        return jnp.sum(acc, axis=-1, keepdims=True)

    tau = jnp.zeros((tq, 1), jnp.int32)
    for bit in range(31, -1, -1):
        cand = tau | jnp.int32(-2 ** 31 if bit == 31 else 1 << bit)
        tau = jnp.where(count_ge(cand ^ int_min) >= float(n_top), cand, tau)
    thr = tau ^ int_min

    clean = jnp.max(jnp.abs(count_ge(thr) - float(n_top))) == 0.0

    @pl.when(clean)
    def _():
        def keep_ge(t, _):
            sl = pl.ds(pl.multiple_of(t * LANES, LANES), LANES)
            mask_ref[:, sl] = jnp.where(key_ref[:, sl] >= thr, 1.0, 0.0)
            return 0
        lax.fori_loop(0, n_tiles, keep_ge, 0, unroll=unroll)

    @pl.when(jnp.logical_not(clean))
    def _():
        ones = jnp.ones((LANES, LANES), _BF)

        def count_gt(t, acc):
            sl = pl.ds(pl.multiple_of(t * LANES, LANES), LANES)
            return acc + jnp.where(key_ref[:, sl] > thr, 1.0, 0.0)

        n_gt = jnp.sum(lax.fori_loop(0, n_tiles, count_gt, jnp.zeros((tq, LANES), jnp.float32)),
                       axis=-1, keepdims=True)
        need = float(n_top) - n_gt

        def cut(t, eq_before):
            sl = pl.ds(pl.multiple_of(t * LANES, LANES), LANES)
            key = key_ref[:, sl]
            eq = jnp.where(key == thr, 1.0, 0.0)
            rank = (jnp.dot(eq.astype(_BF), tri_ref[...], preferred_element_type=jnp.float32)
                    + jnp.dot(eq_before.astype(_BF), ones, preferred_element_type=jnp.float32))
            keep = (key > thr) | ((key == thr) & (rank <= need))
            mask_ref[:, sl] = jnp.where(keep, 1.0, 0.0)
            return eq_before + eq

        lax.fori_loop(0, n_tiles, cut, jnp.zeros((tq, LANES), jnp.float32))


def _indexer_scores(iqs, ik, iw, iw_lane, n_idx):
    tq = iqs.shape[0] // n_idx
    n = ik.shape[0]
    logits = lax.dot_general(iqs, ik, _NT, preferred_element_type=jnp.float32).reshape(n_idx, tq, n)
    logits = jnp.maximum(logits * IDX_DIM ** -0.5, 0.0)
    sc = jnp.zeros((tq, n), jnp.float32)
    for h in range(n_idx):
        sc = sc + logits[h] * iw[:, iw_lane + h:iw_lane + h + 1]
    return sc * n_idx ** -0.5


def _cover_matrix(n_ch, lanes, n_s):
    ci = np.arange(n_ch)[:, None] * CMP_STRIDE
    sj = np.arange(lanes)[None, :] * SEL_BLOCK
    return jnp.asarray((ci < sj + SEL_BLOCK) & (ci + CMP_BLOCK > sj) & (sj < n_s * SEL_BLOCK), _BF)


def _expand_matrix(lanes, n_keys):
    return jnp.asarray(np.arange(lanes)[:, None] == np.arange(n_keys)[None, :] // SEL_BLOCK, _BF)


def _tri_matrix():
    return jnp.asarray(np.arange(LANES)[:, None] <= np.arange(LANES)[None, :], _BF)


def _pad_rows(x, n):
    return jnp.concatenate([x, jnp.zeros((n - x.shape[0], x.shape[1]), x.dtype)], axis=0)


def _compress_kernel(x_ref, w1_ref, w2_ref, pe_ref, cos_ref, sin_ref, o_ref):
    n_ch = o_ref.shape[-2]
    y = _compress_rows(lambda s: x_ref[pl.ds(s, n_ch, stride=CMP_STRIDE), :], pe_ref[0],
                       lambda r: w1_ref[0, r], w2_ref[0], n_ch)

    @pl.when(pl.program_id(1) == 0)
    def _():
        o_ref[0, 0, 0] = _rotary(y, cos_ref[...], sin_ref[...]).astype(o_ref.dtype)

    @pl.when(pl.program_id(1) != 0)
    def _():
        o_ref[0, 0, 0] = y.astype(o_ref.dtype)


def compress_prompt(proj, n_batch, seq, cmp_col, w1, w2, pe, cos_end, sin_end):
    n_ch = seq // CMP_STRIDE
    col0 = cmp_col // HEAD_DIM
    return pl.pallas_call(
        _compress_kernel,
        grid=(n_batch, 2, N_KV_A),
        in_specs=[pl.BlockSpec((seq, HEAD_DIM), lambda b, kv, g: (b, col0 + kv * N_KV_A + g)),
                  pl.BlockSpec((1,) + w1.shape[1:], lambda b, kv, g: (kv, 0, 0, 0)),
                  pl.BlockSpec((1, w2.shape[1], HEAD_DIM), lambda b, kv, g: (kv, 0, 0)),
                  pl.BlockSpec((1, CMP_BLOCK, HEAD_DIM), lambda b, kv, g: (kv, 0, 0)),
                  pl.BlockSpec((n_ch, HEAD_DIM), lambda b, kv, g: (0, 0)),
                  pl.BlockSpec((n_ch, HEAD_DIM), lambda b, kv, g: (0, 0))],
        out_specs=pl.BlockSpec((1, 1, 1, n_ch, HEAD_DIM), lambda b, kv, g: (b, kv, g, 0, 0)),
        out_shape=jax.ShapeDtypeStruct((n_batch, 2, N_KV_A, n_ch, HEAD_DIM), _BF),
        compiler_params=_params("parallel", "parallel", "parallel"),
    )(proj, w1, w2, pe, cos_end, sin_end)


def _nsa_prompt_kernel(q_ref, kc_ref, vc_ref, selk_ref, selv_ref, wink_ref, winv_ref, gate_ref,
                       cover_ref, expand_ref, o_ref, selexp_ref, sk_ref, sv_ref, wk_ref, wv_ref,
                       *, n_heads, n_s, chunk):
    tq = q_ref.shape[0]
    seq = selexp_ref.shape[-1]
    n_grp = N_KV_A
    qi = pl.program_id(1)
    row0 = pl.multiple_of(qi * tq, tq)
    t_col = qi * tq + lax.broadcasted_iota(jnp.int32, (tq, 1), 0)
    qw = n_heads * HEAD_DIM

    @pl.when(qi == 0)
    def _():
        for ref in (sk_ref, sv_ref, wk_ref, wv_ref):
            ref[...] = jnp.zeros(ref.shape, ref.dtype)

    ones = jnp.ones((tq, HEAD_DIM), _BF)
    causal = lax.broadcasted_iota(jnp.int32, (tq, seq), 1) <= t_col
    qs, o_cmp = [], []
    for g in range(n_grp):
        lanes = slice(g * HEAD_DIM, (g + 1) * HEAD_DIM)
        sk_ref[g, pl.ds(row0, tq), :] = selk_ref[:, lanes].astype(_BF)
        sv_ref[g, pl.ds(row0, tq), :] = jnp.concatenate([selv_ref[:, lanes].astype(_BF), ones], axis=1)
        wk_ref[g, pl.ds(row0, tq), :] = wink_ref[:, lanes].astype(_BF)
        wv_ref[g, pl.ds(row0, tq), :] = jnp.concatenate([winv_ref[:, lanes].astype(_BF), ones], axis=1)
        qs.append(_stack_heads(q_ref[:, g * qw:(g + 1) * qw], n_heads).astype(_BF))
    sels = []
    for g in range(n_grp):
        o, sel = _cmp_attend_and_choose(qs[g], kc_ref[0, 0, g], vc_ref[0, 0, g], cover_ref[...], t_col, n_heads, n_s)
        o_cmp.append(o)
        sels.append(sel)
    for g in range(n_grp):
        chosen = jnp.dot(jnp.where(sels[g], 1.0, 0.0).astype(_BF), expand_ref[...],
                         preferred_element_type=jnp.float32)
        selexp_ref[g] = jnp.where(causal, chosen, 0.0)

    span = WINDOW + tq
    start = pl.multiple_of(jnp.maximum(row0 - WINDOW, 0), tq)
    d = t_col - (start + lax.broadcasted_iota(jnp.int32, (tq, span), 1))
    visible = (d >= 0) & (d < WINDOW)
    gates = jax.nn.sigmoid(gate_ref[...])
    n_all = n_grp * n_heads
    for g in range(n_grp):
        def sel_mask(c, j):
            return selexp_ref[g, :, pl.ds(pl.multiple_of(c * chunk + j * LANES, LANES), LANES)] > 0.5

        o_slc = _attend(qs[g], sk_ref.at[g], sv_ref.at[g], 0, (row0 + tq + chunk - 1) // chunk, chunk, sel_mask,
                        n_heads)
        o_win = _attend(qs[g], wk_ref.at[g], wv_ref.at[g], start, 1, span,
                        lambda c, j: visible[:, j * LANES:(j + 1) * LANES], n_heads)
        for r in range(n_heads):
            h = g * n_heads + r
            o = (gates[:, h:h + 1] * o_cmp[g][r] + gates[:, n_all + h:n_all + h + 1] * o_slc[r]
                 + gates[:, 2 * n_all + h:2 * n_all + h + 1] * o_win[r])
            o_ref[:, h * HEAD_DIM:(h + 1) * HEAD_DIM] = o.astype(o_ref.dtype)


def nsa_prompt(proj, kvc, lay, n_batch, seq, tq, chunk=512):
    n_heads = lay.n_heads_a // N_KV_A
    n_ch = kvc.shape[-2]
    n_s = -(-seq // SEL_BLOCK)
    assert tq % LANES == 0 and n_s <= LANES and seq % chunk == 0 and chunk % tq == 0
    assert WINDOW % tq == 0 and seq >= WINDOW + tq and lay.ga % LANES == 0
    nq = seq // tq
    qw = lay.n_heads_a * HEAD_DIM
    kvw = N_KV_A * HEAD_DIM

    def tile(col):
        return pl.BlockSpec((tq, kvw), lambda b, i: (b * nq + i, col // kvw))

    return pl.pallas_call(
        functools.partial(_nsa_prompt_kernel, n_heads=n_heads, n_s=n_s, chunk=chunk),
        grid=(n_batch, nq),
        in_specs=[pl.BlockSpec((tq, qw), lambda b, i: (b * nq + i, lay.qa // qw)),
                  pl.BlockSpec((1, 1, N_KV_A, n_ch, HEAD_DIM), lambda b, i: (b, 0, 0, 0, 0)),
                  pl.BlockSpec((1, 1, N_KV_A, n_ch, HEAD_DIM), lambda b, i: (b, 1, 0, 0, 0)),
                  tile(lay.selk), tile(lay.selv), tile(lay.wink), tile(lay.winv),
                  pl.BlockSpec((tq, LANES), lambda b, i: (b * nq + i, lay.ga // LANES)),
                  pl.BlockSpec((n_ch, LANES), lambda b, i: (0, 0)),
                  pl.BlockSpec((LANES, seq), lambda b, i: (0, 0))],
        out_specs=pl.BlockSpec((tq, qw), lambda b, i: (b * nq + i, 0)),
        out_shape=jax.ShapeDtypeStruct((n_batch * seq, qw), _BF),
        scratch_shapes=[pltpu.VMEM((N_KV_A, tq, seq), jnp.float32),
                        pltpu.VMEM((N_KV_A, seq, HEAD_DIM), _BF), pltpu.VMEM((N_KV_A, seq, 2 * HEAD_DIM), _BF),
                        pltpu.VMEM((N_KV_A, seq, HEAD_DIM), _BF), pltpu.VMEM((N_KV_A, seq, 2 * HEAD_DIM), _BF)],
        compiler_params=_params("parallel", "arbitrary"),
    )(proj, kvc, kvc, proj, proj, proj, proj, proj, _cover_matrix(n_ch, LANES, n_s), _expand_matrix(LANES, seq))


def _dsa_prompt_kernel(iq_ref, iw_ref, q_ref, ik_ref, k_ref, v_ref, tri_ref, o_ref,
                       score_ref, key_ref, mask_ref, ikb_ref, kb_ref, vb_ref,
                       *, n_idx, n_heads, n_top, iw_lane, chunk):
    tq = iq_ref.shape[0]
    seq = mask_ref.shape[1]
    qi = pl.program_id(1)
    n_tiles = qi + 1
    row0 = pl.multiple_of(qi * tq, tq)
    t_col = qi * tq + lax.broadcasted_iota(jnp.int32, (tq, 1), 0)
    key_iota = lax.broadcasted_iota(jnp.int32, (tq, tq), 1)

    @pl.when(qi == 0)
    def _():
        for ref in (ikb_ref, kb_ref, vb_ref):
            ref[...] = jnp.zeros(ref.shape, ref.dtype)

    ikb_ref[pl.ds(row0, tq), :] = ik_ref[...].astype(_BF)
    kb_ref[pl.ds(row0, tq), :] = k_ref[...].astype(_BF)
    vb_ref[pl.ds(row0, tq), :] = jnp.concatenate([v_ref[...].astype(_BF), jnp.ones((tq, HEAD_DIM), _BF)], axis=1)
    iqs = _stack_heads(iq_ref[...], n_idx).astype(_BF)
    iw = iw_ref[...]

    def score_tile(kt, _):
        sl = pl.ds(pl.multiple_of(kt * tq, tq), tq)
        sc = _indexer_scores(iqs, ikb_ref[sl, :], iw, iw_lane, n_idx)
        score_ref[:, sl] = jnp.where(kt * tq + key_iota <= t_col, sc, NEG)
        return 0

    lax.fori_loop(0, n_tiles, score_tile, 0)
    mask_ref[...] = jnp.zeros(mask_ref.shape, mask_ref.dtype)
    _topk_mask(score_ref, key_ref, mask_ref, tri_ref, n_tiles * (tq // LANES), n_top)
    causal = lax.broadcasted_iota(jnp.int32, (tq, seq), 1) <= t_col
    mask_ref[...] = jnp.where(causal, mask_ref[...], 0.0)

    def dsa_mask(c, j):
        return mask_ref[:, pl.ds(pl.multiple_of(c * chunk + j * LANES, LANES), LANES)] > 0.5

    qs = _stack_heads(q_ref[...], n_heads).astype(_BF)
    o = _attend(qs, kb_ref, vb_ref, 0, (row0 + tq + chunk - 1) // chunk, chunk, dsa_mask, n_heads)
    for h in range(n_heads):
        o_ref[:, h * HEAD_DIM:(h + 1) * HEAD_DIM] = o[h].astype(o_ref.dtype)


def dsa_prompt(proj, lay, n_batch, seq, tq, chunk=512):
    assert tq % LANES == 0 and seq % chunk == 0 and chunk % tq == 0
    nq = seq // tq
    n_top = min(DSA_TOPK, seq // 4)
    iqw = lay.n_idx_heads * IDX_DIM
    qw = lay.n_heads_b * HEAD_DIM

    def tile(col):
        c0 = col // HEAD_DIM
        return pl.BlockSpec((tq, HEAD_DIM), lambda b, i: (b * nq + i, c0))

    return pl.pallas_call(
        functools.partial(_dsa_prompt_kernel, n_idx=lay.n_idx_heads, n_heads=lay.n_heads_b, n_top=n_top,
                          iw_lane=lay.iw % LANES, chunk=chunk),
        grid=(n_batch, nq),
        in_specs=[pl.BlockSpec((tq, iqw), lambda b, i: (b * nq + i, lay.iq // iqw)),
                  pl.BlockSpec((tq, LANES), lambda b, i: (b * nq + i, lay.iw // LANES)),
                  pl.BlockSpec((tq, qw), lambda b, i: (b * nq + i, lay.qb // qw)),
                  tile(lay.ik), tile(lay.dk), tile(lay.dv),
                  pl.BlockSpec((LANES, LANES), lambda b, i: (0, 0))],
        out_specs=pl.BlockSpec((tq, qw), lambda b, i: (b * nq + i, 0)),
        out_shape=jax.ShapeDtypeStruct((n_batch * seq, qw), _BF),
        scratch_shapes=[pltpu.VMEM((tq, seq), jnp.float32), pltpu.VMEM((tq, seq), jnp.int32),
                        pltpu.VMEM((tq, seq), jnp.float32),
                        pltpu.VMEM((seq, IDX_DIM), _BF), pltpu.VMEM((seq, HEAD_DIM), _BF),
                        pltpu.VMEM((seq, 2 * HEAD_DIM), _BF)],
        compiler_params=_params("parallel", "arbitrary"),
    )(proj, proj, proj, proj, proj, proj, _tri_matrix())


def _page_specs(rows, per_step=None):
    per_step = PAGES_PER_STEP if per_step is None else per_step
    def spec(k):
        return pl.BlockSpec((1, rows, LANES), lambda b, s, pt: (pt[b, s * per_step + k], 0, 0))
    return [spec(k) for k in range(per_step)]


def _rows_view(x, lead):
    return x.reshape(x.shape[:lead] + (-1, x.shape[-1]))


def _kind_rows(ref, kind, n_kinds, n_rows):
    return ref.at[0][pl.ds(kind, n_rows, stride=n_kinds), :]


NSA_PAGES_PER_STEP = 8


def _nsa_sample_kernel(pt_ref, *refs, n_heads, n_s, past_len):
    pages = refs[:NSA_PAGES_PER_STEP]
    (q_ref, gate_ref, nsk_ref, nsv_ref, nwk_ref, nwv_ref, win_ref, w1_ref, w2_ref, pe_ref, cos_ref, sin_ref,
     cover_ref, expand_ref, o_ref, rows_ref, selk_ref, selv_ref) = refs[NSA_PAGES_PER_STEP:]
    step = pl.program_id(1)
    ts = q_ref.shape[0]
    qw = n_heads * HEAD_DIM
    n_grp = N_KV_A
    n_kinds = 4 * n_grp
    page = pages[0].shape[1] // n_kinds
    t_col = past_len + lax.broadcasted_iota(jnp.int32, (ts, 1), 0)

    for k in range(NSA_PAGES_PER_STEP):
        start = pl.multiple_of((step * NSA_PAGES_PER_STEP + k) * page, page)
        for c in range(2 * n_grp):
            rows_ref.at[c][pl.ds(start, page), :] = _kind_rows(pages[k], c, n_kinds, page)
        for g in range(n_grp):
            selk_ref.at[g][pl.ds(start, page), :] = _kind_rows(pages[k], 2 * n_grp + g, n_kinds, page).astype(_BF)
            selv_ref.at[g][pl.ds(start, page), :] = _kind_rows(pages[k], 3 * n_grp + g, n_kinds, page).astype(_BF)

    @pl.when(step == pl.num_programs(1) - 1)
    def _():
        n_ch = cover_ref.shape[0]
        gates = jax.nn.sigmoid(gate_ref[...])
        n_all = n_grp * n_heads
        w_len = win_ref.shape[1] // (2 * n_grp)
        row = lax.broadcasted_iota(jnp.int32, (ts, LANES), 0)
        lane = lax.broadcasted_iota(jnp.int32, (ts, LANES), 1)
        new_causal = (lane <= row) & (lane < ts)
        wlane = lax.broadcasted_iota(jnp.int32, (ts, w_len + LANES), 1)
        k_pos = past_len - w_len + wlane
        d = t_col - k_pos
        win_mask = (d >= 0) & (d < WINDOW) & (k_pos >= 0) & (wlane < w_len + ts)
        new_blk = past_len // SEL_BLOCK

        def group_lanes(x, g):
            return x[:, g * HEAD_DIM:(g + 1) * HEAD_DIM]

        for g in range(n_grp):
            qs = _stack_heads(q_ref[:, g * qw:(g + 1) * qw], n_heads).astype(_BF)
            kc, vc = [_compress_rows(lambda s: rows_ref.at[kv * n_grp + g][pl.ds(s, n_ch, stride=CMP_STRIDE), :],
                                     pe_ref[kv], lambda r: w1_ref[kv, r], w2_ref[kv], n_ch) for kv in range(2)]
            kc = _rotary(kc, cos_ref[...], sin_ref[...])
            o_cmp, sel = _cmp_attend_and_choose(qs, kc.astype(_BF), vc.astype(_BF), cover_ref[...], t_col, n_heads, n_s)
            sel_f = jnp.where(sel, 1.0, 0.0)
            chosen = jnp.dot(sel_f.astype(_BF), expand_ref[...], preferred_element_type=jnp.float32) > 0.5
            carry = _flash_update(qs, selk_ref[g], selv_ref[g], chosen, _flash_init(n_heads, ts), n_heads)
            in_new = sel_f[:, new_blk:new_blk + 1] > 0.5
            k_new = _pad_rows(group_lanes(nsk_ref[...], g), LANES).astype(_BF)
            v_new = _pad_rows(group_lanes(nsv_ref[...], g), LANES).astype(_BF)
            o_slc = _flash_finish(_flash_update(qs, k_new, v_new, new_causal & in_new, carry, n_heads))
            kw = jnp.concatenate([_kind_rows(win_ref, g, 2 * n_grp, w_len),
                                  _pad_rows(group_lanes(nwk_ref[...], g), LANES)], axis=0).astype(_BF)
            vw = jnp.concatenate([_kind_rows(win_ref, n_grp + g, 2 * n_grp, w_len),
                                  _pad_rows(group_lanes(nwv_ref[...], g), LANES)], axis=0).astype(_BF)
            o_win = _flash_finish(_flash_update(qs, kw, vw, win_mask, _flash_init(n_heads, ts), n_heads))
            for r in range(n_heads):
                h = g * n_heads + r
                o = (gates[:, h:h + 1] * o_cmp[r] + gates[:, n_all + h:n_all + h + 1] * o_slc[r]
                     + gates[:, 2 * n_all + h:2 * n_all + h + 1] * o_win[r])
                o_ref[0, :, h * HEAD_DIM:(h + 1) * HEAD_DIM] = o.astype(o_ref.dtype)


def nsa_sample(proj, row0, cache, win_buf, page_table, w1, w2, pe, cos_end, sin_end, lay, ts):
    n_batch, n_pages = page_table.shape
    page = cache.shape[1]
    past_len = n_pages * page
    pool = _rows_view(cache, 1)
    win_rows = _rows_view(win_buf, 1)
    n_heads = lay.n_heads_a // N_KV_A
    n_ch = past_len // CMP_STRIDE
    n_s = -(-(past_len + ts) // SEL_BLOCK)
    sel_lanes = -(-n_s // LANES) * LANES
    kvw = N_KV_A * HEAD_DIM
    assert n_pages % NSA_PAGES_PER_STEP == 0 and row0 % ts == 0 and ts <= SEL_BLOCK and past_len % SEL_BLOCK == 0
    assert ts % 8 == 0 and lay.ga % LANES == 0
    assert (past_len + ts - CMP_BLOCK) // CMP_STRIDE + 1 <= n_ch and n_ch * CMP_STRIDE <= past_len
    r0 = row0 // ts
    qw = lay.n_heads_a * HEAD_DIM
    rows = lambda width, col: pl.BlockSpec((ts, width), lambda b, s, pt: (r0 + b, col // width))
    once = lambda shape: pl.BlockSpec(shape, lambda b, s, pt: (0,) * len(shape), pipeline_mode=pl.Buffered(1))
    grid_spec = pltpu.PrefetchScalarGridSpec(
        num_scalar_prefetch=1,
        grid=(n_batch, n_pages // NSA_PAGES_PER_STEP),
        in_specs=_page_specs(pool.shape[1], NSA_PAGES_PER_STEP) + [
            rows(qw, lay.qa), rows(LANES, lay.ga),
            rows(kvw, lay.selk), rows(kvw, lay.selv), rows(kvw, lay.wink), rows(kvw, lay.winv),
            pl.BlockSpec((1,) + win_rows.shape[1:], lambda b, s, pt: (b, 0, 0)),
            once(w1.shape), once(w2.shape), once(pe.shape), once((n_ch, HEAD_DIM)), once((n_ch, HEAD_DIM)),
            once((n_ch, sel_lanes)), once((sel_lanes, past_len))],
        out_specs=pl.BlockSpec((1, ts, qw), lambda b, s, pt: (b, 0, 0)),
        scratch_shapes=[pltpu.VMEM((2 * N_KV_A, past_len, HEAD_DIM), jnp.float32),
                        pltpu.VMEM((N_KV_A, past_len, HEAD_DIM), _BF),
                        pltpu.VMEM((N_KV_A, past_len, HEAD_DIM), _BF)],
    )
    return pl.pallas_call(
        functools.partial(_nsa_sample_kernel, n_heads=n_heads, n_s=n_s, past_len=past_len),
        grid_spec=grid_spec,
        out_shape=jax.ShapeDtypeStruct((n_batch, ts, qw), _BF),
        compiler_params=_params("parallel", "arbitrary"),
    )(page_table, *([pool] * NSA_PAGES_PER_STEP), proj, proj, proj, proj, proj, proj, win_rows,
      w1, w2, pe, cos_end, sin_end, _cover_matrix(n_ch, sel_lanes, n_s), _expand_matrix(sel_lanes, past_len))


def _dsa_sample_select_kernel(pt_ref, *refs, n_idx, n_top, iw_lane, past_len):
    pages = refs[:PAGES_PER_STEP]
    iq_ref, iw_ref, nik_ref, tri_ref, mask_ref, score_ref, key_ref = refs[PAGES_PER_STEP:]
    step = pl.program_id(1)
    ts = iq_ref.shape[0]
    keys_per_step = PAGES_PER_STEP * pages[0].shape[1]
    iqs = _stack_heads(iq_ref[...], n_idx).astype(_BF)
    iw = iw_ref[...]
    ik = jnp.concatenate([p[0] for p in pages], axis=0).astype(_BF)
    start = pl.multiple_of(step * keys_per_step, keys_per_step)
    score_ref[:, pl.ds(start, keys_per_step)] = _indexer_scores(iqs, ik, iw, iw_lane, n_idx)

    @pl.when(step == pl.num_programs(1) - 1)
    def _():
        sc = _indexer_scores(iqs, _pad_rows(nik_ref[...], LANES).astype(_BF), iw, iw_lane, n_idx)
        row = lax.broadcasted_iota(jnp.int32, (ts, LANES), 0)
        lane = lax.broadcasted_iota(jnp.int32, (ts, LANES), 1)
        score_ref[:, past_len:past_len + LANES] = jnp.where(lane < ts, jnp.where(lane <= row, sc, NEG), -jnp.inf)
        _topk_mask(score_ref, key_ref, mask_ref.at[0], tri_ref, past_len // LANES + 1, n_top)


def dsa_sample_select(proj, row0, cache_idx, page_table, lay, ts):
    n_batch, n_pages = page_table.shape
    page = cache_idx.shape[1]
    past_len = n_pages * page
    assert n_pages % PAGES_PER_STEP == 0 and row0 % ts == 0 and ts <= LANES and past_len % LANES == 0
    n_top = min(DSA_TOPK, (past_len + ts) // 4)
    r0 = row0 // ts
    iqw = lay.n_idx_heads * IDX_DIM
    width = past_len + LANES
    rows = lambda w, col: pl.BlockSpec((ts, w), lambda b, s, pt: (r0 + b, col // w))
    grid_spec = pltpu.PrefetchScalarGridSpec(
        num_scalar_prefetch=1,
        grid=(n_batch, n_pages // PAGES_PER_STEP),
        in_specs=_page_specs(page) + [
            rows(iqw, lay.iq), rows(LANES, lay.iw), rows(IDX_DIM, lay.ik),
            pl.BlockSpec((LANES, LANES), lambda b, s, pt: (0, 0))],
        out_specs=pl.BlockSpec((1, ts, width), lambda b, s, pt: (b, 0, 0)),
        scratch_shapes=[pltpu.VMEM((ts, width), jnp.float32), pltpu.VMEM((ts, width), jnp.int32)],
    )
    return pl.pallas_call(
        functools.partial(_dsa_sample_select_kernel, n_idx=lay.n_idx_heads, n_top=n_top,
                          iw_lane=lay.iw % LANES, past_len=past_len),
        grid_spec=grid_spec,
        out_shape=jax.ShapeDtypeStruct((n_batch, ts, width), jnp.float32),
        compiler_params=_params("parallel", "arbitrary"),
    )(page_table, *([cache_idx] * PAGES_PER_STEP), proj, proj, proj, _tri_matrix())


def _dsa_sample_attend_kernel(pt_ref, *refs, n_heads):
    pages = refs[:PAGES_PER_STEP]
    q_ref, mask_ref, nmask_ref, nk_ref, nv_ref, o_ref, m_ref, l_ref, acc_ref = refs[PAGES_PER_STEP:]
    step = pl.program_id(1)
    ts = q_ref.shape[0]
    qs = _stack_heads(q_ref[...], n_heads).astype(_BF)

    @pl.when(step == 0)
    def _():
        m_ref[...], l_ref[...], acc_ref[...] = _flash_init(n_heads, ts)

    page = pages[0].shape[1] // 2
    k = jnp.concatenate([_kind_rows(p, 0, 2, page) for p in pages], axis=0).astype(_BF)
    v = jnp.concatenate([_kind_rows(p, 1, 2, page) for p in pages], axis=0).astype(_BF)
    m_ref[...], l_ref[...], acc_ref[...] = _flash_update(qs, k, v, mask_ref[0] > 0.5,
                                                         (m_ref[...], l_ref[...], acc_ref[...]), n_heads)

    @pl.when(step == pl.num_programs(1) - 1)
    def _():
        row = lax.broadcasted_iota(jnp.int32, (ts, LANES), 0)
        lane = lax.broadcasted_iota(jnp.int32, (ts, LANES), 1)
        mask = (nmask_ref[0] > 0.5) & (lane <= row) & (lane < ts)
        o = _flash_finish(_flash_update(qs, _pad_rows(nk_ref[...], LANES).astype(_BF),
                                        _pad_rows(nv_ref[...], LANES).astype(_BF), mask,
                                        (m_ref[...], l_ref[...], acc_ref[...]), n_heads))
        for h in range(n_heads):
            o_ref[0, :, h * HEAD_DIM:(h + 1) * HEAD_DIM] = o[h].astype(o_ref.dtype)


def dsa_sample_attend(proj, row0, mask, cache_kv, page_table, lay, ts):
    n_batch, n_pages = page_table.shape
    page = cache_kv.shape[1]
    past_len = n_pages * page
    pool = _rows_view(cache_kv, 1)
    keys_per_step = PAGES_PER_STEP * page
    r0 = row0 // ts
    qw = lay.n_heads_b * HEAD_DIM
    rows = lambda w, col: pl.BlockSpec((ts, w), lambda b, s, pt: (r0 + b, col // w))
    grid_spec = pltpu.PrefetchScalarGridSpec(
        num_scalar_prefetch=1,
        grid=(n_batch, n_pages // PAGES_PER_STEP),
        in_specs=_page_specs(pool.shape[1]) + [
            rows(qw, lay.qb),
            pl.BlockSpec((1, ts, keys_per_step), lambda b, s, pt: (b, 0, s)),
            pl.BlockSpec((1, ts, LANES), lambda b, s, pt: (b, 0, past_len // LANES)),
            rows(HEAD_DIM, lay.dk), rows(HEAD_DIM, lay.dv)],
        out_specs=pl.BlockSpec((1, ts, qw), lambda b, s, pt: (b, 0, 0)),
        scratch_shapes=[pltpu.VMEM((lay.n_heads_b, ts, 1), jnp.float32),
                        pltpu.VMEM((lay.n_heads_b, ts, 1), jnp.float32),
                        pltpu.VMEM((lay.n_heads_b, ts, HEAD_DIM), jnp.float32)],
    )
    return pl.pallas_call(
        functools.partial(_dsa_sample_attend_kernel, n_heads=lay.n_heads_b),
        grid_spec=grid_spec,
        out_shape=jax.ShapeDtypeStruct((n_batch, ts, qw), _BF),
        compiler_params=_params("parallel", "arbitrary"),
    )(page_table, *([pool] * PAGES_PER_STEP), proj, mask, mask, proj, proj)


class Layout:
    def __init__(self, d_model, n_heads_a, n_heads_b, n_idx_heads):
        self.n_heads_a, self.n_heads_b, self.n_idx_heads = n_heads_a, n_heads_b, n_idx_heads
        src = np.cumsum([0, n_heads_a * HEAD_DIM, 6 * N_KV_A * HEAD_DIM, 3 * n_heads_a,
                         n_heads_b * HEAD_DIM, 2 * HEAD_DIM, n_idx_heads * IDX_DIM, n_idx_heads,
                         IDX_DIM, 2 * d_model])
        s_qa, s_kva, s_ga, s_qb, s_kvb, s_iq, s_iw, s_ik, s_mg, s_end = (int(v) for v in src)
        small = 3 * n_heads_a + n_idx_heads
        assert small <= LANES
        self.small_pad = LANES - small
        kvw = N_KV_A * HEAD_DIM
        self.pieces = [
            (s_qa, s_kva, True),
            (s_qb, s_kvb, True),
            (s_iq, s_iw, True),
            (s_kva, s_kva + 2 * kvw, False),
            (s_kva + 2 * kvw, s_kva + 3 * kvw, True),
            (s_kva + 3 * kvw, s_kva + 4 * kvw, False),
            (s_kva + 4 * kvw, s_kva + 5 * kvw, True),
            (s_kva + 5 * kvw, s_ga, False),
            (s_kvb, s_kvb + HEAD_DIM, True),
            (s_kvb + HEAD_DIM, s_iq, False),
            (s_ik, s_mg, True),
            (s_ga, s_qb, False),
            (s_iw, s_ik, False),
            None,
            (s_mg, s_end, False),
        ]
        off = 0
        starts = []
        for p in self.pieces:
            starts.append(off)
            off += self.small_pad if p is None else p[1] - p[0]
        (self.qa, self.qb, self.iq, self.cmp, self.selk, self.selv, self.wink, self.winv, self.dk,
         self.dv, self.ik, self.ga, self.iw, _, self.mg) = starts
        self.width = off
        assert self.width % LANES == 0
        flags = np.zeros(self.width // LANES, np.int32)
        for st, p in zip(starts, self.pieces):
            if p is not None and p[2]:
                assert st % LANES == 0 and (p[1] - p[0]) % LANES == 0
                flags[st // LANES:(st + p[1] - p[0]) // LANES] = 1
        self.rope_flags = flags

    def pack(self, w_in):
        cols = []
        for p in self.pieces:
            if p is None:
                cols.append(jnp.zeros((w_in.shape[0], self.small_pad), w_in.dtype))
            else:
                cols.append(w_in[:, p[0]:p[1]])
        return jnp.concatenate(cols, axis=1).astype(_BF)


def rope_tables(pos):
    half = HEAD_DIM // 2
    inv = ROPE_THETA ** (-jnp.arange(half, dtype=jnp.float32) / half)
    ang = pos.astype(jnp.float32)[:, None] * inv[None, :]
    cos, sin = jnp.cos(ang), jnp.sin(ang)
    return jnp.concatenate([cos, cos], axis=1), jnp.concatenate([-sin, sin], axis=1)


def _swiglu_half_step(h, g, wg, wu, wd, w_index):
    xn = rmsnorm(h, g, _BF)
    m = h.shape[0]
    a = ffn_gate_up(xn, wg, wu, w_index, tm=_row_tile(m, ROW_TILE_WIDE), tn=2 * LANES)
    return resid_matmul(a, wd, w_index, h, 0.5, tm=_row_tile(m, ROW_TILE_DOWN), tn=2 * LANES)


def kernel(x_prompt, x_sample, cache_nsa_kv, cache_nsa_win, cache_dsa_kv, cache_dsa_idx, page_table,
           g_norm, w_ffn_gate, w_ffn_up, w_ffn_down, w_in, w_cmp1, w_cmp2, cmp_pos,
           w_br_a, w_br_b, w_out, g_final):
    B, T, D = x_prompt.shape
    DB, Ts, _ = x_sample.shape
    depth = g_norm.shape[0]
    page = cache_nsa_kv.shape[2]
    past_len = page_table.shape[1] * page
    n_heads_a = w_br_a.shape[1] // HEAD_DIM
    n_heads_b = w_br_b.shape[1] // HEAD_DIM
    G = N_KV_A
    lay = Layout(D, n_heads_a, n_heads_b, n_heads_b // 2)
    Mp, Ms = B * T, DB * Ts
    tm_wide = _row_tile(Mp + Ms, ROW_TILE_WIDE)

    pos_p = jnp.arange(T, dtype=jnp.int32)
    pos_s = past_len + jnp.arange(Ts, dtype=jnp.int32)
    cos, sin = rope_tables(jnp.concatenate([jnp.tile(pos_p, B), jnp.tile(pos_s, DB)]))
    rope_flags = jnp.asarray(lay.rope_flags)

    def block_end_tables(n_ch):
        return rope_tables(jnp.arange(n_ch, dtype=jnp.int32) * CMP_STRIDE + CMP_BLOCK - 1)

    h = jnp.concatenate([x_prompt.reshape(Mp, D), x_sample.reshape(Ms, D)], axis=0)
    outs = [[] for _ in range(8)]
    for l in range(depth):
        h = _swiglu_half_step(h, g_norm[l, 0], w_ffn_gate, w_ffn_up, w_ffn_down, (l, 0))

        u = rmsnorm(h, g_norm[l, 1], _BF)
        proj, nsa_rows, win_rows, dsa_rows, ik_rows = in_project(u, lay.pack(w_in[l]), rope_flags, cos, sin, lay,
                                                                 tm=tm_wide, tn=4 * LANES)
        w1 = w_cmp1[l].reshape(2, 2, CMP_STRIDE * HEAD_DIM, -1).astype(_BF)
        w2 = w_cmp2[l].astype(_BF)

        kvc_p = compress_prompt(proj, B, T, lay.cmp, w1, w2, cmp_pos[l], *block_end_tables(T // CMP_STRIDE))
        o_a_p = nsa_prompt(proj, kvc_p, lay, B, T, tq=Q_TILE)
        o_b_p = dsa_prompt(proj, lay, B, T, tq=Q_TILE)

        o_a_s = nsa_sample(proj, Mp, cache_nsa_kv[l], cache_nsa_win[l], page_table, w1, w2, cmp_pos[l],
                           *block_end_tables(past_len // CMP_STRIDE), lay, Ts)
        top_mask = dsa_sample_select(proj, Mp, cache_dsa_idx[l], page_table, lay, Ts)
        o_b_s = dsa_sample_attend(proj, Mp, top_mask, cache_dsa_kv[l], page_table, lay, Ts)

        o_a = jnp.concatenate([o_a_p, o_a_s.reshape(Ms, -1)], axis=0)
        o_b = jnp.concatenate([o_b_p, o_b_s.reshape(Ms, -1)], axis=0)
        m = merge_branches(o_a, o_b, w_br_a, w_br_b, (l,), proj, lay.mg, tm=tm_wide, tn=4 * LANES)
        h = resid_matmul(m, w_out, (l,), h, 1.0, tm=tm_wide, tn=4 * LANES)

        win_p = win_rows[:Mp].reshape(B, T, 2, G, HEAD_DIM)
        win_s = win_rows[Mp:].reshape(DB, Ts, 2, G, HEAD_DIM)
        outs[0].append(nsa_rows[:Mp].reshape(B, T, 4, G, HEAD_DIM))
        outs[1].append(win_p[:, T - min(WINDOW, T):])
        outs[2].append(dsa_rows[:Mp].reshape(B, T, 2, HEAD_DIM))
        outs[3].append(ik_rows[:Mp].reshape(B, T, IDX_DIM))
        outs[4].append(nsa_rows[Mp:].reshape(DB, Ts, 4, G, HEAD_DIM))
        outs[5].append(jnp.concatenate([cache_nsa_win[l], win_s], axis=1)[:, Ts:])
        outs[6].append(dsa_rows[Mp:].reshape(DB, Ts, 2, HEAD_DIM))
        outs[7].append(ik_rows[Mp:].reshape(DB, Ts, IDX_DIM))

        h = _swiglu_half_step(h, g_norm[l, 2], w_ffn_gate, w_ffn_up, w_ffn_down, (l, 1))

    y_p = rmsnorm(h, g_final, jnp.float32, 0, Mp)
    y_s = rmsnorm(h, g_final, jnp.float32, Mp, Ms)
    return (y_p.reshape(B, T, D), y_s.reshape(DB, Ts, D), *(jnp.stack(o) for o in outs))
```

```python
import functools

import jax
import jax.numpy as jnp
import numpy as np
from jax import lax
from jax.experimental import pallas as pl
from jax.experimental.pallas import tpu as pltpu

HEAD_DIM = 128
N_KV_A = 2
IDX_DIM = 128
CMP_BLOCK = 32
CMP_STRIDE = 16
SEL_BLOCK = 64
N_SEL = 16
N_LOCAL_SEL = 2
WINDOW = 512
DSA_TOPK = 256
ROPE_THETA = 10000.0
RMS_EPS = 1e-6
NEG = -1e30
BIG = 1e30
TINY = 1e-30

LANES = 128
VMEM_LIMIT = 56 * 1024 * 1024
PAGES_PER_STEP = 32

_NT = (((1,), (1,)), ((), ()))
_BF = jnp.bfloat16


ROW_TILE_WIDE = 1408
ROW_TILE_DOWN = 1056
Q_TILE = 2 * LANES


def _params(*sem):
    return pltpu.CompilerParams(dimension_semantics=sem, vmem_limit_bytes=VMEM_LIMIT)


def _row_tile(m, limit):
    return max(t for t in range(16, limit + 1, 16) if m % t == 0)


def _rmsnorm_kernel(x_ref, g_ref, o_ref):
    x = x_ref[...]
    y = x * lax.rsqrt(jnp.mean(x * x, axis=-1, keepdims=True) + RMS_EPS)
    o_ref[...] = (y * g_ref[...]).astype(o_ref.dtype)


def rmsnorm(x, g, out_dtype, row0=0, n_rows=None, tm=256):
    D = x.shape[1]
    n_rows = x.shape[0] - row0 if n_rows is None else n_rows
    assert row0 % tm == 0 and n_rows % tm == 0
    return pl.pallas_call(
        _rmsnorm_kernel,
        grid=(n_rows // tm,),
        in_specs=[pl.BlockSpec((tm, D), lambda i: (row0 // tm + i, 0)),
                  pl.BlockSpec((1, D), lambda i: (0, 0))],
        out_specs=pl.BlockSpec((tm, D), lambda i: (i, 0)),
        out_shape=jax.ShapeDtypeStruct((n_rows, D), out_dtype),
        compiler_params=_params("parallel"),
    )(x, g.reshape(1, D))


def _gateup_kernel(x_ref, wg_ref, wu_ref, o_ref):
    x = x_ref[...]
    g = jnp.dot(x, wg_ref[...].astype(_BF), preferred_element_type=jnp.float32)
    u = jnp.dot(x, wu_ref[...].astype(_BF), preferred_element_type=jnp.float32)
    o_ref[...] = (g * jax.nn.sigmoid(g) * u).astype(o_ref.dtype)


def _weight_spec(w, w_index, rows, tn):
    assert w.ndim == len(w_index) + 2 and w.shape[-2] == rows
    return pl.BlockSpec((None,) * len(w_index) + (rows, tn), lambda i, j: tuple(w_index) + (0, j))


def ffn_gate_up(xn, wg, wu, w_index, tm, tn):
    M, D = xn.shape
    F = wg.shape[-1]
    return pl.pallas_call(
        _gateup_kernel,
        grid=(M // tm, F // tn),
        in_specs=[pl.BlockSpec((tm, D), lambda i, j: (i, 0)),
                  _weight_spec(wg, w_index, D, tn),
                  _weight_spec(wu, w_index, D, tn)],
        out_specs=pl.BlockSpec((tm, tn), lambda i, j: (i, j)),
        out_shape=jax.ShapeDtypeStruct((M, F), _BF),
        compiler_params=_params("parallel", "parallel"),
    )(xn, wg, wu)


def _resid_matmul_kernel(a_ref, w_ref, r_ref, o_ref, *, scale):
    acc = jnp.dot(a_ref[...], w_ref[...].astype(_BF), preferred_element_type=jnp.float32)
    o_ref[...] = r_ref[...] + scale * acc


def resid_matmul(a, w, w_index, resid, scale, tm, tn):
    M, K = a.shape
    N = w.shape[-1]
    return pl.pallas_call(
        functools.partial(_resid_matmul_kernel, scale=scale),
        grid=(M // tm, N // tn),
        in_specs=[pl.BlockSpec((tm, K), lambda i, j: (i, 0), pipeline_mode=pl.Buffered(1)),
                  _weight_spec(w, w_index, K, tn),
                  pl.BlockSpec((tm, tn), lambda i, j: (i, j))],
        out_specs=pl.BlockSpec((tm, tn), lambda i, j: (i, j)),
        out_shape=jax.ShapeDtypeStruct((M, N), jnp.float32),
        compiler_params=_params("parallel", "parallel"),
    )(a, w, resid)


def _rotary(y, cos, sin):
    return y * cos + pltpu.roll(y, HEAD_DIM // 2, axis=1) * sin


def _inproj_kernel(flags_ref, x_ref, w_ref, cos_ref, sin_ref, o_ref, *row_refs, n_chunks, routes):
    j = pl.program_id(1)
    acc = jnp.dot(x_ref[...], w_ref[...], preferred_element_type=jnp.float32)
    for c in range(n_chunks):
        sl = slice(c * LANES, (c + 1) * LANES)
        y = acc[:, sl]
        flag = flags_ref[j * n_chunks + c]

        @pl.when(flag == 1)
        def _():
            o_ref[:, sl] = _rotary(y, cos_ref[...], sin_ref[...])

        @pl.when(flag == 0)
        def _():
            o_ref[:, sl] = y

    tm = o_ref.shape[0]
    for chunk, out_idx, kind, n_kinds in routes:
        @pl.when(j == chunk // n_chunks)
        def _():
            c = chunk % n_chunks
            row_refs[out_idx][pl.ds(kind, tm, stride=n_kinds), :] = o_ref[:, c * LANES:(c + 1) * LANES]


def in_project(u, w, rope_flags, cos, sin, lay, tm, tn):
    M, D = u.shape
    N = w.shape[1]
    n_chunks = tn // LANES
    G = N_KV_A
    kinds = [4 * G, 2 * G, 2, 1]
    routes = ([(lay.cmp // LANES + k, 0, k, kinds[0]) for k in range(kinds[0])]
              + [(lay.wink // LANES + k, 1, k, kinds[1]) for k in range(kinds[1])]
              + [(lay.dk // LANES + k, 2, k, kinds[2]) for k in range(kinds[2])]
              + [(lay.ik // LANES, 3, 0, 1)])
    row_shapes = [(M, 4, G, HEAD_DIM), (M, 2, G, HEAD_DIM), (M, 2, HEAD_DIM), (M, IDX_DIM)]
    grid_spec = pltpu.PrefetchScalarGridSpec(
        num_scalar_prefetch=1,
        grid=(M // tm, N // tn),
        in_specs=[pl.BlockSpec((tm, D), lambda i, j, f: (i, 0), pipeline_mode=pl.Buffered(1)),
                  pl.BlockSpec((D, tn), lambda i, j, f: (0, j)),
                  pl.BlockSpec((tm, LANES), lambda i, j, f: (i, 0)),
                  pl.BlockSpec((tm, LANES), lambda i, j, f: (i, 0))],
        out_specs=[pl.BlockSpec((tm, tn), lambda i, j, f: (i, j))]
        + [pl.BlockSpec((tm * nk, LANES), lambda i, j, f: (i, 0)) for nk in kinds],
    )
    proj, *rows = pl.pallas_call(
        functools.partial(_inproj_kernel, n_chunks=n_chunks, routes=routes),
        grid_spec=grid_spec,
        out_shape=[jax.ShapeDtypeStruct((M, N), jnp.float32)]
        + [jax.ShapeDtypeStruct((M * nk, LANES), jnp.float32) for nk in kinds],
        compiler_params=_params("parallel", "arbitrary"),
    )(rope_flags, u, w, cos, sin)
    return (proj, *(r.reshape(shp) for r, shp in zip(rows, row_shapes)))


def _merge_kernel(oa_ref, ob_ref, wa_ref, wb_ref, ga_ref, gb_ref, o_ref):
    ya = jnp.dot(oa_ref[...], wa_ref[...].astype(_BF), preferred_element_type=jnp.float32)
    yb = jnp.dot(ob_ref[...], wb_ref[...].astype(_BF), preferred_element_type=jnp.float32)
    m = jax.nn.sigmoid(ga_ref[...]) * ya + jax.nn.sigmoid(gb_ref[...]) * yb
    o_ref[...] = m.astype(o_ref.dtype)


def merge_branches(o_a, o_b, w_a, w_b, w_index, proj, mg_col, tm, tn):
    M, K = o_a.shape
    N = w_a.shape[-1]
    ja = mg_col // tn
    jb = (mg_col + N) // tn
    return pl.pallas_call(
        _merge_kernel,
        grid=(M // tm, N // tn),
        in_specs=[pl.BlockSpec((tm, K), lambda i, j: (i, 0), pipeline_mode=pl.Buffered(1)),
                  pl.BlockSpec((tm, K), lambda i, j: (i, 0), pipeline_mode=pl.Buffered(1)),
                  _weight_spec(w_a, w_index, K, tn),
                  _weight_spec(w_b, w_index, K, tn),
                  pl.BlockSpec((tm, tn), lambda i, j: (i, ja + j)),
                  pl.BlockSpec((tm, tn), lambda i, j: (i, jb + j))],
        out_specs=pl.BlockSpec((tm, tn), lambda i, j: (i, j)),
        out_shape=jax.ShapeDtypeStruct((M, N), _BF),
        compiler_params=_params("parallel", "parallel"),
    )(o_a, o_b, w_a, w_b, proj, proj)


def _stack_heads(x, n_heads):
    return jnp.concatenate([x[:, h * HEAD_DIM:(h + 1) * HEAD_DIM] for h in range(n_heads)], axis=0)


def _flash_update(qs, k, v, mask, carry, n_rep):
    m, l, acc = carry
    rows, tk = qs.shape[0], k.shape[0]
    tq = rows // n_rep
    scale = HEAD_DIM ** -0.5
    s = lax.dot_general(qs, k, _NT, preferred_element_type=jnp.float32).reshape(n_rep, tq, tk)
    mask = mask[None]
    s = jnp.where(mask, s, NEG)
    m_new = jnp.maximum(m, jnp.max(s, axis=-1, keepdims=True))
    p = jnp.where(mask, jnp.exp((s - m_new) * scale), 0.0)
    alpha = jnp.exp((m - m_new) * scale)
    l = alpha * l + jnp.sum(p, axis=-1, keepdims=True)
    pv = jnp.dot(p.reshape(rows, tk).astype(_BF), v, preferred_element_type=jnp.float32)
    return m_new, l, alpha * acc + pv.reshape(n_rep, tq, HEAD_DIM)


def _flash_init(n_rep, tq):
    return (jnp.full((n_rep, tq, 1), NEG, jnp.float32), jnp.zeros((n_rep, tq, 1), jnp.float32),
            jnp.zeros((n_rep, tq, HEAD_DIM), jnp.float32))


def _flash_finish(carry):
    _, l, acc = carry
    return acc * (1.0 / jnp.maximum(l, TINY))


def _attend(qs, k_ref, v_ref, start, n_chunks, chunk, mask_fn, n_rep):
    rows = qs.shape[0]
    tq = rows // n_rep
    n_lane_tiles = chunk // LANES
    scale = HEAD_DIM ** -0.5

    def scores(c):
        sl = pl.ds(pl.multiple_of(start + c * chunk, LANES), chunk)
        return lax.dot_general(qs, k_ref[sl, :], _NT, preferred_element_type=jnp.float32), sl

    def lane_tile(s, j):
        return s[:, j * LANES:(j + 1) * LANES].reshape(n_rep, tq, LANES)

    def row_max(c, mx):
        s, _ = scores(c)
        for j in range(n_lane_tiles):
            mx = jnp.maximum(mx, jnp.where(mask_fn(c, j)[None], lane_tile(s, j), NEG))
        return mx

    mx = lax.fori_loop(0, n_chunks, row_max, jnp.full((n_rep, tq, LANES), NEG, jnp.float32))
    m = jnp.broadcast_to(jnp.max(mx, axis=-1, keepdims=True), mx.shape)

    def accumulate(c, acc):
        s, sl = scores(c)
        p = [jnp.where(mask_fn(c, j)[None], jnp.exp((lane_tile(s, j) - m) * scale), 0.0)
             .astype(_BF).reshape(rows, LANES) for j in range(n_lane_tiles)]
        return acc + jnp.dot(jnp.concatenate(p, axis=1), v_ref[sl, :], preferred_element_type=jnp.float32)

    acc = lax.fori_loop(0, n_chunks, accumulate, jnp.zeros((rows, 2 * HEAD_DIM), jnp.float32))
    out = acc[:, :HEAD_DIM] * (1.0 / jnp.maximum(acc[:, HEAD_DIM:], TINY))
    return out.reshape(n_rep, tq, HEAD_DIM)


def _compress_rows(load, pe, w1, w2, n_ch):
    rows = [load(s) for s in range(CMP_STRIDE)]
    half = [jnp.concatenate([(rows[s] + pe[r * CMP_STRIDE + s:r * CMP_STRIDE + s + 1, :]).astype(_BF)
                             for s in range(CMP_STRIDE)], axis=1) for r in range(2)]
    h0 = jnp.dot(half[0], w1(0), preferred_element_type=jnp.float32)
    h1 = jnp.dot(half[1], w1(1), preferred_element_type=jnp.float32)
    h = h0 + pltpu.roll(h1, n_ch - 1, axis=0)
    return jnp.dot((h * jax.nn.sigmoid(h)).astype(_BF), w2, preferred_element_type=jnp.float32)


def _cmp_attend_and_choose(qs, kc, vc, cover, t_col, n_heads, n_s):
    rows = qs.shape[0]
    tq = rows // n_heads
    n_ch, lanes = cover.shape
    scale = HEAD_DIM ** -0.5
    s = lax.dot_general(qs, kc, _NT, preferred_element_type=jnp.float32).reshape(n_heads, tq, n_ch)
    end = lax.broadcasted_iota(jnp.int32, (tq, n_ch), 1) * CMP_STRIDE + (CMP_BLOCK - 1)
    cmask = (end <= t_col)[None]
    s = jnp.where(cmask, s, NEG)
    m = jnp.max(s, axis=-1, keepdims=True)
    p = jnp.where(cmask, jnp.exp((s - m) * scale), 0.0)
    p = p * (1.0 / jnp.maximum(jnp.sum(p, axis=-1, keepdims=True), TINY))
    o_cmp = jnp.dot(p.reshape(rows, n_ch).astype(_BF), vc, preferred_element_type=jnp.float32)
    imp = jnp.dot(jnp.sum(p, axis=0).astype(_BF), cover, preferred_element_type=jnp.float32)

    lane = lax.broadcasted_iota(jnp.int32, (tq, lanes), 1)
    lane_f = lane.astype(jnp.float32)
    jt = lax.shift_right_arithmetic(t_col, jnp.int32(SEL_BLOCK.bit_length() - 1))
    adm = lane <= jt
    forced = adm & ((lane == 0) | (lane > jt - N_LOCAL_SEL))
    work = jnp.where(forced, BIG, jnp.where(adm, imp, NEG))
    work = jnp.where(lane < n_s, work, -jnp.inf)
    sel = jnp.zeros((tq, lanes), jnp.bool_)
    for _ in range(min(N_SEL, n_s)):
        mx = jnp.max(work, axis=-1, keepdims=True)
        first = jnp.min(jnp.where(work == mx, lane_f, float(lanes)), axis=-1, keepdims=True)
        pick = lane_f == first
        sel = sel | pick
        work = jnp.where(pick, -jnp.inf, work)
    return o_cmp.reshape(n_heads, tq, HEAD_DIM), sel


def _topk_mask(score_ref, key_ref, mask_ref, tri_ref, n_tiles, n_top):
    tq = score_ref.shape[0]
    int_min = jnp.int32(-2 ** 31)

    def to_key(t, _):
        sl = pl.ds(pl.multiple_of(t * LANES, LANES), LANES)
        bits = lax.bitcast_convert_type(score_ref[:, sl], jnp.int32)
        key_ref[:, sl] = bits ^ ((bits >> 31) & jnp.int32(0x7FFFFFFF))
        return 0

    lax.fori_loop(0, n_tiles, to_key, 0)

    unroll = 8 if isinstance(n_tiles, int) else 1

    def count_ge(cand):
        def body(t, acc):
            sl = pl.ds(pl.multiple_of(t * LANES, LANES), LANES)
            return acc + jnp.where(key_ref[:, sl] >= cand, 1.0, 0.0)
        acc = lax.fori_loop(0, n_tiles, body, jnp.zeros((tq, LANES), jnp.float32), unroll=unroll)
        return jnp.sum(acc, axis=-1, keepdims=True)

    tau = jnp.zeros((tq, 1), jnp.int32)
    for bit in range(31, -1, -1):
        cand = tau | jnp.int32(-2 ** 31 if bit == 31 else 1 << bit)
        tau = jnp.where(count_ge(cand ^ int_min) >= float(n_top), cand, tau)
    thr = tau ^ int_min

    clean = jnp.max(jnp.abs(count_ge(thr) - float(n_top))) == 0.0

    @pl.when(clean)
    def _():
        def keep_ge(t, _):
            sl = pl.ds(pl.multiple_of(t * LANES, LANES), LANES)
            mask_ref[:, sl] = jnp.where(key_ref[:, sl] >= thr, 1.0, 0.0)
            return 0
        lax.fori_loop(0, n_tiles, keep_ge, 0, unroll=unroll)

    @pl.when(jnp.logical_not(clean))
    def _():
        ones = jnp.ones((LANES, LANES), _BF)

        def count_gt(t, acc):
            sl = pl.ds(pl.multiple_of(t * LANES, LANES), LANES)
            return acc + jnp.where(key_ref[:, sl] > thr, 1.0, 0.0)

        n_gt = jnp.sum(lax.fori_loop(0, n_tiles, count_gt, jnp.zeros((tq, LANES), jnp.float32)),
                       axis=-1, keepdims=True)
        need = float(n_top) - n_gt

        def cut(t, eq_before):
            sl = pl.ds(pl.multiple_of(t * LANES, LANES), LANES)
            key = key_ref[:, sl]
            eq = jnp.where(key == thr, 1.0, 0.0)
            rank = (jnp.dot(eq.astype(_BF), tri_ref[...], preferred_element_type=jnp.float32)
                    + jnp.dot(eq_before.astype(_BF), ones, preferred_element_type=jnp.float32))
            keep = (key > thr) | ((key == thr) & (rank <= need))
            mask_ref[:, sl] = jnp.where(keep, 1.0, 0.0)
            return eq_before + eq

        lax.fori_loop(0, n_tiles, cut, jnp.zeros((tq, LANES), jnp.float32))


def _topk_mask_keys_major(score_ref, key_ref, tri_t_ref, n_tiles, n_top):
    tq = score_ref.shape[1]
    int_min = jnp.int32(-2 ** 31)

    def tile(t):
        return pl.ds(pl.multiple_of(t * LANES, LANES), LANES)

    def to_key(t, _):
        bits = lax.bitcast_convert_type(score_ref[tile(t), :], jnp.int32)
        key_ref[tile(t), :] = bits ^ ((bits >> 31) & jnp.int32(0x7FFFFFFF))
        return 0

    lax.fori_loop(0, n_tiles, to_key, 0)

    def count(pred):
        def body(t, acc):
            hit = jnp.where(pred(key_ref[tile(t), :]), 1.0, 0.0)
            return acc + jnp.sum(hit.reshape(LANES // 8, 8, tq), axis=0)
        acc = lax.fori_loop(0, n_tiles, body, jnp.zeros((8, tq), jnp.float32))
        return jnp.sum(acc, axis=0, keepdims=True)

    tau = jnp.zeros((1, tq), jnp.int32)
    for bit in range(31, -1, -1):
        cand = tau | jnp.int32(-2 ** 31 if bit == 31 else 1 << bit)
        cand_s = cand ^ int_min
        tau = jnp.where(count(lambda key: key >= cand_s) >= float(n_top), cand, tau)
    thr = tau ^ int_min
    clean = jnp.max(jnp.abs(count(lambda key: key >= thr) - float(n_top))) == 0.0

    @pl.when(clean)
    def _():
        def keep_ge(t, _):
            score_ref[tile(t), :] = jnp.where(key_ref[tile(t), :] >= thr, 1.0, 0.0)
            return 0
        lax.fori_loop(0, n_tiles, keep_ge, 0)

    @pl.when(jnp.logical_not(clean))
    def _():
        need = float(n_top) - count(lambda key: key > thr)

        def cut(t, eq_before):
            key = key_ref[tile(t), :]
            eq = jnp.where(key == thr, 1.0, 0.0)
            rank = jnp.dot(tri_t_ref[...], eq.astype(_BF), preferred_element_type=jnp.float32) + eq_before
            keep = (key > thr) | ((key == thr) & (rank <= need))
            score_ref[tile(t), :] = jnp.where(keep, 1.0, 0.0)
            return eq_before + jnp.sum(eq, axis=0, keepdims=True)

        lax.fori_loop(0, n_tiles, cut, jnp.zeros((1, tq), jnp.float32))


def _indexer_scores(iqs, ik, iw, iw_lane, n_idx):
    tq = iqs.shape[0] // n_idx
    n = ik.shape[0]
    logits = lax.dot_general(iqs, ik, _NT, preferred_element_type=jnp.float32).reshape(n_idx, tq, n)
    logits = jnp.maximum(logits * IDX_DIM ** -0.5, 0.0)
    sc = jnp.zeros((tq, n), jnp.float32)
    for h in range(n_idx):
        sc = sc + logits[h] * iw[:, iw_lane + h:iw_lane + h + 1]
    return sc * n_idx ** -0.5


def _cover_matrix(n_ch, lanes, n_s):
    ci = np.arange(n_ch)[:, None] * CMP_STRIDE
    sj = np.arange(lanes)[None, :] * SEL_BLOCK
    return jnp.asarray((ci < sj + SEL_BLOCK) & (ci + CMP_BLOCK > sj) & (sj < n_s * SEL_BLOCK), _BF)


def _expand_matrix(lanes, n_keys):
    return jnp.asarray(np.arange(lanes)[:, None] == np.arange(n_keys)[None, :] // SEL_BLOCK, _BF)


def _tri_matrix():
    return jnp.asarray(np.arange(LANES)[:, None] <= np.arange(LANES)[None, :], _BF)


def _pad_rows(x, n):
    return jnp.concatenate([x, jnp.zeros((n - x.shape[0], x.shape[1]), x.dtype)], axis=0)


def _compress_kernel(x_ref, w1_ref, w2_ref, pe_ref, cos_ref, sin_ref, o_ref):
    n_ch = o_ref.shape[-2]
    y = _compress_rows(lambda s: x_ref[pl.ds(s, n_ch, stride=CMP_STRIDE), :], pe_ref[0],
                       lambda r: w1_ref[0, r], w2_ref[0], n_ch)

    @pl.when(pl.program_id(1) == 0)
    def _():
        o_ref[0, 0, 0] = _rotary(y, cos_ref[...], sin_ref[...]).astype(o_ref.dtype)

    @pl.when(pl.program_id(1) != 0)
    def _():
        o_ref[0, 0, 0] = y.astype(o_ref.dtype)


def compress_prompt(proj, n_batch, seq, cmp_col, w1, w2, pe, cos_end, sin_end):
    n_ch = seq // CMP_STRIDE
    col0 = cmp_col // HEAD_DIM
    return pl.pallas_call(
        _compress_kernel,
        grid=(n_batch, 2, N_KV_A),
        in_specs=[pl.BlockSpec((seq, HEAD_DIM), lambda b, kv, g: (b, col0 + kv * N_KV_A + g)),
                  pl.BlockSpec((1,) + w1.shape[1:], lambda b, kv, g: (kv, 0, 0, 0)),
                  pl.BlockSpec((1, w2.shape[1], HEAD_DIM), lambda b, kv, g: (kv, 0, 0)),
                  pl.BlockSpec((1, CMP_BLOCK, HEAD_DIM), lambda b, kv, g: (kv, 0, 0)),
                  pl.BlockSpec((n_ch, HEAD_DIM), lambda b, kv, g: (0, 0)),
                  pl.BlockSpec((n_ch, HEAD_DIM), lambda b, kv, g: (0, 0))],
        out_specs=pl.BlockSpec((1, 1, 1, n_ch, HEAD_DIM), lambda b, kv, g: (b, kv, g, 0, 0)),
        out_shape=jax.ShapeDtypeStruct((n_batch, 2, N_KV_A, n_ch, HEAD_DIM), _BF),
        compiler_params=_params("parallel", "parallel", "parallel"),
    )(proj, w1, w2, pe, cos_end, sin_end)


def _nsa_prompt_kernel(q_ref, kc_ref, vc_ref, selk_ref, selv_ref, wink_ref, winv_ref, gate_ref,
                       cover_ref, expand_ref, o_ref, selexp_ref, sk_ref, sv_ref, wk_ref, wv_ref,
                       *, n_heads, n_s, chunk):
    tq = q_ref.shape[0]
    seq = selexp_ref.shape[-1]
    n_grp = N_KV_A
    qi = pl.program_id(1)
    row0 = pl.multiple_of(qi * tq, tq)
    t_col = qi * tq + lax.broadcasted_iota(jnp.int32, (tq, 1), 0)
    qw = n_heads * HEAD_DIM

    @pl.when(qi == 0)
    def _():
        for ref in (sk_ref, sv_ref, wk_ref, wv_ref):
            ref[...] = jnp.zeros(ref.shape, ref.dtype)

    ones = jnp.ones((tq, HEAD_DIM), _BF)
    causal = lax.broadcasted_iota(jnp.int32, (tq, seq), 1) <= t_col
    qs, o_cmp = [], []
    for g in range(n_grp):
        lanes = slice(g * HEAD_DIM, (g + 1) * HEAD_DIM)
        sk_ref[g, pl.ds(row0, tq), :] = selk_ref[:, lanes].astype(_BF)
        sv_ref[g, pl.ds(row0, tq), :] = jnp.concatenate([selv_ref[:, lanes].astype(_BF), ones], axis=1)
        wk_ref[g, pl.ds(row0, tq), :] = wink_ref[:, lanes].astype(_BF)
        wv_ref[g, pl.ds(row0, tq), :] = jnp.concatenate([winv_ref[:, lanes].astype(_BF), ones], axis=1)
        qs.append(_stack_heads(q_ref[:, g * qw:(g + 1) * qw], n_heads).astype(_BF))
    sels = []
    for g in range(n_grp):
        o, sel = _cmp_attend_and_choose(qs[g], kc_ref[0, 0, g], vc_ref[0, 0, g], cover_ref[...], t_col, n_heads, n_s)
        o_cmp.append(o)
        sels.append(sel)
    for g in range(n_grp):
        chosen = jnp.dot(jnp.where(sels[g], 1.0, 0.0).astype(_BF), expand_ref[...],
                         preferred_element_type=jnp.float32)
        selexp_ref[g] = jnp.where(causal, chosen, 0.0)

    span = WINDOW + tq
    start = pl.multiple_of(jnp.maximum(row0 - WINDOW, 0), tq)
    d = t_col - (start + lax.broadcasted_iota(jnp.int32, (tq, span), 1))
    visible = (d >= 0) & (d < WINDOW)
    gates = jax.nn.sigmoid(gate_ref[...])
    n_all = n_grp * n_heads
    for g in range(n_grp):
        def sel_mask(c, j):
            return selexp_ref[g, :, pl.ds(pl.multiple_of(c * chunk + j * LANES, LANES), LANES)] > 0.5

        o_slc = _attend(qs[g], sk_ref.at[g], sv_ref.at[g], 0, (row0 + tq + chunk - 1) // chunk, chunk, sel_mask,
                        n_heads)
        o_win = _attend(qs[g], wk_ref.at[g], wv_ref.at[g], start, 1, span,
                        lambda c, j: visible[:, j * LANES:(j + 1) * LANES], n_heads)
        for r in range(n_heads):
            h = g * n_heads + r
            o = (gates[:, h:h + 1] * o_cmp[g][r] + gates[:, n_all + h:n_all + h + 1] * o_slc[r]
                 + gates[:, 2 * n_all + h:2 * n_all + h + 1] * o_win[r])
            o_ref[:, h * HEAD_DIM:(h + 1) * HEAD_DIM] = o.astype(o_ref.dtype)


def nsa_prompt(proj, kvc, lay, n_batch, seq, tq, chunk=512):
    n_heads = lay.n_heads_a // N_KV_A
    n_ch = kvc.shape[-2]
    n_s = -(-seq // SEL_BLOCK)
    assert tq % LANES == 0 and n_s <= LANES and seq % chunk == 0 and chunk % tq == 0
    assert WINDOW % tq == 0 and seq >= WINDOW + tq and lay.ga % LANES == 0
    nq = seq // tq
    qw = lay.n_heads_a * HEAD_DIM
    kvw = N_KV_A * HEAD_DIM

    def tile(col):
        return pl.BlockSpec((tq, kvw), lambda b, i: (b * nq + i, col // kvw))

    return pl.pallas_call(
        functools.partial(_nsa_prompt_kernel, n_heads=n_heads, n_s=n_s, chunk=chunk),
        grid=(n_batch, nq),
        in_specs=[pl.BlockSpec((tq, qw), lambda b, i: (b * nq + i, lay.qa // qw)),
                  pl.BlockSpec((1, 1, N_KV_A, n_ch, HEAD_DIM), lambda b, i: (b, 0, 0, 0, 0)),
                  pl.BlockSpec((1, 1, N_KV_A, n_ch, HEAD_DIM), lambda b, i: (b, 1, 0, 0, 0)),
                  tile(lay.selk), tile(lay.selv), tile(lay.wink), tile(lay.winv),
                  pl.BlockSpec((tq, LANES), lambda b, i: (b * nq + i, lay.ga // LANES)),
                  pl.BlockSpec((n_ch, LANES), lambda b, i: (0, 0)),
                  pl.BlockSpec((LANES, seq), lambda b, i: (0, 0))],
        out_specs=pl.BlockSpec((tq, qw), lambda b, i: (b * nq + i, 0)),
        out_shape=jax.ShapeDtypeStruct((n_batch * seq, qw), _BF),
        scratch_shapes=[pltpu.VMEM((N_KV_A, tq, seq), jnp.float32),
                        pltpu.VMEM((N_KV_A, seq, HEAD_DIM), _BF), pltpu.VMEM((N_KV_A, seq, 2 * HEAD_DIM), _BF),
                        pltpu.VMEM((N_KV_A, seq, HEAD_DIM), _BF), pltpu.VMEM((N_KV_A, seq, 2 * HEAD_DIM), _BF)],
        compiler_params=_params("parallel", "arbitrary"),
    )(proj, kvc, kvc, proj, proj, proj, proj, proj, _cover_matrix(n_ch, LANES, n_s), _expand_matrix(LANES, seq))


def _dsa_prompt_kernel(iq_ref, iw_ref, q_ref, ik_ref, k_ref, v_ref, tri_ref, o_ref,
                       score_ref, key_ref, mask_ref, ikb_ref, kb_ref, vb_ref,
                       *, n_idx, n_heads, n_top, iw_lane, chunk):
    tq = iq_ref.shape[0]
    qi = pl.program_id(1)
    n_tiles = qi + 1
    row0 = pl.multiple_of(qi * tq, tq)
    t_row = qi * tq + lax.broadcasted_iota(jnp.int32, (1, tq), 1)

    @pl.when(qi == 0)
    def _():
        for ref in (ikb_ref, kb_ref, vb_ref):
            ref[...] = jnp.zeros(ref.shape, ref.dtype)

    ikb_ref[pl.ds(row0, tq), :] = ik_ref[...].astype(_BF)
    kb_ref[pl.ds(row0, tq), :] = k_ref[...].astype(_BF)
    vb_ref[pl.ds(row0, tq), :] = jnp.concatenate([v_ref[...].astype(_BF), jnp.ones((tq, HEAD_DIM), _BF)], axis=1)
    iqs = _stack_heads(iq_ref[...], n_idx).astype(_BF)
    iw_t = iw_ref[...].T

    def key_pos(kt, n):
        return kt * n + lax.broadcasted_iota(jnp.int32, (n, 1), 0)

    def score_tile(kt, _):
        sl = pl.ds(pl.multiple_of(kt * tq, tq), tq)
        logits = lax.dot_general(ikb_ref[sl, :], iqs, _NT, preferred_element_type=jnp.float32)
        sc = jnp.zeros((tq, tq), jnp.float32)
        for h in range(n_idx):
            sc = sc + (jnp.maximum(logits[:, h * tq:(h + 1) * tq] * IDX_DIM ** -0.5, 0.0)
                       * iw_t[iw_lane + h:iw_lane + h + 1, :])
        score_ref[sl, :] = jnp.where(key_pos(kt, tq) <= t_row, sc * n_idx ** -0.5, NEG)
        return 0

    lax.fori_loop(0, n_tiles, score_tile, 0)
    n_tiles128 = n_tiles * (tq // LANES)
    _topk_mask_keys_major(score_ref, key_ref, tri_ref, n_tiles128, n_top)
    mask_ref[...] = jnp.zeros(mask_ref.shape, mask_ref.dtype)

    def to_rows(t, _):
        sl = pl.ds(pl.multiple_of(t * LANES, LANES), LANES)
        mask_ref[:, sl] = jnp.where(key_pos(t, LANES) <= t_row, score_ref[sl, :], 0.0).T
        return 0

    lax.fori_loop(0, n_tiles128, to_rows, 0)

    def dsa_mask(c, j):
        return mask_ref[:, pl.ds(pl.multiple_of(c * chunk + j * LANES, LANES), LANES)] > 0.5

    qs = _stack_heads(q_ref[...], n_heads).astype(_BF)
    o = _attend(qs, kb_ref, vb_ref, 0, (row0 + tq + chunk - 1) // chunk, chunk, dsa_mask, n_heads)
    for h in range(n_heads):
        o_ref[:, h * HEAD_DIM:(h + 1) * HEAD_DIM] = o[h].astype(o_ref.dtype)


def dsa_prompt(proj, lay, n_batch, seq, tq, chunk=512):
    assert tq % LANES == 0 and seq % chunk == 0 and chunk % tq == 0
    nq = seq // tq
    n_top = min(DSA_TOPK, seq // 4)
    iqw = lay.n_idx_heads * IDX_DIM
    qw = lay.n_heads_b * HEAD_DIM

    def tile(col):
        c0 = col // HEAD_DIM
        return pl.BlockSpec((tq, HEAD_DIM), lambda b, i: (b * nq + i, c0))

    return pl.pallas_call(
        functools.partial(_dsa_prompt_kernel, n_idx=lay.n_idx_heads, n_heads=lay.n_heads_b, n_top=n_top,
                          iw_lane=lay.iw % LANES, chunk=chunk),
        grid=(n_batch, nq),
        in_specs=[pl.BlockSpec((tq, iqw), lambda b, i: (b * nq + i, lay.iq // iqw)),
                  pl.BlockSpec((tq, LANES), lambda b, i: (b * nq + i, lay.iw // LANES)),
                  pl.BlockSpec((tq, qw), lambda b, i: (b * nq + i, lay.qb // qw)),
                  tile(lay.ik), tile(lay.dk), tile(lay.dv),
                  pl.BlockSpec((LANES, LANES), lambda b, i: (0, 0))],
        out_specs=pl.BlockSpec((tq, qw), lambda b, i: (b * nq + i, 0)),
        out_shape=jax.ShapeDtypeStruct((n_batch * seq, qw), _BF),
        scratch_shapes=[pltpu.VMEM((seq, tq), jnp.float32), pltpu.VMEM((seq, tq), jnp.int32),
                        pltpu.VMEM((tq, seq), jnp.float32),
                        pltpu.VMEM((seq, IDX_DIM), _BF), pltpu.VMEM((seq, HEAD_DIM), _BF),
                        pltpu.VMEM((seq, 2 * HEAD_DIM), _BF)],
        compiler_params=_params("parallel", "arbitrary"),
    )(proj, proj, proj, proj, proj, proj, _tri_matrix().T)


def _page_specs(rows, per_step=None):
    per_step = PAGES_PER_STEP if per_step is None else per_step
    def spec(k):
        return pl.BlockSpec((1, rows, LANES), lambda b, s, pt: (pt[b, s * per_step + k], 0, 0))
    return [spec(k) for k in range(per_step)]


def _rows_view(x, lead):
    return x.reshape(x.shape[:lead] + (-1, x.shape[-1]))


def _kind_rows(ref, kind, n_kinds, n_rows):
    return ref.at[0][pl.ds(kind, n_rows, stride=n_kinds), :]


NSA_PAGES_PER_STEP = 8


def _nsa_sample_kernel(pt_ref, *refs, n_heads, n_s, past_len):
    pages = refs[:NSA_PAGES_PER_STEP]
    (q_ref, gate_ref, nsk_ref, nsv_ref, nwk_ref, nwv_ref, win_ref, w1_ref, w2_ref, pe_ref, cos_ref, sin_ref,
     cover_ref, expand_ref, o_ref, rows_ref, selk_ref, selv_ref) = refs[NSA_PAGES_PER_STEP:]
    step = pl.program_id(1)
    ts = q_ref.shape[0]
    qw = n_heads * HEAD_DIM
    n_grp = N_KV_A
    n_kinds = 4 * n_grp
    page = pages[0].shape[1] // n_kinds
    t_col = past_len + lax.broadcasted_iota(jnp.int32, (ts, 1), 0)

    for k in range(NSA_PAGES_PER_STEP):
        start = pl.multiple_of((step * NSA_PAGES_PER_STEP + k) * page, page)
        for c in range(2 * n_grp):
            rows_ref.at[c][pl.ds(start, page), :] = _kind_rows(pages[k], c, n_kinds, page)
        for g in range(n_grp):
            selk_ref.at[g][pl.ds(start, page), :] = _kind_rows(pages[k], 2 * n_grp + g, n_kinds, page).astype(_BF)
            selv_ref.at[g][pl.ds(start, page), :] = _kind_rows(pages[k], 3 * n_grp + g, n_kinds, page).astype(_BF)

    @pl.when(step == pl.num_programs(1) - 1)
    def _():
        n_ch = cover_ref.shape[0]
        gates = jax.nn.sigmoid(gate_ref[...])
        n_all = n_grp * n_heads
        w_len = win_ref.shape[1] // (2 * n_grp)
        row = lax.broadcasted_iota(jnp.int32, (ts, LANES), 0)
        lane = lax.broadcasted_iota(jnp.int32, (ts, LANES), 1)
        new_causal = (lane <= row) & (lane < ts)
        wlane = lax.broadcasted_iota(jnp.int32, (ts, w_len + LANES), 1)
        k_pos = past_len - w_len + wlane
        d = t_col - k_pos
        win_mask = (d >= 0) & (d < WINDOW) & (k_pos >= 0) & (wlane < w_len + ts)
        new_blk = past_len // SEL_BLOCK

        def group_lanes(x, g):
            return x[:, g * HEAD_DIM:(g + 1) * HEAD_DIM]

        for g in range(n_grp):
            qs = _stack_heads(q_ref[:, g * qw:(g + 1) * qw], n_heads).astype(_BF)
            kc, vc = [_compress_rows(lambda s: rows_ref.at[kv * n_grp + g][pl.ds(s, n_ch, stride=CMP_STRIDE), :],
                                     pe_ref[kv], lambda r: w1_ref[kv, r], w2_ref[kv], n_ch) for kv in range(2)]
            kc = _rotary(kc, cos_ref[...], sin_ref[...])
            o_cmp, sel = _cmp_attend_and_choose(qs, kc.astype(_BF), vc.astype(_BF), cover_ref[...], t_col, n_heads, n_s)
            sel_f = jnp.where(sel, 1.0, 0.0)
            chosen = jnp.dot(sel_f.astype(_BF), expand_ref[...], preferred_element_type=jnp.float32) > 0.5
            carry = _flash_update(qs, selk_ref[g], selv_ref[g], chosen, _flash_init(n_heads, ts), n_heads)
            in_new = sel_f[:, new_blk:new_blk + 1] > 0.5
            k_new = _pad_rows(group_lanes(nsk_ref[...], g), LANES).astype(_BF)
            v_new = _pad_rows(group_lanes(nsv_ref[...], g), LANES).astype(_BF)
            o_slc = _flash_finish(_flash_update(qs, k_new, v_new, new_causal & in_new, carry, n_heads))
            kw = jnp.concatenate([_kind_rows(win_ref, g, 2 * n_grp, w_len),
                                  _pad_rows(group_lanes(nwk_ref[...], g), LANES)], axis=0).astype(_BF)
            vw = jnp.concatenate([_kind_rows(win_ref, n_grp + g, 2 * n_grp, w_len),
                                  _pad_rows(group_lanes(nwv_ref[...], g), LANES)], axis=0).astype(_BF)
            o_win = _flash_finish(_flash_update(qs, kw, vw, win_mask, _flash_init(n_heads, ts), n_heads))
            for r in range(n_heads):
                h = g * n_heads + r
                o = (gates[:, h:h + 1] * o_cmp[r] + gates[:, n_all + h:n_all + h + 1] * o_slc[r]
                     + gates[:, 2 * n_all + h:2 * n_all + h + 1] * o_win[r])
                o_ref[0, :, h * HEAD_DIM:(h + 1) * HEAD_DIM] = o.astype(o_ref.dtype)


def nsa_sample(proj, row0, cache, win_buf, page_table, w1, w2, pe, cos_end, sin_end, lay, ts):
    n_batch, n_pages = page_table.shape
    page = cache.shape[1]
    past_len = n_pages * page
    pool = _rows_view(cache, 1)
    win_rows = _rows_view(win_buf, 1)
    n_heads = lay.n_heads_a // N_KV_A
    n_ch = past_len // CMP_STRIDE
    n_s = -(-(past_len + ts) // SEL_BLOCK)
    sel_lanes = -(-n_s // LANES) * LANES
    kvw = N_KV_A * HEAD_DIM
    assert n_pages % NSA_PAGES_PER_STEP == 0 and row0 % ts == 0 and ts <= SEL_BLOCK and past_len % SEL_BLOCK == 0
    assert ts % 8 == 0 and lay.ga % LANES == 0
    assert (past_len + ts - CMP_BLOCK) // CMP_STRIDE + 1 <= n_ch and n_ch * CMP_STRIDE <= past_len
    r0 = row0 // ts
    qw = lay.n_heads_a * HEAD_DIM
    rows = lambda width, col: pl.BlockSpec((ts, width), lambda b, s, pt: (r0 + b, col // width))
    once = lambda shape: pl.BlockSpec(shape, lambda b, s, pt: (0,) * len(shape), pipeline_mode=pl.Buffered(1))
    grid_spec = pltpu.PrefetchScalarGridSpec(
        num_scalar_prefetch=1,
        grid=(n_batch, n_pages // NSA_PAGES_PER_STEP),
        in_specs=_page_specs(pool.shape[1], NSA_PAGES_PER_STEP) + [
            rows(qw, lay.qa), rows(LANES, lay.ga),
            rows(kvw, lay.selk), rows(kvw, lay.selv), rows(kvw, lay.wink), rows(kvw, lay.winv),
            pl.BlockSpec((1,) + win_rows.shape[1:], lambda b, s, pt: (b, 0, 0)),
            once(w1.shape), once(w2.shape), once(pe.shape), once((n_ch, HEAD_DIM)), once((n_ch, HEAD_DIM)),
            once((n_ch, sel_lanes)), once((sel_lanes, past_len))],
        out_specs=pl.BlockSpec((1, ts, qw), lambda b, s, pt: (b, 0, 0)),
        scratch_shapes=[pltpu.VMEM((2 * N_KV_A, past_len, HEAD_DIM), jnp.float32),
                        pltpu.VMEM((N_KV_A, past_len, HEAD_DIM), _BF),
                        pltpu.VMEM((N_KV_A, past_len, HEAD_DIM), _BF)],
    )
    return pl.pallas_call(
        functools.partial(_nsa_sample_kernel, n_heads=n_heads, n_s=n_s, past_len=past_len),
        grid_spec=grid_spec,
        out_shape=jax.ShapeDtypeStruct((n_batch, ts, qw), _BF),
        compiler_params=_params("parallel", "arbitrary"),
    )(page_table, *([pool] * NSA_PAGES_PER_STEP), proj, proj, proj, proj, proj, proj, win_rows,
      w1, w2, pe, cos_end, sin_end, _cover_matrix(n_ch, sel_lanes, n_s), _expand_matrix(sel_lanes, past_len))


def _dsa_sample_select_kernel(pt_ref, *refs, n_idx, n_top, iw_lane, past_len):
    pages = refs[:PAGES_PER_STEP]
    iq_ref, iw_ref, nik_ref, tri_ref, mask_ref, score_ref, key_ref = refs[PAGES_PER_STEP:]
    step = pl.program_id(1)
    ts = iq_ref.shape[0]
    keys_per_step = PAGES_PER_STEP * pages[0].shape[1]
    iqs = _stack_heads(iq_ref[...], n_idx).astype(_BF)
    iw = iw_ref[...]
    ik = jnp.concatenate([p[0] for p in pages], axis=0).astype(_BF)
    start = pl.multiple_of(step * keys_per_step, keys_per_step)
    score_ref[:, pl.ds(start, keys_per_step)] = _indexer_scores(iqs, ik, iw, iw_lane, n_idx)

    @pl.when(step == pl.num_programs(1) - 1)
    def _():
        sc = _indexer_scores(iqs, _pad_rows(nik_ref[...], LANES).astype(_BF), iw, iw_lane, n_idx)
        row = lax.broadcasted_iota(jnp.int32, (ts, LANES), 0)
        lane = lax.broadcasted_iota(jnp.int32, (ts, LANES), 1)
        score_ref[:, past_len:past_len + LANES] = jnp.where(lane < ts, jnp.where(lane <= row, sc, NEG), -jnp.inf)
        _topk_mask(score_ref, key_ref, mask_ref.at[0], tri_ref, past_len // LANES + 1, n_top)


def dsa_sample_select(proj, row0, cache_idx, page_table, lay, ts):
    n_batch, n_pages = page_table.shape
    page = cache_idx.shape[1]
    past_len = n_pages * page
    assert n_pages % PAGES_PER_STEP == 0 and row0 % ts == 0 and ts <= LANES and past_len % LANES == 0
    n_top = min(DSA_TOPK, (past_len + ts) // 4)
    r0 = row0 // ts
    iqw = lay.n_idx_heads * IDX_DIM
    width = past_len + LANES
    rows = lambda w, col: pl.BlockSpec((ts, w), lambda b, s, pt: (r0 + b, col // w))
    grid_spec = pltpu.PrefetchScalarGridSpec(
        num_scalar_prefetch=1,
        grid=(n_batch, n_pages // PAGES_PER_STEP),
        in_specs=_page_specs(page) + [
            rows(iqw, lay.iq), rows(LANES, lay.iw), rows(IDX_DIM, lay.ik),
            pl.BlockSpec((LANES, LANES), lambda b, s, pt: (0, 0))],
        out_specs=pl.BlockSpec((1, ts, width), lambda b, s, pt: (b, 0, 0)),
        scratch_shapes=[pltpu.VMEM((ts, width), jnp.float32), pltpu.VMEM((ts, width), jnp.int32)],
    )
    return pl.pallas_call(
        functools.partial(_dsa_sample_select_kernel, n_idx=lay.n_idx_heads, n_top=n_top,
                          iw_lane=lay.iw % LANES, past_len=past_len),
        grid_spec=grid_spec,
        out_shape=jax.ShapeDtypeStruct((n_batch, ts, width), jnp.float32),
        compiler_params=_params("parallel", "arbitrary"),
    )(page_table, *([cache_idx] * PAGES_PER_STEP), proj, proj, proj, _tri_matrix())


def _dsa_sample_attend_kernel(pt_ref, *refs, n_heads):
    pages = refs[:PAGES_PER_STEP]
    q_ref, mask_ref, nmask_ref, nk_ref, nv_ref, o_ref, m_ref, l_ref, acc_ref = refs[PAGES_PER_STEP:]
    step = pl.program_id(1)
    ts = q_ref.shape[0]
    qs = _stack_heads(q_ref[...], n_heads).astype(_BF)

    @pl.when(step == 0)
    def _():
        m_ref[...], l_ref[...], acc_ref[...] = _flash_init(n_heads, ts)

    page = pages[0].shape[1] // 2
    k = jnp.concatenate([_kind_rows(p, 0, 2, page) for p in pages], axis=0).astype(_BF)
    v = jnp.concatenate([_kind_rows(p, 1, 2, page) for p in pages], axis=0).astype(_BF)
    m_ref[...], l_ref[...], acc_ref[...] = _flash_update(qs, k, v, mask_ref[0] > 0.5,
                                                         (m_ref[...], l_ref[...], acc_ref[...]), n_heads)

    @pl.when(step == pl.num_programs(1) - 1)
    def _():
        row = lax.broadcasted_iota(jnp.int32, (ts, LANES), 0)
        lane = lax.broadcasted_iota(jnp.int32, (ts, LANES), 1)
        mask = (nmask_ref[0] > 0.5) & (lane <= row) & (lane < ts)
        o = _flash_finish(_flash_update(qs, _pad_rows(nk_ref[...], LANES).astype(_BF),
                                        _pad_rows(nv_ref[...], LANES).astype(_BF), mask,
                                        (m_ref[...], l_ref[...], acc_ref[...]), n_heads))
        for h in range(n_heads):
            o_ref[0, :, h * HEAD_DIM:(h + 1) * HEAD_DIM] = o[h].astype(o_ref.dtype)


def dsa_sample_attend(proj, row0, mask, cache_kv, page_table, lay, ts):
    n_batch, n_pages = page_table.shape
    page = cache_kv.shape[1]
    past_len = n_pages * page
    pool = _rows_view(cache_kv, 1)
    keys_per_step = PAGES_PER_STEP * page
    r0 = row0 // ts
    qw = lay.n_heads_b * HEAD_DIM
    rows = lambda w, col: pl.BlockSpec((ts, w), lambda b, s, pt: (r0 + b, col // w))
    grid_spec = pltpu.PrefetchScalarGridSpec(
        num_scalar_prefetch=1,
        grid=(n_batch, n_pages // PAGES_PER_STEP),
        in_specs=_page_specs(pool.shape[1]) + [
            rows(qw, lay.qb),
            pl.BlockSpec((1, ts, keys_per_step), lambda b, s, pt: (b, 0, s)),
            pl.BlockSpec((1, ts, LANES), lambda b, s, pt: (b, 0, past_len // LANES)),
            rows(HEAD_DIM, lay.dk), rows(HEAD_DIM, lay.dv)],
        out_specs=pl.BlockSpec((1, ts, qw), lambda b, s, pt: (b, 0, 0)),
        scratch_shapes=[pltpu.VMEM((lay.n_heads_b, ts, 1), jnp.float32),
                        pltpu.VMEM((lay.n_heads_b, ts, 1), jnp.float32),
                        pltpu.VMEM((lay.n_heads_b, ts, HEAD_DIM), jnp.float32)],
    )
    return pl.pallas_call(
        functools.partial(_dsa_sample_attend_kernel, n_heads=lay.n_heads_b),
        grid_spec=grid_spec,
        out_shape=jax.ShapeDtypeStruct((n_batch, ts, qw), _BF),
        compiler_params=_params("parallel", "arbitrary"),
    )(page_table, *([pool] * PAGES_PER_STEP), proj, mask, mask, proj, proj)


class Layout:
    def __init__(self, d_model, n_heads_a, n_heads_b, n_idx_heads):
        self.n_heads_a, self.n_heads_b, self.n_idx_heads = n_heads_a, n_heads_b, n_idx_heads
        src = np.cumsum([0, n_heads_a * HEAD_DIM, 6 * N_KV_A * HEAD_DIM, 3 * n_heads_a,
                         n_heads_b * HEAD_DIM, 2 * HEAD_DIM, n_idx_heads * IDX_DIM, n_idx_heads,
                         IDX_DIM, 2 * d_model])
        s_qa, s_kva, s_ga, s_qb, s_kvb, s_iq, s_iw, s_ik, s_mg, s_end = (int(v) for v in src)
        small = 3 * n_heads_a + n_idx_heads
        assert small <= LANES
        self.small_pad = LANES - small
        kvw = N_KV_A * HEAD_DIM
        self.pieces = [
            (s_qa, s_kva, True),
            (s_qb, s_kvb, True),
            (s_iq, s_iw, True),
            (s_kva, s_kva + 2 * kvw, False),
            (s_kva + 2 * kvw, s_kva + 3 * kvw, True),
            (s_kva + 3 * kvw, s_kva + 4 * kvw, False),
            (s_kva + 4 * kvw, s_kva + 5 * kvw, True),
            (s_kva + 5 * kvw, s_ga, False),
            (s_kvb, s_kvb + HEAD_DIM, True),
            (s_kvb + HEAD_DIM, s_iq, False),
            (s_ik, s_mg, True),
            (s_ga, s_qb, False),
            (s_iw, s_ik, False),
            None,
            (s_mg, s_end, False),
        ]
        off = 0
        starts = []
        for p in self.pieces:
            starts.append(off)
            off += self.small_pad if p is None else p[1] - p[0]
        (self.qa, self.qb, self.iq, self.cmp, self.selk, self.selv, self.wink, self.winv, self.dk,
         self.dv, self.ik, self.ga, self.iw, _, self.mg) = starts
        self.width = off
        assert self.width % LANES == 0
        flags = np.zeros(self.width // LANES, np.int32)
        for st, p in zip(starts, self.pieces):
            if p is not None and p[2]:
                assert st % LANES == 0 and (p[1] - p[0]) % LANES == 0
                flags[st // LANES:(st + p[1] - p[0]) // LANES] = 1
        self.rope_flags = flags

    def pack(self, w_in):
        cols = []
        for p in self.pieces:
            if p is None:
                cols.append(jnp.zeros((w_in.shape[0], self.small_pad), w_in.dtype))
            else:
                cols.append(w_in[:, p[0]:p[1]])
        return jnp.concatenate(cols, axis=1).astype(_BF)


def rope_tables(pos):
    half = HEAD_DIM // 2
    inv = ROPE_THETA ** (-jnp.arange(half, dtype=jnp.float32) / half)
    ang = pos.astype(jnp.float32)[:, None] * inv[None, :]
    cos, sin = jnp.cos(ang), jnp.sin(ang)
    return jnp.concatenate([cos, cos], axis=1), jnp.concatenate([-sin, sin], axis=1)


def _swiglu_half_step(h, g, wg, wu, wd, w_index):
    xn = rmsnorm(h, g, _BF)
    m = h.shape[0]
    a = ffn_gate_up(xn, wg, wu, w_index, tm=_row_tile(m, ROW_TILE_WIDE), tn=2 * LANES)
    return resid_matmul(a, wd, w_index, h, 0.5, tm=_row_tile(m, ROW_TILE_DOWN), tn=2 * LANES)


def kernel(x_prompt, x_sample, cache_nsa_kv, cache_nsa_win, cache_dsa_kv, cache_dsa_idx, page_table,
           g_norm, w_ffn_gate, w_ffn_up, w_ffn_down, w_in, w_cmp1, w_cmp2, cmp_pos,
           w_br_a, w_br_b, w_out, g_final):
    B, T, D = x_prompt.shape
    DB, Ts, _ = x_sample.shape
    depth = g_norm.shape[0]
    page = cache_nsa_kv.shape[2]
    past_len = page_table.shape[1] * page
    n_heads_a = w_br_a.shape[1] // HEAD_DIM
    n_heads_b = w_br_b.shape[1] // HEAD_DIM
    G = N_KV_A
    lay = Layout(D, n_heads_a, n_heads_b, n_heads_b // 2)
    Mp, Ms = B * T, DB * Ts
    tm_wide = _row_tile(Mp + Ms, ROW_TILE_WIDE)

    pos_p = jnp.arange(T, dtype=jnp.int32)
    pos_s = past_len + jnp.arange(Ts, dtype=jnp.int32)
    cos, sin = rope_tables(jnp.concatenate([jnp.tile(pos_p, B), jnp.tile(pos_s, DB)]))
    rope_flags = jnp.asarray(lay.rope_flags)

    def block_end_tables(n_ch):
        return rope_tables(jnp.arange(n_ch, dtype=jnp.int32) * CMP_STRIDE + CMP_BLOCK - 1)

    h = jnp.concatenate([x_prompt.reshape(Mp, D), x_sample.reshape(Ms, D)], axis=0)
    outs = [[] for _ in range(8)]
    for l in range(depth):
        h = _swiglu_half_step(h, g_norm[l, 0], w_ffn_gate, w_ffn_up, w_ffn_down, (l, 0))

        u = rmsnorm(h, g_norm[l, 1], _BF)
        proj, nsa_rows, win_rows, dsa_rows, ik_rows = in_project(u, lay.pack(w_in[l]), rope_flags, cos, sin, lay,
                                                                 tm=tm_wide, tn=4 * LANES)
        w1 = w_cmp1[l].reshape(2, 2, CMP_STRIDE * HEAD_DIM, -1).astype(_BF)
        w2 = w_cmp2[l].astype(_BF)

        kvc_p = compress_prompt(proj, B, T, lay.cmp, w1, w2, cmp_pos[l], *block_end_tables(T // CMP_STRIDE))
        o_a_p = nsa_prompt(proj, kvc_p, lay, B, T, tq=Q_TILE)
        o_b_p = dsa_prompt(proj, lay, B, T, tq=Q_TILE)

        o_a_s = nsa_sample(proj, Mp, cache_nsa_kv[l], cache_nsa_win[l], page_table, w1, w2, cmp_pos[l],
                           *block_end_tables(past_len // CMP_STRIDE), lay, Ts)
        top_mask = dsa_sample_select(proj, Mp, cache_dsa_idx[l], page_table, lay, Ts)
        o_b_s = dsa_sample_attend(proj, Mp, top_mask, cache_dsa_kv[l], page_table, lay, Ts)

        o_a = jnp.concatenate([o_a_p, o_a_s.reshape(Ms, -1)], axis=0)
        o_b = jnp.concatenate([o_b_p, o_b_s.reshape(Ms, -1)], axis=0)
        m = merge_branches(o_a, o_b, w_br_a, w_br_b, (l,), proj, lay.mg, tm=tm_wide, tn=4 * LANES)
        h = resid_matmul(m, w_out, (l,), h, 1.0, tm=tm_wide, tn=4 * LANES)

        win_p = win_rows[:Mp].reshape(B, T, 2, G, HEAD_DIM)
        win_s = win_rows[Mp:].reshape(DB, Ts, 2, G, HEAD_DIM)
        outs[0].append(nsa_rows[:Mp].reshape(B, T, 4, G, HEAD_DIM))
        outs[1].append(win_p[:, T - min(WINDOW, T):])
        outs[2].append(dsa_rows[:Mp].reshape(B, T, 2, HEAD_DIM))
        outs[3].append(ik_rows[:Mp].reshape(B, T, IDX_DIM))
        outs[4].append(nsa_rows[Mp:].reshape(DB, Ts, 4, G, HEAD_DIM))
        outs[5].append(jnp.concatenate([cache_nsa_win[l], win_s], axis=1)[:, Ts:])
        outs[6].append(dsa_rows[Mp:].reshape(DB, Ts, 2, HEAD_DIM))
        outs[7].append(ik_rows[Mp:].reshape(DB, Ts, IDX_DIM))

        h = _swiglu_half_step(h, g_norm[l, 2], w_ffn_gate, w_ffn_up, w_ffn_down, (l, 1))

    y_p = rmsnorm(h, g_final, jnp.float32, 0, Mp)
    y_s = rmsnorm(h, g_final, jnp.float32, Mp, Ms)
    return (y_p.reshape(B, T, D), y_s.reshape(DB, Ts, D), *(jnp.stack(o) for o in outs))
```

```python
import functools

import jax
import jax.numpy as jnp
import numpy as np
from jax import lax
from jax.experimental import pallas as pl
from jax.experimental.pallas import tpu as pltpu

HEAD_DIM = 128
N_KV_A = 2
IDX_DIM = 128
CMP_BLOCK = 32
CMP_STRIDE = 16
SEL_BLOCK = 64
N_SEL = 16
N_LOCAL_SEL = 2
WINDOW = 512
DSA_TOPK = 256
ROPE_THETA = 10000.0
RMS_EPS = 1e-6
NEG = -1e30
BIG = 1e30
TINY = 1e-30

LANES = 128
VMEM_LIMIT = 56 * 1024 * 1024
PAGES_PER_STEP = 64

_NT = (((1,), (1,)), ((), ()))
_BF = jnp.bfloat16


ROW_TILE_WIDE = 1408
ROW_TILE_DOWN = 1056
Q_TILE = 2 * LANES


def _params(*sem):
    return pltpu.CompilerParams(dimension_semantics=sem, vmem_limit_bytes=VMEM_LIMIT)


def _row_tile(m, limit):
    return max(t for t in range(16, limit + 1, 16) if m % t == 0)


def _rmsnorm_kernel(x_ref, g_ref, o_ref):
    x = x_ref[...]
    y = x * lax.rsqrt(jnp.mean(x * x, axis=-1, keepdims=True) + RMS_EPS)
    o_ref[...] = (y * g_ref[...]).astype(o_ref.dtype)


def rmsnorm(x, g, out_dtype, row0=0, n_rows=None, tm=256):
    D = x.shape[1]
    n_rows = x.shape[0] - row0 if n_rows is None else n_rows
    assert row0 % tm == 0 and n_rows % tm == 0
    return pl.pallas_call(
        _rmsnorm_kernel,
        grid=(n_rows // tm,),
        in_specs=[pl.BlockSpec((tm, D), lambda i: (row0 // tm + i, 0)),
                  pl.BlockSpec((1, D), lambda i: (0, 0))],
        out_specs=pl.BlockSpec((tm, D), lambda i: (i, 0)),
        out_shape=jax.ShapeDtypeStruct((n_rows, D), out_dtype),
        compiler_params=_params("parallel"),
    )(x, g.reshape(1, D))


def _gateup_kernel(x_ref, wg_ref, wu_ref, o_ref):
    x = x_ref[...]
    g = jnp.dot(x, wg_ref[...].astype(_BF), preferred_element_type=jnp.float32)
    u = jnp.dot(x, wu_ref[...].astype(_BF), preferred_element_type=jnp.float32)
    o_ref[...] = (g * jax.nn.sigmoid(g) * u).astype(o_ref.dtype)


def _weight_spec(w, w_index, rows, tn):
    assert w.ndim == len(w_index) + 2 and w.shape[-2] == rows
    return pl.BlockSpec((None,) * len(w_index) + (rows, tn), lambda i, j: tuple(w_index) + (0, j))


def ffn_gate_up(xn, wg, wu, w_index, tm, tn):
    M, D = xn.shape
    F = wg.shape[-1]
    return pl.pallas_call(
        _gateup_kernel,
        grid=(M // tm, F // tn),
        in_specs=[pl.BlockSpec((tm, D), lambda i, j: (i, 0)),
                  _weight_spec(wg, w_index, D, tn),
                  _weight_spec(wu, w_index, D, tn)],
        out_specs=pl.BlockSpec((tm, tn), lambda i, j: (i, j)),
        out_shape=jax.ShapeDtypeStruct((M, F), _BF),
        compiler_params=_params("parallel", "parallel"),
    )(xn, wg, wu)


def _resid_matmul_kernel(a_ref, w_ref, r_ref, o_ref, *, scale):
    acc = jnp.dot(a_ref[...], w_ref[...].astype(_BF), preferred_element_type=jnp.float32)
    o_ref[...] = r_ref[...] + scale * acc


def resid_matmul(a, w, w_index, resid, scale, tm, tn):
    M, K = a.shape
    N = w.shape[-1]
    return pl.pallas_call(
        functools.partial(_resid_matmul_kernel, scale=scale),
        grid=(M // tm, N // tn),
        in_specs=[pl.BlockSpec((tm, K), lambda i, j: (i, 0), pipeline_mode=pl.Buffered(1)),
                  _weight_spec(w, w_index, K, tn),
                  pl.BlockSpec((tm, tn), lambda i, j: (i, j))],
        out_specs=pl.BlockSpec((tm, tn), lambda i, j: (i, j)),
        out_shape=jax.ShapeDtypeStruct((M, N), jnp.float32),
        compiler_params=_params("parallel", "parallel"),
    )(a, w, resid)


def _rotary(y, cos, sin):
    return y * cos + pltpu.roll(y, HEAD_DIM // 2, axis=1) * sin


def _inproj_kernel(flags_ref, x_ref, w_ref, cos_ref, sin_ref, o_ref, *row_refs, n_chunks, routes):
    j = pl.program_id(1)
    acc = jnp.dot(x_ref[...], w_ref[...], preferred_element_type=jnp.float32)
    for c in range(n_chunks):
        sl = slice(c * LANES, (c + 1) * LANES)
        y = acc[:, sl]
        flag = flags_ref[j * n_chunks + c]

        @pl.when(flag == 1)
        def _():
            o_ref[:, sl] = _rotary(y, cos_ref[...], sin_ref[...])

        @pl.when(flag == 0)
        def _():
            o_ref[:, sl] = y

    tm = o_ref.shape[0]
    for chunk, out_idx, kind, n_kinds in routes:
        @pl.when(j == chunk // n_chunks)
        def _():
            c = chunk % n_chunks
            row_refs[out_idx][pl.ds(kind, tm, stride=n_kinds), :] = o_ref[:, c * LANES:(c + 1) * LANES]


def in_project(u, w, rope_flags, cos, sin, lay, tm, tn):
    M, D = u.shape
    N = w.shape[1]
    n_chunks = tn // LANES
    G = N_KV_A
    kinds = [4 * G, 2 * G, 2, 1]
    routes = ([(lay.cmp // LANES + k, 0, k, kinds[0]) for k in range(kinds[0])]
              + [(lay.wink // LANES + k, 1, k, kinds[1]) for k in range(kinds[1])]
              + [(lay.dk // LANES + k, 2, k, kinds[2]) for k in range(kinds[2])]
              + [(lay.ik // LANES, 3, 0, 1)])
    row_shapes = [(M, 4, G, HEAD_DIM), (M, 2, G, HEAD_DIM), (M, 2, HEAD_DIM), (M, IDX_DIM)]
    grid_spec = pltpu.PrefetchScalarGridSpec(
        num_scalar_prefetch=1,
        grid=(M // tm, N // tn),
        in_specs=[pl.BlockSpec((tm, D), lambda i, j, f: (i, 0), pipeline_mode=pl.Buffered(1)),
                  pl.BlockSpec((D, tn), lambda i, j, f: (0, j)),
                  pl.BlockSpec((tm, LANES), lambda i, j, f: (i, 0)),
                  pl.BlockSpec((tm, LANES), lambda i, j, f: (i, 0))],
        out_specs=[pl.BlockSpec((tm, tn), lambda i, j, f: (i, j))]
        + [pl.BlockSpec((tm * nk, LANES), lambda i, j, f: (i, 0)) for nk in kinds],
    )
    proj, *rows = pl.pallas_call(
        functools.partial(_inproj_kernel, n_chunks=n_chunks, routes=routes),
        grid_spec=grid_spec,
        out_shape=[jax.ShapeDtypeStruct((M, N), jnp.float32)]
        + [jax.ShapeDtypeStruct((M * nk, LANES), jnp.float32) for nk in kinds],
        compiler_params=_params("parallel", "arbitrary"),
    )(rope_flags, u, w, cos, sin)
    return (proj, *(r.reshape(shp) for r, shp in zip(rows, row_shapes)))


def _merge_kernel(oa_ref, ob_ref, wa_ref, wb_ref, ga_ref, gb_ref, o_ref):
    ya = jnp.dot(oa_ref[...], wa_ref[...].astype(_BF), preferred_element_type=jnp.float32)
    yb = jnp.dot(ob_ref[...], wb_ref[...].astype(_BF), preferred_element_type=jnp.float32)
    m = jax.nn.sigmoid(ga_ref[...]) * ya + jax.nn.sigmoid(gb_ref[...]) * yb
    o_ref[...] = m.astype(o_ref.dtype)


def merge_branches(o_a, o_b, w_a, w_b, w_index, proj, mg_col, tm, tn):
    M, K = o_a.shape
    N = w_a.shape[-1]
    ja = mg_col // tn
    jb = (mg_col + N) // tn
    return pl.pallas_call(
        _merge_kernel,
        grid=(M // tm, N // tn),
        in_specs=[pl.BlockSpec((tm, K), lambda i, j: (i, 0), pipeline_mode=pl.Buffered(1)),
                  pl.BlockSpec((tm, K), lambda i, j: (i, 0), pipeline_mode=pl.Buffered(1)),
                  _weight_spec(w_a, w_index, K, tn),
                  _weight_spec(w_b, w_index, K, tn),
                  pl.BlockSpec((tm, tn), lambda i, j: (i, ja + j)),
                  pl.BlockSpec((tm, tn), lambda i, j: (i, jb + j))],
        out_specs=pl.BlockSpec((tm, tn), lambda i, j: (i, j)),
        out_shape=jax.ShapeDtypeStruct((M, N), _BF),
        compiler_params=_params("parallel", "parallel"),
    )(o_a, o_b, w_a, w_b, proj, proj)


def _stack_heads(x, n_heads):
    return jnp.concatenate([x[:, h * HEAD_DIM:(h + 1) * HEAD_DIM] for h in range(n_heads)], axis=0)


def _flash_update(qs, k, v, mask, carry, n_rep):
    m, l, acc = carry
    rows, tk = qs.shape[0], k.shape[0]
    tq = rows // n_rep
    scale = HEAD_DIM ** -0.5
    s = lax.dot_general(qs, k, _NT, preferred_element_type=jnp.float32).reshape(n_rep, tq, tk)
    mask = mask[None]
    s = jnp.where(mask, s, NEG)
    m_new = jnp.maximum(m, jnp.max(s, axis=-1, keepdims=True))
    p = jnp.where(mask, jnp.exp((s - m_new) * scale), 0.0)
    alpha = jnp.exp((m - m_new) * scale)
    l = alpha * l + jnp.sum(p, axis=-1, keepdims=True)
    pv = jnp.dot(p.reshape(rows, tk).astype(_BF), v, preferred_element_type=jnp.float32)
    return m_new, l, alpha * acc + pv.reshape(n_rep, tq, HEAD_DIM)


def _flash_init(n_rep, tq):
    return (jnp.full((n_rep, tq, 1), NEG, jnp.float32), jnp.zeros((n_rep, tq, 1), jnp.float32),
            jnp.zeros((n_rep, tq, HEAD_DIM), jnp.float32))


def _flash_finish(carry):
    _, l, acc = carry
    return acc * (1.0 / jnp.maximum(l, TINY))


def _attend(qs, k_ref, v_ref, start, n_chunks, chunk, mask_fn, n_rep):
    rows = qs.shape[0]
    tq = rows // n_rep
    n_lane_tiles = chunk // LANES
    scale = HEAD_DIM ** -0.5

    def scores(c):
        sl = pl.ds(pl.multiple_of(start + c * chunk, LANES), chunk)
        return lax.dot_general(qs, k_ref[sl, :], _NT, preferred_element_type=jnp.float32), sl

    def lane_tile(s, j):
        return s[:, j * LANES:(j + 1) * LANES].reshape(n_rep, tq, LANES)

    def row_max(c, mx):
        s, _ = scores(c)
        for j in range(n_lane_tiles):
            mx = jnp.maximum(mx, jnp.where(mask_fn(c, j)[None], lane_tile(s, j), NEG))
        return mx

    mx = lax.fori_loop(0, n_chunks, row_max, jnp.full((n_rep, tq, LANES), NEG, jnp.float32))
    m = jnp.broadcast_to(jnp.max(mx, axis=-1, keepdims=True), mx.shape)

    def accumulate(c, acc):
        s, sl = scores(c)
        p = [jnp.where(mask_fn(c, j)[None], jnp.exp((lane_tile(s, j) - m) * scale), 0.0)
             .astype(_BF).reshape(rows, LANES) for j in range(n_lane_tiles)]
        return acc + jnp.dot(jnp.concatenate(p, axis=1), v_ref[sl, :], preferred_element_type=jnp.float32)

    acc = lax.fori_loop(0, n_chunks, accumulate, jnp.zeros((rows, 2 * HEAD_DIM), jnp.float32))
    out = acc[:, :HEAD_DIM] * (1.0 / jnp.maximum(acc[:, HEAD_DIM:], TINY))
    return out.reshape(n_rep, tq, HEAD_DIM)


def _compress_rows(load, pe, w1, w2, n_ch):
    rows = [load(s) for s in range(CMP_STRIDE)]
    half = [jnp.concatenate([(rows[s] + pe[r * CMP_STRIDE + s:r * CMP_STRIDE + s + 1, :]).astype(_BF)
                             for s in range(CMP_STRIDE)], axis=1) for r in range(2)]
    h0 = jnp.dot(half[0], w1(0), preferred_element_type=jnp.float32)
    h1 = jnp.dot(half[1], w1(1), preferred_element_type=jnp.float32)
    h = h0 + pltpu.roll(h1, n_ch - 1, axis=0)
    return jnp.dot((h * jax.nn.sigmoid(h)).astype(_BF), w2, preferred_element_type=jnp.float32)


def _cmp_attend(qs, kc, vc, t_col, n_heads):
    rows = qs.shape[0]
    tq = rows // n_heads
    n_ch = kc.shape[0]
    scale = HEAD_DIM ** -0.5
    s = lax.dot_general(qs, kc, _NT, preferred_element_type=jnp.float32).reshape(n_heads, tq, n_ch)
    end = lax.broadcasted_iota(jnp.int32, (tq, n_ch), 1) * CMP_STRIDE + (CMP_BLOCK - 1)
    cmask = (end <= t_col)[None]
    s = jnp.where(cmask, s, NEG)
    m = jnp.max(s, axis=-1, keepdims=True)
    p = jnp.where(cmask, jnp.exp((s - m) * scale), 0.0)
    p = p * (1.0 / jnp.maximum(jnp.sum(p, axis=-1, keepdims=True), TINY))
    o_cmp = jnp.dot(p.reshape(rows, n_ch).astype(_BF), vc, preferred_element_type=jnp.float32)
    return o_cmp.reshape(n_heads, tq, HEAD_DIM), jnp.sum(p, axis=0)


def _cmp_attend_and_choose(qs, kc, vc, cover, t_col, n_heads, n_s):
    o_cmp, psum = _cmp_attend(qs, kc, vc, t_col, n_heads)
    tq, lanes = psum.shape[0], cover.shape[1]
    imp = jnp.dot(psum.astype(_BF), cover, preferred_element_type=jnp.float32)

    lane = lax.broadcasted_iota(jnp.int32, (tq, lanes), 1)
    lane_f = lane.astype(jnp.float32)
    jt = lax.shift_right_arithmetic(t_col, jnp.int32(SEL_BLOCK.bit_length() - 1))
    adm = lane <= jt
    forced = adm & ((lane == 0) | (lane > jt - N_LOCAL_SEL))
    work = jnp.where(forced, BIG, jnp.where(adm, imp, NEG))
    work = jnp.where(lane < n_s, work, -jnp.inf)
    sel = jnp.zeros((tq, lanes), jnp.bool_)
    for _ in range(min(N_SEL, n_s)):
        mx = jnp.max(work, axis=-1, keepdims=True)
        first = jnp.min(jnp.where(work == mx, lane_f, float(lanes)), axis=-1, keepdims=True)
        pick = lane_f == first
        sel = sel | pick
        work = jnp.where(pick, -jnp.inf, work)
    return o_cmp, sel


def _choose_blocks_keys_major(psum, cover_t, t_row, n_s):
    tq = psum.shape[0]
    imp = jnp.dot(cover_t, psum.T.astype(_BF), preferred_element_type=jnp.float32)
    blk = lax.broadcasted_iota(jnp.int32, (LANES, tq), 0)
    blk_f = blk.astype(jnp.float32)
    jt = lax.shift_right_arithmetic(t_row, jnp.int32(SEL_BLOCK.bit_length() - 1))
    adm = blk <= jt
    forced = adm & ((blk == 0) | (blk > jt - N_LOCAL_SEL))
    work = jnp.where(forced, BIG, jnp.where(adm, imp, NEG))
    work = jnp.where(blk < n_s, work, -jnp.inf)
    sel = jnp.zeros((LANES, tq), jnp.bool_)
    for _ in range(min(N_SEL, n_s)):
        mx = jnp.max(work, axis=0, keepdims=True)
        first = jnp.min(jnp.where(work == mx, blk_f, float(LANES)), axis=0, keepdims=True)
        pick = blk_f == first
        sel = sel | pick
        work = jnp.where(pick, -jnp.inf, work)
    return jnp.where(sel, 1.0, 0.0)


def _topk_mask(score_ref, key_ref, mask_ref, tri_ref, n_tiles, n_top):
    tq = score_ref.shape[0]
    int_min = jnp.int32(-2 ** 31)

    def to_key(t, _):
        sl = pl.ds(pl.multiple_of(t * LANES, LANES), LANES)
        bits = lax.bitcast_convert_type(score_ref[:, sl], jnp.int32)
        key_ref[:, sl] = bits ^ ((bits >> 31) & jnp.int32(0x7FFFFFFF))
        return 0

    lax.fori_loop(0, n_tiles, to_key, 0)

    unroll = 8 if isinstance(n_tiles, int) else 1

    def count_ge(cand):
        def body(t, acc):
            sl = pl.ds(pl.multiple_of(t * LANES, LANES), LANES)
            return acc + jnp.where(key_ref[:, sl] >= cand, 1.0, 0.0)
        acc = lax.fori_loop(0, n_tiles, body, jnp.zeros((tq, LANES), jnp.float32), unroll=unroll)
        return jnp.sum(acc, axis=-1, keepdims=True)

    tau = jnp.zeros((tq, 1), jnp.int32)
    for bit in range(31, -1, -1):
        cand = tau | jnp.int32(-2 ** 31 if bit == 31 else 1 << bit)
        tau = jnp.where(count_ge(cand ^ int_min) >= float(n_top), cand, tau)
    thr = tau ^ int_min

    clean = jnp.max(jnp.abs(count_ge(thr) - float(n_top))) == 0.0

    @pl.when(clean)
    def _():
        def keep_ge(t, _):
            sl = pl.ds(pl.multiple_of(t * LANES, LANES), LANES)
            mask_ref[:, sl] = jnp.where(key_ref[:, sl] >= thr, 1.0, 0.0)
            return 0
        lax.fori_loop(0, n_tiles, keep_ge, 0, unroll=unroll)

    @pl.when(jnp.logical_not(clean))
    def _():
        ones = jnp.ones((LANES, LANES), _BF)

        def count_gt(t, acc):
            sl = pl.ds(pl.multiple_of(t * LANES, LANES), LANES)
            return acc + jnp.where(key_ref[:, sl] > thr, 1.0, 0.0)

        n_gt = jnp.sum(lax.fori_loop(0, n_tiles, count_gt, jnp.zeros((tq, LANES), jnp.float32)),
                       axis=-1, keepdims=True)
        need = float(n_top) - n_gt

        def cut(t, eq_before):
            sl = pl.ds(pl.multiple_of(t * LANES, LANES), LANES)
            key = key_ref[:, sl]
            eq = jnp.where(key == thr, 1.0, 0.0)
            rank = (jnp.dot(eq.astype(_BF), tri_ref[...], preferred_element_type=jnp.float32)
                    + jnp.dot(eq_before.astype(_BF), ones, preferred_element_type=jnp.float32))
            keep = (key > thr) | ((key == thr) & (rank <= need))
            mask_ref[:, sl] = jnp.where(keep, 1.0, 0.0)
            return eq_before + eq

        lax.fori_loop(0, n_tiles, cut, jnp.zeros((tq, LANES), jnp.float32))


def _topk_mask_keys_major(score_ref, key_ref, tri_t_ref, n_tiles, n_top, tiles_per_iter=1):
    tq = score_ref.shape[1]
    int_min = jnp.int32(-2 ** 31)

    def tile(t):
        return pl.ds(pl.multiple_of(t * LANES, LANES), LANES)

    def to_key(t, _):
        bits = lax.bitcast_convert_type(score_ref[tile(t), :], jnp.int32)
        key_ref[tile(t), :] = bits ^ ((bits >> 31) & jnp.int32(0x7FFFFFFF))
        return 0

    lax.fori_loop(0, n_tiles, to_key, 0)

    def count(pred):
        def body(t, acc):
            for u in range(tiles_per_iter):
                hit = jnp.where(pred(key_ref[tile(t * tiles_per_iter + u), :]), 1.0, 0.0)
                acc = acc + jnp.sum(hit.reshape(LANES // 8, 8, tq), axis=0)
            return acc
        acc = lax.fori_loop(0, n_tiles // tiles_per_iter, body, jnp.zeros((8, tq), jnp.float32))
        return jnp.sum(acc, axis=0, keepdims=True)

    tau = jnp.zeros((1, tq), jnp.int32)
    for bit in range(31, -1, -1):
        cand = tau | jnp.int32(-2 ** 31 if bit == 31 else 1 << bit)
        cand_s = cand ^ int_min
        tau = jnp.where(count(lambda key: key >= cand_s) >= float(n_top), cand, tau)
    thr = tau ^ int_min
    clean = jnp.max(jnp.abs(count(lambda key: key >= thr) - float(n_top))) == 0.0

    @pl.when(clean)
    def _():
        def keep_ge(t, _):
            score_ref[tile(t), :] = jnp.where(key_ref[tile(t), :] >= thr, 1.0, 0.0)
            return 0
        lax.fori_loop(0, n_tiles, keep_ge, 0)

    @pl.when(jnp.logical_not(clean))
    def _():
        need = float(n_top) - count(lambda key: key > thr)

        def cut(t, eq_before):
            key = key_ref[tile(t), :]
            eq = jnp.where(key == thr, 1.0, 0.0)
            rank = jnp.dot(tri_t_ref[...], eq.astype(_BF), preferred_element_type=jnp.float32) + eq_before
            keep = (key > thr) | ((key == thr) & (rank <= need))
            score_ref[tile(t), :] = jnp.where(keep, 1.0, 0.0)
            return eq_before + jnp.sum(eq, axis=0, keepdims=True)

        lax.fori_loop(0, n_tiles, cut, jnp.zeros((1, tq), jnp.float32))


def _indexer_scores(iqs, ik, iw, iw_lane, n_idx):
    tq = iqs.shape[0] // n_idx
    n = ik.shape[0]
    logits = lax.dot_general(iqs, ik, _NT, preferred_element_type=jnp.float32).reshape(n_idx, tq, n)
    logits = jnp.maximum(logits * IDX_DIM ** -0.5, 0.0)
    sc = jnp.zeros((tq, n), jnp.float32)
    for h in range(n_idx):
        sc = sc + logits[h] * iw[:, iw_lane + h:iw_lane + h + 1]
    return sc * n_idx ** -0.5


def _cover_matrix(n_ch, lanes, n_s):
    ci = np.arange(n_ch)[:, None] * CMP_STRIDE
    sj = np.arange(lanes)[None, :] * SEL_BLOCK
    return jnp.asarray((ci < sj + SEL_BLOCK) & (ci + CMP_BLOCK > sj) & (sj < n_s * SEL_BLOCK), _BF)


def _expand_matrix(lanes, n_keys):
    return jnp.asarray(np.arange(lanes)[:, None] == np.arange(n_keys)[None, :] // SEL_BLOCK, _BF)


def _tri_matrix():
    return jnp.asarray(np.arange(LANES)[:, None] <= np.arange(LANES)[None, :], _BF)


def _pad_rows(x, n):
    return jnp.concatenate([x, jnp.zeros((n - x.shape[0], x.shape[1]), x.dtype)], axis=0)


def _compress_kernel(x_ref, w1_ref, w2_ref, pe_ref, cos_ref, sin_ref, o_ref):
    n_ch = o_ref.shape[-2]
    y = _compress_rows(lambda s: x_ref[pl.ds(s, n_ch, stride=CMP_STRIDE), :], pe_ref[0],
                       lambda r: w1_ref[0, r], w2_ref[0], n_ch)

    @pl.when(pl.program_id(1) == 0)
    def _():
        o_ref[0, 0, 0] = _rotary(y, cos_ref[...], sin_ref[...]).astype(o_ref.dtype)

    @pl.when(pl.program_id(1) != 0)
    def _():
        o_ref[0, 0, 0] = y.astype(o_ref.dtype)


def compress_prompt(proj, n_batch, seq, cmp_col, w1, w2, pe, cos_end, sin_end):
    n_ch = seq // CMP_STRIDE
    col0 = cmp_col // HEAD_DIM
    return pl.pallas_call(
        _compress_kernel,
        grid=(n_batch, 2, N_KV_A),
        in_specs=[pl.BlockSpec((seq, HEAD_DIM), lambda b, kv, g: (b, col0 + kv * N_KV_A + g)),
                  pl.BlockSpec((1,) + w1.shape[1:], lambda b, kv, g: (kv, 0, 0, 0)),
                  pl.BlockSpec((1, w2.shape[1], HEAD_DIM), lambda b, kv, g: (kv, 0, 0)),
                  pl.BlockSpec((1, CMP_BLOCK, HEAD_DIM), lambda b, kv, g: (kv, 0, 0)),
                  pl.BlockSpec((n_ch, HEAD_DIM), lambda b, kv, g: (0, 0)),
                  pl.BlockSpec((n_ch, HEAD_DIM), lambda b, kv, g: (0, 0))],
        out_specs=pl.BlockSpec((1, 1, 1, n_ch, HEAD_DIM), lambda b, kv, g: (b, kv, g, 0, 0)),
        out_shape=jax.ShapeDtypeStruct((n_batch, 2, N_KV_A, n_ch, HEAD_DIM), _BF),
        compiler_params=_params("parallel", "parallel", "parallel"),
    )(proj, w1, w2, pe, cos_end, sin_end)


def _nsa_prompt_kernel(q_ref, kc_ref, vc_ref, selk_ref, selv_ref, wink_ref, winv_ref, gate_ref,
                       cover_ref, expand_ref, o_ref, selexp_ref, sk_ref, sv_ref, wk_ref, wv_ref,
                       *, n_heads, n_s, chunk):
    tq = q_ref.shape[0]
    seq = selexp_ref.shape[-1]
    n_grp = N_KV_A
    qi = pl.program_id(1)
    row0 = pl.multiple_of(qi * tq, tq)
    t_col = qi * tq + lax.broadcasted_iota(jnp.int32, (tq, 1), 0)
    qw = n_heads * HEAD_DIM

    @pl.when(qi == 0)
    def _():
        for ref in (sk_ref, sv_ref, wk_ref, wv_ref):
            ref[...] = jnp.zeros(ref.shape, ref.dtype)

    ones = jnp.ones((tq, HEAD_DIM), _BF)
    causal = lax.broadcasted_iota(jnp.int32, (tq, seq), 1) <= t_col
    qs, o_cmp = [], []
    for g in range(n_grp):
        lanes = slice(g * HEAD_DIM, (g + 1) * HEAD_DIM)
        sk_ref[g, pl.ds(row0, tq), :] = selk_ref[:, lanes].astype(_BF)
        sv_ref[g, pl.ds(row0, tq), :] = jnp.concatenate([selv_ref[:, lanes].astype(_BF), ones], axis=1)
        wk_ref[g, pl.ds(row0, tq), :] = wink_ref[:, lanes].astype(_BF)
        wv_ref[g, pl.ds(row0, tq), :] = jnp.concatenate([winv_ref[:, lanes].astype(_BF), ones], axis=1)
        qs.append(_stack_heads(q_ref[:, g * qw:(g + 1) * qw], n_heads).astype(_BF))
    sels = []
    t_row = qi * tq + lax.broadcasted_iota(jnp.int32, (1, tq), 1)
    for g in range(n_grp):
        o, psum = _cmp_attend(qs[g], kc_ref[0, 0, g], vc_ref[0, 0, g], t_col, n_heads)
        o_cmp.append(o)
        sels.append(_choose_blocks_keys_major(psum, cover_ref[...], t_row, n_s))
    for g in range(n_grp):
        chosen = jnp.dot(sels[g].T.astype(_BF), expand_ref[...], preferred_element_type=jnp.float32)
        selexp_ref[g] = jnp.where(causal, chosen, 0.0)

    span = WINDOW + tq
    start = pl.multiple_of(jnp.maximum(row0 - WINDOW, 0), tq)
    d = t_col - (start + lax.broadcasted_iota(jnp.int32, (tq, span), 1))
    visible = (d >= 0) & (d < WINDOW)
    gates = jax.nn.sigmoid(gate_ref[...])
    n_all = n_grp * n_heads
    for g in range(n_grp):
        def sel_mask(c, j):
            return selexp_ref[g, :, pl.ds(pl.multiple_of(c * chunk + j * LANES, LANES), LANES)] > 0.5

        o_slc = _attend(qs[g], sk_ref.at[g], sv_ref.at[g], 0, (row0 + tq + chunk - 1) // chunk, chunk, sel_mask,
                        n_heads)
        o_win = _attend(qs[g], wk_ref.at[g], wv_ref.at[g], start, 1, span,
                        lambda c, j: visible[:, j * LANES:(j + 1) * LANES], n_heads)
        for r in range(n_heads):
            h = g * n_heads + r
            o = (gates[:, h:h + 1] * o_cmp[g][r] + gates[:, n_all + h:n_all + h + 1] * o_slc[r]
                 + gates[:, 2 * n_all + h:2 * n_all + h + 1] * o_win[r])
            o_ref[:, h * HEAD_DIM:(h + 1) * HEAD_DIM] = o.astype(o_ref.dtype)


def nsa_prompt(proj, kvc, lay, n_batch, seq, tq, chunk=512):
    n_heads = lay.n_heads_a // N_KV_A
    n_ch = kvc.shape[-2]
    n_s = -(-seq // SEL_BLOCK)
    assert tq % LANES == 0 and n_s <= LANES and seq % chunk == 0 and chunk % tq == 0
    assert WINDOW % tq == 0 and seq >= WINDOW + tq and lay.ga % LANES == 0
    nq = seq // tq
    qw = lay.n_heads_a * HEAD_DIM
    kvw = N_KV_A * HEAD_DIM

    def tile(col):
        return pl.BlockSpec((tq, kvw), lambda b, i: (b * nq + i, col // kvw))

    return pl.pallas_call(
        functools.partial(_nsa_prompt_kernel, n_heads=n_heads, n_s=n_s, chunk=chunk),
        grid=(n_batch, nq),
        in_specs=[pl.BlockSpec((tq, qw), lambda b, i: (b * nq + i, lay.qa // qw)),
                  pl.BlockSpec((1, 1, N_KV_A, n_ch, HEAD_DIM), lambda b, i: (b, 0, 0, 0, 0)),
                  pl.BlockSpec((1, 1, N_KV_A, n_ch, HEAD_DIM), lambda b, i: (b, 1, 0, 0, 0)),
                  tile(lay.selk), tile(lay.selv), tile(lay.wink), tile(lay.winv),
                  pl.BlockSpec((tq, LANES), lambda b, i: (b * nq + i, lay.ga // LANES)),
                  pl.BlockSpec((LANES, n_ch), lambda b, i: (0, 0)),
                  pl.BlockSpec((LANES, seq), lambda b, i: (0, 0))],
        out_specs=pl.BlockSpec((tq, qw), lambda b, i: (b * nq + i, 0)),
        out_shape=jax.ShapeDtypeStruct((n_batch * seq, qw), _BF),
        scratch_shapes=[pltpu.VMEM((N_KV_A, tq, seq), jnp.float32),
                        pltpu.VMEM((N_KV_A, seq, HEAD_DIM), _BF), pltpu.VMEM((N_KV_A, seq, 2 * HEAD_DIM), _BF),
                        pltpu.VMEM((N_KV_A, seq, HEAD_DIM), _BF), pltpu.VMEM((N_KV_A, seq, 2 * HEAD_DIM), _BF)],
        compiler_params=_params("parallel", "arbitrary"),
    )(proj, kvc, kvc, proj, proj, proj, proj, proj, _cover_matrix(n_ch, LANES, n_s).T, _expand_matrix(LANES, seq))


def _dsa_prompt_kernel(iq_ref, iw_ref, q_ref, ik_ref, k_ref, v_ref, tri_ref, o_ref,
                       score_ref, key_ref, mask_ref, ikb_ref, kb_ref, vb_ref,
                       *, n_idx, n_heads, n_top, iw_lane, chunk):
    tq = iq_ref.shape[0]
    qi = pl.program_id(1)
    n_tiles = qi + 1
    row0 = pl.multiple_of(qi * tq, tq)
    t_row = qi * tq + lax.broadcasted_iota(jnp.int32, (1, tq), 1)

    @pl.when(qi == 0)
    def _():
        for ref in (ikb_ref, kb_ref, vb_ref):
            ref[...] = jnp.zeros(ref.shape, ref.dtype)

    ikb_ref[pl.ds(row0, tq), :] = ik_ref[...].astype(_BF)
    kb_ref[pl.ds(row0, tq), :] = k_ref[...].astype(_BF)
    vb_ref[pl.ds(row0, tq), :] = jnp.concatenate([v_ref[...].astype(_BF), jnp.ones((tq, HEAD_DIM), _BF)], axis=1)
    iqs = _stack_heads(iq_ref[...], n_idx).astype(_BF)
    iw_t = iw_ref[...].T

    def key_pos(kt, n):
        return kt * n + lax.broadcasted_iota(jnp.int32, (n, 1), 0)

    def score_tile(kt, _):
        sl = pl.ds(pl.multiple_of(kt * tq, tq), tq)
        logits = lax.dot_general(ikb_ref[sl, :], iqs, _NT, preferred_element_type=jnp.float32)
        sc = jnp.zeros((tq, tq), jnp.float32)
        for h in range(n_idx):
            sc = sc + (jnp.maximum(logits[:, h * tq:(h + 1) * tq] * IDX_DIM ** -0.5, 0.0)
                       * iw_t[iw_lane + h:iw_lane + h + 1, :])
        score_ref[sl, :] = jnp.where(key_pos(kt, tq) <= t_row, sc * n_idx ** -0.5, NEG)
        return 0

    lax.fori_loop(0, n_tiles, score_tile, 0)
    n_tiles128 = n_tiles * (tq // LANES)
    _topk_mask_keys_major(score_ref, key_ref, tri_ref, n_tiles128, n_top, tiles_per_iter=tq // LANES)
    mask_ref[...] = jnp.zeros(mask_ref.shape, mask_ref.dtype)

    def to_rows(t, _):
        sl = pl.ds(pl.multiple_of(t * LANES, LANES), LANES)
        mask_ref[:, sl] = jnp.where(key_pos(t, LANES) <= t_row, score_ref[sl, :], 0.0).T
        return 0

    lax.fori_loop(0, n_tiles128, to_rows, 0)

    def dsa_mask(c, j):
        return mask_ref[:, pl.ds(pl.multiple_of(c * chunk + j * LANES, LANES), LANES)] > 0.5

    qs = _stack_heads(q_ref[...], n_heads).astype(_BF)
    o = _attend(qs, kb_ref, vb_ref, 0, (row0 + tq + chunk - 1) // chunk, chunk, dsa_mask, n_heads)
    for h in range(n_heads):
        o_ref[:, h * HEAD_DIM:(h + 1) * HEAD_DIM] = o[h].astype(o_ref.dtype)


def dsa_prompt(proj, lay, n_batch, seq, tq, chunk=512):
    assert tq % LANES == 0 and seq % chunk == 0 and chunk % tq == 0
    nq = seq // tq
    n_top = min(DSA_TOPK, seq // 4)
    iqw = lay.n_idx_heads * IDX_DIM
    qw = lay.n_heads_b * HEAD_DIM

    def tile(col):
        c0 = col // HEAD_DIM
        return pl.BlockSpec((tq, HEAD_DIM), lambda b, i: (b * nq + i, c0))

    return pl.pallas_call(
        functools.partial(_dsa_prompt_kernel, n_idx=lay.n_idx_heads, n_heads=lay.n_heads_b, n_top=n_top,
                          iw_lane=lay.iw % LANES, chunk=chunk),
        grid=(n_batch, nq),
        in_specs=[pl.BlockSpec((tq, iqw), lambda b, i: (b * nq + i, lay.iq // iqw)),
                  pl.BlockSpec((tq, LANES), lambda b, i: (b * nq + i, lay.iw // LANES)),
                  pl.BlockSpec((tq, qw), lambda b, i: (b * nq + i, lay.qb // qw)),
                  tile(lay.ik), tile(lay.dk), tile(lay.dv),
                  pl.BlockSpec((LANES, LANES), lambda b, i: (0, 0))],
        out_specs=pl.BlockSpec((tq, qw), lambda b, i: (b * nq + i, 0)),
        out_shape=jax.ShapeDtypeStruct((n_batch * seq, qw), _BF),
        scratch_shapes=[pltpu.VMEM((seq, tq), jnp.float32), pltpu.VMEM((seq, tq), jnp.int32),
                        pltpu.VMEM((tq, seq), jnp.float32),
                        pltpu.VMEM((seq, IDX_DIM), _BF), pltpu.VMEM((seq, HEAD_DIM), _BF),
                        pltpu.VMEM((seq, 2 * HEAD_DIM), _BF)],
        compiler_params=_params("parallel", "arbitrary"),
    )(proj, proj, proj, proj, proj, proj, _tri_matrix().T)


def _page_specs(rows, per_step=None):
    per_step = PAGES_PER_STEP if per_step is None else per_step
    def spec(k):
        return pl.BlockSpec((1, rows, LANES), lambda b, s, pt: (pt[b, s * per_step + k], 0, 0))
    return [spec(k) for k in range(per_step)]


def _rows_view(x, lead):
    return x.reshape(x.shape[:lead] + (-1, x.shape[-1]))


def _kind_rows(ref, kind, n_kinds, n_rows):
    return ref.at[0][pl.ds(kind, n_rows, stride=n_kinds), :]


NSA_PAGES_PER_STEP = 8


def _nsa_sample_kernel(pt_ref, *refs, n_heads, n_s, past_len):
    pages = refs[:NSA_PAGES_PER_STEP]
    (q_ref, gate_ref, nsk_ref, nsv_ref, nwk_ref, nwv_ref, win_ref, w1_ref, w2_ref, pe_ref, cos_ref, sin_ref,
     cover_ref, expand_ref, o_ref, rows_ref, selk_ref, selv_ref) = refs[NSA_PAGES_PER_STEP:]
    step = pl.program_id(1)
    ts = q_ref.shape[0]
    qw = n_heads * HEAD_DIM
    n_grp = N_KV_A
    n_kinds = 4 * n_grp
    page = pages[0].shape[1] // n_kinds
    t_col = past_len + lax.broadcasted_iota(jnp.int32, (ts, 1), 0)

    for k in range(NSA_PAGES_PER_STEP):
        start = pl.multiple_of((step * NSA_PAGES_PER_STEP + k) * page, page)
        for c in range(2 * n_grp):
            rows_ref.at[c][pl.ds(start, page), :] = _kind_rows(pages[k], c, n_kinds, page)
        for g in range(n_grp):
            selk_ref.at[g][pl.ds(start, page), :] = _kind_rows(pages[k], 2 * n_grp + g, n_kinds, page).astype(_BF)
            selv_ref.at[g][pl.ds(start, page), :] = _kind_rows(pages[k], 3 * n_grp + g, n_kinds, page).astype(_BF)

    @pl.when(step == pl.num_programs(1) - 1)
    def _():
        n_ch = cover_ref.shape[0]
        gates = jax.nn.sigmoid(gate_ref[...])
        n_all = n_grp * n_heads
        w_len = win_ref.shape[1] // (2 * n_grp)
        row = lax.broadcasted_iota(jnp.int32, (ts, LANES), 0)
        lane = lax.broadcasted_iota(jnp.int32, (ts, LANES), 1)
        new_causal = (lane <= row) & (lane < ts)
        wlane = lax.broadcasted_iota(jnp.int32, (ts, w_len + LANES), 1)
        k_pos = past_len - w_len + wlane
        d = t_col - k_pos
        win_mask = (d >= 0) & (d < WINDOW) & (k_pos >= 0) & (wlane < w_len + ts)
        new_blk = past_len // SEL_BLOCK

        def group_lanes(x, g):
            return x[:, g * HEAD_DIM:(g + 1) * HEAD_DIM]

        for g in range(n_grp):
            qs = _stack_heads(q_ref[:, g * qw:(g + 1) * qw], n_heads).astype(_BF)
            kc, vc = [_compress_rows(lambda s: rows_ref.at[kv * n_grp + g][pl.ds(s, n_ch, stride=CMP_STRIDE), :],
                                     pe_ref[kv], lambda r: w1_ref[kv, r], w2_ref[kv], n_ch) for kv in range(2)]
            kc = _rotary(kc, cos_ref[...], sin_ref[...])
            o_cmp, sel = _cmp_attend_and_choose(qs, kc.astype(_BF), vc.astype(_BF), cover_ref[...], t_col, n_heads, n_s)
            sel_f = jnp.where(sel, 1.0, 0.0)
            chosen = jnp.dot(sel_f.astype(_BF), expand_ref[...], preferred_element_type=jnp.float32) > 0.5
            carry = _flash_update(qs, selk_ref[g], selv_ref[g], chosen, _flash_init(n_heads, ts), n_heads)
            in_new = sel_f[:, new_blk:new_blk + 1] > 0.5
            k_new = _pad_rows(group_lanes(nsk_ref[...], g), LANES).astype(_BF)
            v_new = _pad_rows(group_lanes(nsv_ref[...], g), LANES).astype(_BF)
            o_slc = _flash_finish(_flash_update(qs, k_new, v_new, new_causal & in_new, carry, n_heads))
            kw = jnp.concatenate([_kind_rows(win_ref, g, 2 * n_grp, w_len),
                                  _pad_rows(group_lanes(nwk_ref[...], g), LANES)], axis=0).astype(_BF)
            vw = jnp.concatenate([_kind_rows(win_ref, n_grp + g, 2 * n_grp, w_len),
                                  _pad_rows(group_lanes(nwv_ref[...], g), LANES)], axis=0).astype(_BF)
            o_win = _flash_finish(_flash_update(qs, kw, vw, win_mask, _flash_init(n_heads, ts), n_heads))
            for r in range(n_heads):
                h = g * n_heads + r
                o = (gates[:, h:h + 1] * o_cmp[r] + gates[:, n_all + h:n_all + h + 1] * o_slc[r]
                     + gates[:, 2 * n_all + h:2 * n_all + h + 1] * o_win[r])
                o_ref[0, :, h * HEAD_DIM:(h + 1) * HEAD_DIM] = o.astype(o_ref.dtype)


def nsa_sample(proj, row0, cache, win_buf, page_table, w1, w2, pe, cos_end, sin_end, lay, ts):
    n_batch, n_pages = page_table.shape
    page = cache.shape[1]
    past_len = n_pages * page
    pool = _rows_view(cache, 1)
    win_rows = _rows_view(win_buf, 1)
    n_heads = lay.n_heads_a // N_KV_A
    n_ch = past_len // CMP_STRIDE
    n_s = -(-(past_len + ts) // SEL_BLOCK)
    sel_lanes = -(-n_s // LANES) * LANES
    kvw = N_KV_A * HEAD_DIM
    assert n_pages % NSA_PAGES_PER_STEP == 0 and row0 % ts == 0 and ts <= SEL_BLOCK and past_len % SEL_BLOCK == 0
    assert ts % 8 == 0 and lay.ga % LANES == 0
    assert (past_len + ts - CMP_BLOCK) // CMP_STRIDE + 1 <= n_ch and n_ch * CMP_STRIDE <= past_len
    r0 = row0 // ts
    qw = lay.n_heads_a * HEAD_DIM
    rows = lambda width, col: pl.BlockSpec((ts, width), lambda b, s, pt: (r0 + b, col // width))
    once = lambda shape: pl.BlockSpec(shape, lambda b, s, pt: (0,) * len(shape), pipeline_mode=pl.Buffered(1))
    grid_spec = pltpu.PrefetchScalarGridSpec(
        num_scalar_prefetch=1,
        grid=(n_batch, n_pages // NSA_PAGES_PER_STEP),
        in_specs=_page_specs(pool.shape[1], NSA_PAGES_PER_STEP) + [
            rows(qw, lay.qa), rows(LANES, lay.ga),
            rows(kvw, lay.selk), rows(kvw, lay.selv), rows(kvw, lay.wink), rows(kvw, lay.winv),
            pl.BlockSpec((1,) + win_rows.shape[1:], lambda b, s, pt: (b, 0, 0)),
            once(w1.shape), once(w2.shape), once(pe.shape), once((n_ch, HEAD_DIM)), once((n_ch, HEAD_DIM)),
            once((n_ch, sel_lanes)), once((sel_lanes, past_len))],
        out_specs=pl.BlockSpec((1, ts, qw), lambda b, s, pt: (b, 0, 0)),
        scratch_shapes=[pltpu.VMEM((2 * N_KV_A, past_len, HEAD_DIM), jnp.float32),
                        pltpu.VMEM((N_KV_A, past_len, HEAD_DIM), _BF),
                        pltpu.VMEM((N_KV_A, past_len, HEAD_DIM), _BF)],
    )
    return pl.pallas_call(
        functools.partial(_nsa_sample_kernel, n_heads=n_heads, n_s=n_s, past_len=past_len),
        grid_spec=grid_spec,
        out_shape=jax.ShapeDtypeStruct((n_batch, ts, qw), _BF),
        compiler_params=_params("parallel", "arbitrary"),
    )(page_table, *([pool] * NSA_PAGES_PER_STEP), proj, proj, proj, proj, proj, proj, win_rows,
      w1, w2, pe, cos_end, sin_end, _cover_matrix(n_ch, sel_lanes, n_s), _expand_matrix(sel_lanes, past_len))


def _dsa_sample_select_kernel(pt_ref, *refs, n_idx, n_top, iw_lane, past_len):
    pages = refs[:PAGES_PER_STEP]
    iq_ref, iw_ref, nik_ref, tri_ref, mask_ref, score_ref, key_ref = refs[PAGES_PER_STEP:]
    step = pl.program_id(1)
    ts = iq_ref.shape[0]
    keys_per_step = PAGES_PER_STEP * pages[0].shape[1]
    iqs = _stack_heads(iq_ref[...], n_idx).astype(_BF)
    iw = iw_ref[...]
    ik = jnp.concatenate([p[0] for p in pages], axis=0).astype(_BF)
    start = pl.multiple_of(step * keys_per_step, keys_per_step)
    score_ref[:, pl.ds(start, keys_per_step)] = _indexer_scores(iqs, ik, iw, iw_lane, n_idx)

    @pl.when(step == pl.num_programs(1) - 1)
    def _():
        sc = _indexer_scores(iqs, _pad_rows(nik_ref[...], LANES).astype(_BF), iw, iw_lane, n_idx)
        row = lax.broadcasted_iota(jnp.int32, (ts, LANES), 0)
        lane = lax.broadcasted_iota(jnp.int32, (ts, LANES), 1)
        score_ref[:, past_len:past_len + LANES] = jnp.where(lane < ts, jnp.where(lane <= row, sc, NEG), -jnp.inf)
        _topk_mask(score_ref, key_ref, mask_ref.at[0], tri_ref, past_len // LANES + 1, n_top)


def dsa_sample_select(proj, row0, cache_idx, page_table, lay, ts):
    n_batch, n_pages = page_table.shape
    page = cache_idx.shape[1]
    past_len = n_pages * page
    assert n_pages % PAGES_PER_STEP == 0 and row0 % ts == 0 and ts <= LANES and past_len % LANES == 0
    n_top = min(DSA_TOPK, (past_len + ts) // 4)
    r0 = row0 // ts
    iqw = lay.n_idx_heads * IDX_DIM
    width = past_len + LANES
    rows = lambda w, col: pl.BlockSpec((ts, w), lambda b, s, pt: (r0 + b, col // w))
    grid_spec = pltpu.PrefetchScalarGridSpec(
        num_scalar_prefetch=1,
        grid=(n_batch, n_pages // PAGES_PER_STEP),
        in_specs=_page_specs(page) + [
            rows(iqw, lay.iq), rows(LANES, lay.iw), rows(IDX_DIM, lay.ik),
            pl.BlockSpec((LANES, LANES), lambda b, s, pt: (0, 0))],
        out_specs=pl.BlockSpec((1, ts, width), lambda b, s, pt: (b, 0, 0)),
        scratch_shapes=[pltpu.VMEM((ts, width), jnp.float32), pltpu.VMEM((ts, width), jnp.int32)],
    )
    return pl.pallas_call(
        functools.partial(_dsa_sample_select_kernel, n_idx=lay.n_idx_heads, n_top=n_top,
                          iw_lane=lay.iw % LANES, past_len=past_len),
        grid_spec=grid_spec,
        out_shape=jax.ShapeDtypeStruct((n_batch, ts, width), jnp.float32),
        compiler_params=_params("parallel", "arbitrary"),
    )(page_table, *([cache_idx] * PAGES_PER_STEP), proj, proj, proj, _tri_matrix())


def _dsa_sample_attend_kernel(pt_ref, *refs, n_heads):
    pages = refs[:PAGES_PER_STEP]
    q_ref, mask_ref, nmask_ref, nk_ref, nv_ref, o_ref, m_ref, l_ref, acc_ref = refs[PAGES_PER_STEP:]
    step = pl.program_id(1)
    ts = q_ref.shape[0]
    qs = _stack_heads(q_ref[...], n_heads).astype(_BF)

    @pl.when(step == 0)
    def _():
        m_ref[...], l_ref[...], acc_ref[...] = _flash_init(n_heads, ts)

    page = pages[0].shape[1] // 2
    k = jnp.concatenate([_kind_rows(p, 0, 2, page) for p in pages], axis=0).astype(_BF)
    v = jnp.concatenate([_kind_rows(p, 1, 2, page) for p in pages], axis=0).astype(_BF)
    m_ref[...], l_ref[...], acc_ref[...] = _flash_update(qs, k, v, mask_ref[0] > 0.5,
                                                         (m_ref[...], l_ref[...], acc_ref[...]), n_heads)

    @pl.when(step == pl.num_programs(1) - 1)
    def _():
        row = lax.broadcasted_iota(jnp.int32, (ts, LANES), 0)
        lane = lax.broadcasted_iota(jnp.int32, (ts, LANES), 1)
        mask = (nmask_ref[0] > 0.5) & (lane <= row) & (lane < ts)
        o = _flash_finish(_flash_update(qs, _pad_rows(nk_ref[...], LANES).astype(_BF),
                                        _pad_rows(nv_ref[...], LANES).astype(_BF), mask,
                                        (m_ref[...], l_ref[...], acc_ref[...]), n_heads))
        for h in range(n_heads):
            o_ref[0, :, h * HEAD_DIM:(h + 1) * HEAD_DIM] = o[h].astype(o_ref.dtype)


def dsa_sample_attend(proj, row0, mask, cache_kv, page_table, lay, ts):
    n_batch, n_pages = page_table.shape
    page = cache_kv.shape[1]
    past_len = n_pages * page
    pool = _rows_view(cache_kv, 1)
    keys_per_step = PAGES_PER_STEP * page
    r0 = row0 // ts
    qw = lay.n_heads_b * HEAD_DIM
    rows = lambda w, col: pl.BlockSpec((ts, w), lambda b, s, pt: (r0 + b, col // w))
    grid_spec = pltpu.PrefetchScalarGridSpec(
        num_scalar_prefetch=1,
        grid=(n_batch, n_pages // PAGES_PER_STEP),
        in_specs=_page_specs(pool.shape[1]) + [
            rows(qw, lay.qb),
            pl.BlockSpec((1, ts, keys_per_step), lambda b, s, pt: (b, 0, s)),
            pl.BlockSpec((1, ts, LANES), lambda b, s, pt: (b, 0, past_len // LANES)),
            rows(HEAD_DIM, lay.dk), rows(HEAD_DIM, lay.dv)],
        out_specs=pl.BlockSpec((1, ts, qw), lambda b, s, pt: (b, 0, 0)),
        scratch_shapes=[pltpu.VMEM((lay.n_heads_b, ts, 1), jnp.float32),
                        pltpu.VMEM((lay.n_heads_b, ts, 1), jnp.float32),
                        pltpu.VMEM((lay.n_heads_b, ts, HEAD_DIM), jnp.float32)],
    )
    return pl.pallas_call(
        functools.partial(_dsa_sample_attend_kernel, n_heads=lay.n_heads_b),
        grid_spec=grid_spec,
        out_shape=jax.ShapeDtypeStruct((n_batch, ts, qw), _BF),
        compiler_params=_params("parallel", "arbitrary"),
    )(page_table, *([pool] * PAGES_PER_STEP), proj, mask, mask, proj, proj)


class Layout:
    def __init__(self, d_model, n_heads_a, n_heads_b, n_idx_heads):
        self.n_heads_a, self.n_heads_b, self.n_idx_heads = n_heads_a, n_heads_b, n_idx_heads
        src = np.cumsum([0, n_heads_a * HEAD_DIM, 6 * N_KV_A * HEAD_DIM, 3 * n_heads_a,
                         n_heads_b * HEAD_DIM, 2 * HEAD_DIM, n_idx_heads * IDX_DIM, n_idx_heads,
                         IDX_DIM, 2 * d_model])
        s_qa, s_kva, s_ga, s_qb, s_kvb, s_iq, s_iw, s_ik, s_mg, s_end = (int(v) for v in src)
        small = 3 * n_heads_a + n_idx_heads
        assert small <= LANES
        self.small_pad = LANES - small
        kvw = N_KV_A * HEAD_DIM
        self.pieces = [
            (s_qa, s_kva, True),
            (s_qb, s_kvb, True),
            (s_iq, s_iw, True),
            (s_kva, s_kva + 2 * kvw, False),
            (s_kva + 2 * kvw, s_kva + 3 * kvw, True),
            (s_kva + 3 * kvw, s_kva + 4 * kvw, False),
            (s_kva + 4 * kvw, s_kva + 5 * kvw, True),
            (s_kva + 5 * kvw, s_ga, False),
            (s_kvb, s_kvb + HEAD_DIM, True),
            (s_kvb + HEAD_DIM, s_iq, False),
            (s_ik, s_mg, True),
            (s_ga, s_qb, False),
            (s_iw, s_ik, False),
            None,
            (s_mg, s_end, False),
        ]
        off = 0
        starts = []
        for p in self.pieces:
            starts.append(off)
            off += self.small_pad if p is None else p[1] - p[0]
        (self.qa, self.qb, self.iq, self.cmp, self.selk, self.selv, self.wink, self.winv, self.dk,
         self.dv, self.ik, self.ga, self.iw, _, self.mg) = starts
        self.width = off
        assert self.width % LANES == 0
        flags = np.zeros(self.width // LANES, np.int32)
        for st, p in zip(starts, self.pieces):
            if p is not None and p[2]:
                assert st % LANES == 0 and (p[1] - p[0]) % LANES == 0
                flags[st // LANES:(st + p[1] - p[0]) // LANES] = 1
        self.rope_flags = flags

    def pack(self, w_in):
        cols = []
        for p in self.pieces:
            if p is None:
                cols.append(jnp.zeros((w_in.shape[0], self.small_pad), w_in.dtype))
            else:
                cols.append(w_in[:, p[0]:p[1]])
        return jnp.concatenate(cols, axis=1).astype(_BF)


def rope_tables(pos):
    half = HEAD_DIM // 2
    inv = ROPE_THETA ** (-jnp.arange(half, dtype=jnp.float32) / half)
    ang = pos.astype(jnp.float32)[:, None] * inv[None, :]
    cos, sin = jnp.cos(ang), jnp.sin(ang)
    return jnp.concatenate([cos, cos], axis=1), jnp.concatenate([-sin, sin], axis=1)


def _swiglu_half_step(h, g, wg, wu, wd, w_index):
    xn = rmsnorm(h, g, _BF)
    m = h.shape[0]
    a = ffn_gate_up(xn, wg, wu, w_index, tm=_row_tile(m, ROW_TILE_WIDE), tn=2 * LANES)
    return resid_matmul(a, wd, w_index, h, 0.5, tm=_row_tile(m, ROW_TILE_DOWN), tn=2 * LANES)


def kernel(x_prompt, x_sample, cache_nsa_kv, cache_nsa_win, cache_dsa_kv, cache_dsa_idx, page_table,
           g_norm, w_ffn_gate, w_ffn_up, w_ffn_down, w_in, w_cmp1, w_cmp2, cmp_pos,
           w_br_a, w_br_b, w_out, g_final):
    B, T, D = x_prompt.shape
    DB, Ts, _ = x_sample.shape
    depth = g_norm.shape[0]
    page = cache_nsa_kv.shape[2]
    past_len = page_table.shape[1] * page
    n_heads_a = w_br_a.shape[1] // HEAD_DIM
    n_heads_b = w_br_b.shape[1] // HEAD_DIM
    G = N_KV_A
    lay = Layout(D, n_heads_a, n_heads_b, n_heads_b // 2)
    Mp, Ms = B * T, DB * Ts
    tm_wide = _row_tile(Mp + Ms, ROW_TILE_WIDE)

    pos_p = jnp.arange(T, dtype=jnp.int32)
    pos_s = past_len + jnp.arange(Ts, dtype=jnp.int32)
    cos, sin = rope_tables(jnp.concatenate([jnp.tile(pos_p, B), jnp.tile(pos_s, DB)]))
    rope_flags = jnp.asarray(lay.rope_flags)

    def block_end_tables(n_ch):
        return rope_tables(jnp.arange(n_ch, dtype=jnp.int32) * CMP_STRIDE + CMP_BLOCK - 1)

    h = jnp.concatenate([x_prompt.reshape(Mp, D), x_sample.reshape(Ms, D)], axis=0)
    outs = [[] for _ in range(8)]
    for l in range(depth):
        h = _swiglu_half_step(h, g_norm[l, 0], w_ffn_gate, w_ffn_up, w_ffn_down, (l, 0))

        u = rmsnorm(h, g_norm[l, 1], _BF)
        proj, nsa_rows, win_rows, dsa_rows, ik_rows = in_project(u, lay.pack(w_in[l]), rope_flags, cos, sin, lay,
                                                                 tm=tm_wide, tn=4 * LANES)
        w1 = w_cmp1[l].reshape(2, 2, CMP_STRIDE * HEAD_DIM, -1).astype(_BF)
        w2 = w_cmp2[l].astype(_BF)

        kvc_p = compress_prompt(proj, B, T, lay.cmp, w1, w2, cmp_pos[l], *block_end_tables(T // CMP_STRIDE))
        o_a_p = nsa_prompt(proj, kvc_p, lay, B, T, tq=Q_TILE)
        o_b_p = dsa_prompt(proj, lay, B, T, tq=Q_TILE)

        o_a_s = nsa_sample(proj, Mp, cache_nsa_kv[l], cache_nsa_win[l], page_table, w1, w2, cmp_pos[l],
                           *block_end_tables(past_len // CMP_STRIDE), lay, Ts)
        top_mask = dsa_sample_select(proj, Mp, cache_dsa_idx[l], page_table, lay, Ts)
        o_b_s = dsa_sample_attend(proj, Mp, top_mask, cache_dsa_kv[l], page_table, lay, Ts)

        o_a = jnp.concatenate([o_a_p, o_a_s.reshape(Ms, -1)], axis=0)
        o_b = jnp.concatenate([o_b_p, o_b_s.reshape(Ms, -1)], axis=0)
        m = merge_branches(o_a, o_b, w_br_a, w_br_b, (l,), proj, lay.mg, tm=tm_wide, tn=4 * LANES)
        h = resid_matmul(m, w_out, (l,), h, 1.0, tm=tm_wide, tn=4 * LANES)

        win_p = win_rows[:Mp].reshape(B, T, 2, G, HEAD_DIM)
        win_s = win_rows[Mp:].reshape(DB, Ts, 2, G, HEAD_DIM)
        outs[0].append(nsa_rows[:Mp].reshape(B, T, 4, G, HEAD_DIM))
        outs[1].append(win_p[:, T - min(WINDOW, T):])
        outs[2].append(dsa_rows[:Mp].reshape(B, T, 2, HEAD_DIM))
        outs[3].append(ik_rows[:Mp].reshape(B, T, IDX_DIM))
        outs[4].append(nsa_rows[Mp:].reshape(DB, Ts, 4, G, HEAD_DIM))
        outs[5].append(jnp.concatenate([cache_nsa_win[l], win_s], axis=1)[:, Ts:])
        outs[6].append(dsa_rows[Mp:].reshape(DB, Ts, 2, HEAD_DIM))
        outs[7].append(ik_rows[Mp:].reshape(DB, Ts, IDX_DIM))

        h = _swiglu_half_step(h, g_norm[l, 2], w_ffn_gate, w_ffn_up, w_ffn_down, (l, 1))

    y_p = rmsnorm(h, g_final, jnp.float32, 0, Mp)
    y_s = rmsnorm(h, g_final, jnp.float32, Mp, Ms)
    return (y_p.reshape(B, T, D), y_s.reshape(DB, Ts, D), *(jnp.stack(o) for o in outs))
```

```python
import functools

import jax
import jax.numpy as jnp
import numpy as np
from jax import lax
from jax.experimental import pallas as pl
from jax.experimental.pallas import tpu as pltpu

HEAD_DIM = 128
N_KV_A = 2
IDX_DIM = 128
CMP_BLOCK = 32
CMP_STRIDE = 16
SEL_BLOCK = 64
N_SEL = 16
N_LOCAL_SEL = 2
WINDOW = 512
DSA_TOPK = 256
ROPE_THETA = 10000.0
RMS_EPS = 1e-6
NEG = -1e30
BIG = 1e30
TINY = 1e-30

LANES = 128
VMEM_LIMIT = 56 * 1024 * 1024
PAGES_PER_STEP = 64

_NT = (((1,), (1,)), ((), ()))
_BF = jnp.bfloat16


ROW_TILE_WIDE = 1408
ROW_TILE_DOWN = 1056
Q_TILE = 2 * LANES


def _params(*sem):
    return pltpu.CompilerParams(dimension_semantics=sem, vmem_limit_bytes=VMEM_LIMIT)


def _row_tile(m, limit):
    return max(t for t in range(16, limit + 1, 16) if m % t == 0)


def _rmsnorm_kernel(x_ref, g_ref, o_ref):
    x = x_ref[...]
    y = x * lax.rsqrt(jnp.mean(x * x, axis=-1, keepdims=True) + RMS_EPS)
    o_ref[...] = (y * g_ref[...]).astype(o_ref.dtype)


def rmsnorm(x, g, out_dtype, row0=0, n_rows=None, tm=256):
    D = x.shape[1]
    n_rows = x.shape[0] - row0 if n_rows is None else n_rows
    assert row0 % tm == 0 and n_rows % tm == 0
    return pl.pallas_call(
        _rmsnorm_kernel,
        grid=(n_rows // tm,),
        in_specs=[pl.BlockSpec((tm, D), lambda i: (row0 // tm + i, 0)),
                  pl.BlockSpec((1, D), lambda i: (0, 0))],
        out_specs=pl.BlockSpec((tm, D), lambda i: (i, 0)),
        out_shape=jax.ShapeDtypeStruct((n_rows, D), out_dtype),
        compiler_params=_params("parallel"),
    )(x, g.reshape(1, D))


def _gateup_kernel(x_ref, wg_ref, wu_ref, o_ref):
    x = x_ref[...]
    g = jnp.dot(x, wg_ref[...].astype(_BF), preferred_element_type=jnp.float32)
    u = jnp.dot(x, wu_ref[...].astype(_BF), preferred_element_type=jnp.float32)
    o_ref[...] = (g * jax.nn.sigmoid(g) * u).astype(o_ref.dtype)


def _weight_spec(w, w_index, rows, tn):
    assert w.ndim == len(w_index) + 2 and w.shape[-2] == rows
    return pl.BlockSpec((None,) * len(w_index) + (rows, tn), lambda i, j: tuple(w_index) + (0, j))


def ffn_gate_up(xn, wg, wu, w_index, tm, tn):
    M, D = xn.shape
    F = wg.shape[-1]
    return pl.pallas_call(
        _gateup_kernel,
        grid=(M // tm, F // tn),
        in_specs=[pl.BlockSpec((tm, D), lambda i, j: (i, 0)),
                  _weight_spec(wg, w_index, D, tn),
                  _weight_spec(wu, w_index, D, tn)],
        out_specs=pl.BlockSpec((tm, tn), lambda i, j: (i, j)),
        out_shape=jax.ShapeDtypeStruct((M, F), _BF),
        compiler_params=_params("parallel", "parallel"),
    )(xn, wg, wu)


def _resid_matmul_kernel(a_ref, w_ref, r_ref, o_ref, *, scale):
    acc = jnp.dot(a_ref[...], w_ref[...].astype(_BF), preferred_element_type=jnp.float32)
    o_ref[...] = r_ref[...] + scale * acc


def resid_matmul(a, w, w_index, resid, scale, tm, tn):
    M, K = a.shape
    N = w.shape[-1]
    return pl.pallas_call(
        functools.partial(_resid_matmul_kernel, scale=scale),
        grid=(M // tm, N // tn),
        in_specs=[pl.BlockSpec((tm, K), lambda i, j: (i, 0), pipeline_mode=pl.Buffered(1)),
                  _weight_spec(w, w_index, K, tn),
                  pl.BlockSpec((tm, tn), lambda i, j: (i, j))],
        out_specs=pl.BlockSpec((tm, tn), lambda i, j: (i, j)),
        out_shape=jax.ShapeDtypeStruct((M, N), jnp.float32),
        compiler_params=_params("parallel", "parallel"),
    )(a, w, resid)


def _rotary(y, cos, sin):
    return y * cos + pltpu.roll(y, HEAD_DIM // 2, axis=1) * sin


def _inproj_kernel(flags_ref, x_ref, w_ref, cos_ref, sin_ref, o_ref, *row_refs, n_chunks, routes):
    j = pl.program_id(1)
    acc = jnp.dot(x_ref[...], w_ref[...], preferred_element_type=jnp.float32)
    for c in range(n_chunks):
        sl = slice(c * LANES, (c + 1) * LANES)
        y = acc[:, sl]
        flag = flags_ref[j * n_chunks + c]

        @pl.when(flag == 1)
        def _():
            o_ref[:, sl] = _rotary(y, cos_ref[...], sin_ref[...])

        @pl.when(flag == 0)
        def _():
            o_ref[:, sl] = y

    tm = o_ref.shape[0]
    for chunk, out_idx, kind, n_kinds in routes:
        @pl.when(j == chunk // n_chunks)
        def _():
            c = chunk % n_chunks
            row_refs[out_idx][pl.ds(kind, tm, stride=n_kinds), :] = o_ref[:, c * LANES:(c + 1) * LANES]


def in_project(u, w, rope_flags, cos, sin, lay, tm, tn):
    M, D = u.shape
    N = w.shape[1]
    n_chunks = tn // LANES
    G = N_KV_A
    kinds = [4 * G, 2 * G, 2, 1]
    routes = ([(lay.cmp // LANES + k, 0, k, kinds[0]) for k in range(kinds[0])]
              + [(lay.wink // LANES + k, 1, k, kinds[1]) for k in range(kinds[1])]
              + [(lay.dk // LANES + k, 2, k, kinds[2]) for k in range(kinds[2])]
              + [(lay.ik // LANES, 3, 0, 1)])
    row_shapes = [(M, 4, G, HEAD_DIM), (M, 2, G, HEAD_DIM), (M, 2, HEAD_DIM), (M, IDX_DIM)]
    grid_spec = pltpu.PrefetchScalarGridSpec(
        num_scalar_prefetch=1,
        grid=(M // tm, N // tn),
        in_specs=[pl.BlockSpec((tm, D), lambda i, j, f: (i, 0), pipeline_mode=pl.Buffered(1)),
                  pl.BlockSpec((D, tn), lambda i, j, f: (0, j)),
                  pl.BlockSpec((tm, LANES), lambda i, j, f: (i, 0)),
                  pl.BlockSpec((tm, LANES), lambda i, j, f: (i, 0))],
        out_specs=[pl.BlockSpec((tm, tn), lambda i, j, f: (i, j))]
        + [pl.BlockSpec((tm * nk, LANES), lambda i, j, f: (i, 0)) for nk in kinds],
    )
    proj, *rows = pl.pallas_call(
        functools.partial(_inproj_kernel, n_chunks=n_chunks, routes=routes),
        grid_spec=grid_spec,
        out_shape=[jax.ShapeDtypeStruct((M, N), jnp.float32)]
        + [jax.ShapeDtypeStruct((M * nk, LANES), jnp.float32) for nk in kinds],
        compiler_params=_params("parallel", "arbitrary"),
    )(rope_flags, u, w, cos, sin)
    return (proj, *(r.reshape(shp) for r, shp in zip(rows, row_shapes)))


def _merge_kernel(oa_ref, ob_ref, wa_ref, wb_ref, ga_ref, gb_ref, o_ref):
    ya = jnp.dot(oa_ref[...], wa_ref[...].astype(_BF), preferred_element_type=jnp.float32)
    yb = jnp.dot(ob_ref[...], wb_ref[...].astype(_BF), preferred_element_type=jnp.float32)
    m = jax.nn.sigmoid(ga_ref[...]) * ya + jax.nn.sigmoid(gb_ref[...]) * yb
    o_ref[...] = m.astype(o_ref.dtype)


def merge_branches(o_a, o_b, w_a, w_b, w_index, proj, mg_col, tm, tn):
    M, K = o_a.shape
    N = w_a.shape[-1]
    ja = mg_col // tn
    jb = (mg_col + N) // tn
    return pl.pallas_call(
        _merge_kernel,
        grid=(M // tm, N // tn),
        in_specs=[pl.BlockSpec((tm, K), lambda i, j: (i, 0), pipeline_mode=pl.Buffered(1)),
                  pl.BlockSpec((tm, K), lambda i, j: (i, 0), pipeline_mode=pl.Buffered(1)),
                  _weight_spec(w_a, w_index, K, tn),
                  _weight_spec(w_b, w_index, K, tn),
                  pl.BlockSpec((tm, tn), lambda i, j: (i, ja + j)),
                  pl.BlockSpec((tm, tn), lambda i, j: (i, jb + j))],
        out_specs=pl.BlockSpec((tm, tn), lambda i, j: (i, j)),
        out_shape=jax.ShapeDtypeStruct((M, N), _BF),
        compiler_params=_params("parallel", "parallel"),
    )(o_a, o_b, w_a, w_b, proj, proj)


def _stack_heads(x, n_heads):
    return jnp.concatenate([x[:, h * HEAD_DIM:(h + 1) * HEAD_DIM] for h in range(n_heads)], axis=0)


def _flash_update(qs, k, v, mask, carry, n_rep):
    m, l, acc = carry
    rows, tk = qs.shape[0], k.shape[0]
    tq = rows // n_rep
    scale = HEAD_DIM ** -0.5
    s = lax.dot_general(qs, k, _NT, preferred_element_type=jnp.float32).reshape(n_rep, tq, tk)
    mask = mask[None]
    s = jnp.where(mask, s, NEG)
    m_new = jnp.maximum(m, jnp.max(s, axis=-1, keepdims=True))
    p = jnp.where(mask, jnp.exp((s - m_new) * scale), 0.0)
    alpha = jnp.exp((m - m_new) * scale)
    l = alpha * l + jnp.sum(p, axis=-1, keepdims=True)
    pv = jnp.dot(p.reshape(rows, tk).astype(_BF), v, preferred_element_type=jnp.float32)
    return m_new, l, alpha * acc + pv.reshape(n_rep, tq, HEAD_DIM)


def _flash_init(n_rep, tq):
    return (jnp.full((n_rep, tq, 1), NEG, jnp.float32), jnp.zeros((n_rep, tq, 1), jnp.float32),
            jnp.zeros((n_rep, tq, HEAD_DIM), jnp.float32))


def _flash_finish(carry):
    _, l, acc = carry
    return acc * (1.0 / jnp.maximum(l, TINY))


def _attend(qs, k_ref, v_ref, start, n_chunks, chunk, mask_fn, n_rep):
    rows = qs.shape[0]
    tq = rows // n_rep
    n_lane_tiles = chunk // LANES
    scale = HEAD_DIM ** -0.5

    def scores(c):
        sl = pl.ds(pl.multiple_of(start + c * chunk, LANES), chunk)
        return lax.dot_general(qs, k_ref[sl, :], _NT, preferred_element_type=jnp.float32), sl

    def lane_tile(s, j):
        return s[:, j * LANES:(j + 1) * LANES].reshape(n_rep, tq, LANES)

    def row_max(c, mx):
        s, _ = scores(c)
        for j in range(n_lane_tiles):
            mx = jnp.maximum(mx, jnp.where(mask_fn(c, j)[None], lane_tile(s, j), NEG))
        return mx

    mx = lax.fori_loop(0, n_chunks, row_max, jnp.full((n_rep, tq, LANES), NEG, jnp.float32))
    m = jnp.broadcast_to(jnp.max(mx, axis=-1, keepdims=True), mx.shape)

    def accumulate(c, acc):
        s, sl = scores(c)
        p = [jnp.where(mask_fn(c, j)[None], jnp.exp((lane_tile(s, j) - m) * scale), 0.0)
             .astype(_BF).reshape(rows, LANES) for j in range(n_lane_tiles)]
        return acc + jnp.dot(jnp.concatenate(p, axis=1), v_ref[sl, :], preferred_element_type=jnp.float32)

    acc = lax.fori_loop(0, n_chunks, accumulate, jnp.zeros((rows, 2 * HEAD_DIM), jnp.float32))
    out = acc[:, :HEAD_DIM] * (1.0 / jnp.maximum(acc[:, HEAD_DIM:], TINY))
    return out.reshape(n_rep, tq, HEAD_DIM)


def _compress_rows(load, pe, w1, w2, n_ch):
    rows = [load(s) for s in range(CMP_STRIDE)]
    half = [jnp.concatenate([(rows[s] + pe[r * CMP_STRIDE + s:r * CMP_STRIDE + s + 1, :]).astype(_BF)
                             for s in range(CMP_STRIDE)], axis=1) for r in range(2)]
    h0 = jnp.dot(half[0], w1(0), preferred_element_type=jnp.float32)
    h1 = jnp.dot(half[1], w1(1), preferred_element_type=jnp.float32)
    h = h0 + pltpu.roll(h1, n_ch - 1, axis=0)
    return jnp.dot((h * jax.nn.sigmoid(h)).astype(_BF), w2, preferred_element_type=jnp.float32)


def _cmp_attend(qs, kc, vc, t_col, n_heads):
    rows = qs.shape[0]
    tq = rows // n_heads
    n_ch = kc.shape[0]
    scale = HEAD_DIM ** -0.5
    s = lax.dot_general(qs, kc, _NT, preferred_element_type=jnp.float32).reshape(n_heads, tq, n_ch)
    end = lax.broadcasted_iota(jnp.int32, (tq, n_ch), 1) * CMP_STRIDE + (CMP_BLOCK - 1)
    cmask = (end <= t_col)[None]
    s = jnp.where(cmask, s, NEG)
    m = jnp.max(s, axis=-1, keepdims=True)
    p = jnp.where(cmask, jnp.exp((s - m) * scale), 0.0)
    p = p * (1.0 / jnp.maximum(jnp.sum(p, axis=-1, keepdims=True), TINY))
    o_cmp = jnp.dot(p.reshape(rows, n_ch).astype(_BF), vc, preferred_element_type=jnp.float32)
    return o_cmp.reshape(n_heads, tq, HEAD_DIM), jnp.sum(p, axis=0)


def _cmp_attend_and_choose(qs, kc, vc, cover, t_col, n_heads, n_s):
    o_cmp, psum = _cmp_attend(qs, kc, vc, t_col, n_heads)
    tq, lanes = psum.shape[0], cover.shape[1]
    imp = jnp.dot(psum.astype(_BF), cover, preferred_element_type=jnp.float32)

    lane = lax.broadcasted_iota(jnp.int32, (tq, lanes), 1)
    lane_f = lane.astype(jnp.float32)
    jt = lax.shift_right_arithmetic(t_col, jnp.int32(SEL_BLOCK.bit_length() - 1))
    adm = lane <= jt
    forced = adm & ((lane == 0) | (lane > jt - N_LOCAL_SEL))
    work = jnp.where(forced, BIG, jnp.where(adm, imp, NEG))
    work = jnp.where(lane < n_s, work, -jnp.inf)
    sel = jnp.zeros((tq, lanes), jnp.bool_)
    for _ in range(min(N_SEL, n_s)):
        mx = jnp.max(work, axis=-1, keepdims=True)
        first = jnp.min(jnp.where(work == mx, lane_f, float(lanes)), axis=-1, keepdims=True)
        pick = lane_f == first
        sel = sel | pick
        work = jnp.where(pick, -jnp.inf, work)
    return o_cmp, sel


def _choose_blocks_keys_major(psum, cover_t, t_row, n_s):
    tq = psum.shape[0]
    imp = jnp.dot(cover_t, psum.T.astype(_BF), preferred_element_type=jnp.float32)
    blk = lax.broadcasted_iota(jnp.int32, (LANES, tq), 0)
    blk_f = blk.astype(jnp.float32)
    jt = lax.shift_right_arithmetic(t_row, jnp.int32(SEL_BLOCK.bit_length() - 1))
    adm = blk <= jt
    forced = adm & ((blk == 0) | (blk > jt - N_LOCAL_SEL))
    work = jnp.where(forced, BIG, jnp.where(adm, imp, NEG))
    work = jnp.where(blk < n_s, work, -jnp.inf)
    sel = jnp.zeros((LANES, tq), jnp.bool_)
    for _ in range(min(N_SEL, n_s)):
        mx = jnp.max(work, axis=0, keepdims=True)
        first = jnp.min(jnp.where(work == mx, blk_f, float(LANES)), axis=0, keepdims=True)
        pick = blk_f == first
        sel = sel | pick
        work = jnp.where(pick, -jnp.inf, work)
    return jnp.where(sel, 1.0, 0.0)


def _topk_mask(score_ref, key_ref, mask_ref, tri_ref, n_tiles, n_top):
    tq = score_ref.shape[0]
    int_min = jnp.int32(-2 ** 31)

    def to_key(t, _):
        sl = pl.ds(pl.multiple_of(t * LANES, LANES), LANES)
        bits = lax.bitcast_convert_type(score_ref[:, sl], jnp.int32)
        key_ref[:, sl] = bits ^ ((bits >> 31) & jnp.int32(0x7FFFFFFF))
        return 0

    lax.fori_loop(0, n_tiles, to_key, 0)

    unroll = 8 if isinstance(n_tiles, int) else 1

    def count_ge(cand):
        def body(t, acc):
            sl = pl.ds(pl.multiple_of(t * LANES, LANES), LANES)
            return acc + jnp.where(key_ref[:, sl] >= cand, 1.0, 0.0)
        acc = lax.fori_loop(0, n_tiles, body, jnp.zeros((tq, LANES), jnp.float32), unroll=unroll)
        return jnp.sum(acc, axis=-1, keepdims=True)

    tau = jnp.zeros((tq, 1), jnp.int32)
    for bit in range(31, -1, -1):
        cand = tau | jnp.int32(-2 ** 31 if bit == 31 else 1 << bit)
        tau = jnp.where(count_ge(cand ^ int_min) >= float(n_top), cand, tau)
    thr = tau ^ int_min

    clean = jnp.max(jnp.abs(count_ge(thr) - float(n_top))) == 0.0

    @pl.when(clean)
    def _():
        def keep_ge(t, _):
            sl = pl.ds(pl.multiple_of(t * LANES, LANES), LANES)
            mask_ref[:, sl] = jnp.where(key_ref[:, sl] >= thr, 1.0, 0.0)
            return 0
        lax.fori_loop(0, n_tiles, keep_ge, 0, unroll=unroll)

    @pl.when(jnp.logical_not(clean))
    def _():
        ones = jnp.ones((LANES, LANES), _BF)

        def count_gt(t, acc):
            sl = pl.ds(pl.multiple_of(t * LANES, LANES), LANES)
            return acc + jnp.where(key_ref[:, sl] > thr, 1.0, 0.0)

        n_gt = jnp.sum(lax.fori_loop(0, n_tiles, count_gt, jnp.zeros((tq, LANES), jnp.float32)),
                       axis=-1, keepdims=True)
        need = float(n_top) - n_gt

        def cut(t, eq_before):
            sl = pl.ds(pl.multiple_of(t * LANES, LANES), LANES)
            key = key_ref[:, sl]
            eq = jnp.where(key == thr, 1.0, 0.0)
            rank = (jnp.dot(eq.astype(_BF), tri_ref[...], preferred_element_type=jnp.float32)
                    + jnp.dot(eq_before.astype(_BF), ones, preferred_element_type=jnp.float32))
            keep = (key > thr) | ((key == thr) & (rank <= need))
            mask_ref[:, sl] = jnp.where(keep, 1.0, 0.0)
            return eq_before + eq

        lax.fori_loop(0, n_tiles, cut, jnp.zeros((tq, LANES), jnp.float32))


def _topk_mask_keys_major(score_ref, key_ref, tri_t_ref, n_tiles, n_top, tiles_per_iter=1):
    tq = score_ref.shape[1]
    int_min = jnp.int32(-2 ** 31)

    def tile(t):
        return pl.ds(pl.multiple_of(t * LANES, LANES), LANES)

    def to_key(t, _):
        bits = lax.bitcast_convert_type(score_ref[tile(t), :], jnp.int32)
        key_ref[tile(t), :] = bits ^ ((bits >> 31) & jnp.int32(0x7FFFFFFF))
        return 0

    lax.fori_loop(0, n_tiles, to_key, 0)

    def count(pred):
        def body(t, acc):
            for u in range(tiles_per_iter):
                hit = jnp.where(pred(key_ref[tile(t * tiles_per_iter + u), :]), 1.0, 0.0)
                acc = acc + jnp.sum(hit.reshape(LANES // 8, 8, tq), axis=0)
            return acc
        acc = lax.fori_loop(0, n_tiles // tiles_per_iter, body, jnp.zeros((8, tq), jnp.float32))
        return jnp.sum(acc, axis=0, keepdims=True)

    tau = jnp.zeros((1, tq), jnp.int32)
    for bit in range(31, -1, -1):
        cand = tau | jnp.int32(-2 ** 31 if bit == 31 else 1 << bit)
        cand_s = cand ^ int_min
        tau = jnp.where(count(lambda key: key >= cand_s) >= float(n_top), cand, tau)
    thr = tau ^ int_min
    clean = jnp.max(jnp.abs(count(lambda key: key >= thr) - float(n_top))) == 0.0

    @pl.when(clean)
    def _():
        def keep_ge(t, _):
            score_ref[tile(t), :] = jnp.where(key_ref[tile(t), :] >= thr, 1.0, 0.0)
            return 0
        lax.fori_loop(0, n_tiles, keep_ge, 0)

    @pl.when(jnp.logical_not(clean))
    def _():
        need = float(n_top) - count(lambda key: key > thr)

        def cut(t, eq_before):
            key = key_ref[tile(t), :]
            eq = jnp.where(key == thr, 1.0, 0.0)
            rank = jnp.dot(tri_t_ref[...], eq.astype(_BF), preferred_element_type=jnp.float32) + eq_before
            keep = (key > thr) | ((key == thr) & (rank <= need))
            score_ref[tile(t), :] = jnp.where(keep, 1.0, 0.0)
            return eq_before + jnp.sum(eq, axis=0, keepdims=True)

        lax.fori_loop(0, n_tiles, cut, jnp.zeros((1, tq), jnp.float32))


def _indexer_scores(iqs, ik, iw, iw_lane, n_idx):
    tq = iqs.shape[0] // n_idx
    n = ik.shape[0]
    logits = lax.dot_general(iqs, ik, _NT, preferred_element_type=jnp.float32).reshape(n_idx, tq, n)
    logits = jnp.maximum(logits * IDX_DIM ** -0.5, 0.0)
    sc = jnp.zeros((tq, n), jnp.float32)
    for h in range(n_idx):
        sc = sc + logits[h] * iw[:, iw_lane + h:iw_lane + h + 1]
    return sc * n_idx ** -0.5


def _cover_matrix(n_ch, lanes, n_s):
    ci = np.arange(n_ch)[:, None] * CMP_STRIDE
    sj = np.arange(lanes)[None, :] * SEL_BLOCK
    return jnp.asarray((ci < sj + SEL_BLOCK) & (ci + CMP_BLOCK > sj) & (sj < n_s * SEL_BLOCK), _BF)


def _expand_matrix(lanes, n_keys):
    return jnp.asarray(np.arange(lanes)[:, None] == np.arange(n_keys)[None, :] // SEL_BLOCK, _BF)


def _tri_matrix():
    return jnp.asarray(np.arange(LANES)[:, None] <= np.arange(LANES)[None, :], _BF)


def _pad_rows(x, n):
    return jnp.concatenate([x, jnp.zeros((n - x.shape[0], x.shape[1]), x.dtype)], axis=0)


def _compress_kernel(x_ref, w1_ref, w2_ref, pe_ref, cos_ref, sin_ref, o_ref):
    n_ch = o_ref.shape[-2]
    y = _compress_rows(lambda s: x_ref[pl.ds(s, n_ch, stride=CMP_STRIDE), :], pe_ref[0],
                       lambda r: w1_ref[0, r], w2_ref[0], n_ch)

    @pl.when(pl.program_id(1) == 0)
    def _():
        o_ref[0, 0, 0] = _rotary(y, cos_ref[...], sin_ref[...]).astype(o_ref.dtype)

    @pl.when(pl.program_id(1) != 0)
    def _():
        o_ref[0, 0, 0] = y.astype(o_ref.dtype)


def compress_prompt(proj, n_batch, seq, cmp_col, w1, w2, pe, cos_end, sin_end):
    n_ch = seq // CMP_STRIDE
    col0 = cmp_col // HEAD_DIM
    return pl.pallas_call(
        _compress_kernel,
        grid=(n_batch, 2, N_KV_A),
        in_specs=[pl.BlockSpec((seq, HEAD_DIM), lambda b, kv, g: (b, col0 + kv * N_KV_A + g)),
                  pl.BlockSpec((1,) + w1.shape[1:], lambda b, kv, g: (kv, 0, 0, 0)),
                  pl.BlockSpec((1, w2.shape[1], HEAD_DIM), lambda b, kv, g: (kv, 0, 0)),
                  pl.BlockSpec((1, CMP_BLOCK, HEAD_DIM), lambda b, kv, g: (kv, 0, 0)),
                  pl.BlockSpec((n_ch, HEAD_DIM), lambda b, kv, g: (0, 0)),
                  pl.BlockSpec((n_ch, HEAD_DIM), lambda b, kv, g: (0, 0))],
        out_specs=pl.BlockSpec((1, 1, 1, n_ch, HEAD_DIM), lambda b, kv, g: (b, kv, g, 0, 0)),
        out_shape=jax.ShapeDtypeStruct((n_batch, 2, N_KV_A, n_ch, HEAD_DIM), _BF),
        compiler_params=_params("parallel", "parallel", "parallel"),
    )(proj, w1, w2, pe, cos_end, sin_end)


def _nsa_prompt_kernel(q_ref, kc_ref, vc_ref, selk_ref, selv_ref, wink_ref, winv_ref, gate_ref,
                       cover_ref, expand_ref, o_ref, selexp_ref, sk_ref, sv_ref, wk_ref, wv_ref,
                       *, n_heads, n_s, chunk):
    tq = q_ref.shape[0]
    seq = selexp_ref.shape[-1]
    n_grp = N_KV_A
    qi = pl.program_id(1)
    row0 = pl.multiple_of(qi * tq, tq)
    t_col = qi * tq + lax.broadcasted_iota(jnp.int32, (tq, 1), 0)
    qw = n_heads * HEAD_DIM

    @pl.when(qi == 0)
    def _():
        for ref in (sk_ref, sv_ref, wk_ref, wv_ref):
            ref[...] = jnp.zeros(ref.shape, ref.dtype)

    ones = jnp.ones((tq, HEAD_DIM), _BF)
    causal = lax.broadcasted_iota(jnp.int32, (tq, seq), 1) <= t_col
    qs, o_cmp = [], []
    for g in range(n_grp):
        lanes = slice(g * HEAD_DIM, (g + 1) * HEAD_DIM)
        sk_ref[g, pl.ds(row0, tq), :] = selk_ref[:, lanes].astype(_BF)
        sv_ref[g, pl.ds(row0, tq), :] = jnp.concatenate([selv_ref[:, lanes].astype(_BF), ones], axis=1)
        wk_ref[g, pl.ds(row0, tq), :] = wink_ref[:, lanes].astype(_BF)
        wv_ref[g, pl.ds(row0, tq), :] = jnp.concatenate([winv_ref[:, lanes].astype(_BF), ones], axis=1)
        qs.append(_stack_heads(q_ref[:, g * qw:(g + 1) * qw], n_heads).astype(_BF))
    sels = []
    t_row = qi * tq + lax.broadcasted_iota(jnp.int32, (1, tq), 1)
    for g in range(n_grp):
        o, psum = _cmp_attend(qs[g], kc_ref[0, 0, g], vc_ref[0, 0, g], t_col, n_heads)
        o_cmp.append(o)
        sels.append(_choose_blocks_keys_major(psum, cover_ref[...], t_row, n_s))
    for g in range(n_grp):
        chosen = jnp.dot(sels[g].T.astype(_BF), expand_ref[...], preferred_element_type=jnp.float32)
        selexp_ref[g] = jnp.where(causal, chosen, 0.0)

    span = WINDOW + tq
    start = pl.multiple_of(jnp.maximum(row0 - WINDOW, 0), tq)
    d = t_col - (start + lax.broadcasted_iota(jnp.int32, (tq, span), 1))
    visible = (d >= 0) & (d < WINDOW)
    gates = jax.nn.sigmoid(gate_ref[...])
    n_all = n_grp * n_heads
    for g in range(n_grp):
        def sel_mask(c, j):
            return selexp_ref[g, :, pl.ds(pl.multiple_of(c * chunk + j * LANES, LANES), LANES)] > 0.5

        o_slc = _attend(qs[g], sk_ref.at[g], sv_ref.at[g], 0, (row0 + tq + chunk - 1) // chunk, chunk, sel_mask,
                        n_heads)
        o_win = _attend(qs[g], wk_ref.at[g], wv_ref.at[g], start, 1, span,
                        lambda c, j: visible[:, j * LANES:(j + 1) * LANES], n_heads)
        for r in range(n_heads):
            h = g * n_heads + r
            o = (gates[:, h:h + 1] * o_cmp[g][r] + gates[:, n_all + h:n_all + h + 1] * o_slc[r]
                 + gates[:, 2 * n_all + h:2 * n_all + h + 1] * o_win[r])
            o_ref[:, h * HEAD_DIM:(h + 1) * HEAD_DIM] = o.astype(o_ref.dtype)


def nsa_prompt(proj, kvc, lay, n_batch, seq, tq, chunk=512):
    n_heads = lay.n_heads_a // N_KV_A
    n_ch = kvc.shape[-2]
    n_s = -(-seq // SEL_BLOCK)
    assert tq % LANES == 0 and n_s <= LANES and seq % chunk == 0 and chunk % tq == 0
    assert WINDOW % tq == 0 and seq >= WINDOW + tq and lay.ga % LANES == 0
    nq = seq // tq
    qw = lay.n_heads_a * HEAD_DIM
    kvw = N_KV_A * HEAD_DIM

    def tile(col):
        return pl.BlockSpec((tq, kvw), lambda b, i: (b * nq + i, col // kvw))

    return pl.pallas_call(
        functools.partial(_nsa_prompt_kernel, n_heads=n_heads, n_s=n_s, chunk=chunk),
        grid=(n_batch, nq),
        in_specs=[pl.BlockSpec((tq, qw), lambda b, i: (b * nq + i, lay.qa // qw)),
                  pl.BlockSpec((1, 1, N_KV_A, n_ch, HEAD_DIM), lambda b, i: (b, 0, 0, 0, 0)),
                  pl.BlockSpec((1, 1, N_KV_A, n_ch, HEAD_DIM), lambda b, i: (b, 1, 0, 0, 0)),
                  tile(lay.selk), tile(lay.selv), tile(lay.wink), tile(lay.winv),
                  pl.BlockSpec((tq, LANES), lambda b, i: (b * nq + i, lay.ga // LANES)),
                  pl.BlockSpec((LANES, n_ch), lambda b, i: (0, 0)),
                  pl.BlockSpec((LANES, seq), lambda b, i: (0, 0))],
        out_specs=pl.BlockSpec((tq, qw), lambda b, i: (b * nq + i, 0)),
        out_shape=jax.ShapeDtypeStruct((n_batch * seq, qw), _BF),
        scratch_shapes=[pltpu.VMEM((N_KV_A, tq, seq), jnp.float32),
                        pltpu.VMEM((N_KV_A, seq, HEAD_DIM), _BF), pltpu.VMEM((N_KV_A, seq, 2 * HEAD_DIM), _BF),
                        pltpu.VMEM((N_KV_A, seq, HEAD_DIM), _BF), pltpu.VMEM((N_KV_A, seq, 2 * HEAD_DIM), _BF)],
        compiler_params=_params("parallel", "arbitrary"),
    )(proj, kvc, kvc, proj, proj, proj, proj, proj, _cover_matrix(n_ch, LANES, n_s).T, _expand_matrix(LANES, seq))


def _dsa_prompt_kernel(iq_ref, iw_ref, q_ref, ik_ref, k_ref, v_ref, tri_ref, o_ref,
                       score_ref, key_ref, mask_ref, ikb_ref, kb_ref, vb_ref,
                       *, n_idx, n_heads, n_top, iw_lane, chunk):
    tq = iq_ref.shape[0]
    qi = pl.program_id(1)
    n_tiles = qi + 1
    row0 = pl.multiple_of(qi * tq, tq)
    t_row = qi * tq + lax.broadcasted_iota(jnp.int32, (1, tq), 1)

    @pl.when(qi == 0)
    def _():
        for ref in (ikb_ref, kb_ref, vb_ref):
            ref[...] = jnp.zeros(ref.shape, ref.dtype)

    ikb_ref[pl.ds(row0, tq), :] = ik_ref[...].astype(_BF)
    kb_ref[pl.ds(row0, tq), :] = k_ref[...].astype(_BF)
    vb_ref[pl.ds(row0, tq), :] = jnp.concatenate([v_ref[...].astype(_BF), jnp.ones((tq, HEAD_DIM), _BF)], axis=1)
    iqs = _stack_heads(iq_ref[...], n_idx).astype(_BF)
    iw_t = iw_ref[...].T

    def key_pos(kt, n):
        return kt * n + lax.broadcasted_iota(jnp.int32, (n, 1), 0)

    def score_tile(kt, _):
        sl = pl.ds(pl.multiple_of(kt * tq, tq), tq)
        logits = lax.dot_general(ikb_ref[sl, :], iqs, _NT, preferred_element_type=jnp.float32)
        sc = jnp.zeros((tq, tq), jnp.float32)
        for h in range(n_idx):
            sc = sc + (jnp.maximum(logits[:, h * tq:(h + 1) * tq] * IDX_DIM ** -0.5, 0.0)
                       * iw_t[iw_lane + h:iw_lane + h + 1, :])
        score_ref[sl, :] = jnp.where(key_pos(kt, tq) <= t_row, sc * n_idx ** -0.5, NEG)
        return 0

    lax.fori_loop(0, n_tiles, score_tile, 0)
    n_tiles128 = n_tiles * (tq // LANES)
    _topk_mask_keys_major(score_ref, key_ref, tri_ref, n_tiles128, n_top, tiles_per_iter=tq // LANES)
    mask_ref[...] = jnp.zeros(mask_ref.shape, mask_ref.dtype)

    def to_rows(t, _):
        sl = pl.ds(pl.multiple_of(t * LANES, LANES), LANES)
        mask_ref[:, sl] = jnp.where(key_pos(t, LANES) <= t_row, score_ref[sl, :], 0.0).T
        return 0

    lax.fori_loop(0, n_tiles128, to_rows, 0)

    def dsa_mask(c, j):
        return mask_ref[:, pl.ds(pl.multiple_of(c * chunk + j * LANES, LANES), LANES)] > 0.5

    qs = _stack_heads(q_ref[...], n_heads).astype(_BF)
    o = _attend(qs, kb_ref, vb_ref, 0, (row0 + tq + chunk - 1) // chunk, chunk, dsa_mask, n_heads)
    for h in range(n_heads):
        o_ref[:, h * HEAD_DIM:(h + 1) * HEAD_DIM] = o[h].astype(o_ref.dtype)


def dsa_prompt(proj, lay, n_batch, seq, tq, chunk=512):
    assert tq % LANES == 0 and seq % chunk == 0 and chunk % tq == 0
    nq = seq // tq
    n_top = min(DSA_TOPK, seq // 4)
    iqw = lay.n_idx_heads * IDX_DIM
    qw = lay.n_heads_b * HEAD_DIM

    def tile(col):
        c0 = col // HEAD_DIM
        return pl.BlockSpec((tq, HEAD_DIM), lambda b, i: (b * nq + i, c0))

    return pl.pallas_call(
        functools.partial(_dsa_prompt_kernel, n_idx=lay.n_idx_heads, n_heads=lay.n_heads_b, n_top=n_top,
                          iw_lane=lay.iw % LANES, chunk=chunk),
        grid=(n_batch, nq),
        in_specs=[pl.BlockSpec((tq, iqw), lambda b, i: (b * nq + i, lay.iq // iqw)),
                  pl.BlockSpec((tq, LANES), lambda b, i: (b * nq + i, lay.iw // LANES)),
                  pl.BlockSpec((tq, qw), lambda b, i: (b * nq + i, lay.qb // qw)),
                  tile(lay.ik), tile(lay.dk), tile(lay.dv),
                  pl.BlockSpec((LANES, LANES), lambda b, i: (0, 0))],
        out_specs=pl.BlockSpec((tq, qw), lambda b, i: (b * nq + i, 0)),
        out_shape=jax.ShapeDtypeStruct((n_batch * seq, qw), _BF),
        scratch_shapes=[pltpu.VMEM((seq, tq), jnp.float32), pltpu.VMEM((seq, tq), jnp.int32),
                        pltpu.VMEM((tq, seq), jnp.float32),
                        pltpu.VMEM((seq, IDX_DIM), _BF), pltpu.VMEM((seq, HEAD_DIM), _BF),
                        pltpu.VMEM((seq, 2 * HEAD_DIM), _BF)],
        compiler_params=_params("parallel", "arbitrary"),
    )(proj, proj, proj, proj, proj, proj, _tri_matrix().T)


def _page_specs(rows, per_step=None):
    per_step = PAGES_PER_STEP if per_step is None else per_step
    def spec(k):
        return pl.BlockSpec((1, rows, LANES), lambda b, s, pt: (pt[b, s * per_step + k], 0, 0))
    return [spec(k) for k in range(per_step)]


def _rows_view(x, lead):
    return x.reshape(x.shape[:lead] + (-1, x.shape[-1]))


def _kind_rows(ref, kind, n_kinds, n_rows):
    page = ref.at[0] if len(ref.shape) == 3 else ref
    return page[pl.ds(kind, n_rows, stride=n_kinds), :]


NSA_PAGES_PER_STEP = 8
NSA_PAGE_SLOTS = 4


def _nsa_sample_kernel(pt_ref, pool_ref, q_ref, gate_ref, nsk_ref, nsv_ref, nwk_ref, nwv_ref, win_ref,
                       w1_ref, w2_ref, pe_ref, cos_ref, sin_ref, cover_ref, expand_ref, o_ref,
                       rows_ref, selk_ref, selv_ref, page_buf, page_sem, *, n_heads, n_s, past_len):
    step = pl.program_id(1)
    n_steps = pl.num_programs(1)
    n_total = pl.num_programs(0) * n_steps
    now = pl.program_id(0) * n_steps + step

    def page_copy(p, k):
        page_id = pt_ref[p // n_steps, (p % n_steps) * NSA_PAGES_PER_STEP + k]
        slot = p % NSA_PAGE_SLOTS
        return pltpu.make_async_copy(pool_ref.at[page_id], page_buf.at[slot, k], page_sem.at[slot])

    def start_pages(p):
        for k in range(NSA_PAGES_PER_STEP):
            page_copy(p, k).start()

    @pl.when(now == 0)
    def _():
        for p in range(NSA_PAGE_SLOTS - 1):
            start_pages(p)

    @pl.when(now + NSA_PAGE_SLOTS - 1 < n_total)
    def _():
        start_pages(now + NSA_PAGE_SLOTS - 1)

    for k in range(NSA_PAGES_PER_STEP):
        page_copy(now, k).wait()
    pages = [page_buf.at[now % NSA_PAGE_SLOTS, k] for k in range(NSA_PAGES_PER_STEP)]
    ts = q_ref.shape[0]
    qw = n_heads * HEAD_DIM
    n_grp = N_KV_A
    n_kinds = 4 * n_grp
    page = pages[0].shape[0] // n_kinds
    t_col = past_len + lax.broadcasted_iota(jnp.int32, (ts, 1), 0)

    for k in range(NSA_PAGES_PER_STEP):
        start = pl.multiple_of((step * NSA_PAGES_PER_STEP + k) * page, page)
        for c in range(2 * n_grp):
            rows_ref.at[c][pl.ds(start, page), :] = _kind_rows(pages[k], c, n_kinds, page)
        for g in range(n_grp):
            selk_ref.at[g][pl.ds(start, page), :] = _kind_rows(pages[k], 2 * n_grp + g, n_kinds, page).astype(_BF)
            selv_ref.at[g][pl.ds(start, page), :] = _kind_rows(pages[k], 3 * n_grp + g, n_kinds, page).astype(_BF)

    @pl.when(step == pl.num_programs(1) - 1)
    def _():
        n_ch = cover_ref.shape[0]
        gates = jax.nn.sigmoid(gate_ref[...])
        n_all = n_grp * n_heads
        w_len = win_ref.shape[1] // (2 * n_grp)
        row = lax.broadcasted_iota(jnp.int32, (ts, LANES), 0)
        lane = lax.broadcasted_iota(jnp.int32, (ts, LANES), 1)
        new_causal = (lane <= row) & (lane < ts)
        wlane = lax.broadcasted_iota(jnp.int32, (ts, w_len + LANES), 1)
        k_pos = past_len - w_len + wlane
        d = t_col - k_pos
        win_mask = (d >= 0) & (d < WINDOW) & (k_pos >= 0) & (wlane < w_len + ts)
        new_blk = past_len // SEL_BLOCK

        def group_lanes(x, g):
            return x[:, g * HEAD_DIM:(g + 1) * HEAD_DIM]

        for g in range(n_grp):
            qs = _stack_heads(q_ref[:, g * qw:(g + 1) * qw], n_heads).astype(_BF)
            kc, vc = [_compress_rows(lambda s: rows_ref.at[kv * n_grp + g][pl.ds(s, n_ch, stride=CMP_STRIDE), :],
                                     pe_ref[kv], lambda r: w1_ref[kv, r], w2_ref[kv], n_ch) for kv in range(2)]
            kc = _rotary(kc, cos_ref[...], sin_ref[...])
            o_cmp, sel = _cmp_attend_and_choose(qs, kc.astype(_BF), vc.astype(_BF), cover_ref[...], t_col, n_heads, n_s)
            sel_f = jnp.where(sel, 1.0, 0.0)
            chosen = jnp.dot(sel_f.astype(_BF), expand_ref[...], preferred_element_type=jnp.float32) > 0.5
            carry = _flash_update(qs, selk_ref[g], selv_ref[g], chosen, _flash_init(n_heads, ts), n_heads)
            in_new = sel_f[:, new_blk:new_blk + 1] > 0.5
            k_new = _pad_rows(group_lanes(nsk_ref[...], g), LANES).astype(_BF)
            v_new = _pad_rows(group_lanes(nsv_ref[...], g), LANES).astype(_BF)
            o_slc = _flash_finish(_flash_update(qs, k_new, v_new, new_causal & in_new, carry, n_heads))
            kw = jnp.concatenate([_kind_rows(win_ref, g, 2 * n_grp, w_len),
                                  _pad_rows(group_lanes(nwk_ref[...], g), LANES)], axis=0).astype(_BF)
            vw = jnp.concatenate([_kind_rows(win_ref, n_grp + g, 2 * n_grp, w_len),
                                  _pad_rows(group_lanes(nwv_ref[...], g), LANES)], axis=0).astype(_BF)
            o_win = _flash_finish(_flash_update(qs, kw, vw, win_mask, _flash_init(n_heads, ts), n_heads))
            for r in range(n_heads):
                h = g * n_heads + r
                o = (gates[:, h:h + 1] * o_cmp[r] + gates[:, n_all + h:n_all + h + 1] * o_slc[r]
                     + gates[:, 2 * n_all + h:2 * n_all + h + 1] * o_win[r])
                o_ref[0, :, h * HEAD_DIM:(h + 1) * HEAD_DIM] = o.astype(o_ref.dtype)


def nsa_sample(proj, row0, cache, win_buf, page_table, w1, w2, pe, cos_end, sin_end, lay, ts):
    n_batch, n_pages = page_table.shape
    page = cache.shape[1]
    past_len = n_pages * page
    pool = _rows_view(cache, 1)
    win_rows = _rows_view(win_buf, 1)
    n_heads = lay.n_heads_a // N_KV_A
    n_ch = past_len // CMP_STRIDE
    n_s = -(-(past_len + ts) // SEL_BLOCK)
    sel_lanes = -(-n_s // LANES) * LANES
    kvw = N_KV_A * HEAD_DIM
    assert n_pages % NSA_PAGES_PER_STEP == 0 and row0 % ts == 0 and ts <= SEL_BLOCK and past_len % SEL_BLOCK == 0
    assert n_batch * (n_pages // NSA_PAGES_PER_STEP) >= NSA_PAGE_SLOTS - 1
    assert ts % 8 == 0 and lay.ga % LANES == 0
    assert (past_len + ts - CMP_BLOCK) // CMP_STRIDE + 1 <= n_ch and n_ch * CMP_STRIDE <= past_len
    r0 = row0 // ts
    qw = lay.n_heads_a * HEAD_DIM
    rows = lambda width, col: pl.BlockSpec((ts, width), lambda b, s, pt: (r0 + b, col // width))
    once = lambda shape: pl.BlockSpec(shape, lambda b, s, pt: (0,) * len(shape), pipeline_mode=pl.Buffered(1))
    grid_spec = pltpu.PrefetchScalarGridSpec(
        num_scalar_prefetch=1,
        grid=(n_batch, n_pages // NSA_PAGES_PER_STEP),
        in_specs=[
            pl.BlockSpec(memory_space=pl.ANY),
            rows(qw, lay.qa), rows(LANES, lay.ga),
            rows(kvw, lay.selk), rows(kvw, lay.selv), rows(kvw, lay.wink), rows(kvw, lay.winv),
            pl.BlockSpec((1,) + win_rows.shape[1:], lambda b, s, pt: (b, 0, 0)),
            once(w1.shape), once(w2.shape), once(pe.shape), once((n_ch, HEAD_DIM)), once((n_ch, HEAD_DIM)),
            once((n_ch, sel_lanes)), once((sel_lanes, past_len))],
        out_specs=pl.BlockSpec((1, ts, qw), lambda b, s, pt: (b, 0, 0)),
        scratch_shapes=[pltpu.VMEM((2 * N_KV_A, past_len, HEAD_DIM), jnp.float32),
                        pltpu.VMEM((N_KV_A, past_len, HEAD_DIM), _BF),
                        pltpu.VMEM((N_KV_A, past_len, HEAD_DIM), _BF),
                        pltpu.VMEM((NSA_PAGE_SLOTS, NSA_PAGES_PER_STEP) + pool.shape[1:], jnp.float32),
                        pltpu.SemaphoreType.DMA((NSA_PAGE_SLOTS,))],
    )
    return pl.pallas_call(
        functools.partial(_nsa_sample_kernel, n_heads=n_heads, n_s=n_s, past_len=past_len),
        grid_spec=grid_spec,
        out_shape=jax.ShapeDtypeStruct((n_batch, ts, qw), _BF),
        compiler_params=_params("arbitrary", "arbitrary"),
    )(page_table, pool, proj, proj, proj, proj, proj, proj, win_rows,
      w1, w2, pe, cos_end, sin_end, _cover_matrix(n_ch, sel_lanes, n_s), _expand_matrix(sel_lanes, past_len))


def _dsa_sample_select_kernel(pt_ref, *refs, n_idx, n_top, iw_lane, past_len):
    pages = refs[:PAGES_PER_STEP]
    iq_ref, iw_ref, nik_ref, tri_ref, mask_ref, score_ref, key_ref = refs[PAGES_PER_STEP:]
    step = pl.program_id(1)
    ts = iq_ref.shape[0]
    keys_per_step = PAGES_PER_STEP * pages[0].shape[1]
    iqs = _stack_heads(iq_ref[...], n_idx).astype(_BF)
    iw = iw_ref[...]
    ik = jnp.concatenate([p[0] for p in pages], axis=0).astype(_BF)
    start = pl.multiple_of(step * keys_per_step, keys_per_step)
    score_ref[:, pl.ds(start, keys_per_step)] = _indexer_scores(iqs, ik, iw, iw_lane, n_idx)

    @pl.when(step == pl.num_programs(1) - 1)
    def _():
        sc = _indexer_scores(iqs, _pad_rows(nik_ref[...], LANES).astype(_BF), iw, iw_lane, n_idx)
        row = lax.broadcasted_iota(jnp.int32, (ts, LANES), 0)
        lane = lax.broadcasted_iota(jnp.int32, (ts, LANES), 1)
        score_ref[:, past_len:past_len + LANES] = jnp.where(lane < ts, jnp.where(lane <= row, sc, NEG), -jnp.inf)
        _topk_mask(score_ref, key_ref, mask_ref.at[0], tri_ref, past_len // LANES + 1, n_top)


def dsa_sample_select(proj, row0, cache_idx, page_table, lay, ts):
    n_batch, n_pages = page_table.shape
    page = cache_idx.shape[1]
    past_len = n_pages * page
    assert n_pages % PAGES_PER_STEP == 0 and row0 % ts == 0 and ts <= LANES and past_len % LANES == 0
    n_top = min(DSA_TOPK, (past_len + ts) // 4)
    r0 = row0 // ts
    iqw = lay.n_idx_heads * IDX_DIM
    width = past_len + LANES
    rows = lambda w, col: pl.BlockSpec((ts, w), lambda b, s, pt: (r0 + b, col // w))
    grid_spec = pltpu.PrefetchScalarGridSpec(
        num_scalar_prefetch=1,
        grid=(n_batch, n_pages // PAGES_PER_STEP),
        in_specs=_page_specs(page) + [
            rows(iqw, lay.iq), rows(LANES, lay.iw), rows(IDX_DIM, lay.ik),
            pl.BlockSpec((LANES, LANES), lambda b, s, pt: (0, 0))],
        out_specs=pl.BlockSpec((1, ts, width), lambda b, s, pt: (b, 0, 0)),
        scratch_shapes=[pltpu.VMEM((ts, width), jnp.float32), pltpu.VMEM((ts, width), jnp.int32)],
    )
    return pl.pallas_call(
        functools.partial(_dsa_sample_select_kernel, n_idx=lay.n_idx_heads, n_top=n_top,
                          iw_lane=lay.iw % LANES, past_len=past_len),
        grid_spec=grid_spec,
        out_shape=jax.ShapeDtypeStruct((n_batch, ts, width), jnp.float32),
        compiler_params=_params("parallel", "arbitrary"),
    )(page_table, *([cache_idx] * PAGES_PER_STEP), proj, proj, proj, _tri_matrix())


def _dsa_sample_attend_kernel(pt_ref, *refs, n_heads):
    pages = refs[:PAGES_PER_STEP]
    q_ref, mask_ref, nmask_ref, nk_ref, nv_ref, o_ref, m_ref, l_ref, acc_ref = refs[PAGES_PER_STEP:]
    step = pl.program_id(1)
    ts = q_ref.shape[0]
    qs = _stack_heads(q_ref[...], n_heads).astype(_BF)

    @pl.when(step == 0)
    def _():
        m_ref[...], l_ref[...], acc_ref[...] = _flash_init(n_heads, ts)

    page = pages[0].shape[1] // 2
    k = jnp.concatenate([_kind_rows(p, 0, 2, page) for p in pages], axis=0).astype(_BF)
    v = jnp.concatenate([_kind_rows(p, 1, 2, page) for p in pages], axis=0).astype(_BF)
    m_ref[...], l_ref[...], acc_ref[...] = _flash_update(qs, k, v, mask_ref[0] > 0.5,
                                                         (m_ref[...], l_ref[...], acc_ref[...]), n_heads)

    @pl.when(step == pl.num_programs(1) - 1)
    def _():
        row = lax.broadcasted_iota(jnp.int32, (ts, LANES), 0)
        lane = lax.broadcasted_iota(jnp.int32, (ts, LANES), 1)
        mask = (nmask_ref[0] > 0.5) & (lane <= row) & (lane < ts)
        o = _flash_finish(_flash_update(qs, _pad_rows(nk_ref[...], LANES).astype(_BF),
                                        _pad_rows(nv_ref[...], LANES).astype(_BF), mask,
                                        (m_ref[...], l_ref[...], acc_ref[...]), n_heads))
        for h in range(n_heads):
            o_ref[0, :, h * HEAD_DIM:(h + 1) * HEAD_DIM] = o[h].astype(o_ref.dtype)


def dsa_sample_attend(proj, row0, mask, cache_kv, page_table, lay, ts):
    n_batch, n_pages = page_table.shape
    page = cache_kv.shape[1]
    past_len = n_pages * page
    pool = _rows_view(cache_kv, 1)
    keys_per_step = PAGES_PER_STEP * page
    r0 = row0 // ts
    qw = lay.n_heads_b * HEAD_DIM
    rows = lambda w, col: pl.BlockSpec((ts, w), lambda b, s, pt: (r0 + b, col // w))
    grid_spec = pltpu.PrefetchScalarGridSpec(
        num_scalar_prefetch=1,
        grid=(n_batch, n_pages // PAGES_PER_STEP),
        in_specs=_page_specs(pool.shape[1]) + [
            rows(qw, lay.qb),
            pl.BlockSpec((1, ts, keys_per_step), lambda b, s, pt: (b, 0, s)),
            pl.BlockSpec((1, ts, LANES), lambda b, s, pt: (b, 0, past_len // LANES)),
            rows(HEAD_DIM, lay.dk), rows(HEAD_DIM, lay.dv)],
        out_specs=pl.BlockSpec((1, ts, qw), lambda b, s, pt: (b, 0, 0)),
        scratch_shapes=[pltpu.VMEM((lay.n_heads_b, ts, 1), jnp.float32),
                        pltpu.VMEM((lay.n_heads_b, ts, 1), jnp.float32),
                        pltpu.VMEM((lay.n_heads_b, ts, HEAD_DIM), jnp.float32)],
    )
    return pl.pallas_call(
        functools.partial(_dsa_sample_attend_kernel, n_heads=lay.n_heads_b),
        grid_spec=grid_spec,
        out_shape=jax.ShapeDtypeStruct((n_batch, ts, qw), _BF),
        compiler_params=_params("parallel", "arbitrary"),
    )(page_table, *([pool] * PAGES_PER_STEP), proj, mask, mask, proj, proj)


class Layout:
    def __init__(self, d_model, n_heads_a, n_heads_b, n_idx_heads):
        self.n_heads_a, self.n_heads_b, self.n_idx_heads = n_heads_a, n_heads_b, n_idx_heads
        src = np.cumsum([0, n_heads_a * HEAD_DIM, 6 * N_KV_A * HEAD_DIM, 3 * n_heads_a,
                         n_heads_b * HEAD_DIM, 2 * HEAD_DIM, n_idx_heads * IDX_DIM, n_idx_heads,
                         IDX_DIM, 2 * d_model])
        s_qa, s_kva, s_ga, s_qb, s_kvb, s_iq, s_iw, s_ik, s_mg, s_end = (int(v) for v in src)
        small = 3 * n_heads_a + n_idx_heads
        assert small <= LANES
        self.small_pad = LANES - small
        kvw = N_KV_A * HEAD_DIM
        self.pieces = [
            (s_qa, s_kva, True),
            (s_qb, s_kvb, True),
            (s_iq, s_iw, True),
            (s_kva, s_kva + 2 * kvw, False),
            (s_kva + 2 * kvw, s_kva + 3 * kvw, True),
            (s_kva + 3 * kvw, s_kva + 4 * kvw, False),
            (s_kva + 4 * kvw, s_kva + 5 * kvw, True),
            (s_kva + 5 * kvw, s_ga, False),
            (s_kvb, s_kvb + HEAD_DIM, True),
            (s_kvb + HEAD_DIM, s_iq, False),
            (s_ik, s_mg, True),
            (s_ga, s_qb, False),
            (s_iw, s_ik, False),
            None,
            (s_mg, s_end, False),
        ]
        off = 0
        starts = []
        for p in self.pieces:
            starts.append(off)
            off += self.small_pad if p is None else p[1] - p[0]
        (self.qa, self.qb, self.iq, self.cmp, self.selk, self.selv, self.wink, self.winv, self.dk,
         self.dv, self.ik, self.ga, self.iw, _, self.mg) = starts
        self.width = off
        assert self.width % LANES == 0
        flags = np.zeros(self.width // LANES, np.int32)
        for st, p in zip(starts, self.pieces):
            if p is not None and p[2]:
                assert st % LANES == 0 and (p[1] - p[0]) % LANES == 0
                flags[st // LANES:(st + p[1] - p[0]) // LANES] = 1
        self.rope_flags = flags

    def pack(self, w_in):
        cols = []
        for p in self.pieces:
            if p is None:
                cols.append(jnp.zeros((w_in.shape[0], self.small_pad), w_in.dtype))
            else:
                cols.append(w_in[:, p[0]:p[1]])
        return jnp.concatenate(cols, axis=1).astype(_BF)


def rope_tables(pos):
    half = HEAD_DIM // 2
    inv = ROPE_THETA ** (-jnp.arange(half, dtype=jnp.float32) / half)
    ang = pos.astype(jnp.float32)[:, None] * inv[None, :]
    cos, sin = jnp.cos(ang), jnp.sin(ang)
    return jnp.concatenate([cos, cos], axis=1), jnp.concatenate([-sin, sin], axis=1)


def _swiglu_half_step(h, g, wg, wu, wd, w_index):
    xn = rmsnorm(h, g, _BF)
    m = h.shape[0]
    a = ffn_gate_up(xn, wg, wu, w_index, tm=_row_tile(m, ROW_TILE_WIDE), tn=2 * LANES)
    return resid_matmul(a, wd, w_index, h, 0.5, tm=_row_tile(m, ROW_TILE_DOWN), tn=2 * LANES)


def kernel(x_prompt, x_sample, cache_nsa_kv, cache_nsa_win, cache_dsa_kv, cache_dsa_idx, page_table,
           g_norm, w_ffn_gate, w_ffn_up, w_ffn_down, w_in, w_cmp1, w_cmp2, cmp_pos,
           w_br_a, w_br_b, w_out, g_final):
    B, T, D = x_prompt.shape
    DB, Ts, _ = x_sample.shape
    depth = g_norm.shape[0]
    page = cache_nsa_kv.shape[2]
    past_len = page_table.shape[1] * page
    n_heads_a = w_br_a.shape[1] // HEAD_DIM
    n_heads_b = w_br_b.shape[1] // HEAD_DIM
    G = N_KV_A
    lay = Layout(D, n_heads_a, n_heads_b, n_heads_b // 2)
    Mp, Ms = B * T, DB * Ts
    tm_wide = _row_tile(Mp + Ms, ROW_TILE_WIDE)

    pos_p = jnp.arange(T, dtype=jnp.int32)
    pos_s = past_len + jnp.arange(Ts, dtype=jnp.int32)
    cos, sin = rope_tables(jnp.concatenate([jnp.tile(pos_p, B), jnp.tile(pos_s, DB)]))
    rope_flags = jnp.asarray(lay.rope_flags)

    def block_end_tables(n_ch):
        return rope_tables(jnp.arange(n_ch, dtype=jnp.int32) * CMP_STRIDE + CMP_BLOCK - 1)

    h = jnp.concatenate([x_prompt.reshape(Mp, D), x_sample.reshape(Ms, D)], axis=0)
    outs = [[] for _ in range(8)]
    for l in range(depth):
        h = _swiglu_half_step(h, g_norm[l, 0], w_ffn_gate, w_ffn_up, w_ffn_down, (l, 0))

        u = rmsnorm(h, g_norm[l, 1], _BF)
        proj, nsa_rows, win_rows, dsa_rows, ik_rows = in_project(u, lay.pack(w_in[l]), rope_flags, cos, sin, lay,
                                                                 tm=tm_wide, tn=4 * LANES)
        w1 = w_cmp1[l].reshape(2, 2, CMP_STRIDE * HEAD_DIM, -1).astype(_BF)
        w2 = w_cmp2[l].astype(_BF)

        kvc_p = compress_prompt(proj, B, T, lay.cmp, w1, w2, cmp_pos[l], *block_end_tables(T // CMP_STRIDE))
        o_a_p = nsa_prompt(proj, kvc_p, lay, B, T, tq=Q_TILE)
        o_b_p = dsa_prompt(proj, lay, B, T, tq=Q_TILE)

        o_a_s = nsa_sample(proj, Mp, cache_nsa_kv[l], cache_nsa_win[l], page_table, w1, w2, cmp_pos[l],
                           *block_end_tables(past_len // CMP_STRIDE), lay, Ts)
        top_mask = dsa_sample_select(proj, Mp, cache_dsa_idx[l], page_table, lay, Ts)
        o_b_s = dsa_sample_attend(proj, Mp, top_mask, cache_dsa_kv[l], page_table, lay, Ts)

        o_a = jnp.concatenate([o_a_p, o_a_s.reshape(Ms, -1)], axis=0)
        o_b = jnp.concatenate([o_b_p, o_b_s.reshape(Ms, -1)], axis=0)
        m = merge_branches(o_a, o_b, w_br_a, w_br_b, (l,), proj, lay.mg, tm=tm_wide, tn=4 * LANES)
        h = resid_matmul(m, w_out, (l,), h, 1.0, tm=tm_wide, tn=4 * LANES)

        win_p = win_rows[:Mp].reshape(B, T, 2, G, HEAD_DIM)
        win_s = win_rows[Mp:].reshape(DB, Ts, 2, G, HEAD_DIM)
        outs[0].append(nsa_rows[:Mp].reshape(B, T, 4, G, HEAD_DIM))
        outs[1].append(win_p[:, T - min(WINDOW, T):])
        outs[2].append(dsa_rows[:Mp].reshape(B, T, 2, HEAD_DIM))
        outs[3].append(ik_rows[:Mp].reshape(B, T, IDX_DIM))
        outs[4].append(nsa_rows[Mp:].reshape(DB, Ts, 4, G, HEAD_DIM))
        outs[5].append(jnp.concatenate([cache_nsa_win[l], win_s], axis=1)[:, Ts:])
        outs[6].append(dsa_rows[Mp:].reshape(DB, Ts, 2, HEAD_DIM))
        outs[7].append(ik_rows[Mp:].reshape(DB, Ts, IDX_DIM))

        h = _swiglu_half_step(h, g_norm[l, 2], w_ffn_gate, w_ffn_up, w_ffn_down, (l, 1))

    y_p = rmsnorm(h, g_final, jnp.float32, 0, Mp)
    y_s = rmsnorm(h, g_final, jnp.float32, Mp, Ms)
    return (y_p.reshape(B, T, D), y_s.reshape(DB, Ts, D), *(jnp.stack(o) for o in outs))
```
